```python
import math
import jax, jax.numpy as jnp
from jax import lax
import numpy as np

D_MODEL = 1024
BATCH = 16
SEQ = 2048
DEPTH = 2

MIX_WIDTH = D_MODEL
MLA_WIDTH = D_MODEL // 2
SSM_WIDTH = MIX_WIDTH - MLA_WIDTH
V_HEAD = 64
N_HEADS = MLA_WIDTH // V_HEAD
QK_NOPE = 64
QK_ROPE = 32
Q_LORA = D_MODEL // 4
KV_LORA = D_MODEL // 8
ROPE_BASE = 10000.0
Q_BLOCK = 128
ATTN_SCALE = 1.0 / math.sqrt(QK_NOPE + QK_ROPE)
SSM_CH = 16
SSM_GROUPS = SSM_WIDTH // SSM_CH
SSM_STATE = 64
STEP_MIN = 0.001
STEP_MAX = 0.1
OFF_KV = Q_LORA
OFF_KR = OFF_KV + KV_LORA
OFF_SSM = OFF_KR + QK_ROPE
IN_WIDTH = OFF_SSM + SSM_WIDTH
N_EXPERT_GROUPS = 4
EXPERTS_PER_GROUP = 8
N_EXPERTS = N_EXPERT_GROUPS * EXPERTS_PER_GROUP
TOP_K = 2
D_EXPERT = D_MODEL // 4
DISPATCH_BLOCK = 128
PLE_DIM = 256
EPS = 1e-6

kernel_name = "hymba_mla_s5_hiermoe_ple"


def rmsnorm(x, g):
    xf = x.astype(jnp.float32)
    y = xf * lax.rsqrt(jnp.mean(xf * xf, axis=-1, keepdims=True) + EPS)
    return (y * g.astype(jnp.float32)).astype(x.dtype)


def rope_tables(positions):
    half = QK_ROPE // 2
    inv_freq = ROPE_BASE ** (-jnp.arange(half, dtype=jnp.float32) / half)
    ang = positions.astype(jnp.float32)[..., None] * inv_freq
    return jnp.cos(ang)[:, :, None, :], jnp.sin(ang)[:, :, None, :]


def apply_rope(x, cos, sin):
    half = QK_ROPE // 2
    xf = x.astype(jnp.float32)
    x1, x2 = xf[..., :half], xf[..., half:]
    return jnp.concatenate([x1 * cos - x2 * sin, x1 * sin + x2 * cos], axis=-1).astype(x.dtype)


def causal_attention(q, k, v):
    bsz, seq, heads, dqk = q.shape
    nb = seq // Q_BLOCK
    qb = q.reshape(bsz, nb, Q_BLOCK, heads, dqk).transpose(1, 0, 2, 3, 4)
    kpos = jnp.arange(seq)

    def one_block(args):
        qi, bi = args
        s = jnp.einsum('bqhd,bkhd->bhqk', qi, k).astype(jnp.float32) * ATTN_SCALE
        qpos = bi * Q_BLOCK + jnp.arange(Q_BLOCK)
        s = jnp.where(kpos[None, :] <= qpos[:, None], s, -1e30)
        pr = jax.nn.softmax(s, axis=-1).astype(v.dtype)
        return jnp.einsum('bhqk,bkhd->bqhd', pr, v)

    ob = lax.map(one_block, (qb, jnp.arange(nb)))
    return ob.transpose(1, 0, 2, 3, 4).reshape(bsz, seq, heads * v.shape[-1])


def mla_group(q_lat, kv_lat, k_rope_in, cos, sin, g_q, w_q_up, g_kv, w_kv_up):
    bsz, seq, _ = q_lat.shape
    q = (rmsnorm(q_lat, g_q) @ w_q_up).reshape(bsz, seq, N_HEADS, QK_NOPE + QK_ROPE)
    q = jnp.concatenate([q[..., :QK_NOPE], apply_rope(q[..., QK_NOPE:], cos, sin)], axis=-1)
    kv = (rmsnorm(kv_lat, g_kv) @ w_kv_up).reshape(bsz, seq, N_HEADS, QK_NOPE + V_HEAD)
    k_pe = apply_rope(k_rope_in[:, :, None, :], cos, sin)
    k = jnp.concatenate([kv[..., :QK_NOPE],
                         jnp.broadcast_to(k_pe, (bsz, seq, N_HEADS, QK_ROPE))], axis=-1)
    v = kv[..., QK_NOPE:]
    return causal_attention(q, k, v)


def _lin_rec_combine(e1, e2):
    a1, b1 = e1
    a2, b2 = e2
    return a2 * a1, a2 * b1 + b2


def s5_group(u, a_re, a_im, b_re, b_im, c_re, c_im, d, log_step, w_glu, b_glu):
    f32 = jnp.float32
    bsz, seq, _ = u.shape
    u32 = u.astype(f32).reshape(bsz, seq, SSM_GROUPS, SSM_CH)
    lam = lax.complex(a_re.astype(f32), a_im.astype(f32))
    delta = jnp.exp(log_step.astype(f32))[:, None]
    lam_bar = jnp.exp(lam * delta)
    b_fac = (lam_bar - 1.0) / lam
    bu = lax.complex(jnp.einsum('blgh,gph->blgp', u32, b_re.astype(f32)),
                     jnp.einsum('blgh,gph->blgp', u32, b_im.astype(f32))) * b_fac
    a = jnp.broadcast_to(lam_bar, (1, seq) + lam_bar.shape)
    _, states = lax.associative_scan(_lin_rec_combine, (a, bu), axis=1)
    y = (jnp.einsum('blgp,ghp->blgh', jnp.real(states), c_re.astype(f32))
         - jnp.einsum('blgp,ghp->blgh', jnp.imag(states), c_im.astype(f32))
         + d.astype(f32) * u32)
    y = jax.nn.gelu(y.reshape(bsz, seq, SSM_WIDTH))
    y = y * jax.nn.sigmoid(y @ w_glu.astype(f32) + b_glu.astype(f32))
    return y.astype(u.dtype)


def hier_moe(xn, w_gr, b_gr, w_er, b_er, w_g, w_u, w_d):
    bsz, seq, dm = xn.shape
    t = bsz * seq
    xf = xn.reshape(t, dm)
    g_prob = jax.nn.softmax((xf @ w_gr).astype(jnp.float32) + b_gr.astype(jnp.float32), axis=-1)
    g_top_p, g_top = lax.top_k(g_prob, 1)
    e_logits = ((xf @ w_er).astype(jnp.float32) + b_er.astype(jnp.float32)).reshape(t, N_EXPERT_GROUPS, EXPERTS_PER_GROUP)
    e_sel = jnp.take_along_axis(e_logits, g_top[:, :, None], axis=1)[:, 0]
    e_top_l, e_top = lax.top_k(e_sel, TOP_K)
    weights = g_top_p * jax.nn.softmax(e_top_l, axis=-1)
    expert_id = g_top * EXPERTS_PER_GROUP + e_top
    flat_e = expert_id.reshape(-1)
    flat_tok = jnp.repeat(jnp.arange(t, dtype=jnp.int32), TOP_K, total_repeat_length=t * TOP_K)
    flat_w = weights.reshape(-1)
    order = jnp.argsort(flat_e)
    se, stok, sw = flat_e[order], flat_tok[order], flat_w[order]
    counts = jnp.bincount(flat_e, length=N_EXPERTS)
    starts = jnp.cumsum(counts) - counts
    pcounts = (counts + DISPATCH_BLOCK - 1) // DISPATCH_BLOCK * DISPATCH_BLOCK
    pends = jnp.cumsum(pcounts)
    pstarts = pends - pcounts
    dest = pstarts[se] + (jnp.arange(t * TOP_K) - starts[se])
    n_slots = t * TOP_K + N_EXPERTS * DISPATCH_BLOCK
    n_blocks = n_slots // DISPATCH_BLOCK
    xbuf = jnp.zeros((n_slots, dm), xn.dtype).at[dest].set(xf[stok])
    block_e = jnp.minimum(jnp.searchsorted(pends, jnp.arange(n_blocks) * DISPATCH_BLOCK, side='right'),
                          N_EXPERTS - 1)

    def expert_block(args):
        xb, e = args
        hdn = jax.nn.silu(xb @ w_g[e]) * (xb @ w_u[e])
        return hdn @ w_d[e]

    ybuf = lax.map(expert_block, (xbuf.reshape(n_blocks, DISPATCH_BLOCK, dm), block_e)).reshape(n_slots, dm)
    y = jax.ops.segment_sum(ybuf[dest] * sw[:, None].astype(xn.dtype), stok, num_segments=t)
    return y.reshape(bsz, seq, dm)


def setup_inputs(seed: int = 0) -> dict:
    key = jax.random.key(seed)
    ks = jax.random.split(key, 40)
    nrm = lambda k, shape, s: jax.random.normal(k, shape, jnp.float32) * s
    gain = lambda k, n: 1.0 + 0.01 * jax.random.normal(k, (DEPTH, n), jnp.float32)
    L = DEPTH
    a_im = jnp.pi * jnp.arange(SSM_STATE, dtype=jnp.float32)
    return {
        "x": nrm(ks[0], (BATCH, SEQ, D_MODEL), 1.0),
        "p": nrm(ks[1], (DEPTH, BATCH, SEQ, PLE_DIM), 1.0),
        "positions": jnp.broadcast_to(jnp.arange(SEQ, dtype=jnp.int32), (BATCH, SEQ)),
        "g_mix_norm": gain(ks[2], D_MODEL),
        "w_in": nrm(ks[3], (L, D_MODEL, IN_WIDTH), D_MODEL ** -0.5),
        "g_q_lat": gain(ks[4], Q_LORA),
        "w_q_up": nrm(ks[5], (L, Q_LORA, N_HEADS * (QK_NOPE + QK_ROPE)), Q_LORA ** -0.5),
        "g_kv_lat": gain(ks[6], KV_LORA),
        "w_kv_up": nrm(ks[7], (L, KV_LORA, N_HEADS * (QK_NOPE + V_HEAD)), KV_LORA ** -0.5),
        "ssm_a_re": -0.5 + 0.01 * jax.random.normal(ks[8], (L, SSM_GROUPS, SSM_STATE), jnp.float32),
        "ssm_a_im": a_im + 0.01 * jax.random.normal(ks[9], (L, SSM_GROUPS, SSM_STATE), jnp.float32),
        "ssm_b_re": nrm(ks[10], (L, SSM_GROUPS, SSM_STATE, SSM_CH), (2 * SSM_CH) ** -0.5),
        "ssm_b_im": nrm(ks[11], (L, SSM_GROUPS, SSM_STATE, SSM_CH), (2 * SSM_CH) ** -0.5),
        "ssm_c_re": nrm(ks[12], (L, SSM_GROUPS, SSM_CH, SSM_STATE), SSM_STATE ** -0.5),
        "ssm_c_im": nrm(ks[13], (L, SSM_GROUPS, SSM_CH, SSM_STATE), SSM_STATE ** -0.5),
        "ssm_d": nrm(ks[14], (L, SSM_GROUPS, SSM_CH), 1.0),
        "ssm_log_step": jax.random.uniform(ks[15], (L, SSM_GROUPS), jnp.float32,
                                           math.log(STEP_MIN), math.log(STEP_MAX)),
        "w_glu": nrm(ks[16], (L, SSM_WIDTH, SSM_WIDTH), SSM_WIDTH ** -0.5),
        "b_glu": nrm(ks[17], (L, SSM_WIDTH), 0.01),
        "g_attn_out": gain(ks[18], MLA_WIDTH),
        "g_ssm_out": gain(ks[19], SSM_WIDTH),
        "w_out": nrm(ks[20], (L, MIX_WIDTH, D_MODEL), MIX_WIDTH ** -0.5),
        "g_moe_norm": gain(ks[21], D_MODEL),
        "w_group_router": nrm(ks[22], (L, D_MODEL, N_EXPERT_GROUPS), D_MODEL ** -0.5),
        "b_group_router": nrm(ks[23], (L, N_EXPERT_GROUPS), 0.01),
        "w_expert_router": nrm(ks[24], (L, D_MODEL, N_EXPERTS), D_MODEL ** -0.5),
        "b_expert_router": nrm(ks[25], (L, N_EXPERTS), 0.01),
        "w_exp_gate": nrm(ks[26], (L, N_EXPERTS, D_MODEL, D_EXPERT), D_MODEL ** -0.5),
        "w_exp_up": nrm(ks[27], (L, N_EXPERTS, D_MODEL, D_EXPERT), D_MODEL ** -0.5),
        "w_exp_down": nrm(ks[28], (L, N_EXPERTS, D_EXPERT, D_MODEL), D_EXPERT ** -0.5),
        "g_ple_norm": gain(ks[29], D_MODEL),
        "w_ple_gate": nrm(ks[30], (L, D_MODEL, D_MODEL), D_MODEL ** -0.5),
        "w_ple_proj": nrm(ks[31], (L, PLE_DIM, D_MODEL), PLE_DIM ** -0.5),
        "g_final": 1.0 + 0.01 * jax.random.normal(ks[32], (D_MODEL,), jnp.float32),
    }


def reference(x, p, positions, g_mix_norm, w_in, g_q_lat, w_q_up, g_kv_lat, w_kv_up,
              ssm_a_re, ssm_a_im, ssm_b_re, ssm_b_im, ssm_c_re, ssm_c_im, ssm_d, ssm_log_step,
              w_glu, b_glu, g_attn_out, g_ssm_out, w_out, g_moe_norm,
              w_group_router, b_group_router, w_expert_router, b_expert_router,
              w_exp_gate, w_exp_up, w_exp_down, g_ple_norm, w_ple_gate, w_ple_proj, g_final):
    cos, sin = rope_tables(positions)
    h = x
    for i in range(DEPTH):
        z = rmsnorm(h, g_mix_norm[i]) @ w_in[i]
        attn = mla_group(z[..., :OFF_KV], z[..., OFF_KV:OFF_KR], z[..., OFF_KR:OFF_SSM], cos, sin,
                         g_q_lat[i], w_q_up[i], g_kv_lat[i], w_kv_up[i])
        ssm = s5_group(z[..., OFF_SSM:], ssm_a_re[i], ssm_a_im[i], ssm_b_re[i], ssm_b_im[i],
                       ssm_c_re[i], ssm_c_im[i], ssm_d[i], ssm_log_step[i], w_glu[i], b_glu[i])
        mix = jnp.concatenate([rmsnorm(attn, g_attn_out[i]), rmsnorm(ssm, g_ssm_out[i])], axis=-1)
        h = h + mix @ w_out[i]
        h = h + hier_moe(rmsnorm(h, g_moe_norm[i]), w_group_router[i], b_group_router[i],
                         w_expert_router[i], b_expert_router[i],
                         w_exp_gate[i], w_exp_up[i], w_exp_down[i])
        gate = jax.nn.sigmoid(rmsnorm(h, g_ple_norm[i]) @ w_ple_gate[i])
        h = h + (p[i] @ w_ple_proj[i]) * gate
    return rmsnorm(h, g_final)
```

```python
import functools
import math

import jax
import jax.numpy as jnp
from jax import lax
from jax.experimental import pallas as pl
from jax.experimental.pallas import tpu as pltpu

F32 = jnp.float32
BF16 = jnp.bfloat16
I32 = jnp.int32

D_MODEL = 1024
MLA_WIDTH = 512
SSM_WIDTH = 512
V_HEAD = 64
N_HEADS = 8
QK_NOPE = 64
QK_ROPE = 32
HALF_ROPE = QK_ROPE // 2
Q_LORA = 256
KV_LORA = 128
ROPE_BASE = 10000.0
ATTN_SCALE = 1.0 / math.sqrt(QK_NOPE + QK_ROPE)
SSM_CH = 16
SSM_GROUPS = 32
SSM_STATE = 64
N_EXPERT_GROUPS = 4
EXPERTS_PER_GROUP = 8
N_EXPERTS = 32
TOP_K = 2
D_EXPERT = 256
PLE_DIM = 256
EPS = 1e-6

LANES = 128
HEAD_PAD = 128
ROPE_LANE0 = QK_NOPE
N_STATE = SSM_GROUPS * SSM_STATE
NEG = -1e30
VMEM_LIMIT = 56 * 1024 * 1024


def _tiles(n_tok, seq):
    return dict(
        tm=min(512, seq),
        tq=min(512, seq),
        lc=min(32, seq),
        te=512,
        td=min(512, n_tok),
        tf=min(512, n_tok),
    )


def _rms(x, g):
    ms = jnp.mean(x * x, axis=-1, keepdims=True)
    return x * lax.rsqrt(ms + EPS) * g


def _sigmoid(x):
    return 1.0 / (1.0 + jnp.exp(-x))


def _cparams(sem):
    return pltpu.CompilerParams(dimension_semantics=sem, vmem_limit_bytes=VMEM_LIMIT)


def _full(shape):
    nd = len(shape)
    return pl.BlockSpec(shape, lambda *_: (0,) * nd)


def _rope_table_kernel(pos_ref, invf_ref, cos_ref, sin_ref):
    ang = pos_ref[...].astype(F32) * invf_ref[...]
    cos_ref[...] = jnp.cos(ang)
    sin_ref[...] = jnp.sin(ang)


def _rope_tables(pos_col, tm):
    n_tok = pos_col.shape[0]
    lane = jnp.arange(LANES)
    in_rope = (lane >= ROPE_LANE0) & (lane < ROPE_LANE0 + QK_ROPE)
    freq = ROPE_BASE ** (-((lane - ROPE_LANE0) % HALF_ROPE).astype(F32) / HALF_ROPE)
    invf = jnp.where(in_rope, freq, 0.0).astype(F32)[None, :]
    return pl.pallas_call(
        _rope_table_kernel,
        out_shape=(jax.ShapeDtypeStruct((n_tok, LANES), F32),) * 2,
        grid=(n_tok // tm,),
        in_specs=[pl.BlockSpec((tm, 1), lambda i: (i, 0)), _full((1, LANES))],
        out_specs=(pl.BlockSpec((tm, LANES), lambda i: (i, 0)),) * 2,
        compiler_params=_cparams(("parallel",)),
        name="rope_tables",
    )(pos_col, invf)


def _rope(x, cosf, s_lo, s_hi):
    n = x.shape[-1]
    return (x * cosf + pltpu.roll(x, n - HALF_ROPE, 1) * s_lo + pltpu.roll(x, HALF_ROPE, 1) * s_hi)


def _inproj_kernel(h_ref, cos_ref, sin_ref, gmix_ref, win_ref, gq_ref, wq_ref, gkv_ref, wkv_ref,
                   q_ref, k_ref, v_ref, u_ref):
    xn = _rms(h_ref[...], gmix_ref[...]).astype(BF16)
    z = jnp.dot(xn, win_ref[...], preferred_element_type=F32)
    cosf = cos_ref[...]
    sinf = sin_ref[...]
    lane = lax.broadcasted_iota(I32, (1, LANES), 1)
    s_lo = jnp.where((lane >= ROPE_LANE0) & (lane < ROPE_LANE0 + HALF_ROPE), -sinf, 0.0)
    s_hi = jnp.where((lane >= ROPE_LANE0 + HALF_ROPE) & (lane < ROPE_LANE0 + QK_ROPE), sinf, 0.0)

    qn = _rms(z[:, :Q_LORA], gq_ref[...]).astype(BF16)
    q = jnp.dot(qn, wq_ref[...], preferred_element_type=F32)
    kvn = _rms(z[:, Q_LORA:Q_LORA + KV_LORA], gkv_ref[...]).astype(BF16)
    kv = jnp.dot(kvn, wkv_ref[...], preferred_element_type=F32)
    kpe = _rope(z[:, Q_LORA + KV_LORA:Q_LORA + KV_LORA + HEAD_PAD], cosf, s_lo, s_hi)
    for hd in range(N_HEADS):
        sl = slice(hd * HEAD_PAD, (hd + 1) * HEAD_PAD)
        q_ref[:, sl] = _rope(q[:, sl], cosf, s_lo, s_hi).astype(BF16)
        k_ref[:, sl] = (kv[:, sl] + kpe).astype(BF16)
    v_ref[...] = kv[:, N_HEADS * HEAD_PAD:].astype(BF16)
    u_ref[...] = z[:, Q_LORA + KV_LORA + HEAD_PAD:]


def _inproj(h, cosf, sinf, wts, bsz, seq, tm):
    n_tok = bsz * seq
    nl = seq // tm
    row = lambda b, l: (b * nl + l, 0)
    qk_w = N_HEADS * HEAD_PAD
    return pl.pallas_call(
        _inproj_kernel,
        out_shape=(jax.ShapeDtypeStruct((n_tok, qk_w), BF16),
                   jax.ShapeDtypeStruct((n_tok, qk_w), BF16),
                   jax.ShapeDtypeStruct((n_tok, MLA_WIDTH), BF16),
                   jax.ShapeDtypeStruct((seq, bsz * SSM_WIDTH), F32)),
        grid=(bsz, nl),
        in_specs=[pl.BlockSpec((tm, D_MODEL), row),
                  pl.BlockSpec((tm, LANES), row), pl.BlockSpec((tm, LANES), row),
                  _full((1, D_MODEL)), _full((D_MODEL, D_MODEL)),
                  _full((1, Q_LORA)), _full((Q_LORA, qk_w)),
                  _full((1, KV_LORA)), _full((KV_LORA, qk_w + MLA_WIDTH))],
        out_specs=(pl.BlockSpec((tm, qk_w), row), pl.BlockSpec((tm, qk_w), row),
                   pl.BlockSpec((tm, MLA_WIDTH), row),
                   pl.BlockSpec((tm, SSM_WIDTH), lambda b, l: (l, b))),
        compiler_params=_cparams(("parallel", "parallel")),
        name="inproj",
    )(h, cosf, sinf, wts["g_mix"], wts["w_in"], wts["g_q"], wts["w_q"], wts["g_kv"], wts["w_kv"])


def _attn_kernel(q_ref, k_ref, v_ref, g_ref, o_ref, acc_ref, *, tq):
    qi = pl.program_id(1)
    row = lax.broadcasted_iota(I32, (tq, tq), 0)
    col = lax.broadcasted_iota(I32, (tq, tq), 1)
    causal = col <= row
    for hd in range(N_HEADS):
        qh = q_ref[:, hd * HEAD_PAD:(hd + 1) * HEAD_PAD]

        def step(j, carry, masked, hd=hd, qh=qh):
            m, l, acc = carry
            rows = pl.ds(pl.multiple_of(j * tq, tq), tq)
            kt = k_ref[rows, hd * HEAD_PAD:(hd + 1) * HEAD_PAD]
            vt = v_ref[rows, hd * V_HEAD:(hd + 1) * V_HEAD]
            s = lax.dot_general(qh, kt, (((1,), (1,)), ((), ())), preferred_element_type=F32)
            if masked:
                s = jnp.where(causal, s, NEG)
            m_new = jnp.maximum(m, jnp.max(s, axis=-1, keepdims=True))
            p = jnp.exp(s - m_new)
            alpha = jnp.exp(m - m_new)
            l = alpha * l + jnp.sum(p, axis=-1, keepdims=True)
            acc = alpha * acc + jnp.dot(p.astype(BF16), vt, preferred_element_type=F32)
            return m_new, l, acc

        init = (jnp.full((tq, 1), NEG, F32), jnp.zeros((tq, 1), F32), jnp.zeros((tq, V_HEAD), F32))
        carry = lax.fori_loop(0, qi, functools.partial(step, masked=False), init)
        _, l, acc = step(qi, carry, masked=True)
        acc_ref[:, hd * V_HEAD:(hd + 1) * V_HEAD] = acc / l
    o_ref[...] = _rms(acc_ref[...], g_ref[...]).astype(BF16)


def _attention(q, k, v, g_attn, bsz, seq, tq):
    n_tok = bsz * seq
    nq = seq // tq
    qk_w = N_HEADS * HEAD_PAD
    return pl.pallas_call(
        functools.partial(_attn_kernel, tq=tq),
        out_shape=jax.ShapeDtypeStruct((n_tok, MLA_WIDTH), BF16),
        grid=(bsz, nq),
        in_specs=[pl.BlockSpec((tq, qk_w), lambda b, i: (b * nq + i, 0)),
                  pl.BlockSpec((seq, qk_w), lambda b, i: (b, 0)),
                  pl.BlockSpec((seq, MLA_WIDTH), lambda b, i: (b, 0)),
                  _full((1, MLA_WIDTH))],
        out_specs=pl.BlockSpec((tq, MLA_WIDTH), lambda b, i: (b * nq + i, 0)),
        scratch_shapes=[pltpu.VMEM((tq, MLA_WIDTH), F32)],
        compiler_params=_cparams(("parallel", "parallel")),
        name="attn",
    )(q, k, v, g_attn)


S5_SLABS = 4
S5_SLAB_U = SSM_WIDTH // S5_SLABS
S5_SLAB_X = N_STATE // S5_SLABS


def _s5_kernel(u_ref, wb_ref, are_ref, aim_ref, wc_ref, d_ref, wglu_ref, bglu_ref, g_ref,
               o_ref, xr_ref, xi_ref, sr_ref, si_ref, *, lc, nb):
    @pl.when(pl.program_id(0) == 0)
    def _():
        xr_ref[...] = jnp.zeros_like(xr_ref)
        xi_ref[...] = jnp.zeros_like(xi_ref)

    rows = lc * nb
    u = u_ref[...].reshape(rows, SSM_WIDTH)
    ub = u.astype(BF16)
    for j in range(S5_SLABS):
        bu = jnp.dot(ub[:, j * S5_SLAB_U:(j + 1) * S5_SLAB_U], wb_ref[j], preferred_element_type=F32)
        sr_ref[:, j * S5_SLAB_X:(j + 1) * S5_SLAB_X] = bu[:, :S5_SLAB_X]
        si_ref[:, j * S5_SLAB_X:(j + 1) * S5_SLAB_X] = bu[:, S5_SLAB_X:]
    for c in range(S5_SLABS):
        sl = slice(c * S5_SLAB_X, (c + 1) * S5_SLAB_X)
        ar = jnp.broadcast_to(are_ref[:, sl], (nb, S5_SLAB_X))
        ai = jnp.broadcast_to(aim_ref[:, sl], (nb, S5_SLAB_X))

        def scan_step(t, carry, sl=sl, ar=ar, ai=ai):
            xr, xi = carry
            r = pl.ds(pl.multiple_of(t * nb, nb), nb)
            nr = ar * xr - ai * xi + sr_ref[r, sl]
            ni = ar * xi + ai * xr + si_ref[r, sl]
            sr_ref[r, sl] = nr
            si_ref[r, sl] = ni
            return nr, ni

        xr, xi = lax.fori_loop(0, lc, scan_step, (xr_ref[:, sl], xi_ref[:, sl]), unroll=4)
        xr_ref[:, sl] = xr
        xi_ref[:, sl] = xi
    ys = []
    for j in range(S5_SLABS):
        sl = slice(j * S5_SLAB_X, (j + 1) * S5_SLAB_X)
        ys.append(jnp.dot(sr_ref[:, sl].astype(BF16), wc_ref[0, j], preferred_element_type=F32)
                  + jnp.dot(si_ref[:, sl].astype(BF16), wc_ref[1, j], preferred_element_type=F32))
    y = jnp.concatenate(ys, axis=-1) + d_ref[...] * u
    y = 0.5 * y * (1.0 + jnp.tanh(math.sqrt(2.0 / math.pi) * (y + 0.044715 * (y * y * y))))
    gate = jnp.dot(y.astype(BF16), wglu_ref[...], preferred_element_type=F32) + bglu_ref[...]
    y = y * _sigmoid(gate)
    o_ref[...] = _rms(y, g_ref[...]).astype(BF16).reshape(lc, nb, SSM_WIDTH)


def _s5(u_tm, wts, bsz, seq, lc):
    rows = lc * bsz
    return pl.pallas_call(
        functools.partial(_s5_kernel, lc=lc, nb=bsz),
        out_shape=jax.ShapeDtypeStruct((seq, bsz, SSM_WIDTH), BF16),
        grid=(seq // lc,),
        in_specs=[pl.BlockSpec((lc, bsz, SSM_WIDTH), lambda i: (i, 0, 0)),
                  _full((S5_SLABS, S5_SLAB_U, 2 * S5_SLAB_X)),
                  _full((1, N_STATE)), _full((1, N_STATE)),
                  _full((2, S5_SLABS, S5_SLAB_X, S5_SLAB_U)),
                  _full((1, SSM_WIDTH)), _full((SSM_WIDTH, SSM_WIDTH)), _full((1, SSM_WIDTH)),
                  _full((1, SSM_WIDTH))],
        out_specs=pl.BlockSpec((lc, bsz, SSM_WIDTH), lambda i: (i, 0, 0)),
        scratch_shapes=[pltpu.VMEM((bsz, N_STATE), F32), pltpu.VMEM((bsz, N_STATE), F32),
                        pltpu.VMEM((rows, N_STATE), F32), pltpu.VMEM((rows, N_STATE), F32)],
        compiler_params=_cparams(("arbitrary",)),
        name="s5",
    )(u_tm, wts["s5_wb"], wts["s5_are"], wts["s5_aim"], wts["s5_wc"], wts["s5_d"],
      wts["w_glu"], wts["b_glu"], wts["g_ssm"])


ROUTE_E0 = N_EXPERT_GROUPS


def _outproj_kernel(h_ref, a_ref, s_ref, wout_ref, gmoe_ref, wrh_ref, wrl_ref, br_ref, tril_ref,
                    h1_ref, xn_ref, route_ref, rw_ref, cnt_ref, *, tm):
    @pl.when((pl.program_id(0) == 0) & (pl.program_id(1) == 0))
    def _():
        cnt_ref[...] = jnp.zeros_like(cnt_ref)

    h1 = (h_ref[...]
          + jnp.dot(a_ref[...], wout_ref[:MLA_WIDTH, :], preferred_element_type=F32)
          + jnp.dot(s_ref[...], wout_ref[MLA_WIDTH:, :], preferred_element_type=F32))
    h1_ref[...] = h1
    xn = _rms(h1, gmoe_ref[...])
    xn_ref[...] = xn
    x_hi = xn.astype(BF16)
    x_lo = (xn - x_hi.astype(F32)).astype(BF16)
    lg = (jnp.dot(x_hi, wrh_ref[...], preferred_element_type=F32)
          + (jnp.dot(x_lo, wrh_ref[...], preferred_element_type=F32)
             + jnp.dot(x_hi, wrl_ref[...], preferred_element_type=F32))) + br_ref[...]
    lane = lax.broadcasted_iota(I32, (tm, LANES), 1).astype(F32)
    big = float(LANES)
    gl = jnp.where(lane < N_EXPERT_GROUPS, lg, NEG)
    gmax = jnp.max(gl, axis=-1, keepdims=True)
    gsum = jnp.sum(jnp.exp(gl - gmax), axis=-1, keepdims=True)
    g_top_p = 1.0 / gsum
    gidx = jnp.min(jnp.where(gl == gmax, lane, big), axis=-1, keepdims=True)
    lo = ROUTE_E0 + EXPERTS_PER_GROUP * gidx
    sel = jnp.where((lane >= lo) & (lane < lo + EXPERTS_PER_GROUP), lg, NEG)
    m1 = jnp.max(sel, axis=-1, keepdims=True)
    i1 = jnp.min(jnp.where(sel == m1, lane, big), axis=-1, keepdims=True)
    sel2 = jnp.where(lane == i1, NEG, sel)
    m2 = jnp.max(sel2, axis=-1, keepdims=True)
    i2 = jnp.min(jnp.where(sel2 == m2, lane, big), axis=-1, keepdims=True)
    e21 = jnp.exp(m2 - m1)
    w1 = g_top_p / (1.0 + e21)
    w2 = g_top_p * e21 / (1.0 + e21)
    e1 = i1 - ROUTE_E0
    e2 = i2 - ROUTE_E0
    oh = jnp.where((lane == e1) | (lane == e2), 1.0, 0.0)
    before = cnt_ref[...] + jnp.dot(tril_ref[...], oh.astype(BF16), preferred_element_type=F32)
    r1 = jnp.sum(jnp.where(lane == e1, before, 0.0), axis=-1, keepdims=True)
    r2 = jnp.sum(jnp.where(lane == e2, before, 0.0), axis=-1, keepdims=True)
    cnt_ref[...] = cnt_ref[...] + jnp.sum(oh, axis=0, keepdims=True)
    route = jnp.where(lane == 0, e1, jnp.where(lane == 1, e2, jnp.where(lane == 2, r1, r2)))
    route_ref[...] = route.astype(I32)
    rw_ref[...] = jnp.where(lane == 0, w1, w2)


def _outproj(h, attn_n, ssm_tm, wts, tril, bsz, seq, tm):
    n_tok = bsz * seq
    nl = seq // tm
    row = lambda b, l: (b * nl + l, 0)
    return pl.pallas_call(
        functools.partial(_outproj_kernel, tm=tm),
        out_shape=(jax.ShapeDtypeStruct((n_tok, D_MODEL), F32),
                   jax.ShapeDtypeStruct((n_tok, D_MODEL), F32),
                   jax.ShapeDtypeStruct((n_tok, LANES), I32),
                   jax.ShapeDtypeStruct((n_tok, LANES), F32),
                   jax.ShapeDtypeStruct((1, LANES), F32)),
        grid=(bsz, nl),
        in_specs=[pl.BlockSpec((tm, D_MODEL), row),
                  pl.BlockSpec((tm, MLA_WIDTH), row),
                  pl.BlockSpec((tm, SSM_WIDTH), lambda b, l: (l, b)),
                  _full((D_MODEL, D_MODEL)), _full((1, D_MODEL)),
                  _full((D_MODEL, LANES)), _full((D_MODEL, LANES)), _full((1, LANES)),
                  _full((tm, tm))],
        out_specs=(pl.BlockSpec((tm, D_MODEL), row), pl.BlockSpec((tm, D_MODEL), row),
                   pl.BlockSpec((tm, LANES), row), pl.BlockSpec((tm, LANES), row),
                   _full((1, LANES))),
        compiler_params=_cparams(("arbitrary", "arbitrary")),
        name="outproj",
    )(h, attn_n, ssm_tm, wts["w_out"], wts["g_moe"], wts["w_r_hi"], wts["w_r_lo"], wts["b_r"], tril)


def _row_copy(src, src_row, dst, dst_row, sem):
    return pltpu.make_async_copy(src.at[pl.ds(src_row, 1)], dst.at[pl.ds(dst_row, 1)], sem)


def _dispatch_kernel(dest_ref, xn_ref, xbuf_in_ref, xbuf_ref, sems, *, td):
    del xbuf_in_ref
    base = pl.program_id(0) * td

    def issue(r, _):
        for k in range(TOP_K):
            _row_copy(xn_ref, base + r, xbuf_ref, dest_ref[TOP_K * r + k], sems.at[k]).start()
        return 0

    lax.fori_loop(0, td, issue, 0, unroll=8)
    for k in range(TOP_K):
        pltpu.make_async_copy(xn_ref.at[pl.ds(0, td)], xbuf_ref.at[pl.ds(0, td)], sems.at[k]).wait()


def _dispatch(dest_flat, xn, n_slots, td):
    n_tok = xn.shape[0]
    xbuf0 = jnp.zeros((n_slots, D_MODEL), F32)
    return pl.pallas_call(
        functools.partial(_dispatch_kernel, td=td),
        out_shape=jax.ShapeDtypeStruct((n_slots, D_MODEL), F32),
        grid=(n_tok // td,),
        in_specs=[pl.BlockSpec((TOP_K * td,), lambda i: (i,), memory_space=pltpu.SMEM),
                  pl.BlockSpec(memory_space=pl.ANY),
                  pl.BlockSpec(memory_space=pl.ANY)],
        out_specs=pl.BlockSpec(memory_space=pl.ANY),
        scratch_shapes=[pltpu.SemaphoreType.DMA((TOP_K,))],
        input_output_aliases={2: 0},
        compiler_params=_cparams(("arbitrary",)),
        name="dispatch",
    )(dest_flat, xn, xbuf0)


def _experts_kernel(be_ref, x_ref, wg_ref, wu_ref, wd_ref, y_ref):
    del be_ref
    xb = x_ref[...].astype(BF16)
    g = jnp.dot(xb, wg_ref[...], preferred_element_type=F32)
    u = jnp.dot(xb, wu_ref[...], preferred_element_type=F32)
    hdn = (g * _sigmoid(g) * u).astype(BF16)
    y_ref[...] = jnp.dot(hdn, wd_ref[...], preferred_element_type=F32)


def _experts(block_e, xbuf, w_gate, w_up, w_down, te):
    n_slots = xbuf.shape[0]
    grid_spec = pltpu.PrefetchScalarGridSpec(
        num_scalar_prefetch=1,
        grid=(n_slots // te,),
        in_specs=[pl.BlockSpec((te, D_MODEL), lambda i, be: (i, 0)),
                  pl.BlockSpec((None, D_MODEL, D_EXPERT), lambda i, be: (be[i], 0, 0)),
                  pl.BlockSpec((None, D_MODEL, D_EXPERT), lambda i, be: (be[i], 0, 0)),
                  pl.BlockSpec((None, D_EXPERT, D_MODEL), lambda i, be: (be[i], 0, 0))],
        out_specs=pl.BlockSpec((te, D_MODEL), lambda i, be: (i, 0)),
    )
    return pl.pallas_call(
        _experts_kernel,
        out_shape=jax.ShapeDtypeStruct((n_slots, D_MODEL), F32),
        grid_spec=grid_spec,
        compiler_params=_cparams(("arbitrary",)),
        name="experts",
    )(block_e, xbuf, w_gate, w_up, w_down)


def _combine_kernel(dest_ref, ybuf_ref, h1_ref, rw_ref, p_ref, gple_ref, wgate_ref, wproj_ref, gfin_ref,
                    o_ref, yv_ref, sems, *, tf, final):
    def issue(r, _):
        for k in range(TOP_K):
            _row_copy(ybuf_ref, dest_ref[TOP_K * r + k], yv_ref.at[k], r, sems.at[k]).start()
        return 0

    lax.fori_loop(0, tf, issue, 0, unroll=8)
    for k in range(TOP_K):
        pltpu.make_async_copy(ybuf_ref.at[pl.ds(0, tf)], yv_ref.at[k], sems.at[k]).wait()
    rw = rw_ref[...]
    h2 = h1_ref[...] + rw[:, 0:1] * yv_ref[0] + rw[:, 1:2] * yv_ref[1]
    hn = _rms(h2, gple_ref[...]).astype(BF16)
    gate = _sigmoid(jnp.dot(hn, wgate_ref[...], preferred_element_type=F32))
    ple = jnp.dot(p_ref[...].astype(BF16), wproj_ref[...], preferred_element_type=F32)
    h3 = h2 + ple * gate
    if final:
        h3 = _rms(h3, gfin_ref[...])
    o_ref[...] = h3


def _combine(dest_flat, ybuf, h1, rw, p_l, wts, g_final, tf, final):
    n_tok = h1.shape[0]
    row = lambda i: (i, 0)
    return pl.pallas_call(
        functools.partial(_combine_kernel, tf=tf, final=final),
        out_shape=jax.ShapeDtypeStruct((n_tok, D_MODEL), F32),
        grid=(n_tok // tf,),
        in_specs=[pl.BlockSpec((TOP_K * tf,), lambda i: (i,), memory_space=pltpu.SMEM),
                  pl.BlockSpec(memory_space=pl.ANY),
                  pl.BlockSpec((tf, D_MODEL), row), pl.BlockSpec((tf, LANES), row),
                  pl.BlockSpec((tf, PLE_DIM), row),
                  _full((1, D_MODEL)), _full((D_MODEL, D_MODEL)), _full((PLE_DIM, D_MODEL)),
                  _full((1, D_MODEL))],
        out_specs=pl.BlockSpec((tf, D_MODEL), row),
        scratch_shapes=[pltpu.VMEM((TOP_K, tf, D_MODEL), F32), pltpu.SemaphoreType.DMA((TOP_K,))],
        compiler_params=_cparams(("arbitrary",)),
        name="combine",
    )(dest_flat, ybuf, h1, rw, p_l, wts["g_ple"], wts["w_ple_gate"], wts["w_ple_proj"], g_final)


def _layer_weights(i, g_mix_norm, w_in, g_q_lat, w_q_up, g_kv_lat, w_kv_up,
                   ssm_a_re, ssm_a_im, ssm_b_re, ssm_b_im, ssm_c_re, ssm_c_im, ssm_d, ssm_log_step,
                   w_glu, b_glu, g_attn_out, g_ssm_out, w_out, g_moe_norm,
                   w_group_router, b_group_router, w_expert_router, b_expert_router,
                   g_ple_norm, w_ple_gate, w_ple_proj):
    off_kv, off_kr, off_ssm = Q_LORA, Q_LORA + KV_LORA, Q_LORA + KV_LORA + QK_ROPE
    wi = w_in[i]
    kr_block = jnp.zeros((D_MODEL, HEAD_PAD), F32).at[:, ROPE_LANE0:ROPE_LANE0 + QK_ROPE].set(wi[:, off_kr:off_ssm])
    w_in_p = jnp.concatenate([wi[:, :off_kr], kr_block, wi[:, off_ssm:]], axis=1)
    wq = w_q_up[i].reshape(Q_LORA, N_HEADS, QK_NOPE + QK_ROPE) * ATTN_SCALE
    wq = jnp.pad(wq, ((0, 0), (0, 0), (0, HEAD_PAD - QK_NOPE - QK_ROPE))).reshape(Q_LORA, N_HEADS * HEAD_PAD)
    wkv = w_kv_up[i].reshape(KV_LORA, N_HEADS, QK_NOPE + V_HEAD)
    wk = jnp.pad(wkv[..., :QK_NOPE], ((0, 0), (0, 0), (0, HEAD_PAD - QK_NOPE))).reshape(KV_LORA, N_HEADS * HEAD_PAD)
    wv = wkv[..., QK_NOPE:].reshape(KV_LORA, MLA_WIDTH)
    lam = lax.complex(ssm_a_re[i], ssm_a_im[i])
    lam_bar = jnp.exp(lam * jnp.exp(ssm_log_step[i])[:, None])
    b_fac = (lam_bar - 1.0) / lam
    bc = lax.complex(ssm_b_re[i], ssm_b_im[i]) * b_fac[:, :, None]
    n_per = SSM_GROUPS // S5_SLABS

    def block_diag(blocks):
        s, n, a, b = blocks.shape
        eye = jnp.eye(n, dtype=blocks.dtype)
        return jnp.einsum("snab,nm->snamb", blocks, eye).reshape(s, n * a, n * b)

    b_hp_re = jnp.real(bc).transpose(0, 2, 1).reshape(S5_SLABS, n_per, SSM_CH, SSM_STATE)
    b_hp_im = jnp.imag(bc).transpose(0, 2, 1).reshape(S5_SLABS, n_per, SSM_CH, SSM_STATE)
    s5_wb = jnp.concatenate([block_diag(b_hp_re), block_diag(b_hp_im)], axis=-1)
    c_re = ssm_c_re[i].transpose(0, 2, 1).reshape(S5_SLABS, n_per, SSM_STATE, SSM_CH)
    c_im = ssm_c_im[i].transpose(0, 2, 1).reshape(S5_SLABS, n_per, SSM_STATE, SSM_CH)
    s5_wc = jnp.stack([block_diag(c_re), -block_diag(c_im)])
    w_r = jnp.zeros((D_MODEL, LANES), F32)
    w_r = w_r.at[:, :N_EXPERT_GROUPS].set(w_group_router[i])
    w_r = w_r.at[:, ROUTE_E0:ROUTE_E0 + N_EXPERTS].set(w_expert_router[i])
    w_r_hi = w_r.astype(BF16)
    w_r_lo = (w_r - w_r_hi.astype(F32)).astype(BF16)
    b_r = jnp.zeros((1, LANES), F32)
    b_r = b_r.at[0, :N_EXPERT_GROUPS].set(b_group_router[i])
    b_r = b_r.at[0, ROUTE_E0:ROUTE_E0 + N_EXPERTS].set(b_expert_router[i])
    return dict(
        g_mix=g_mix_norm[i][None], w_in=w_in_p.astype(BF16),
        g_q=g_q_lat[i][None], w_q=wq.astype(BF16),
        g_kv=g_kv_lat[i][None], w_kv=jnp.concatenate([wk, wv], axis=1).astype(BF16),
        s5_wb=s5_wb.astype(BF16), s5_wc=s5_wc.astype(BF16),
        s5_are=jnp.real(lam_bar).reshape(1, N_STATE), s5_aim=jnp.imag(lam_bar).reshape(1, N_STATE),
        s5_d=ssm_d[i].reshape(1, SSM_WIDTH), w_glu=w_glu[i].astype(BF16), b_glu=b_glu[i][None],
        g_attn=g_attn_out[i][None], g_ssm=g_ssm_out[i][None],
        w_out=w_out[i].astype(BF16), g_moe=g_moe_norm[i][None],
        w_r_hi=w_r_hi, w_r_lo=w_r_lo, b_r=b_r,
        g_ple=g_ple_norm[i][None], w_ple_gate=w_ple_gate[i].astype(BF16),
        w_ple_proj=w_ple_proj[i].astype(BF16),
    )


def _slot_layout(route, cnt, n_blocks, te):
    counts = cnt[0, :N_EXPERTS].astype(I32)
    pcounts = (counts + te - 1) // te * te
    pends = jnp.cumsum(pcounts)
    pstarts = pends - pcounts
    eid = route[:, :TOP_K]
    rank = route[:, TOP_K:2 * TOP_K]
    start = jnp.sum(jnp.where(eid[:, :, None] == jnp.arange(N_EXPERTS)[None, None, :],
                              pstarts[None, None, :], 0), axis=-1)
    dest = (start + rank).reshape(-1).astype(I32)
    block_e = jnp.minimum(jnp.searchsorted(pends, jnp.arange(n_blocks) * te, side="right"),
                          N_EXPERTS - 1).astype(I32)
    return dest, block_e


def kernel(x, p, positions, g_mix_norm, w_in, g_q_lat, w_q_up, g_kv_lat, w_kv_up, ssm_a_re, ssm_a_im, ssm_b_re, ssm_b_im, ssm_c_re, ssm_c_im, ssm_d, ssm_log_step, w_glu, b_glu, g_attn_out, g_ssm_out, w_out, g_moe_norm, w_group_router, b_group_router, w_expert_router, b_expert_router, w_exp_gate, w_exp_up, w_exp_down, g_ple_norm, w_ple_gate, w_ple_proj, g_final):
    bsz, seq, _ = x.shape
    depth = p.shape[0]
    n_tok = bsz * seq
    t = _tiles(n_tok, seq)
    n_slots = TOP_K * n_tok + N_EXPERTS * t["te"]
    n_blocks = n_slots // t["te"]

    cosf, sinf = _rope_tables(positions.reshape(n_tok, 1), t["tm"])
    tril = jnp.tril(jnp.ones((t["tm"], t["tm"]), F32), k=-1).astype(BF16)
    wg_all = w_exp_gate.astype(BF16)
    wu_all = w_exp_up.astype(BF16)
    wd_all = w_exp_down.astype(BF16)
    g_fin = g_final[None]

    h = x.reshape(n_tok, D_MODEL)
    for i in range(depth):
        wts = _layer_weights(i, g_mix_norm, w_in, g_q_lat, w_q_up, g_kv_lat, w_kv_up,
                             ssm_a_re, ssm_a_im, ssm_b_re, ssm_b_im, ssm_c_re, ssm_c_im, ssm_d,
                             ssm_log_step, w_glu, b_glu, g_attn_out, g_ssm_out, w_out, g_moe_norm,
                             w_group_router, b_group_router, w_expert_router, b_expert_router,
                             g_ple_norm, w_ple_gate, w_ple_proj)
        q, k, v, u_tm = _inproj(h, cosf, sinf, wts, bsz, seq, t["tm"])
        attn_n = _attention(q, k, v, wts["g_attn"], bsz, seq, t["tq"])
        ssm_n = _s5(u_tm.reshape(seq, bsz, SSM_WIDTH), wts, bsz, seq, t["lc"])
        h1, xn, route, rw, cnt = _outproj(h, attn_n, ssm_n.reshape(seq, bsz * SSM_WIDTH), wts, tril,
                                          bsz, seq, t["tm"])
        dest, block_e = _slot_layout(route, cnt, n_blocks, t["te"])
        xbuf = _dispatch(dest, xn, n_slots, t["td"])
        ybuf = _experts(block_e, xbuf, wg_all[i], wu_all[i], wd_all[i], t["te"])
        h = _combine(dest, ybuf, h1, rw, p[i].reshape(n_tok, PLE_DIM), wts, g_fin, t["tf"],
                     final=(i == depth - 1))
    return h.reshape(bsz, seq, D_MODEL)
```

```python
import functools
import math

import jax
import jax.numpy as jnp
from jax import lax
from jax.experimental import pallas as pl
from jax.experimental.pallas import tpu as pltpu

F32 = jnp.float32
BF16 = jnp.bfloat16
I32 = jnp.int32

D_MODEL = 1024
MLA_WIDTH = 512
SSM_WIDTH = 512
V_HEAD = 64
N_HEADS = 8
QK_NOPE = 64
QK_ROPE = 32
HALF_ROPE = QK_ROPE // 2
Q_LORA = 256
KV_LORA = 128
ROPE_BASE = 10000.0
ATTN_SCALE = 1.0 / math.sqrt(QK_NOPE + QK_ROPE)
SSM_CH = 16
SSM_GROUPS = 32
SSM_STATE = 64
N_EXPERT_GROUPS = 4
EXPERTS_PER_GROUP = 8
N_EXPERTS = 32
TOP_K = 2
D_EXPERT = 256
PLE_DIM = 256
EPS = 1e-6

LANES = 128
HEAD_PAD = 128
ROPE_LANE0 = QK_NOPE
N_STATE = SSM_GROUPS * SSM_STATE
NEG = -1e30
VMEM_LIMIT = 56 * 1024 * 1024


def _tiles(n_tok, seq):
    return dict(
        tm=min(512, seq),
        tq=min(512, seq),
        lc=min(32, seq),
        te=512,
        td=min(512, n_tok),
        tf=min(512, n_tok),
    )


def _rms(x, g):
    ms = jnp.mean(x * x, axis=-1, keepdims=True)
    return x * lax.rsqrt(ms + EPS) * g


def _sigmoid(x):
    return 1.0 / (1.0 + jnp.exp(-x))


def _cparams(sem):
    return pltpu.CompilerParams(dimension_semantics=sem, vmem_limit_bytes=VMEM_LIMIT)


def _full(shape):
    nd = len(shape)
    return pl.BlockSpec(shape, lambda *_: (0,) * nd)


def _rope_table_kernel(pos_ref, invf_ref, cos_ref, sin_ref):
    ang = pos_ref[...].astype(F32) * invf_ref[...]
    cos_ref[...] = jnp.cos(ang)
    sin_ref[...] = jnp.sin(ang)


def _rope_tables(pos_col, tm):
    n_tok = pos_col.shape[0]
    lane = jnp.arange(LANES)
    in_rope = (lane >= ROPE_LANE0) & (lane < ROPE_LANE0 + QK_ROPE)
    freq = ROPE_BASE ** (-((lane - ROPE_LANE0) % HALF_ROPE).astype(F32) / HALF_ROPE)
    invf = jnp.where(in_rope, freq, 0.0).astype(F32)[None, :]
    return pl.pallas_call(
        _rope_table_kernel,
        out_shape=(jax.ShapeDtypeStruct((n_tok, LANES), F32),) * 2,
        grid=(n_tok // tm,),
        in_specs=[pl.BlockSpec((tm, 1), lambda i: (i, 0)), _full((1, LANES))],
        out_specs=(pl.BlockSpec((tm, LANES), lambda i: (i, 0)),) * 2,
        compiler_params=_cparams(("parallel",)),
        name="rope_tables",
    )(pos_col, invf)


def _rope(x, cosf, s_lo, s_hi):
    n = x.shape[-1]
    return (x * cosf + pltpu.roll(x, n - HALF_ROPE, 1) * s_lo + pltpu.roll(x, HALF_ROPE, 1) * s_hi)


def _inproj_kernel(h_ref, cos_ref, sin_ref, gmix_ref, win_ref, gq_ref, wq_ref, gkv_ref, wkv_ref,
                   q_ref, k_ref, v_ref, u_ref):
    xn = _rms(h_ref[...], gmix_ref[...]).astype(BF16)
    z = jnp.dot(xn, win_ref[...], preferred_element_type=F32)
    cosf = cos_ref[...]
    sinf = sin_ref[...]
    lane = lax.broadcasted_iota(I32, (1, LANES), 1)
    s_lo = jnp.where((lane >= ROPE_LANE0) & (lane < ROPE_LANE0 + HALF_ROPE), -sinf, 0.0)
    s_hi = jnp.where((lane >= ROPE_LANE0 + HALF_ROPE) & (lane < ROPE_LANE0 + QK_ROPE), sinf, 0.0)

    qn = _rms(z[:, :Q_LORA], gq_ref[...]).astype(BF16)
    q = jnp.dot(qn, wq_ref[...], preferred_element_type=F32)
    kvn = _rms(z[:, Q_LORA:Q_LORA + KV_LORA], gkv_ref[...]).astype(BF16)
    kv = jnp.dot(kvn, wkv_ref[...], preferred_element_type=F32)
    kpe = _rope(z[:, Q_LORA + KV_LORA:Q_LORA + KV_LORA + HEAD_PAD], cosf, s_lo, s_hi)
    for hd in range(N_HEADS):
        sl = slice(hd * HEAD_PAD, (hd + 1) * HEAD_PAD)
        q_ref[:, sl] = _rope(q[:, sl], cosf, s_lo, s_hi).astype(BF16)
        k_ref[:, sl] = (kv[:, sl] + kpe).astype(BF16)
    v_ref[...] = kv[:, N_HEADS * HEAD_PAD:].astype(BF16)
    u_ref[...] = z[:, Q_LORA + KV_LORA + HEAD_PAD:]


def _inproj(h, cosf, sinf, wts, bsz, seq, tm):
    n_tok = bsz * seq
    nl = seq // tm
    row = lambda b, l: (b * nl + l, 0)
    qk_w = N_HEADS * HEAD_PAD
    return pl.pallas_call(
        _inproj_kernel,
        out_shape=(jax.ShapeDtypeStruct((n_tok, qk_w), BF16),
                   jax.ShapeDtypeStruct((n_tok, qk_w), BF16),
                   jax.ShapeDtypeStruct((n_tok, MLA_WIDTH), BF16),
                   jax.ShapeDtypeStruct((seq, bsz * SSM_WIDTH), F32)),
        grid=(bsz, nl),
        in_specs=[pl.BlockSpec((tm, D_MODEL), row),
                  pl.BlockSpec((tm, LANES), row), pl.BlockSpec((tm, LANES), row),
                  _full((1, D_MODEL)), _full((D_MODEL, D_MODEL)),
                  _full((1, Q_LORA)), _full((Q_LORA, qk_w)),
                  _full((1, KV_LORA)), _full((KV_LORA, qk_w + MLA_WIDTH))],
        out_specs=(pl.BlockSpec((tm, qk_w), row), pl.BlockSpec((tm, qk_w), row),
                   pl.BlockSpec((tm, MLA_WIDTH), row),
                   pl.BlockSpec((tm, SSM_WIDTH), lambda b, l: (l, b))),
        compiler_params=_cparams(("parallel", "parallel")),
        name="inproj",
    )(h, cosf, sinf, wts["g_mix"], wts["w_in"], wts["g_q"], wts["w_q"], wts["g_kv"], wts["w_kv"])


def _attn_kernel(q_ref, k_ref, v_ref, g_ref, o_ref, acc_ref, *, tq):
    qi = pl.program_id(1)
    row = lax.broadcasted_iota(I32, (tq, tq), 0)
    col = lax.broadcasted_iota(I32, (tq, tq), 1)
    causal = col <= row
    for hd in range(N_HEADS):
        qh = q_ref[:, hd * HEAD_PAD:(hd + 1) * HEAD_PAD]

        def step(j, carry, masked, hd=hd, qh=qh):
            m, l, acc = carry
            rows = pl.ds(pl.multiple_of(j * tq, tq), tq)
            kt = k_ref[rows, hd * HEAD_PAD:(hd + 1) * HEAD_PAD]
            vt = v_ref[rows, hd * V_HEAD:(hd + 1) * V_HEAD]
            s = lax.dot_general(qh, kt, (((1,), (1,)), ((), ())), preferred_element_type=F32)
            if masked:
                s = jnp.where(causal, s, NEG)
            m_new = jnp.maximum(m, jnp.max(s, axis=-1, keepdims=True))
            p = jnp.exp(s - m_new)
            alpha = jnp.exp(m - m_new)
            l = alpha * l + jnp.sum(p, axis=-1, keepdims=True)
            acc = alpha * acc + jnp.dot(p.astype(BF16), vt, preferred_element_type=F32)
            return m_new, l, acc

        init = (jnp.full((tq, 1), NEG, F32), jnp.zeros((tq, 1), F32), jnp.zeros((tq, V_HEAD), F32))
        carry = lax.fori_loop(0, qi, functools.partial(step, masked=False), init)
        _, l, acc = step(qi, carry, masked=True)
        acc_ref[:, hd * V_HEAD:(hd + 1) * V_HEAD] = acc / l
    o_ref[...] = _rms(acc_ref[...], g_ref[...]).astype(BF16)


def _attention(q, k, v, g_attn, bsz, seq, tq):
    n_tok = bsz * seq
    nq = seq // tq
    qk_w = N_HEADS * HEAD_PAD
    return pl.pallas_call(
        functools.partial(_attn_kernel, tq=tq),
        out_shape=jax.ShapeDtypeStruct((n_tok, MLA_WIDTH), BF16),
        grid=(bsz, nq),
        in_specs=[pl.BlockSpec((tq, qk_w), lambda b, i: (b * nq + i, 0)),
                  pl.BlockSpec((seq, qk_w), lambda b, i: (b, 0)),
                  pl.BlockSpec((seq, MLA_WIDTH), lambda b, i: (b, 0)),
                  _full((1, MLA_WIDTH))],
        out_specs=pl.BlockSpec((tq, MLA_WIDTH), lambda b, i: (b * nq + i, 0)),
        scratch_shapes=[pltpu.VMEM((tq, MLA_WIDTH), F32)],
        compiler_params=_cparams(("parallel", "parallel")),
        name="attn",
    )(q, k, v, g_attn)


S5_SLABS = 4
S5_SLAB_U = SSM_WIDTH // S5_SLABS
S5_SLAB_X = N_STATE // S5_SLABS


def _s5_kernel(u_ref, wb_ref, are_ref, aim_ref, wc_ref, d_ref, wglu_ref, bglu_ref, g_ref,
               o_ref, xr_ref, xi_ref, sr_ref, si_ref, *, lc, nb):
    @pl.when(pl.program_id(0) == 0)
    def _():
        xr_ref[...] = jnp.zeros_like(xr_ref)
        xi_ref[...] = jnp.zeros_like(xi_ref)

    rows = lc * nb
    u = u_ref[...].reshape(rows, SSM_WIDTH)
    ub = u.astype(BF16)
    for j in range(S5_SLABS):
        bu = jnp.dot(ub[:, j * S5_SLAB_U:(j + 1) * S5_SLAB_U], wb_ref[j], preferred_element_type=F32)
        sr_ref[:, j * S5_SLAB_X:(j + 1) * S5_SLAB_X] = bu[:, :S5_SLAB_X]
        si_ref[:, j * S5_SLAB_X:(j + 1) * S5_SLAB_X] = bu[:, S5_SLAB_X:]
    for c in range(S5_SLABS):
        sl = slice(c * S5_SLAB_X, (c + 1) * S5_SLAB_X)
        ar = jnp.broadcast_to(are_ref[:, sl], (nb, S5_SLAB_X))
        ai = jnp.broadcast_to(aim_ref[:, sl], (nb, S5_SLAB_X))

        def scan_step(t, carry, sl=sl, ar=ar, ai=ai):
            xr, xi = carry
            r = pl.ds(pl.multiple_of(t * nb, nb), nb)
            nr = ar * xr - ai * xi + sr_ref[r, sl]
            ni = ar * xi + ai * xr + si_ref[r, sl]
            sr_ref[r, sl] = nr
            si_ref[r, sl] = ni
            return nr, ni

        xr, xi = lax.fori_loop(0, lc, scan_step, (xr_ref[:, sl], xi_ref[:, sl]), unroll=4)
        xr_ref[:, sl] = xr
        xi_ref[:, sl] = xi
    ys = []
    for j in range(S5_SLABS):
        sl = slice(j * S5_SLAB_X, (j + 1) * S5_SLAB_X)
        ys.append(jnp.dot(sr_ref[:, sl].astype(BF16), wc_ref[0, j], preferred_element_type=F32)
                  + jnp.dot(si_ref[:, sl].astype(BF16), wc_ref[1, j], preferred_element_type=F32))
    y = jnp.concatenate(ys, axis=-1) + d_ref[...] * u
    y = 0.5 * y * (1.0 + jnp.tanh(math.sqrt(2.0 / math.pi) * (y + 0.044715 * (y * y * y))))
    gate = jnp.dot(y.astype(BF16), wglu_ref[...], preferred_element_type=F32) + bglu_ref[...]
    y = y * _sigmoid(gate)
    o_ref[...] = _rms(y, g_ref[...]).astype(BF16).reshape(lc, nb, SSM_WIDTH)


def _s5(u_tm, wts, bsz, seq, lc):
    rows = lc * bsz
    return pl.pallas_call(
        functools.partial(_s5_kernel, lc=lc, nb=bsz),
        out_shape=jax.ShapeDtypeStruct((seq, bsz, SSM_WIDTH), BF16),
        grid=(seq // lc,),
        in_specs=[pl.BlockSpec((lc, bsz, SSM_WIDTH), lambda i: (i, 0, 0)),
                  _full((S5_SLABS, S5_SLAB_U, 2 * S5_SLAB_X)),
                  _full((1, N_STATE)), _full((1, N_STATE)),
                  _full((2, S5_SLABS, S5_SLAB_X, S5_SLAB_U)),
                  _full((1, SSM_WIDTH)), _full((SSM_WIDTH, SSM_WIDTH)), _full((1, SSM_WIDTH)),
                  _full((1, SSM_WIDTH))],
        out_specs=pl.BlockSpec((lc, bsz, SSM_WIDTH), lambda i: (i, 0, 0)),
        scratch_shapes=[pltpu.VMEM((bsz, N_STATE), F32), pltpu.VMEM((bsz, N_STATE), F32),
                        pltpu.VMEM((rows, N_STATE), F32), pltpu.VMEM((rows, N_STATE), F32)],
        compiler_params=_cparams(("arbitrary",)),
        name="s5",
    )(u_tm, wts["s5_wb"], wts["s5_are"], wts["s5_aim"], wts["s5_wc"], wts["s5_d"],
      wts["w_glu"], wts["b_glu"], wts["g_ssm"])


ROUTE_E0 = N_EXPERT_GROUPS


def _outproj_kernel(h_ref, a_ref, s_ref, wout_ref, gmoe_ref, wrh_ref, wrl_ref, br_ref, tril_ref,
                    h1_ref, xn_ref, route_ref, rw_ref, cnt_ref, *, tm):
    @pl.when((pl.program_id(0) == 0) & (pl.program_id(1) == 0))
    def _():
        cnt_ref[...] = jnp.zeros_like(cnt_ref)

    h1 = (h_ref[...]
          + jnp.dot(a_ref[...], wout_ref[:MLA_WIDTH, :], preferred_element_type=F32)
          + jnp.dot(s_ref[...], wout_ref[MLA_WIDTH:, :], preferred_element_type=F32))
    h1_ref[...] = h1
    xn = _rms(h1, gmoe_ref[...])
    xn_ref[...] = xn
    x_hi = xn.astype(BF16)
    x_lo = (xn - x_hi.astype(F32)).astype(BF16)
    lg = (jnp.dot(x_hi, wrh_ref[...], preferred_element_type=F32)
          + (jnp.dot(x_lo, wrh_ref[...], preferred_element_type=F32)
             + jnp.dot(x_hi, wrl_ref[...], preferred_element_type=F32))) + br_ref[...]
    lane = lax.broadcasted_iota(I32, (tm, LANES), 1).astype(F32)
    big = float(LANES)
    gl = jnp.where(lane < N_EXPERT_GROUPS, lg, NEG)
    gmax = jnp.max(gl, axis=-1, keepdims=True)
    gsum = jnp.sum(jnp.exp(gl - gmax), axis=-1, keepdims=True)
    g_top_p = 1.0 / gsum
    gidx = jnp.min(jnp.where(gl == gmax, lane, big), axis=-1, keepdims=True)
    lo = ROUTE_E0 + EXPERTS_PER_GROUP * gidx
    sel = jnp.where((lane >= lo) & (lane < lo + EXPERTS_PER_GROUP), lg, NEG)
    m1 = jnp.max(sel, axis=-1, keepdims=True)
    i1 = jnp.min(jnp.where(sel == m1, lane, big), axis=-1, keepdims=True)
    sel2 = jnp.where(lane == i1, NEG, sel)
    m2 = jnp.max(sel2, axis=-1, keepdims=True)
    i2 = jnp.min(jnp.where(sel2 == m2, lane, big), axis=-1, keepdims=True)
    e21 = jnp.exp(m2 - m1)
    w1 = g_top_p / (1.0 + e21)
    w2 = g_top_p * e21 / (1.0 + e21)
    e1 = i1 - ROUTE_E0
    e2 = i2 - ROUTE_E0
    oh = jnp.where((lane == e1) | (lane == e2), 1.0, 0.0)
    before = cnt_ref[...] + jnp.dot(tril_ref[...], oh.astype(BF16), preferred_element_type=F32)
    r1 = jnp.sum(jnp.where(lane == e1, before, 0.0), axis=-1, keepdims=True)
    r2 = jnp.sum(jnp.where(lane == e2, before, 0.0), axis=-1, keepdims=True)
    cnt_ref[...] = cnt_ref[...] + jnp.sum(oh, axis=0, keepdims=True)
    route = jnp.where(lane == 0, e1, jnp.where(lane == 1, e2, jnp.where(lane == 2, r1, r2)))
    route_ref[...] = route.astype(I32)
    rw_ref[...] = jnp.where(lane == 0, w1, w2)


def _outproj(h, attn_n, ssm_tm, wts, tril, bsz, seq, tm):
    n_tok = bsz * seq
    nl = seq // tm
    row = lambda b, l: (b * nl + l, 0)
    return pl.pallas_call(
        functools.partial(_outproj_kernel, tm=tm),
        out_shape=(jax.ShapeDtypeStruct((n_tok, D_MODEL), F32),
                   jax.ShapeDtypeStruct((n_tok, D_MODEL), F32),
                   jax.ShapeDtypeStruct((n_tok, LANES), I32),
                   jax.ShapeDtypeStruct((n_tok, LANES), F32),
                   jax.ShapeDtypeStruct((1, LANES), F32)),
        grid=(bsz, nl),
        in_specs=[pl.BlockSpec((tm, D_MODEL), row),
                  pl.BlockSpec((tm, MLA_WIDTH), row),
                  pl.BlockSpec((tm, SSM_WIDTH), lambda b, l: (l, b)),
                  _full((D_MODEL, D_MODEL)), _full((1, D_MODEL)),
                  _full((D_MODEL, LANES)), _full((D_MODEL, LANES)), _full((1, LANES)),
                  _full((tm, tm))],
        out_specs=(pl.BlockSpec((tm, D_MODEL), row), pl.BlockSpec((tm, D_MODEL), row),
                   pl.BlockSpec((tm, LANES), row), pl.BlockSpec((tm, LANES), row),
                   _full((1, LANES))),
        compiler_params=_cparams(("arbitrary", "arbitrary")),
        name="outproj",
    )(h, attn_n, ssm_tm, wts["w_out"], wts["g_moe"], wts["w_r_hi"], wts["w_r_lo"], wts["b_r"], tril)


def _row_copy(src, src_row, dst, dst_row, sem):
    return pltpu.make_async_copy(src.at[pl.ds(src_row, 1)], dst.at[pl.ds(dst_row, 1)], sem)


def _dispatch_kernel(zflag_ref, dest_ref, xn_ref, xbuf_ref, zero_ref, sems, zsem, *, td, te, n_blocks):
    def zero_copy(b):
        return pltpu.make_async_copy(zero_ref, xbuf_ref.at[pl.ds(pl.multiple_of(b * te, te), te)], zsem)

    @pl.when(pl.program_id(0) == 0)
    def _():
        zero_ref[...] = jnp.zeros_like(zero_ref)

        def start(b, _):
            @pl.when(zflag_ref[b] != 0)
            def _():
                zero_copy(b).start()
            return 0

        def wait(b, _):
            @pl.when(zflag_ref[b] != 0)
            def _():
                zero_copy(b).wait()
            return 0

        lax.fori_loop(0, n_blocks, start, 0)
        lax.fori_loop(0, n_blocks, wait, 0)

    def issue(r, _):
        for k in range(TOP_K):
            _row_copy(xn_ref, r, xbuf_ref, dest_ref[TOP_K * r + k], sems.at[k]).start(priority=k)
        return 0

    lax.fori_loop(0, td, issue, 0, unroll=8)
    for k in range(TOP_K):
        pltpu.make_async_copy(xn_ref, xbuf_ref.at[pl.ds(0, td)], sems.at[k]).wait()


def _dispatch(zflag, dest_flat, xn, n_slots, td, te):
    n_tok = xn.shape[0]
    n_blocks = n_slots // te
    grid_spec = pltpu.PrefetchScalarGridSpec(
        num_scalar_prefetch=1,
        grid=(n_tok // td,),
        in_specs=[pl.BlockSpec((TOP_K * td,), lambda i, zf: (i,), memory_space=pltpu.SMEM),
                  pl.BlockSpec((td, D_MODEL), lambda i, zf: (i, 0))],
        out_specs=pl.BlockSpec(memory_space=pl.ANY),
        scratch_shapes=[pltpu.VMEM((te, D_MODEL), F32), pltpu.SemaphoreType.DMA((TOP_K,)),
                        pltpu.SemaphoreType.DMA],
    )
    return pl.pallas_call(
        functools.partial(_dispatch_kernel, td=td, te=te, n_blocks=n_blocks),
        out_shape=jax.ShapeDtypeStruct((n_slots, D_MODEL), F32),
        grid_spec=grid_spec,
        compiler_params=_cparams(("arbitrary",)),
        name="dispatch",
    )(zflag, dest_flat, xn)


def _experts_kernel(be_ref, x_ref, wg_ref, wu_ref, wd_ref, y_ref):
    del be_ref
    xb = x_ref[...].astype(BF16)
    g = jnp.dot(xb, wg_ref[...].astype(BF16), preferred_element_type=F32)
    u = jnp.dot(xb, wu_ref[...].astype(BF16), preferred_element_type=F32)
    hdn = (g * _sigmoid(g) * u).astype(BF16)
    y_ref[...] = jnp.dot(hdn, wd_ref[...].astype(BF16), preferred_element_type=F32)


def _experts(block_e, xbuf, w_gate, w_up, w_down, layer, te):
    n_slots = xbuf.shape[0]
    grid_spec = pltpu.PrefetchScalarGridSpec(
        num_scalar_prefetch=1,
        grid=(n_slots // te,),
        in_specs=[pl.BlockSpec((te, D_MODEL), lambda i, be: (i, 0)),
                  pl.BlockSpec((None, None, D_MODEL, D_EXPERT), lambda i, be: (layer, be[i], 0, 0)),
                  pl.BlockSpec((None, None, D_MODEL, D_EXPERT), lambda i, be: (layer, be[i], 0, 0)),
                  pl.BlockSpec((None, None, D_EXPERT, D_MODEL), lambda i, be: (layer, be[i], 0, 0))],
        out_specs=pl.BlockSpec((te, D_MODEL), lambda i, be: (i, 0)),
    )
    return pl.pallas_call(
        _experts_kernel,
        out_shape=jax.ShapeDtypeStruct((n_slots, D_MODEL), F32),
        grid_spec=grid_spec,
        compiler_params=_cparams(("arbitrary",)),
        name="experts",
    )(block_e, xbuf, w_gate, w_up, w_down)


def _combine_kernel(dest_ref, ybuf_ref, h1_ref, rw_ref, p_ref, gple_ref, wgate_ref, wproj_ref, gfin_ref,
                    o_ref, yv_ref, sems, *, tf, final):
    def issue(r, _):
        for k in range(TOP_K):
            _row_copy(ybuf_ref, dest_ref[TOP_K * r + k], yv_ref.at[k], r, sems.at[k]).start(priority=k)
        return 0

    lax.fori_loop(0, tf, issue, 0, unroll=8)
    for k in range(TOP_K):
        pltpu.make_async_copy(ybuf_ref.at[pl.ds(0, tf)], yv_ref.at[k], sems.at[k]).wait()
    rw = rw_ref[...]
    h2 = h1_ref[...] + rw[:, 0:1] * yv_ref[0] + rw[:, 1:2] * yv_ref[1]
    hn = _rms(h2, gple_ref[...]).astype(BF16)
    gate = _sigmoid(jnp.dot(hn, wgate_ref[...], preferred_element_type=F32))
    ple = jnp.dot(p_ref[...].astype(BF16), wproj_ref[...], preferred_element_type=F32)
    h3 = h2 + ple * gate
    if final:
        h3 = _rms(h3, gfin_ref[...])
    o_ref[...] = h3


def _combine(dest_flat, ybuf, h1, rw, p_all, layer, wts, g_final, tf, final):
    n_tok = h1.shape[0]
    row = lambda i: (i, 0)
    return pl.pallas_call(
        functools.partial(_combine_kernel, tf=tf, final=final),
        out_shape=jax.ShapeDtypeStruct((n_tok, D_MODEL), F32),
        grid=(n_tok // tf,),
        in_specs=[pl.BlockSpec((TOP_K * tf,), lambda i: (i,), memory_space=pltpu.SMEM),
                  pl.BlockSpec(memory_space=pl.ANY),
                  pl.BlockSpec((tf, D_MODEL), row), pl.BlockSpec((tf, LANES), row),
                  pl.BlockSpec((None, tf, PLE_DIM), lambda i: (layer, i, 0)),
                  _full((1, D_MODEL)), _full((D_MODEL, D_MODEL)), _full((PLE_DIM, D_MODEL)),
                  _full((1, D_MODEL))],
        out_specs=pl.BlockSpec((tf, D_MODEL), row),
        scratch_shapes=[pltpu.VMEM((TOP_K, tf, D_MODEL), F32), pltpu.SemaphoreType.DMA((TOP_K,))],
        compiler_params=_cparams(("arbitrary",)),
        name="combine",
    )(dest_flat, ybuf, h1, rw, p_all, wts["g_ple"], wts["w_ple_gate"], wts["w_ple_proj"], g_final)


def _layer_weights(i, g_mix_norm, w_in, g_q_lat, w_q_up, g_kv_lat, w_kv_up,
                   ssm_a_re, ssm_a_im, ssm_b_re, ssm_b_im, ssm_c_re, ssm_c_im, ssm_d, ssm_log_step,
                   w_glu, b_glu, g_attn_out, g_ssm_out, w_out, g_moe_norm,
                   w_group_router, b_group_router, w_expert_router, b_expert_router,
                   g_ple_norm, w_ple_gate, w_ple_proj):
    off_kv, off_kr, off_ssm = Q_LORA, Q_LORA + KV_LORA, Q_LORA + KV_LORA + QK_ROPE
    wi = w_in[i]
    kr_block = jnp.zeros((D_MODEL, HEAD_PAD), F32).at[:, ROPE_LANE0:ROPE_LANE0 + QK_ROPE].set(wi[:, off_kr:off_ssm])
    w_in_p = jnp.concatenate([wi[:, :off_kr], kr_block, wi[:, off_ssm:]], axis=1)
    wq = w_q_up[i].reshape(Q_LORA, N_HEADS, QK_NOPE + QK_ROPE) * ATTN_SCALE
    wq = jnp.pad(wq, ((0, 0), (0, 0), (0, HEAD_PAD - QK_NOPE - QK_ROPE))).reshape(Q_LORA, N_HEADS * HEAD_PAD)
    wkv = w_kv_up[i].reshape(KV_LORA, N_HEADS, QK_NOPE + V_HEAD)
    wk = jnp.pad(wkv[..., :QK_NOPE], ((0, 0), (0, 0), (0, HEAD_PAD - QK_NOPE))).reshape(KV_LORA, N_HEADS * HEAD_PAD)
    wv = wkv[..., QK_NOPE:].reshape(KV_LORA, MLA_WIDTH)
    lam = lax.complex(ssm_a_re[i], ssm_a_im[i])
    lam_bar = jnp.exp(lam * jnp.exp(ssm_log_step[i])[:, None])
    b_fac = (lam_bar - 1.0) / lam
    bc = lax.complex(ssm_b_re[i], ssm_b_im[i]) * b_fac[:, :, None]
    n_per = SSM_GROUPS // S5_SLABS

    def block_diag(blocks):
        s, n, a, b = blocks.shape
        eye = jnp.eye(n, dtype=blocks.dtype)
        return jnp.einsum("snab,nm->snamb", blocks, eye).reshape(s, n * a, n * b)

    b_hp_re = jnp.real(bc).transpose(0, 2, 1).reshape(S5_SLABS, n_per, SSM_CH, SSM_STATE)
    b_hp_im = jnp.imag(bc).transpose(0, 2, 1).reshape(S5_SLABS, n_per, SSM_CH, SSM_STATE)
    s5_wb = jnp.concatenate([block_diag(b_hp_re), block_diag(b_hp_im)], axis=-1)
    c_re = ssm_c_re[i].transpose(0, 2, 1).reshape(S5_SLABS, n_per, SSM_STATE, SSM_CH)
    c_im = ssm_c_im[i].transpose(0, 2, 1).reshape(S5_SLABS, n_per, SSM_STATE, SSM_CH)
    s5_wc = jnp.stack([block_diag(c_re), -block_diag(c_im)])
    w_r = jnp.zeros((D_MODEL, LANES), F32)
    w_r = w_r.at[:, :N_EXPERT_GROUPS].set(w_group_router[i])
    w_r = w_r.at[:, ROUTE_E0:ROUTE_E0 + N_EXPERTS].set(w_expert_router[i])
    w_r_hi = w_r.astype(BF16)
    w_r_lo = (w_r - w_r_hi.astype(F32)).astype(BF16)
    b_r = jnp.zeros((1, LANES), F32)
    b_r = b_r.at[0, :N_EXPERT_GROUPS].set(b_group_router[i])
    b_r = b_r.at[0, ROUTE_E0:ROUTE_E0 + N_EXPERTS].set(b_expert_router[i])
    return dict(
        g_mix=g_mix_norm[i][None], w_in=w_in_p.astype(BF16),
        g_q=g_q_lat[i][None], w_q=wq.astype(BF16),
        g_kv=g_kv_lat[i][None], w_kv=jnp.concatenate([wk, wv], axis=1).astype(BF16),
        s5_wb=s5_wb.astype(BF16), s5_wc=s5_wc.astype(BF16),
        s5_are=jnp.real(lam_bar).reshape(1, N_STATE), s5_aim=jnp.imag(lam_bar).reshape(1, N_STATE),
        s5_d=ssm_d[i].reshape(1, SSM_WIDTH), w_glu=w_glu[i].astype(BF16), b_glu=b_glu[i][None],
        g_attn=g_attn_out[i][None], g_ssm=g_ssm_out[i][None],
        w_out=w_out[i].astype(BF16), g_moe=g_moe_norm[i][None],
        w_r_hi=w_r_hi, w_r_lo=w_r_lo, b_r=b_r,
        g_ple=g_ple_norm[i][None], w_ple_gate=w_ple_gate[i].astype(BF16),
        w_ple_proj=w_ple_proj[i].astype(BF16),
    )


def _slot_layout(route, cnt, n_blocks, te):
    counts = cnt[0, :N_EXPERTS].astype(I32)
    pcounts = (counts + te - 1) // te * te
    pends = jnp.cumsum(pcounts)
    pstarts = pends - pcounts
    eid = route[:, :TOP_K]
    rank = route[:, TOP_K:2 * TOP_K]
    start = jnp.sum(jnp.where(eid[:, :, None] == jnp.arange(N_EXPERTS)[None, None, :],
                              pstarts[None, None, :], 0), axis=-1)
    dest = (start + rank).reshape(-1).astype(I32)
    blk0 = jnp.arange(n_blocks, dtype=I32) * te
    block_e = jnp.minimum(jnp.sum((pends[None, :] <= blk0[:, None]).astype(I32), axis=1), N_EXPERTS - 1)
    fill = jnp.clip((pstarts + counts)[block_e] - blk0, 0, te)
    zflag = (fill < te).astype(I32)
    return dest, block_e, zflag


def kernel(x, p, positions, g_mix_norm, w_in, g_q_lat, w_q_up, g_kv_lat, w_kv_up, ssm_a_re, ssm_a_im, ssm_b_re, ssm_b_im, ssm_c_re, ssm_c_im, ssm_d, ssm_log_step, w_glu, b_glu, g_attn_out, g_ssm_out, w_out, g_moe_norm, w_group_router, b_group_router, w_expert_router, b_expert_router, w_exp_gate, w_exp_up, w_exp_down, g_ple_norm, w_ple_gate, w_ple_proj, g_final):
    bsz, seq, _ = x.shape
    depth = p.shape[0]
    n_tok = bsz * seq
    t = _tiles(n_tok, seq)
    n_slots = TOP_K * n_tok + N_EXPERTS * t["te"]
    n_blocks = n_slots // t["te"]

    cosf, sinf = _rope_tables(positions.reshape(n_tok, 1), t["tm"])
    tril = jnp.tril(jnp.ones((t["tm"], t["tm"]), F32), k=-1).astype(BF16)
    g_fin = g_final[None]

    h = x.reshape(n_tok, D_MODEL)
    for i in range(depth):
        wts = _layer_weights(i, g_mix_norm, w_in, g_q_lat, w_q_up, g_kv_lat, w_kv_up,
                             ssm_a_re, ssm_a_im, ssm_b_re, ssm_b_im, ssm_c_re, ssm_c_im, ssm_d,
                             ssm_log_step, w_glu, b_glu, g_attn_out, g_ssm_out, w_out, g_moe_norm,
                             w_group_router, b_group_router, w_expert_router, b_expert_router,
                             g_ple_norm, w_ple_gate, w_ple_proj)
        q, k, v, u_tm = _inproj(h, cosf, sinf, wts, bsz, seq, t["tm"])
        attn_n = _attention(q, k, v, wts["g_attn"], bsz, seq, t["tq"])
        ssm_n = _s5(u_tm.reshape(seq, bsz, SSM_WIDTH), wts, bsz, seq, t["lc"])
        h1, xn, route, rw, cnt = _outproj(h, attn_n, ssm_n.reshape(seq, bsz * SSM_WIDTH), wts, tril,
                                          bsz, seq, t["tm"])
        dest, block_e, zflag = _slot_layout(route, cnt, n_blocks, t["te"])
        xbuf = _dispatch(zflag, dest, xn, n_slots, t["td"], t["te"])
        ybuf = _experts(block_e, xbuf, w_exp_gate, w_exp_up, w_exp_down, i, t["te"])
        h = _combine(dest, ybuf, h1, rw, p.reshape(depth, n_tok, PLE_DIM), i, wts, g_fin, t["tf"],
                     final=(i == depth - 1))
    return h.reshape(bsz, seq, D_MODEL)
```

```python
import functools
import math

import jax
import jax.numpy as jnp
from jax import lax
from jax.experimental import pallas as pl
from jax.experimental.pallas import tpu as pltpu

F32 = jnp.float32
BF16 = jnp.bfloat16
I32 = jnp.int32
U32 = jnp.uint32

D_MODEL = 1024
MLA_WIDTH = 512
SSM_WIDTH = 512
V_HEAD = 64
N_HEADS = 8
QK_NOPE = 64
QK_ROPE = 32
HALF_ROPE = QK_ROPE // 2
Q_LORA = 256
KV_LORA = 128
ROPE_BASE = 10000.0
ATTN_SCALE = 1.0 / math.sqrt(QK_NOPE + QK_ROPE)
SSM_CH = 16
SSM_GROUPS = 32
SSM_STATE = 64
N_EXPERT_GROUPS = 4
EXPERTS_PER_GROUP = 8
N_EXPERTS = 32
TOP_K = 2
D_EXPERT = 256
PLE_DIM = 256
EPS = 1e-6

LANES = 128
HEAD_PAD = 128
ROPE_LANE0 = QK_NOPE
N_STATE = SSM_GROUPS * SSM_STATE
PACKED = D_MODEL // 2
NEG = -1e30
VMEM_LIMIT = 56 * 1024 * 1024
SUB_ROWS = 512


def _tiles(n_tok, seq):
    return dict(
        tm=min(1024, seq),
        tq=min(512, seq),
        lc=min(32, seq),
        te=512,
        td=min(512, n_tok),
        tf=min(512, n_tok),
    )


def _rms(x, g):
    ms = jnp.mean(x * x, axis=-1, keepdims=True)
    return x * lax.rsqrt(ms + EPS) * g


def _sigmoid(x):
    return 1.0 / (1.0 + jnp.exp(-x))


def _pack_bf16_pairs(x):
    lo = lax.bitcast_convert_type(x[:, :PACKED].astype(BF16).astype(F32), U32)
    hi = lax.bitcast_convert_type(x[:, PACKED:].astype(BF16).astype(F32), U32)
    return (lo >> 16) | hi


def _unpack_bf16_pairs(w):
    lo = lax.bitcast_convert_type(w << 16, F32)
    hi = lax.bitcast_convert_type(w & jnp.uint32(0xFFFF0000), F32)
    return jnp.concatenate([lo, hi], axis=1)


def _cparams(sem):
    return pltpu.CompilerParams(dimension_semantics=sem, vmem_limit_bytes=VMEM_LIMIT)


def _full(shape):
    nd = len(shape)
    return pl.BlockSpec(shape, lambda *_: (0,) * nd)


def _rope_table_kernel(pos_ref, invf_ref, cos_ref, sin_ref):
    ang = pos_ref[...].astype(F32) * invf_ref[...]
    cos_ref[...] = jnp.cos(ang)
    sin_ref[...] = jnp.sin(ang)


def _rope_tables(pos_col, tm):
    n_tok = pos_col.shape[0]
    lane = jnp.arange(LANES)
    in_rope = (lane >= ROPE_LANE0) & (lane < ROPE_LANE0 + QK_ROPE)
    freq = ROPE_BASE ** (-((lane - ROPE_LANE0) % HALF_ROPE).astype(F32) / HALF_ROPE)
    invf = jnp.where(in_rope, freq, 0.0).astype(F32)[None, :]
    return pl.pallas_call(
        _rope_table_kernel,
        out_shape=(jax.ShapeDtypeStruct((n_tok, LANES), F32),) * 2,
        grid=(n_tok // tm,),
        in_specs=[pl.BlockSpec((tm, 1), lambda i: (i, 0)), _full((1, LANES))],
        out_specs=(pl.BlockSpec((tm, LANES), lambda i: (i, 0)),) * 2,
        compiler_params=_cparams(("parallel",)),
        name="rope_tables",
    )(pos_col, invf)


def _rope(x, cosf, s_lo, s_hi):
    n = x.shape[-1]
    return (x * cosf + pltpu.roll(x, n - HALF_ROPE, 1) * s_lo + pltpu.roll(x, HALF_ROPE, 1) * s_hi)


def _row_chunks(n_rows):
    sub = min(SUB_ROWS, n_rows)
    return [slice(c * sub, (c + 1) * sub) for c in range(n_rows // sub)]


def _inproj_kernel(h_ref, cos_ref, sin_ref, gmix_ref, win_ref, gq_ref, wq_ref, gkv_ref, wkv_ref,
                   q_ref, k_ref, v_ref, u_ref):
    lane = lax.broadcasted_iota(I32, (1, LANES), 1)
    ones_lane = jnp.where(lane == V_HEAD, 1.0, 0.0)
    for r in _row_chunks(h_ref.shape[0]):
        xn = _rms(h_ref[r, :], gmix_ref[...]).astype(BF16)
        z = jnp.dot(xn, win_ref[...], preferred_element_type=F32)
        cosf = cos_ref[r, :]
        sinf = sin_ref[r, :]
        s_lo = jnp.where((lane >= ROPE_LANE0) & (lane < ROPE_LANE0 + HALF_ROPE), -sinf, 0.0)
        s_hi = jnp.where((lane >= ROPE_LANE0 + HALF_ROPE) & (lane < ROPE_LANE0 + QK_ROPE), sinf, 0.0)
        qn = _rms(z[:, :Q_LORA], gq_ref[...]).astype(BF16)
        q = jnp.dot(qn, wq_ref[...], preferred_element_type=F32)
        kvn = _rms(z[:, Q_LORA:Q_LORA + KV_LORA], gkv_ref[...]).astype(BF16)
        kv = jnp.dot(kvn, wkv_ref[...], preferred_element_type=F32)
        kpe = _rope(z[:, Q_LORA + KV_LORA:Q_LORA + KV_LORA + HEAD_PAD], cosf, s_lo, s_hi)
        for hd in range(N_HEADS):
            sl = slice(hd * HEAD_PAD, (hd + 1) * HEAD_PAD)
            vsl = slice(N_HEADS * HEAD_PAD + hd * HEAD_PAD, N_HEADS * HEAD_PAD + (hd + 1) * HEAD_PAD)
            q_ref[r, sl] = _rope(q[:, sl], cosf, s_lo, s_hi).astype(BF16)
            k_ref[r, sl] = (kv[:, sl] + kpe).astype(BF16)
            v_ref[r, sl] = (kv[:, vsl] + ones_lane).astype(BF16)
        u_ref[r, :] = z[:, Q_LORA + KV_LORA + HEAD_PAD:]


def _inproj(h, cosf, sinf, wts, bsz, seq, tm):
    n_tok = bsz * seq
    nl = seq // tm
    row = lambda b, l: (b * nl + l, 0)
    qk_w = N_HEADS * HEAD_PAD
    return pl.pallas_call(
        _inproj_kernel,
        out_shape=(jax.ShapeDtypeStruct((n_tok, qk_w), BF16),
                   jax.ShapeDtypeStruct((n_tok, qk_w), BF16),
                   jax.ShapeDtypeStruct((n_tok, qk_w), BF16),
                   jax.ShapeDtypeStruct((seq, bsz * SSM_WIDTH), F32)),
        grid=(bsz, nl),
        in_specs=[pl.BlockSpec((tm, D_MODEL), row),
                  pl.BlockSpec((tm, LANES), row), pl.BlockSpec((tm, LANES), row),
                  _full((1, D_MODEL)), _full((D_MODEL, D_MODEL)),
                  _full((1, Q_LORA)), _full((Q_LORA, qk_w)),
                  _full((1, KV_LORA)), _full((KV_LORA, 2 * qk_w))],
        out_specs=(pl.BlockSpec((tm, qk_w), row), pl.BlockSpec((tm, qk_w), row),
                   pl.BlockSpec((tm, qk_w), row),
                   pl.BlockSpec((tm, SSM_WIDTH), lambda b, l: (l, b))),
        compiler_params=_cparams(("parallel", "parallel")),
        name="inproj",
    )(h, cosf, sinf, wts["g_mix"], wts["w_in"], wts["g_q"], wts["w_q"], wts["g_kv"], wts["w_kv"])


ATTN_RB = 64


def _attn_kernel(q_ref, k_ref, v_ref, g_ref, o_ref, s_ref, p_ref, m_ref, al_ref, acc_ref, out_ref, *, tq):
    qi = pl.program_id(1)
    col = lax.broadcasted_iota(I32, (ATTN_RB, tq), 1)
    row = lax.broadcasted_iota(I32, (ATTN_RB, tq), 0)
    for pr in range(N_HEADS // 2):
        heads = [2 * pr, 2 * pr + 1]
        m_ref[...] = jnp.full(m_ref.shape, NEG, F32)
        acc_ref[...] = jnp.zeros(acc_ref.shape, F32)

        def step(j, masked, heads=heads):
            rows = pl.ds(pl.multiple_of(j * tq, tq), tq)
            for a, hd in enumerate(heads):
                hs = slice(hd * HEAD_PAD, (hd + 1) * HEAD_PAD)
                s_ref[a] = lax.dot_general(q_ref[:, hs], k_ref[rows, hs], (((1,), (1,)), ((), ())),
                                           preferred_element_type=F32)
            for a in range(2):
                for c in range(tq // ATTN_RB):
                    r = slice(c * ATTN_RB, (c + 1) * ATTN_RB)
                    s = s_ref[a, r, :]
                    if masked:
                        s = jnp.where(col <= row + c * ATTN_RB, s, NEG)
                    m_old = m_ref[a, r, :]
                    m_new = jnp.maximum(m_old, jnp.max(s, axis=-1, keepdims=True))
                    p_ref[a, r, :] = jnp.exp2(s - m_new).astype(BF16)
                    al_ref[a, r, :] = jnp.exp2(m_old - m_new)
                    m_ref[a, r, :] = m_new
            for a, hd in enumerate(heads):
                pv = jnp.dot(p_ref[a], v_ref[rows, hd * HEAD_PAD:(hd + 1) * HEAD_PAD],
                             preferred_element_type=F32)
                acc_ref[a] = al_ref[a] * acc_ref[a] + pv

        def body(j, carry):
            step(j, masked=False)
            return carry

        lax.fori_loop(0, qi, body, 0)
        step(qi, masked=True)
        for a, hd in enumerate(heads):
            acc = acc_ref[a]
            out_ref[:, hd * V_HEAD:(hd + 1) * V_HEAD] = acc[:, :V_HEAD] / acc[:, V_HEAD:V_HEAD + 1]
    o_ref[...] = _rms(out_ref[...], g_ref[...]).astype(BF16)


def _attention(q, k, v, g_attn, bsz, seq, tq):
    n_tok = bsz * seq
    nq = seq // tq
    qk_w = N_HEADS * HEAD_PAD
    return pl.pallas_call(
        functools.partial(_attn_kernel, tq=tq),
        out_shape=jax.ShapeDtypeStruct((n_tok, MLA_WIDTH), BF16),
        grid=(bsz, nq),
        in_specs=[pl.BlockSpec((tq, qk_w), lambda b, i: (b * nq + i, 0)),
                  pl.BlockSpec((seq, qk_w), lambda b, i: (b, 0)),
                  pl.BlockSpec((seq, qk_w), lambda b, i: (b, 0)),
                  _full((1, MLA_WIDTH))],
        out_specs=pl.BlockSpec((tq, MLA_WIDTH), lambda b, i: (b * nq + i, 0)),
        scratch_shapes=[pltpu.VMEM((2, tq, tq), F32), pltpu.VMEM((2, tq, tq), BF16),
                        pltpu.VMEM((2, tq, 1), F32), pltpu.VMEM((2, tq, 1), F32),
                        pltpu.VMEM((2, tq, HEAD_PAD), F32), pltpu.VMEM((tq, MLA_WIDTH), F32)],
        compiler_params=_cparams(("parallel", "parallel")),
        name="attn",
    )(q, k, v, g_attn)


S5_SLABS = 4
S5_SLAB_U = SSM_WIDTH // S5_SLABS
S5_SLAB_X = N_STATE // S5_SLABS


def _s5_kernel(u_ref, wb_ref, are_ref, aim_ref, wc_ref, d_ref, wglu_ref, bglu_ref, g_ref,
               o_ref, xr_ref, xi_ref, sr_ref, si_ref, *, lc, nb):
    @pl.when(pl.program_id(0) == 0)
    def _():
        xr_ref[...] = jnp.zeros_like(xr_ref)
        xi_ref[...] = jnp.zeros_like(xi_ref)

    rows = lc * nb
    u = u_ref[...].reshape(rows, SSM_WIDTH)
    ub = u.astype(BF16)
    for j in range(S5_SLABS):
        bu = jnp.dot(ub[:, j * S5_SLAB_U:(j + 1) * S5_SLAB_U], wb_ref[j], preferred_element_type=F32)
        sr_ref[:, j * S5_SLAB_X:(j + 1) * S5_SLAB_X] = bu[:, :S5_SLAB_X]
        si_ref[:, j * S5_SLAB_X:(j + 1) * S5_SLAB_X] = bu[:, S5_SLAB_X:]
    for c in range(S5_SLABS):
        sl = slice(c * S5_SLAB_X, (c + 1) * S5_SLAB_X)
        ar = jnp.broadcast_to(are_ref[:, sl], (nb, S5_SLAB_X))
        ai = jnp.broadcast_to(aim_ref[:, sl], (nb, S5_SLAB_X))

        def scan_step(t, carry, sl=sl, ar=ar, ai=ai):
            xr, xi = carry
            r = pl.ds(pl.multiple_of(t * nb, nb), nb)
            nr = ar * xr - ai * xi + sr_ref[r, sl]
            ni = ar * xi + ai * xr + si_ref[r, sl]
            sr_ref[r, sl] = nr
            si_ref[r, sl] = ni
            return nr, ni

        xr, xi = lax.fori_loop(0, lc, scan_step, (xr_ref[:, sl], xi_ref[:, sl]), unroll=4)
        xr_ref[:, sl] = xr
        xi_ref[:, sl] = xi
    ys = []
    for j in range(S5_SLABS):
        sl = slice(j * S5_SLAB_X, (j + 1) * S5_SLAB_X)
        ys.append(jnp.dot(sr_ref[:, sl].astype(BF16), wc_ref[0, j], preferred_element_type=F32)
                  + jnp.dot(si_ref[:, sl].astype(BF16), wc_ref[1, j], preferred_element_type=F32))
    y = jnp.concatenate(ys, axis=-1) + d_ref[...] * u
    y = 0.5 * y * (1.0 + jnp.tanh(math.sqrt(2.0 / math.pi) * (y + 0.044715 * (y * y * y))))
    gate = jnp.dot(y.astype(BF16), wglu_ref[...], preferred_element_type=F32) + bglu_ref[...]
    y = y * _sigmoid(gate)
    o_ref[...] = _rms(y, g_ref[...]).astype(BF16).reshape(lc, nb, SSM_WIDTH)


def _s5(u_tm, wts, bsz, seq, lc):
    rows = lc * bsz
    return pl.pallas_call(
        functools.partial(_s5_kernel, lc=lc, nb=bsz),
        out_shape=jax.ShapeDtypeStruct((seq, bsz, SSM_WIDTH), BF16),
        grid=(seq // lc,),
        in_specs=[pl.BlockSpec((lc, bsz, SSM_WIDTH), lambda i: (i, 0, 0)),
                  _full((S5_SLABS, S5_SLAB_U, 2 * S5_SLAB_X)),
                  _full((1, N_STATE)), _full((1, N_STATE)),
                  _full((2, S5_SLABS, S5_SLAB_X, S5_SLAB_U)),
                  _full((1, SSM_WIDTH)), _full((SSM_WIDTH, SSM_WIDTH)), _full((1, SSM_WIDTH)),
                  _full((1, SSM_WIDTH))],
        out_specs=pl.BlockSpec((lc, bsz, SSM_WIDTH), lambda i: (i, 0, 0)),
        scratch_shapes=[pltpu.VMEM((bsz, N_STATE), F32), pltpu.VMEM((bsz, N_STATE), F32),
                        pltpu.VMEM((rows, N_STATE), F32), pltpu.VMEM((rows, N_STATE), F32)],
        compiler_params=_cparams(("arbitrary",)),
        name="s5",
    )(u_tm, wts["s5_wb"], wts["s5_are"], wts["s5_aim"], wts["s5_wc"], wts["s5_d"],
      wts["w_glu"], wts["b_glu"], wts["g_ssm"])


ROUTE_E0 = N_EXPERT_GROUPS


def _outproj_kernel(h_ref, a_ref, s_ref, wout_ref, gmoe_ref, wrh_ref, wrl_ref, br_ref, tril_ref,
                    h1_ref, xn_ref, route_ref, rw_ref, cnt_ref):
    @pl.when((pl.program_id(0) == 0) & (pl.program_id(1) == 0))
    def _():
        cnt_ref[...] = jnp.zeros_like(cnt_ref)

    sub = tril_ref.shape[0]
    lane = lax.broadcasted_iota(I32, (sub, LANES), 1).astype(F32)
    big = float(LANES)
    for r in _row_chunks(h_ref.shape[0]):
        h1 = (h_ref[r, :]
              + jnp.dot(a_ref[r, :], wout_ref[:MLA_WIDTH, :], preferred_element_type=F32)
              + jnp.dot(s_ref[r, :], wout_ref[MLA_WIDTH:, :], preferred_element_type=F32))
        h1_ref[r, :] = h1
        xn = _rms(h1, gmoe_ref[...])
        xn_ref[r, :] = _pack_bf16_pairs(xn)
        x_hi = xn.astype(BF16)
        x_lo = (xn - x_hi.astype(F32)).astype(BF16)
        lg = (jnp.dot(x_hi, wrh_ref[...], preferred_element_type=F32)
              + (jnp.dot(x_lo, wrh_ref[...], preferred_element_type=F32)
                 + jnp.dot(x_hi, wrl_ref[...], preferred_element_type=F32))) + br_ref[...]
        gl = jnp.where(lane < N_EXPERT_GROUPS, lg, NEG)
        gmax = jnp.max(gl, axis=-1, keepdims=True)
        gsum = jnp.sum(jnp.exp(gl - gmax), axis=-1, keepdims=True)
        g_top_p = 1.0 / gsum
        gidx = jnp.min(jnp.where(gl == gmax, lane, big), axis=-1, keepdims=True)
        lo = ROUTE_E0 + EXPERTS_PER_GROUP * gidx
        sel = jnp.where((lane >= lo) & (lane < lo + EXPERTS_PER_GROUP), lg, NEG)
        m1 = jnp.max(sel, axis=-1, keepdims=True)
        i1 = jnp.min(jnp.where(sel == m1, lane, big), axis=-1, keepdims=True)
        sel2 = jnp.where(lane == i1, NEG, sel)
        m2 = jnp.max(sel2, axis=-1, keepdims=True)
        i2 = jnp.min(jnp.where(sel2 == m2, lane, big), axis=-1, keepdims=True)
        e21 = jnp.exp(m2 - m1)
        w1 = g_top_p / (1.0 + e21)
        w2 = g_top_p * e21 / (1.0 + e21)
        e1 = i1 - ROUTE_E0
        e2 = i2 - ROUTE_E0
        oh = jnp.where((lane == e1) | (lane == e2), 1.0, 0.0)
        before = cnt_ref[...] + jnp.dot(tril_ref[...], oh.astype(BF16), preferred_element_type=F32)
        r1 = jnp.sum(jnp.where(lane == e1, before, 0.0), axis=-1, keepdims=True)
        r2 = jnp.sum(jnp.where(lane == e2, before, 0.0), axis=-1, keepdims=True)
        cnt_ref[...] = cnt_ref[...] + jnp.sum(oh, axis=0, keepdims=True)
        route = jnp.where(lane == 0, e1, jnp.where(lane == 1, e2, jnp.where(lane == 2, r1, r2)))
        route_ref[r, :] = route.astype(I32)
        rw_ref[r, :] = jnp.where(lane == 0, w1, w2)


def _outproj(h, attn_n, ssm_tm, wts, tril, bsz, seq, tm):
    n_tok = bsz * seq
    nl = seq // tm
    row = lambda b, l: (b * nl + l, 0)
    return pl.pallas_call(
        _outproj_kernel,
        out_shape=(jax.ShapeDtypeStruct((n_tok, D_MODEL), F32),
                   jax.ShapeDtypeStruct((n_tok, PACKED), U32),
                   jax.ShapeDtypeStruct((n_tok, LANES), I32),
                   jax.ShapeDtypeStruct((n_tok, LANES), F32),
                   jax.ShapeDtypeStruct((1, LANES), F32)),
        grid=(bsz, nl),
        in_specs=[pl.BlockSpec((tm, D_MODEL), row),
                  pl.BlockSpec((tm, MLA_WIDTH), row),
                  pl.BlockSpec((tm, SSM_WIDTH), lambda b, l: (l, b)),
                  _full((D_MODEL, D_MODEL)), _full((1, D_MODEL)),
                  _full((D_MODEL, LANES)), _full((D_MODEL, LANES)), _full((1, LANES)),
                  _full(tril.shape)],
        out_specs=(pl.BlockSpec((tm, D_MODEL), row), pl.BlockSpec((tm, PACKED), row),
                   pl.BlockSpec((tm, LANES), row), pl.BlockSpec((tm, LANES), row),
                   _full((1, LANES))),
        compiler_params=_cparams(("arbitrary", "arbitrary")),
        name="outproj",
    )(h, attn_n, ssm_tm, wts["w_out"], wts["g_moe"], wts["w_r_hi"], wts["w_r_lo"], wts["b_r"], tril)


def _row_copy(src, src_row, dst, dst_row, sem):
    return pltpu.make_async_copy(src.at[pl.ds(src_row, 1)], dst.at[pl.ds(dst_row, 1)], sem)


def _dispatch_kernel(zflag_ref, dest_ref, xn_ref, xbuf_ref, zero_ref, sems, zsem, *, td, te, n_blocks):
    def zero_copy(b):
        return pltpu.make_async_copy(zero_ref, xbuf_ref.at[pl.ds(pl.multiple_of(b * te, te), te)], zsem)

    @pl.when(pl.program_id(0) == 0)
    def _():
        zero_ref[...] = jnp.zeros_like(zero_ref)

        def start(b, _):
            @pl.when(zflag_ref[b] != 0)
            def _():
                zero_copy(b).start()
            return 0

        def wait(b, _):
            @pl.when(zflag_ref[b] != 0)
            def _():
                zero_copy(b).wait()
            return 0

        lax.fori_loop(0, n_blocks, start, 0)
        lax.fori_loop(0, n_blocks, wait, 0)

    def issue(r, _):
        for k in range(TOP_K):
            _row_copy(xn_ref, r, xbuf_ref, dest_ref[TOP_K * r + k], sems.at[k]).start(priority=k)
        return 0

    lax.fori_loop(0, td, issue, 0, unroll=8)
    for k in range(TOP_K):
        pltpu.make_async_copy(xn_ref, xbuf_ref.at[pl.ds(0, td)], sems.at[k]).wait()


def _dispatch(zflag, dest_flat, xn, n_slots, td, te):
    n_tok = xn.shape[0]
    n_blocks = n_slots // te
    grid_spec = pltpu.PrefetchScalarGridSpec(
        num_scalar_prefetch=1,
        grid=(n_tok // td,),
        in_specs=[pl.BlockSpec((TOP_K * td,), lambda i, zf: (i,), memory_space=pltpu.SMEM),
                  pl.BlockSpec((td, PACKED), lambda i, zf: (i, 0))],
        out_specs=pl.BlockSpec(memory_space=pl.ANY),
        scratch_shapes=[pltpu.VMEM((te, PACKED), U32), pltpu.SemaphoreType.DMA((TOP_K,)),
                        pltpu.SemaphoreType.DMA],
    )
    return pl.pallas_call(
        functools.partial(_dispatch_kernel, td=td, te=te, n_blocks=n_blocks),
        out_shape=jax.ShapeDtypeStruct((n_slots, PACKED), U32),
        grid_spec=grid_spec,
        compiler_params=_cparams(("arbitrary",)),
        name="dispatch",
    )(zflag, dest_flat, xn)


def _experts_kernel(be_ref, x_ref, wg_ref, wu_ref, wd_ref, y_ref):
    del be_ref
    xb = _unpack_bf16_pairs(x_ref[...]).astype(BF16)
    g = jnp.dot(xb, wg_ref[...].astype(BF16), preferred_element_type=F32)
    u = jnp.dot(xb, wu_ref[...].astype(BF16), preferred_element_type=F32)
    hdn = (g * _sigmoid(g) * u).astype(BF16)
    y_ref[...] = _pack_bf16_pairs(jnp.dot(hdn, wd_ref[...].astype(BF16), preferred_element_type=F32))


def _experts(block_e, xbuf, w_gate, w_up, w_down, layer, te):
    n_slots = xbuf.shape[0]
    grid_spec = pltpu.PrefetchScalarGridSpec(
        num_scalar_prefetch=1,
        grid=(n_slots // te,),
        in_specs=[pl.BlockSpec((te, PACKED), lambda i, be: (i, 0)),
                  pl.BlockSpec((None, None, D_MODEL, D_EXPERT), lambda i, be: (layer, be[i], 0, 0)),
                  pl.BlockSpec((None, None, D_MODEL, D_EXPERT), lambda i, be: (layer, be[i], 0, 0)),
                  pl.BlockSpec((None, None, D_EXPERT, D_MODEL), lambda i, be: (layer, be[i], 0, 0))],
        out_specs=pl.BlockSpec((te, PACKED), lambda i, be: (i, 0)),
    )
    return pl.pallas_call(
        _experts_kernel,
        out_shape=jax.ShapeDtypeStruct((n_slots, PACKED), U32),
        grid_spec=grid_spec,
        compiler_params=_cparams(("arbitrary",)),
        name="experts",
    )(block_e, xbuf, w_gate, w_up, w_down)


def _combine_kernel(dest_ref, dnext_ref, ybuf_ref, h1_ref, rw_ref, p_ref, gple_ref, wgate_ref, wproj_ref,
                    gfin_ref, o_ref, yv_ref, sems, *, tf, final):
    i = pl.program_id(0)
    slot = i % 2

    def gather(idx_ref, into):
        def issue(r, _):
            for k in range(TOP_K):
                _row_copy(ybuf_ref, idx_ref[TOP_K * r + k], yv_ref.at[into, k], r,
                          sems.at[into, k]).start(priority=k)
            return 0

        lax.fori_loop(0, tf, issue, 0, unroll=8)

    @pl.when(i == 0)
    def _():
        gather(dest_ref, 0)

    @pl.when(i + 1 < pl.num_programs(0))
    def _():
        gather(dnext_ref, 1 - slot)

    for k in range(TOP_K):
        pltpu.make_async_copy(ybuf_ref.at[pl.ds(0, tf)], yv_ref.at[slot, k], sems.at[slot, k]).wait()
    rw = rw_ref[...]
    h2 = (h1_ref[...] + rw[:, 0:1] * _unpack_bf16_pairs(yv_ref[slot, 0])
          + rw[:, 1:2] * _unpack_bf16_pairs(yv_ref[slot, 1]))
    hn = _rms(h2, gple_ref[...]).astype(BF16)
    gate = _sigmoid(jnp.dot(hn, wgate_ref[...], preferred_element_type=F32))
    ple = jnp.dot(p_ref[...].astype(BF16), wproj_ref[...], preferred_element_type=F32)
    h3 = h2 + ple * gate
    if final:
        h3 = _rms(h3, gfin_ref[...])
    o_ref[...] = h3


def _combine(dest_flat, ybuf, h1, rw, p_all, layer, wts, g_final, tf, final):
    n_tok = h1.shape[0]
    n_steps = n_tok // tf
    row = lambda i: (i, 0)
    return pl.pallas_call(
        functools.partial(_combine_kernel, tf=tf, final=final),
        out_shape=jax.ShapeDtypeStruct((n_tok, D_MODEL), F32),
        grid=(n_steps,),
        in_specs=[pl.BlockSpec((TOP_K * tf,), lambda i: (i,), memory_space=pltpu.SMEM),
                  pl.BlockSpec((TOP_K * tf,), lambda i: (jnp.minimum(i + 1, n_steps - 1),),
                               memory_space=pltpu.SMEM),
                  pl.BlockSpec(memory_space=pl.ANY),
                  pl.BlockSpec((tf, D_MODEL), row), pl.BlockSpec((tf, LANES), row),
                  pl.BlockSpec((None, tf, PLE_DIM), lambda i: (layer, i, 0)),
                  _full((1, D_MODEL)), _full((D_MODEL, D_MODEL)), _full((PLE_DIM, D_MODEL)),
                  _full((1, D_MODEL))],
        out_specs=pl.BlockSpec((tf, D_MODEL), row),
        scratch_shapes=[pltpu.VMEM((2, TOP_K, tf, PACKED), U32), pltpu.SemaphoreType.DMA((2, TOP_K))],
        compiler_params=_cparams(("arbitrary",)),
        name="combine",
    )(dest_flat, dest_flat, ybuf, h1, rw, p_all, wts["g_ple"], wts["w_ple_gate"], wts["w_ple_proj"], g_final)


def _layer_weights(i, g_mix_norm, w_in, g_q_lat, w_q_up, g_kv_lat, w_kv_up,
                   ssm_a_re, ssm_a_im, ssm_b_re, ssm_b_im, ssm_c_re, ssm_c_im, ssm_d, ssm_log_step,
                   w_glu, b_glu, g_attn_out, g_ssm_out, w_out, g_moe_norm,
                   w_group_router, b_group_router, w_expert_router, b_expert_router,
                   g_ple_norm, w_ple_gate, w_ple_proj):
    off_kv, off_kr, off_ssm = Q_LORA, Q_LORA + KV_LORA, Q_LORA + KV_LORA + QK_ROPE
    wi = w_in[i]
    kr_block = jnp.zeros((D_MODEL, HEAD_PAD), F32).at[:, ROPE_LANE0:ROPE_LANE0 + QK_ROPE].set(wi[:, off_kr:off_ssm])
    w_in_p = jnp.concatenate([wi[:, :off_kr], kr_block, wi[:, off_ssm:]], axis=1)
    wq = w_q_up[i].reshape(Q_LORA, N_HEADS, QK_NOPE + QK_ROPE) * (ATTN_SCALE * math.log2(math.e))
    wq = jnp.pad(wq, ((0, 0), (0, 0), (0, HEAD_PAD - QK_NOPE - QK_ROPE))).reshape(Q_LORA, N_HEADS * HEAD_PAD)
    wkv = w_kv_up[i].reshape(KV_LORA, N_HEADS, QK_NOPE + V_HEAD)
    wk = jnp.pad(wkv[..., :QK_NOPE], ((0, 0), (0, 0), (0, HEAD_PAD - QK_NOPE))).reshape(KV_LORA, N_HEADS * HEAD_PAD)
    wv = jnp.pad(wkv[..., QK_NOPE:], ((0, 0), (0, 0), (0, HEAD_PAD - V_HEAD))).reshape(KV_LORA, N_HEADS * HEAD_PAD)
    lam = lax.complex(ssm_a_re[i], ssm_a_im[i])
    lam_bar = jnp.exp(lam * jnp.exp(ssm_log_step[i])[:, None])
    b_fac = (lam_bar - 1.0) / lam
    bc = lax.complex(ssm_b_re[i], ssm_b_im[i]) * b_fac[:, :, None]
    n_per = SSM_GROUPS // S5_SLABS

    def block_diag(blocks):
        s, n, a, b = blocks.shape
        eye = jnp.eye(n, dtype=blocks.dtype)
        return jnp.einsum("snab,nm->snamb", blocks, eye).reshape(s, n * a, n * b)

    b_hp_re = jnp.real(bc).transpose(0, 2, 1).reshape(S5_SLABS, n_per, SSM_CH, SSM_STATE)
    b_hp_im = jnp.imag(bc).transpose(0, 2, 1).reshape(S5_SLABS, n_per, SSM_CH, SSM_STATE)
    s5_wb = jnp.concatenate([block_diag(b_hp_re), block_diag(b_hp_im)], axis=-1)
    c_re = ssm_c_re[i].transpose(0, 2, 1).reshape(S5_SLABS, n_per, SSM_STATE, SSM_CH)
    c_im = ssm_c_im[i].transpose(0, 2, 1).reshape(S5_SLABS, n_per, SSM_STATE, SSM_CH)
    s5_wc = jnp.stack([block_diag(c_re), -block_diag(c_im)])
    w_r = jnp.zeros((D_MODEL, LANES), F32)
    w_r = w_r.at[:, :N_EXPERT_GROUPS].set(w_group_router[i])
    w_r = w_r.at[:, ROUTE_E0:ROUTE_E0 + N_EXPERTS].set(w_expert_router[i])
    w_r_hi = w_r.astype(BF16)
    w_r_lo = (w_r - w_r_hi.astype(F32)).astype(BF16)
    b_r = jnp.zeros((1, LANES), F32)
    b_r = b_r.at[0, :N_EXPERT_GROUPS].set(b_group_router[i])
    b_r = b_r.at[0, ROUTE_E0:ROUTE_E0 + N_EXPERTS].set(b_expert_router[i])
    return dict(
        g_mix=g_mix_norm[i][None], w_in=w_in_p.astype(BF16),
        g_q=g_q_lat[i][None], w_q=wq.astype(BF16),
        g_kv=g_kv_lat[i][None], w_kv=jnp.concatenate([wk, wv], axis=1).astype(BF16),
        s5_wb=s5_wb.astype(BF16), s5_wc=s5_wc.astype(BF16),
        s5_are=jnp.real(lam_bar).reshape(1, N_STATE), s5_aim=jnp.imag(lam_bar).reshape(1, N_STATE),
        s5_d=ssm_d[i].reshape(1, SSM_WIDTH), w_glu=w_glu[i].astype(BF16), b_glu=b_glu[i][None],
        g_attn=g_attn_out[i][None], g_ssm=g_ssm_out[i][None],
        w_out=w_out[i].astype(BF16), g_moe=g_moe_norm[i][None],
        w_r_hi=w_r_hi, w_r_lo=w_r_lo, b_r=b_r,
        g_ple=g_ple_norm[i][None], w_ple_gate=w_ple_gate[i].astype(BF16),
        w_ple_proj=w_ple_proj[i].astype(BF16),
    )


def _slot_layout(route, cnt, n_blocks, te):
    counts = cnt[0, :N_EXPERTS].astype(I32)
    pcounts = (counts + te - 1) // te * te
    pends = jnp.cumsum(pcounts)
    pstarts = pends - pcounts
    eid = route[:, :TOP_K]
    rank = route[:, TOP_K:2 * TOP_K]
    start = jnp.sum(jnp.where(eid[:, :, None] == jnp.arange(N_EXPERTS)[None, None, :],
                              pstarts[None, None, :], 0), axis=-1)
    dest = (start + rank).reshape(-1).astype(I32)
    blk0 = jnp.arange(n_blocks, dtype=I32) * te
    block_e = jnp.minimum(jnp.sum((pends[None, :] <= blk0[:, None]).astype(I32), axis=1), N_EXPERTS - 1)
    fill = jnp.clip((pstarts + counts)[block_e] - blk0, 0, te)
    zflag = (fill < te).astype(I32)
    return dest, block_e, zflag


def kernel(x, p, positions, g_mix_norm, w_in, g_q_lat, w_q_up, g_kv_lat, w_kv_up, ssm_a_re, ssm_a_im, ssm_b_re, ssm_b_im, ssm_c_re, ssm_c_im, ssm_d, ssm_log_step, w_glu, b_glu, g_attn_out, g_ssm_out, w_out, g_moe_norm, w_group_router, b_group_router, w_expert_router, b_expert_router, w_exp_gate, w_exp_up, w_exp_down, g_ple_norm, w_ple_gate, w_ple_proj, g_final):
    bsz, seq, _ = x.shape
    depth = p.shape[0]
    n_tok = bsz * seq
    t = _tiles(n_tok, seq)
    n_slots = TOP_K * n_tok + N_EXPERTS * t["te"]
    n_blocks = n_slots // t["te"]

    cosf, sinf = _rope_tables(positions.reshape(n_tok, 1), t["tm"])
    sub = min(SUB_ROWS, t["tm"])
    tril = jnp.tril(jnp.ones((sub, sub), F32), k=-1).astype(BF16)
    g_fin = g_final[None]

    h = x.reshape(n_tok, D_MODEL)
    for i in range(depth):
        wts = _layer_weights(i, g_mix_norm, w_in, g_q_lat, w_q_up, g_kv_lat, w_kv_up,
                             ssm_a_re, ssm_a_im, ssm_b_re, ssm_b_im, ssm_c_re, ssm_c_im, ssm_d,
                             ssm_log_step, w_glu, b_glu, g_attn_out, g_ssm_out, w_out, g_moe_norm,
                             w_group_router, b_group_router, w_expert_router, b_expert_router,
                             g_ple_norm, w_ple_gate, w_ple_proj)
        q, k, v, u_tm = _inproj(h, cosf, sinf, wts, bsz, seq, t["tm"])
        attn_n = _attention(q, k, v, wts["g_attn"], bsz, seq, t["tq"])
        ssm_n = _s5(u_tm.reshape(seq, bsz, SSM_WIDTH), wts, bsz, seq, t["lc"])
        h1, xn, route, rw, cnt = _outproj(h, attn_n, ssm_n.reshape(seq, bsz * SSM_WIDTH), wts, tril,
                                          bsz, seq, t["tm"])
        dest, block_e, zflag = _slot_layout(route, cnt, n_blocks, t["te"])
        xbuf = _dispatch(zflag, dest, xn, n_slots, t["td"], t["te"])
        ybuf = _experts(block_e, xbuf, w_exp_gate, w_exp_up, w_exp_down, i, t["te"])
        h = _combine(dest, ybuf, h1, rw, p.reshape(depth, n_tok, PLE_DIM), i, wts, g_fin, t["tf"],
                     final=(i == depth - 1))
    return h.reshape(bsz, seq, D_MODEL)
```

```python
import functools
import math

import jax
import jax.numpy as jnp
from jax import lax
from jax.experimental import pallas as pl
from jax.experimental.pallas import tpu as pltpu

F32 = jnp.float32
BF16 = jnp.bfloat16
I32 = jnp.int32
U32 = jnp.uint32

D_MODEL = 1024
MLA_WIDTH = 512
SSM_WIDTH = 512
V_HEAD = 64
N_HEADS = 8
QK_NOPE = 64
QK_ROPE = 32
HALF_ROPE = QK_ROPE // 2
Q_LORA = 256
KV_LORA = 128
ROPE_BASE = 10000.0
ATTN_SCALE = 1.0 / math.sqrt(QK_NOPE + QK_ROPE)
SSM_CH = 16
SSM_GROUPS = 32
SSM_STATE = 64
N_EXPERT_GROUPS = 4
EXPERTS_PER_GROUP = 8
N_EXPERTS = 32
TOP_K = 2
D_EXPERT = 256
PLE_DIM = 256
EPS = 1e-6

LANES = 128
SUBLANES = 8
HEAD_PAD = 128
ROPE_LANE0 = QK_NOPE
N_STATE = SSM_GROUPS * SSM_STATE
PACKED = D_MODEL // 2
NEG = -1e30
VMEM_LIMIT = 56 * 1024 * 1024
SUB_ROWS = 512


def _tiles(n_tok, seq):
    return dict(
        tm=min(1024, seq),
        tq=min(512, seq),
        lc=min(32, seq),
        te=512,
        td=min(512, n_tok),
        tf=min(512, n_tok),
    )


def _rms(x, g):
    ms = jnp.mean(x * x, axis=-1, keepdims=True)
    return x * lax.rsqrt(ms + EPS) * g


def _sigmoid(x):
    return 1.0 / (1.0 + jnp.exp(-x))


def _pack_bf16_pairs(x):
    lo = lax.bitcast_convert_type(x[:, :PACKED].astype(BF16).astype(F32), U32)
    hi = lax.bitcast_convert_type(x[:, PACKED:].astype(BF16).astype(F32), U32)
    return (lo >> 16) | hi


def _unpack_bf16_pairs(w):
    lo = lax.bitcast_convert_type(w << 16, F32)
    hi = lax.bitcast_convert_type(w & jnp.uint32(0xFFFF0000), F32)
    return jnp.concatenate([lo, hi], axis=1)


def _cparams(sem):
    return pltpu.CompilerParams(dimension_semantics=sem, vmem_limit_bytes=VMEM_LIMIT)


def _full(shape):
    nd = len(shape)
    return pl.BlockSpec(shape, lambda *_: (0,) * nd)


def _rope_table_kernel(pos_ref, invf_ref, cos_ref, sin_ref):
    ang = pos_ref[...].astype(F32) * invf_ref[...]
    cos_ref[...] = jnp.cos(ang)
    sin_ref[...] = jnp.sin(ang)


def _rope_tables(pos_col, tm):
    n_tok = pos_col.shape[0]
    lane = jnp.arange(LANES)
    in_rope = (lane >= ROPE_LANE0) & (lane < ROPE_LANE0 + QK_ROPE)
    freq = ROPE_BASE ** (-((lane - ROPE_LANE0) % HALF_ROPE).astype(F32) / HALF_ROPE)
    invf = jnp.where(in_rope, freq, 0.0).astype(F32)[None, :]
    return pl.pallas_call(
        _rope_table_kernel,
        out_shape=(jax.ShapeDtypeStruct((n_tok, LANES), F32),) * 2,
        grid=(n_tok // tm,),
        in_specs=[pl.BlockSpec((tm, 1), lambda i: (i, 0)), _full((1, LANES))],
        out_specs=(pl.BlockSpec((tm, LANES), lambda i: (i, 0)),) * 2,
        compiler_params=_cparams(("parallel",)),
        name="rope_tables",
    )(pos_col, invf)


def _rope(x, cosf, s_lo, s_hi):
    n = x.shape[-1]
    return (x * cosf + pltpu.roll(x, n - HALF_ROPE, 1) * s_lo + pltpu.roll(x, HALF_ROPE, 1) * s_hi)


def _row_chunks(n_rows):
    sub = min(SUB_ROWS, n_rows)
    return [slice(c * sub, (c + 1) * sub) for c in range(n_rows // sub)]


def _inproj_kernel(h_ref, cos_ref, sin_ref, gmix_ref, win_ref, gq_ref, wq_ref, gkv_ref, wkv_ref,
                   q_ref, k_ref, v_ref, u_ref):
    lane = lax.broadcasted_iota(I32, (1, LANES), 1)
    ones_lane = jnp.where(lane == V_HEAD, 1.0, 0.0)
    for r in _row_chunks(h_ref.shape[0]):
        xn = _rms(h_ref[r, :], gmix_ref[...]).astype(BF16)
        z = jnp.dot(xn, win_ref[...], preferred_element_type=F32)
        cosf = cos_ref[r, :]
        sinf = sin_ref[r, :]
        s_lo = jnp.where((lane >= ROPE_LANE0) & (lane < ROPE_LANE0 + HALF_ROPE), -sinf, 0.0)
        s_hi = jnp.where((lane >= ROPE_LANE0 + HALF_ROPE) & (lane < ROPE_LANE0 + QK_ROPE), sinf, 0.0)
        qn = _rms(z[:, :Q_LORA], gq_ref[...]).astype(BF16)
        q = jnp.dot(qn, wq_ref[...], preferred_element_type=F32)
        kvn = _rms(z[:, Q_LORA:Q_LORA + KV_LORA], gkv_ref[...]).astype(BF16)
        kv = jnp.dot(kvn, wkv_ref[...], preferred_element_type=F32)
        kpe = _rope(z[:, Q_LORA + KV_LORA:Q_LORA + KV_LORA + HEAD_PAD], cosf, s_lo, s_hi)
        for hd in range(N_HEADS):
            sl = slice(hd * HEAD_PAD, (hd + 1) * HEAD_PAD)
            vsl = slice(N_HEADS * HEAD_PAD + hd * HEAD_PAD, N_HEADS * HEAD_PAD + (hd + 1) * HEAD_PAD)
            q_ref[r, sl] = _rope(q[:, sl], cosf, s_lo, s_hi).astype(BF16)
            k_ref[r, sl] = (kv[:, sl] + kpe).astype(BF16)
            v_ref[r, sl] = (kv[:, vsl] + ones_lane).astype(BF16)
        u_ref[r, :] = z[:, Q_LORA + KV_LORA + HEAD_PAD:]


def _inproj(h, cosf, sinf, wts, bsz, seq, tm):
    n_tok = bsz * seq
    nl = seq // tm
    row = lambda b, l: (b * nl + l, 0)
    qk_w = N_HEADS * HEAD_PAD
    return pl.pallas_call(
        _inproj_kernel,
        out_shape=(jax.ShapeDtypeStruct((n_tok, qk_w), BF16),
                   jax.ShapeDtypeStruct((n_tok, qk_w), BF16),
                   jax.ShapeDtypeStruct((n_tok, qk_w), BF16),
                   jax.ShapeDtypeStruct((seq, bsz * SSM_WIDTH), F32)),
        grid=(bsz, nl),
        in_specs=[pl.BlockSpec((tm, D_MODEL), row),
                  pl.BlockSpec((tm, LANES), row), pl.BlockSpec((tm, LANES), row),
                  _full((1, D_MODEL)), _full((D_MODEL, D_MODEL)),
                  _full((1, Q_LORA)), _full((Q_LORA, qk_w)),
                  _full((1, KV_LORA)), _full((KV_LORA, 2 * qk_w))],
        out_specs=(pl.BlockSpec((tm, qk_w), row), pl.BlockSpec((tm, qk_w), row),
                   pl.BlockSpec((tm, qk_w), row),
                   pl.BlockSpec((tm, SSM_WIDTH), lambda b, l: (l, b))),
        compiler_params=_cparams(("parallel", "parallel")),
        name="inproj",
    )(h, cosf, sinf, wts["g_mix"], wts["w_in"], wts["g_q"], wts["w_q"], wts["g_kv"], wts["w_kv"])


ATTN_RB = 32


def _attn_kernel(q_ref, k_ref, v_ref, g_ref, o_ref, s_ref, p_ref, m_ref, al_ref, acc_ref, out_ref, *, tq):
    qi = pl.program_id(1)
    col = lax.broadcasted_iota(I32, (ATTN_RB, tq), 1)
    row = lax.broadcasted_iota(I32, (ATTN_RB, tq), 0)
    for pr in range(N_HEADS // 2):
        heads = [2 * pr, 2 * pr + 1]
        m_ref[...] = jnp.full(m_ref.shape, NEG, F32)
        acc_ref[...] = jnp.zeros(acc_ref.shape, F32)

        def scores(j, slot, heads=heads):
            rows = pl.ds(pl.multiple_of(j * tq, tq), tq)
            for a, hd in enumerate(heads):
                hs = slice(hd * HEAD_PAD, (hd + 1) * HEAD_PAD)
                s_ref[slot, a] = lax.dot_general(q_ref[:, hs], k_ref[rows, hs], (((1,), (1,)), ((), ())),
                                                 preferred_element_type=F32)

        def softmax_pv(j, slot, masked, heads=heads):
            rows = pl.ds(pl.multiple_of(j * tq, tq), tq)
            for a in range(2):
                for c in range(tq // ATTN_RB):
                    r = slice(c * ATTN_RB, (c + 1) * ATTN_RB)
                    s = s_ref[slot, a, r, :]
                    if masked:
                        s = jnp.where(col <= row + c * ATTN_RB, s, NEG)
                    m_old = m_ref[a, r, :]
                    m_new = jnp.maximum(m_old, jnp.max(s, axis=-1, keepdims=True))
                    p_ref[a, r, :] = jnp.exp2(s - m_new).astype(BF16)
                    al_ref[a, r, :] = jnp.exp2(m_old - m_new)
                    m_ref[a, r, :] = m_new
            for a, hd in enumerate(heads):
                pv = jnp.dot(p_ref[a], v_ref[rows, hd * HEAD_PAD:(hd + 1) * HEAD_PAD],
                             preferred_element_type=F32)
                acc_ref[a] = al_ref[a] * acc_ref[a] + pv

        def two_tiles(jj, carry):
            t = 2 * jj
            scores(t + 1, 1)
            softmax_pv(t, 0, masked=False)
            scores(t + 2, 0)
            softmax_pv(t + 1, 1, masked=False)
            return carry

        scores(0, 0)
        lax.fori_loop(0, qi // 2, two_tiles, 0)

        @pl.when(qi % 2 == 0)
        def _():
            softmax_pv(qi, 0, masked=True)

        @pl.when(qi % 2 == 1)
        def _():
            scores(qi, 1)
            softmax_pv(qi - 1, 0, masked=False)
            softmax_pv(qi, 1, masked=True)

        for a, hd in enumerate(heads):
            acc = acc_ref[a]
            out_ref[:, hd * V_HEAD:(hd + 1) * V_HEAD] = acc[:, :V_HEAD] / acc[:, V_HEAD:V_HEAD + 1]
    o_ref[...] = _rms(out_ref[...], g_ref[...]).astype(BF16)


def _attention(q, k, v, g_attn, bsz, seq, tq):
    n_tok = bsz * seq
    nq = seq // tq
    qk_w = N_HEADS * HEAD_PAD
    return pl.pallas_call(
        functools.partial(_attn_kernel, tq=tq),
        out_shape=jax.ShapeDtypeStruct((n_tok, MLA_WIDTH), BF16),
        grid=(bsz, nq),
        in_specs=[pl.BlockSpec((tq, qk_w), lambda b, i: (b * nq + i, 0)),
                  pl.BlockSpec((seq, qk_w), lambda b, i: (b, 0)),
                  pl.BlockSpec((seq, qk_w), lambda b, i: (b, 0)),
                  _full((1, MLA_WIDTH))],
        out_specs=pl.BlockSpec((tq, MLA_WIDTH), lambda b, i: (b * nq + i, 0)),
        scratch_shapes=[pltpu.VMEM((2, 2, tq, tq), F32), pltpu.VMEM((2, tq, tq), BF16),
                        pltpu.VMEM((2, tq, 1), F32), pltpu.VMEM((2, tq, 1), F32),
                        pltpu.VMEM((2, tq, HEAD_PAD), F32), pltpu.VMEM((tq, MLA_WIDTH), F32)],
        compiler_params=_cparams(("parallel", "parallel")),
        name="attn",
    )(q, k, v, g_attn)


S5_SLABS = 4
S5_SLAB_U = SSM_WIDTH // S5_SLABS
S5_SLAB_X = N_STATE // S5_SLABS


def _s5_kernel(u_ref, wb_ref, are_ref, aim_ref, wc_ref, d_ref, wglu_ref, bglu_ref, g_ref,
               o_ref, xr_ref, xi_ref, sr_ref, si_ref, *, lc, nb):
    @pl.when(pl.program_id(0) == 0)
    def _():
        xr_ref[...] = jnp.zeros_like(xr_ref)
        xi_ref[...] = jnp.zeros_like(xi_ref)

    rows = lc * nb
    u = u_ref[...].reshape(rows, SSM_WIDTH)
    ub = u.astype(BF16)
    for j in range(S5_SLABS):
        bu = jnp.dot(ub[:, j * S5_SLAB_U:(j + 1) * S5_SLAB_U], wb_ref[j], preferred_element_type=F32)
        sr_ref[:, j * S5_SLAB_X:(j + 1) * S5_SLAB_X] = bu[:, :S5_SLAB_X]
        si_ref[:, j * S5_SLAB_X:(j + 1) * S5_SLAB_X] = bu[:, S5_SLAB_X:]
    for c in range(S5_SLABS):
        sl = slice(c * S5_SLAB_X, (c + 1) * S5_SLAB_X)
        ar = jnp.broadcast_to(are_ref[:, sl], (nb, S5_SLAB_X))
        ai = jnp.broadcast_to(aim_ref[:, sl], (nb, S5_SLAB_X))

        def scan_step(t, carry, sl=sl, ar=ar, ai=ai):
            xr, xi = carry
            r = pl.ds(pl.multiple_of(t * nb, nb), nb)
            nr = ar * xr - ai * xi + sr_ref[r, sl]
            ni = ar * xi + ai * xr + si_ref[r, sl]
            sr_ref[r, sl] = nr
            si_ref[r, sl] = ni
            return nr, ni

        xr, xi = lax.fori_loop(0, lc, scan_step, (xr_ref[:, sl], xi_ref[:, sl]), unroll=4)
        xr_ref[:, sl] = xr
        xi_ref[:, sl] = xi
    ys = []
    for j in range(S5_SLABS):
        sl = slice(j * S5_SLAB_X, (j + 1) * S5_SLAB_X)
        ys.append(jnp.dot(sr_ref[:, sl].astype(BF16), wc_ref[0, j], preferred_element_type=F32)
                  + jnp.dot(si_ref[:, sl].astype(BF16), wc_ref[1, j], preferred_element_type=F32))
    y = jnp.concatenate(ys, axis=-1) + d_ref[...] * u
    y = 0.5 * y * (1.0 + jnp.tanh(math.sqrt(2.0 / math.pi) * (y + 0.044715 * (y * y * y))))
    gate = jnp.dot(y.astype(BF16), wglu_ref[...], preferred_element_type=F32) + bglu_ref[...]
    y = y * _sigmoid(gate)
    o_ref[...] = _rms(y, g_ref[...]).astype(BF16).reshape(lc, nb, SSM_WIDTH)


def _s5(u_tm, wts, bsz, seq, lc):
    rows = lc * bsz
    return pl.pallas_call(
        functools.partial(_s5_kernel, lc=lc, nb=bsz),
        out_shape=jax.ShapeDtypeStruct((seq, bsz, SSM_WIDTH), BF16),
        grid=(seq // lc,),
        in_specs=[pl.BlockSpec((lc, bsz, SSM_WIDTH), lambda i: (i, 0, 0)),
                  _full((S5_SLABS, S5_SLAB_U, 2 * S5_SLAB_X)),
                  _full((1, N_STATE)), _full((1, N_STATE)),
                  _full((2, S5_SLABS, S5_SLAB_X, S5_SLAB_U)),
                  _full((1, SSM_WIDTH)), _full((SSM_WIDTH, SSM_WIDTH)), _full((1, SSM_WIDTH)),
                  _full((1, SSM_WIDTH))],
        out_specs=pl.BlockSpec((lc, bsz, SSM_WIDTH), lambda i: (i, 0, 0)),
        scratch_shapes=[pltpu.VMEM((bsz, N_STATE), F32), pltpu.VMEM((bsz, N_STATE), F32),
                        pltpu.VMEM((rows, N_STATE), F32), pltpu.VMEM((rows, N_STATE), F32)],
        compiler_params=_cparams(("arbitrary",)),
        name="s5",
    )(u_tm, wts["s5_wb"], wts["s5_are"], wts["s5_aim"], wts["s5_wc"], wts["s5_d"],
      wts["w_glu"], wts["b_glu"], wts["g_ssm"])


ROUTE_E0 = N_EXPERT_GROUPS


def _outproj_kernel(h_ref, a_ref, s_ref, wout_ref, gmoe_ref, wrh_ref, wrl_ref, br_ref, tril_ref,
                    h1_ref, xn_ref, route_ref, rw_ref, cnt_ref):
    @pl.when((pl.program_id(0) == 0) & (pl.program_id(1) == 0))
    def _():
        cnt_ref[...] = jnp.zeros_like(cnt_ref)

    sub = tril_ref.shape[0]
    lane = lax.broadcasted_iota(I32, (sub, LANES), 1).astype(F32)
    big = float(LANES)
    for r in _row_chunks(h_ref.shape[0]):
        h1 = (h_ref[r, :]
              + jnp.dot(a_ref[r, :], wout_ref[:MLA_WIDTH, :], preferred_element_type=F32)
              + jnp.dot(s_ref[r, :], wout_ref[MLA_WIDTH:, :], preferred_element_type=F32))
        h1_ref[r, :] = h1
        xn = _rms(h1, gmoe_ref[...])
        xn_ref[r, :] = _pack_bf16_pairs(xn)
        x_hi = xn.astype(BF16)
        x_lo = (xn - x_hi.astype(F32)).astype(BF16)
        lg = (jnp.dot(x_hi, wrh_ref[...], preferred_element_type=F32)
              + (jnp.dot(x_lo, wrh_ref[...], preferred_element_type=F32)
                 + jnp.dot(x_hi, wrl_ref[...], preferred_element_type=F32))) + br_ref[...]
        gl = jnp.where(lane < N_EXPERT_GROUPS, lg, NEG)
        gmax = jnp.max(gl, axis=-1, keepdims=True)
        gsum = jnp.sum(jnp.exp(gl - gmax), axis=-1, keepdims=True)
        g_top_p = 1.0 / gsum
        gidx = jnp.min(jnp.where(gl == gmax, lane, big), axis=-1, keepdims=True)
        lo = ROUTE_E0 + EXPERTS_PER_GROUP * gidx
        sel = jnp.where((lane >= lo) & (lane < lo + EXPERTS_PER_GROUP), lg, NEG)
        m1 = jnp.max(sel, axis=-1, keepdims=True)
        i1 = jnp.min(jnp.where(sel == m1, lane, big), axis=-1, keepdims=True)
        sel2 = jnp.where(lane == i1, NEG, sel)
        m2 = jnp.max(sel2, axis=-1, keepdims=True)
        i2 = jnp.min(jnp.where(sel2 == m2, lane, big), axis=-1, keepdims=True)
        e21 = jnp.exp(m2 - m1)
        w1 = g_top_p / (1.0 + e21)
        w2 = g_top_p * e21 / (1.0 + e21)
        e1 = i1 - ROUTE_E0
        e2 = i2 - ROUTE_E0
        oh = jnp.where((lane == e1) | (lane == e2), 1.0, 0.0)
        before = cnt_ref[...] + jnp.dot(tril_ref[...], oh.astype(BF16), preferred_element_type=F32)
        r1 = jnp.sum(jnp.where(lane == e1, before, 0.0), axis=-1, keepdims=True)
        r2 = jnp.sum(jnp.where(lane == e2, before, 0.0), axis=-1, keepdims=True)
        cnt_ref[...] = cnt_ref[...] + jnp.sum(oh, axis=0, keepdims=True)
        route = jnp.where(lane == 0, e1, jnp.where(lane == 1, e2, jnp.where(lane == 2, r1, r2)))
        route_ref[r, :] = route.astype(I32)
        rw_ref[r, :] = jnp.where(lane == 0, w1, w2)


def _outproj(h, attn_n, ssm_tm, wts, tril, bsz, seq, tm):
    n_tok = bsz * seq
    nl = seq // tm
    row = lambda b, l: (b * nl + l, 0)
    return pl.pallas_call(
        _outproj_kernel,
        out_shape=(jax.ShapeDtypeStruct((n_tok, D_MODEL), F32),
                   jax.ShapeDtypeStruct((n_tok, PACKED), U32),
                   jax.ShapeDtypeStruct((n_tok, LANES), I32),
                   jax.ShapeDtypeStruct((n_tok, LANES), F32),
                   jax.ShapeDtypeStruct((1, LANES), F32)),
        grid=(bsz, nl),
        in_specs=[pl.BlockSpec((tm, D_MODEL), row),
                  pl.BlockSpec((tm, MLA_WIDTH), row),
                  pl.BlockSpec((tm, SSM_WIDTH), lambda b, l: (l, b)),
                  _full((D_MODEL, D_MODEL)), _full((1, D_MODEL)),
                  _full((D_MODEL, LANES)), _full((D_MODEL, LANES)), _full((1, LANES)),
                  _full(tril.shape)],
        out_specs=(pl.BlockSpec((tm, D_MODEL), row), pl.BlockSpec((tm, PACKED), row),
                   pl.BlockSpec((tm, LANES), row), pl.BlockSpec((tm, LANES), row),
                   _full((1, LANES))),
        compiler_params=_cparams(("arbitrary", "arbitrary")),
        name="outproj",
    )(h, attn_n, ssm_tm, wts["w_out"], wts["g_moe"], wts["w_r_hi"], wts["w_r_lo"], wts["b_r"], tril)


def _row_copy(src, src_row, dst, dst_row, sem):
    return pltpu.make_async_copy(src.at[pl.ds(src_row, 1)], dst.at[pl.ds(dst_row, 1)], sem)


def _dispatch_kernel(zflag_ref, dest_ref, xn_ref, xbuf_ref, zero_ref, sems, zsem, *, td, te, n_blocks):
    def zero_copy(b):
        return pltpu.make_async_copy(zero_ref, xbuf_ref.at[pl.ds(pl.multiple_of(b * te, te), te)], zsem)

    @pl.when(pl.program_id(0) == 0)
    def _():
        zero_ref[...] = jnp.zeros_like(zero_ref)

        def start(b, _):
            @pl.when(zflag_ref[b] != 0)
            def _():
                zero_copy(b).start()
            return 0

        def wait(b, _):
            @pl.when(zflag_ref[b] != 0)
            def _():
                zero_copy(b).wait()
            return 0

        lax.fori_loop(0, n_blocks, start, 0)
        lax.fori_loop(0, n_blocks, wait, 0)

    def issue(g, _):
        r0 = pl.multiple_of(g * SUBLANES, SUBLANES)
        for u in range(SUBLANES):
            for k in range(TOP_K):
                _row_copy(xn_ref, r0 + u, xbuf_ref, dest_ref[TOP_K * (r0 + u) + k],
                          sems.at[k]).start(priority=k)
        return 0

    lax.fori_loop(0, td // SUBLANES, issue, 0)
    for k in range(TOP_K):
        pltpu.make_async_copy(xn_ref, xbuf_ref.at[pl.ds(0, td)], sems.at[k]).wait()


def _dispatch(zflag, dest_flat, xn, n_slots, td, te):
    n_tok = xn.shape[0]
    n_blocks = n_slots // te
    grid_spec = pltpu.PrefetchScalarGridSpec(
        num_scalar_prefetch=1,
        grid=(n_tok // td,),
        in_specs=[pl.BlockSpec((TOP_K * td,), lambda i, zf: (i,), memory_space=pltpu.SMEM),
                  pl.BlockSpec((td, PACKED), lambda i, zf: (i, 0))],
        out_specs=pl.BlockSpec(memory_space=pl.ANY),
        scratch_shapes=[pltpu.VMEM((te, PACKED), U32), pltpu.SemaphoreType.DMA((TOP_K,)),
                        pltpu.SemaphoreType.DMA],
    )
    return pl.pallas_call(
        functools.partial(_dispatch_kernel, td=td, te=te, n_blocks=n_blocks),
        out_shape=jax.ShapeDtypeStruct((n_slots, PACKED), U32),
        grid_spec=grid_spec,
        compiler_params=_cparams(("arbitrary",)),
        name="dispatch",
    )(zflag, dest_flat, xn)


def _experts_kernel(be_ref, x_ref, wg_ref, wu_ref, wd_ref, y_ref):
    del be_ref
    xb = _unpack_bf16_pairs(x_ref[...]).astype(BF16)
    g = jnp.dot(xb, wg_ref[...].astype(BF16), preferred_element_type=F32)
    u = jnp.dot(xb, wu_ref[...].astype(BF16), preferred_element_type=F32)
    hdn = (g * _sigmoid(g) * u).astype(BF16)
    y_ref[...] = _pack_bf16_pairs(jnp.dot(hdn, wd_ref[...].astype(BF16), preferred_element_type=F32))


def _experts(block_e, xbuf, w_gate, w_up, w_down, layer, te):
    n_slots = xbuf.shape[0]
    grid_spec = pltpu.PrefetchScalarGridSpec(
        num_scalar_prefetch=1,
        grid=(n_slots // te,),
        in_specs=[pl.BlockSpec((te, PACKED), lambda i, be: (i, 0)),
                  pl.BlockSpec((None, None, D_MODEL, D_EXPERT), lambda i, be: (layer, be[i], 0, 0)),
                  pl.BlockSpec((None, None, D_MODEL, D_EXPERT), lambda i, be: (layer, be[i], 0, 0)),
                  pl.BlockSpec((None, None, D_EXPERT, D_MODEL), lambda i, be: (layer, be[i], 0, 0))],
        out_specs=pl.BlockSpec((te, PACKED), lambda i, be: (i, 0)),
    )
    return pl.pallas_call(
        _experts_kernel,
        out_shape=jax.ShapeDtypeStruct((n_slots, PACKED), U32),
        grid_spec=grid_spec,
        compiler_params=_cparams(("arbitrary",)),
        name="experts",
    )(block_e, xbuf, w_gate, w_up, w_down)


def _combine_kernel(dest_ref, dnext_ref, ybuf_ref, h1_ref, rw_ref, p_ref, gple_ref, wgate_ref, wproj_ref,
                    gfin_ref, o_ref, yv_ref, sems, *, tf, final):
    i = pl.program_id(0)
    slot = i % 2

    def gather(idx_ref, into):
        def issue(g, _):
            r0 = pl.multiple_of(g * SUBLANES, SUBLANES)
            for u in range(SUBLANES):
                for k in range(TOP_K):
                    _row_copy(ybuf_ref, idx_ref[TOP_K * (r0 + u) + k], yv_ref.at[into, k], r0 + u,
                              sems.at[into, k]).start(priority=k)
            return 0

        lax.fori_loop(0, tf // SUBLANES, issue, 0)

    @pl.when(i == 0)
    def _():
        gather(dest_ref, 0)

    @pl.when(i + 1 < pl.num_programs(0))
    def _():
        gather(dnext_ref, 1 - slot)

    for k in range(TOP_K):
        pltpu.make_async_copy(ybuf_ref.at[pl.ds(0, tf)], yv_ref.at[slot, k], sems.at[slot, k]).wait()
    rw = rw_ref[...]
    h2 = (h1_ref[...] + rw[:, 0:1] * _unpack_bf16_pairs(yv_ref[slot, 0])
          + rw[:, 1:2] * _unpack_bf16_pairs(yv_ref[slot, 1]))
    hn = _rms(h2, gple_ref[...]).astype(BF16)
    gate = _sigmoid(jnp.dot(hn, wgate_ref[...], preferred_element_type=F32))
    ple = jnp.dot(p_ref[...].astype(BF16), wproj_ref[...], preferred_element_type=F32)
    h3 = h2 + ple * gate
    if final:
        h3 = _rms(h3, gfin_ref[...])
    o_ref[...] = h3


def _combine(dest_flat, ybuf, h1, rw, p_all, layer, wts, g_final, tf, final):
    n_tok = h1.shape[0]
    n_steps = n_tok // tf
    row = lambda i: (i, 0)
    return pl.pallas_call(
        functools.partial(_combine_kernel, tf=tf, final=final),
        out_shape=jax.ShapeDtypeStruct((n_tok, D_MODEL), F32),
        grid=(n_steps,),
        in_specs=[pl.BlockSpec((TOP_K * tf,), lambda i: (i,), memory_space=pltpu.SMEM),
                  pl.BlockSpec((TOP_K * tf,), lambda i: (jnp.minimum(i + 1, n_steps - 1),),
                               memory_space=pltpu.SMEM),
                  pl.BlockSpec(memory_space=pl.ANY),
                  pl.BlockSpec((tf, D_MODEL), row), pl.BlockSpec((tf, LANES), row),
                  pl.BlockSpec((None, tf, PLE_DIM), lambda i: (layer, i, 0)),
                  _full((1, D_MODEL)), _full((D_MODEL, D_MODEL)), _full((PLE_DIM, D_MODEL)),
                  _full((1, D_MODEL))],
        out_specs=pl.BlockSpec((tf, D_MODEL), row),
        scratch_shapes=[pltpu.VMEM((2, TOP_K, tf, PACKED), U32), pltpu.SemaphoreType.DMA((2, TOP_K))],
        compiler_params=_cparams(("arbitrary",)),
        name="combine",
    )(dest_flat, dest_flat, ybuf, h1, rw, p_all, wts["g_ple"], wts["w_ple_gate"], wts["w_ple_proj"], g_final)


def _layer_weights(i, g_mix_norm, w_in, g_q_lat, w_q_up, g_kv_lat, w_kv_up,
                   ssm_a_re, ssm_a_im, ssm_b_re, ssm_b_im, ssm_c_re, ssm_c_im, ssm_d, ssm_log_step,
                   w_glu, b_glu, g_attn_out, g_ssm_out, w_out, g_moe_norm,
                   w_group_router, b_group_router, w_expert_router, b_expert_router,
                   g_ple_norm, w_ple_gate, w_ple_proj):
    off_kv, off_kr, off_ssm = Q_LORA, Q_LORA + KV_LORA, Q_LORA + KV_LORA + QK_ROPE
    wi = w_in[i]
    kr_block = jnp.zeros((D_MODEL, HEAD_PAD), F32).at[:, ROPE_LANE0:ROPE_LANE0 + QK_ROPE].set(wi[:, off_kr:off_ssm])
    w_in_p = jnp.concatenate([wi[:, :off_kr], kr_block, wi[:, off_ssm:]], axis=1)
    wq = w_q_up[i].reshape(Q_LORA, N_HEADS, QK_NOPE + QK_ROPE) * (ATTN_SCALE * math.log2(math.e))
    wq = jnp.pad(wq, ((0, 0), (0, 0), (0, HEAD_PAD - QK_NOPE - QK_ROPE))).reshape(Q_LORA, N_HEADS * HEAD_PAD)
    wkv = w_kv_up[i].reshape(KV_LORA, N_HEADS, QK_NOPE + V_HEAD)
    wk = jnp.pad(wkv[..., :QK_NOPE], ((0, 0), (0, 0), (0, HEAD_PAD - QK_NOPE))).reshape(KV_LORA, N_HEADS * HEAD_PAD)
    wv = jnp.pad(wkv[..., QK_NOPE:], ((0, 0), (0, 0), (0, HEAD_PAD - V_HEAD))).reshape(KV_LORA, N_HEADS * HEAD_PAD)
    lam = lax.complex(ssm_a_re[i], ssm_a_im[i])
    lam_bar = jnp.exp(lam * jnp.exp(ssm_log_step[i])[:, None])
    b_fac = (lam_bar - 1.0) / lam
    bc = lax.complex(ssm_b_re[i], ssm_b_im[i]) * b_fac[:, :, None]
    n_per = SSM_GROUPS // S5_SLABS

    def block_diag(blocks):
        s, n, a, b = blocks.shape
        eye = jnp.eye(n, dtype=blocks.dtype)
        return jnp.einsum("snab,nm->snamb", blocks, eye).reshape(s, n * a, n * b)

    b_hp_re = jnp.real(bc).transpose(0, 2, 1).reshape(S5_SLABS, n_per, SSM_CH, SSM_STATE)
    b_hp_im = jnp.imag(bc).transpose(0, 2, 1).reshape(S5_SLABS, n_per, SSM_CH, SSM_STATE)
    s5_wb = jnp.concatenate([block_diag(b_hp_re), block_diag(b_hp_im)], axis=-1)
    c_re = ssm_c_re[i].transpose(0, 2, 1).reshape(S5_SLABS, n_per, SSM_STATE, SSM_CH)
    c_im = ssm_c_im[i].transpose(0, 2, 1).reshape(S5_SLABS, n_per, SSM_STATE, SSM_CH)
    s5_wc = jnp.stack([block_diag(c_re), -block_diag(c_im)])
    w_r = jnp.zeros((D_MODEL, LANES), F32)
    w_r = w_r.at[:, :N_EXPERT_GROUPS].set(w_group_router[i])
    w_r = w_r.at[:, ROUTE_E0:ROUTE_E0 + N_EXPERTS].set(w_expert_router[i])
    w_r_hi = w_r.astype(BF16)
    w_r_lo = (w_r - w_r_hi.astype(F32)).astype(BF16)
    b_r = jnp.zeros((1, LANES), F32)
    b_r = b_r.at[0, :N_EXPERT_GROUPS].set(b_group_router[i])
    b_r = b_r.at[0, ROUTE_E0:ROUTE_E0 + N_EXPERTS].set(b_expert_router[i])
    return dict(
        g_mix=g_mix_norm[i][None], w_in=w_in_p.astype(BF16),
        g_q=g_q_lat[i][None], w_q=wq.astype(BF16),
        g_kv=g_kv_lat[i][None], w_kv=jnp.concatenate([wk, wv], axis=1).astype(BF16),
        s5_wb=s5_wb.astype(BF16), s5_wc=s5_wc.astype(BF16),
        s5_are=jnp.real(lam_bar).reshape(1, N_STATE), s5_aim=jnp.imag(lam_bar).reshape(1, N_STATE),
        s5_d=ssm_d[i].reshape(1, SSM_WIDTH), w_glu=w_glu[i].astype(BF16), b_glu=b_glu[i][None],
        g_attn=g_attn_out[i][None], g_ssm=g_ssm_out[i][None],
        w_out=w_out[i].astype(BF16), g_moe=g_moe_norm[i][None],
        w_r_hi=w_r_hi, w_r_lo=w_r_lo, b_r=b_r,
        g_ple=g_ple_norm[i][None], w_ple_gate=w_ple_gate[i].astype(BF16),
        w_ple_proj=w_ple_proj[i].astype(BF16),
    )


def _slot_layout(route, cnt, n_blocks, te):
    counts = cnt[0, :N_EXPERTS].astype(I32)
    pcounts = (counts + te - 1) // te * te
    pends = jnp.cumsum(pcounts)
    pstarts = pends - pcounts
    eid = route[:, :TOP_K]
    rank = route[:, TOP_K:2 * TOP_K]
    start = jnp.sum(jnp.where(eid[:, :, None] == jnp.arange(N_EXPERTS)[None, None, :],
                              pstarts[None, None, :], 0), axis=-1)
    dest = (start + rank).reshape(-1).astype(I32)
    blk0 = jnp.arange(n_blocks, dtype=I32) * te
    block_e = jnp.minimum(jnp.sum((pends[None, :] <= blk0[:, None]).astype(I32), axis=1), N_EXPERTS - 1)
    fill = jnp.clip((pstarts + counts)[block_e] - blk0, 0, te)
    zflag = (fill < te).astype(I32)
    return dest, block_e, zflag


def kernel(x, p, positions, g_mix_norm, w_in, g_q_lat, w_q_up, g_kv_lat, w_kv_up, ssm_a_re, ssm_a_im, ssm_b_re, ssm_b_im, ssm_c_re, ssm_c_im, ssm_d, ssm_log_step, w_glu, b_glu, g_attn_out, g_ssm_out, w_out, g_moe_norm, w_group_router, b_group_router, w_expert_router, b_expert_router, w_exp_gate, w_exp_up, w_exp_down, g_ple_norm, w_ple_gate, w_ple_proj, g_final):
    bsz, seq, _ = x.shape
    depth = p.shape[0]
    n_tok = bsz * seq
    t = _tiles(n_tok, seq)
    n_slots = TOP_K * n_tok + N_EXPERTS * t["te"]
    n_blocks = n_slots // t["te"]

    cosf, sinf = _rope_tables(positions.reshape(n_tok, 1), t["tm"])
    sub = min(SUB_ROWS, t["tm"])
    tril = jnp.tril(jnp.ones((sub, sub), F32), k=-1).astype(BF16)
    g_fin = g_final[None]

    h = x.reshape(n_tok, D_MODEL)
    for i in range(depth):
        wts = _layer_weights(i, g_mix_norm, w_in, g_q_lat, w_q_up, g_kv_lat, w_kv_up,
                             ssm_a_re, ssm_a_im, ssm_b_re, ssm_b_im, ssm_c_re, ssm_c_im, ssm_d,
                             ssm_log_step, w_glu, b_glu, g_attn_out, g_ssm_out, w_out, g_moe_norm,
                             w_group_router, b_group_router, w_expert_router, b_expert_router,
                             g_ple_norm, w_ple_gate, w_ple_proj)
        q, k, v, u_tm = _inproj(h, cosf, sinf, wts, bsz, seq, t["tm"])
        attn_n = _attention(q, k, v, wts["g_attn"], bsz, seq, t["tq"])
        ssm_n = _s5(u_tm.reshape(seq, bsz, SSM_WIDTH), wts, bsz, seq, t["lc"])
        h1, xn, route, rw, cnt = _outproj(h, attn_n, ssm_n.reshape(seq, bsz * SSM_WIDTH), wts, tril,
                                          bsz, seq, t["tm"])
        dest, block_e, zflag = _slot_layout(route, cnt, n_blocks, t["te"])
        xbuf = _dispatch(zflag, dest, xn, n_slots, t["td"], t["te"])
        ybuf = _experts(block_e, xbuf, w_exp_gate, w_exp_up, w_exp_down, i, t["te"])
        h = _combine(dest, ybuf, h1, rw, p.reshape(depth, n_tok, PLE_DIM), i, wts, g_fin, t["tf"],
                     final=(i == depth - 1))
    return h.reshape(bsz, seq, D_MODEL)
```

```python
import functools
import math

import jax
import jax.numpy as jnp
from jax import lax
from jax.experimental import pallas as pl
from jax.experimental.pallas import tpu as pltpu

F32 = jnp.float32
BF16 = jnp.bfloat16
I32 = jnp.int32
U32 = jnp.uint32

D_MODEL = 1024
MLA_WIDTH = 512
SSM_WIDTH = 512
V_HEAD = 64
N_HEADS = 8
QK_NOPE = 64
QK_ROPE = 32
HALF_ROPE = QK_ROPE // 2
Q_LORA = 256
KV_LORA = 128
ROPE_BASE = 10000.0
ATTN_SCALE = 1.0 / math.sqrt(QK_NOPE + QK_ROPE)
SSM_CH = 16
SSM_GROUPS = 32
SSM_STATE = 64
N_EXPERT_GROUPS = 4
EXPERTS_PER_GROUP = 8
N_EXPERTS = 32
TOP_K = 2
D_EXPERT = 256
PLE_DIM = 256
EPS = 1e-6

LANES = 128
SUBLANES = 8
HEAD_PAD = 128
ROPE_LANE0 = QK_NOPE
N_STATE = SSM_GROUPS * SSM_STATE
PACKED = D_MODEL // 2
NEG = -1e30
VMEM_LIMIT = 56 * 1024 * 1024
SUB_ROWS = 512


def _tiles(n_tok, seq):
    return dict(
        tm=min(1024, seq),
        tq=min(512, seq),
        lc=min(32, seq),
        te=512,
        td=min(512, n_tok),
        tf=min(512, n_tok),
    )


def _rms(x, g):
    ms = jnp.mean(x * x, axis=-1, keepdims=True)
    return x * lax.rsqrt(ms + EPS) * g


def _sigmoid(x):
    return 1.0 / (1.0 + jnp.exp(-x))


def _pack_bf16_pairs(x):
    lo = lax.bitcast_convert_type(x[:, :PACKED].astype(BF16).astype(F32), U32)
    hi = lax.bitcast_convert_type(x[:, PACKED:].astype(BF16).astype(F32), U32)
    return (lo >> 16) | hi


def _unpack_bf16_pairs(w):
    lo = lax.bitcast_convert_type(w << 16, F32)
    hi = lax.bitcast_convert_type(w & jnp.uint32(0xFFFF0000), F32)
    return jnp.concatenate([lo, hi], axis=1)


def _cparams(sem):
    return pltpu.CompilerParams(dimension_semantics=sem, vmem_limit_bytes=VMEM_LIMIT)


def _full(shape):
    nd = len(shape)
    return pl.BlockSpec(shape, lambda *_: (0,) * nd)


def _rope_table_kernel(pos_ref, invf_ref, cos_ref, sin_ref):
    ang = pos_ref[...].astype(F32) * invf_ref[...]
    cos_ref[...] = jnp.cos(ang)
    sin_ref[...] = jnp.sin(ang)


def _rope_tables(pos_col, tm):
    n_tok = pos_col.shape[0]
    lane = jnp.arange(LANES)
    in_rope = (lane >= ROPE_LANE0) & (lane < ROPE_LANE0 + QK_ROPE)
    freq = ROPE_BASE ** (-((lane - ROPE_LANE0) % HALF_ROPE).astype(F32) / HALF_ROPE)
    invf = jnp.where(in_rope, freq, 0.0).astype(F32)[None, :]
    return pl.pallas_call(
        _rope_table_kernel,
        out_shape=(jax.ShapeDtypeStruct((n_tok, LANES), F32),) * 2,
        grid=(n_tok // tm,),
        in_specs=[pl.BlockSpec((tm, 1), lambda i: (i, 0)), _full((1, LANES))],
        out_specs=(pl.BlockSpec((tm, LANES), lambda i: (i, 0)),) * 2,
        compiler_params=_cparams(("parallel",)),
        name="rope_tables",
    )(pos_col, invf)


def _rope(x, cosf, s_lo, s_hi):
    n = x.shape[-1]
    return (x * cosf + pltpu.roll(x, n - HALF_ROPE, 1) * s_lo + pltpu.roll(x, HALF_ROPE, 1) * s_hi)


def _row_chunks(n_rows):
    sub = min(SUB_ROWS, n_rows)
    return [slice(c * sub, (c + 1) * sub) for c in range(n_rows // sub)]


def _inproj_kernel(h_ref, cos_ref, sin_ref, gmix_ref, win_ref, gq_ref, wq_ref, gkv_ref, wkv_ref,
                   q_ref, k_ref, v_ref, u_ref):
    lane = lax.broadcasted_iota(I32, (1, LANES), 1)
    ones_lane = jnp.where(lane == V_HEAD, 1.0, 0.0)
    for r in _row_chunks(h_ref.shape[0]):
        xn = _rms(h_ref[r, :], gmix_ref[...]).astype(BF16)
        z = jnp.dot(xn, win_ref[...], preferred_element_type=F32)
        cosf = cos_ref[r, :]
        sinf = sin_ref[r, :]
        s_lo = jnp.where((lane >= ROPE_LANE0) & (lane < ROPE_LANE0 + HALF_ROPE), -sinf, 0.0)
        s_hi = jnp.where((lane >= ROPE_LANE0 + HALF_ROPE) & (lane < ROPE_LANE0 + QK_ROPE), sinf, 0.0)
        qn = _rms(z[:, :Q_LORA], gq_ref[...]).astype(BF16)
        q = jnp.dot(qn, wq_ref[...], preferred_element_type=F32)
        kvn = _rms(z[:, Q_LORA:Q_LORA + KV_LORA], gkv_ref[...]).astype(BF16)
        kv = jnp.dot(kvn, wkv_ref[...], preferred_element_type=F32)
        kpe = _rope(z[:, Q_LORA + KV_LORA:Q_LORA + KV_LORA + HEAD_PAD], cosf, s_lo, s_hi)
        for hd in range(N_HEADS):
            sl = slice(hd * HEAD_PAD, (hd + 1) * HEAD_PAD)
            vsl = slice(N_HEADS * HEAD_PAD + hd * HEAD_PAD, N_HEADS * HEAD_PAD + (hd + 1) * HEAD_PAD)
            q_ref[r, sl] = _rope(q[:, sl], cosf, s_lo, s_hi).astype(BF16)
            k_ref[r, sl] = (kv[:, sl] + kpe).astype(BF16)
            v_ref[r, sl] = (kv[:, vsl] + ones_lane).astype(BF16)
        u_ref[r, :] = z[:, Q_LORA + KV_LORA + HEAD_PAD:]


def _inproj(h, cosf, sinf, wts, bsz, seq, tm):
    n_tok = bsz * seq
    nl = seq // tm
    row = lambda b, l: (b * nl + l, 0)
    qk_w = N_HEADS * HEAD_PAD
    return pl.pallas_call(
        _inproj_kernel,
        out_shape=(jax.ShapeDtypeStruct((n_tok, qk_w), BF16),
                   jax.ShapeDtypeStruct((n_tok, qk_w), BF16),
                   jax.ShapeDtypeStruct((n_tok, qk_w), BF16),
                   jax.ShapeDtypeStruct((seq, bsz * SSM_WIDTH), F32)),
        grid=(bsz, nl),
        in_specs=[pl.BlockSpec((tm, D_MODEL), row),
                  pl.BlockSpec((tm, LANES), row), pl.BlockSpec((tm, LANES), row),
                  _full((1, D_MODEL)), _full((D_MODEL, D_MODEL)),
                  _full((1, Q_LORA)), _full((Q_LORA, qk_w)),
                  _full((1, KV_LORA)), _full((KV_LORA, 2 * qk_w))],
        out_specs=(pl.BlockSpec((tm, qk_w), row), pl.BlockSpec((tm, qk_w), row),
                   pl.BlockSpec((tm, qk_w), row),
                   pl.BlockSpec((tm, SSM_WIDTH), lambda b, l: (l, b))),
        compiler_params=_cparams(("parallel", "parallel")),
        name="inproj",
    )(h, cosf, sinf, wts["g_mix"], wts["w_in"], wts["g_q"], wts["w_q"], wts["g_kv"], wts["w_kv"])


ATTN_RB = 32
ATTN_GROUP = 4


def _attn_kernel(q_ref, k_ref, v_ref, g_ref, o_ref, s_ref, p_ref, m_ref, al_ref, acc_ref, out_ref, *, tq):
    qi = pl.program_id(1)
    col = lax.broadcasted_iota(I32, (ATTN_RB, tq), 1)
    row = lax.broadcasted_iota(I32, (ATTN_RB, tq), 0)
    for pr in range(N_HEADS // ATTN_GROUP):
        heads = [ATTN_GROUP * pr + a for a in range(ATTN_GROUP)]
        m_ref[...] = jnp.full(m_ref.shape, NEG, F32)
        acc_ref[...] = jnp.zeros(acc_ref.shape, F32)

        def scores(j, slot, heads=heads):
            rows = pl.ds(pl.multiple_of(j * tq, tq), tq)
            for a, hd in enumerate(heads):
                hs = slice(hd * HEAD_PAD, (hd + 1) * HEAD_PAD)
                s_ref[slot, a] = lax.dot_general(q_ref[:, hs], k_ref[rows, hs], (((1,), (1,)), ((), ())),
                                                 preferred_element_type=F32)

        def softmax_pv(j, slot, masked, heads=heads):
            rows = pl.ds(pl.multiple_of(j * tq, tq), tq)
            for a in range(ATTN_GROUP):
                for c in range(tq // ATTN_RB):
                    r = slice(c * ATTN_RB, (c + 1) * ATTN_RB)
                    s = s_ref[slot, a, r, :]
                    if masked:
                        s = jnp.where(col <= row + c * ATTN_RB, s, NEG)
                    m_old = m_ref[a, r, :]
                    m_new = jnp.maximum(m_old, jnp.max(s, axis=-1, keepdims=True))
                    p_ref[a, r, :] = jnp.exp2(s - m_new).astype(BF16)
                    al_ref[a, r, :] = jnp.exp2(m_old - m_new)
                    m_ref[a, r, :] = m_new
            for a, hd in enumerate(heads):
                pv = jnp.dot(p_ref[a], v_ref[rows, hd * HEAD_PAD:(hd + 1) * HEAD_PAD],
                             preferred_element_type=F32)
                acc_ref[a] = al_ref[a] * acc_ref[a] + pv

        def two_tiles(jj, carry):
            t = 2 * jj
            scores(t + 1, 1)
            softmax_pv(t, 0, masked=False)
            scores(t + 2, 0)
            softmax_pv(t + 1, 1, masked=False)
            return carry

        scores(0, 0)
        lax.fori_loop(0, qi // 2, two_tiles, 0)

        @pl.when(qi % 2 == 0)
        def _():
            softmax_pv(qi, 0, masked=True)

        @pl.when(qi % 2 == 1)
        def _():
            scores(qi, 1)
            softmax_pv(qi - 1, 0, masked=False)
            softmax_pv(qi, 1, masked=True)

        for a, hd in enumerate(heads):
            acc = acc_ref[a]
            out_ref[:, hd * V_HEAD:(hd + 1) * V_HEAD] = acc[:, :V_HEAD] / acc[:, V_HEAD:V_HEAD + 1]
    o_ref[...] = _rms(out_ref[...], g_ref[...]).astype(BF16)


def _attention(q, k, v, g_attn, bsz, seq, tq):
    n_tok = bsz * seq
    nq = seq // tq
    qk_w = N_HEADS * HEAD_PAD
    return pl.pallas_call(
        functools.partial(_attn_kernel, tq=tq),
        out_shape=jax.ShapeDtypeStruct((n_tok, MLA_WIDTH), BF16),
        grid=(bsz, nq),
        in_specs=[pl.BlockSpec((tq, qk_w), lambda b, i: (b * nq + i, 0)),
                  pl.BlockSpec((seq, qk_w), lambda b, i: (b, 0)),
                  pl.BlockSpec((seq, qk_w), lambda b, i: (b, 0)),
                  _full((1, MLA_WIDTH))],
        out_specs=pl.BlockSpec((tq, MLA_WIDTH), lambda b, i: (b * nq + i, 0)),
        scratch_shapes=[pltpu.VMEM((2, ATTN_GROUP, tq, tq), F32), pltpu.VMEM((ATTN_GROUP, tq, tq), BF16),
                        pltpu.VMEM((ATTN_GROUP, tq, 1), F32), pltpu.VMEM((ATTN_GROUP, tq, 1), F32),
                        pltpu.VMEM((ATTN_GROUP, tq, HEAD_PAD), F32), pltpu.VMEM((tq, MLA_WIDTH), F32)],
        compiler_params=_cparams(("parallel", "parallel")),
        name="attn",
    )(q, k, v, g_attn)


S5_SLABS = 4
S5_SLAB_U = SSM_WIDTH // S5_SLABS
S5_SLAB_X = N_STATE // S5_SLABS


def _s5_kernel(u_ref, wb_ref, are_ref, aim_ref, wc_ref, d_ref, wglu_ref, bglu_ref, g_ref,
               o_ref, xr_ref, xi_ref, sr_ref, si_ref, *, lc, nb):
    @pl.when(pl.program_id(0) == 0)
    def _():
        xr_ref[...] = jnp.zeros_like(xr_ref)
        xi_ref[...] = jnp.zeros_like(xi_ref)

    rows = lc * nb
    u = u_ref[...].reshape(rows, SSM_WIDTH)
    ub = u.astype(BF16)
    def drive(j):
        bu = jnp.dot(ub[:, j * S5_SLAB_U:(j + 1) * S5_SLAB_U], wb_ref[j], preferred_element_type=F32)
        sr_ref[:, j * S5_SLAB_X:(j + 1) * S5_SLAB_X] = bu[:, :S5_SLAB_X]
        si_ref[:, j * S5_SLAB_X:(j + 1) * S5_SLAB_X] = bu[:, S5_SLAB_X:]

    ys = []
    drive(0)
    for c in range(S5_SLABS):
        if c + 1 < S5_SLABS:
            drive(c + 1)
        sl = slice(c * S5_SLAB_X, (c + 1) * S5_SLAB_X)
        ar = jnp.broadcast_to(are_ref[:, sl], (nb, S5_SLAB_X))
        ai = jnp.broadcast_to(aim_ref[:, sl], (nb, S5_SLAB_X))
        xr = xr_ref[:, sl]
        xi = xi_ref[:, sl]
        for t in range(lc):
            r = slice(t * nb, (t + 1) * nb)
            xr, xi = (ar * xr - ai * xi + sr_ref[r, sl], ar * xi + ai * xr + si_ref[r, sl])
            sr_ref[r, sl] = xr
            si_ref[r, sl] = xi
        xr_ref[:, sl] = xr
        xi_ref[:, sl] = xi
        ys.append(jnp.dot(sr_ref[:, sl].astype(BF16), wc_ref[0, c], preferred_element_type=F32)
                  + jnp.dot(si_ref[:, sl].astype(BF16), wc_ref[1, c], preferred_element_type=F32))
    y = jnp.concatenate(ys, axis=-1) + d_ref[...] * u
    y = 0.5 * y * (1.0 + jnp.tanh(math.sqrt(2.0 / math.pi) * (y + 0.044715 * (y * y * y))))
    gate = jnp.dot(y.astype(BF16), wglu_ref[...], preferred_element_type=F32) + bglu_ref[...]
    y = y * _sigmoid(gate)
    o_ref[...] = _rms(y, g_ref[...]).astype(BF16).reshape(lc, nb, SSM_WIDTH)


def _s5(u_tm, wts, bsz, seq, lc):
    rows = lc * bsz
    return pl.pallas_call(
        functools.partial(_s5_kernel, lc=lc, nb=bsz),
        out_shape=jax.ShapeDtypeStruct((seq, bsz, SSM_WIDTH), BF16),
        grid=(seq // lc,),
        in_specs=[pl.BlockSpec((lc, bsz, SSM_WIDTH), lambda i: (i, 0, 0)),
                  _full((S5_SLABS, S5_SLAB_U, 2 * S5_SLAB_X)),
                  _full((1, N_STATE)), _full((1, N_STATE)),
                  _full((2, S5_SLABS, S5_SLAB_X, S5_SLAB_U)),
                  _full((1, SSM_WIDTH)), _full((SSM_WIDTH, SSM_WIDTH)), _full((1, SSM_WIDTH)),
                  _full((1, SSM_WIDTH))],
        out_specs=pl.BlockSpec((lc, bsz, SSM_WIDTH), lambda i: (i, 0, 0)),
        scratch_shapes=[pltpu.VMEM((bsz, N_STATE), F32), pltpu.VMEM((bsz, N_STATE), F32),
                        pltpu.VMEM((rows, N_STATE), F32), pltpu.VMEM((rows, N_STATE), F32)],
        compiler_params=_cparams(("arbitrary",)),
        name="s5",
    )(u_tm, wts["s5_wb"], wts["s5_are"], wts["s5_aim"], wts["s5_wc"], wts["s5_d"],
      wts["w_glu"], wts["b_glu"], wts["g_ssm"])


ROUTE_E0 = N_EXPERT_GROUPS


def _outproj_kernel(h_ref, a_ref, s_ref, wout_ref, gmoe_ref, wrh_ref, wrl_ref, br_ref, tril_ref,
                    h1_ref, xn_ref, route_ref, rw_ref, cnt_ref):
    @pl.when((pl.program_id(0) == 0) & (pl.program_id(1) == 0))
    def _():
        cnt_ref[...] = jnp.zeros_like(cnt_ref)

    sub = tril_ref.shape[0]
    lane = lax.broadcasted_iota(I32, (sub, LANES), 1).astype(F32)
    big = float(LANES)
    for r in _row_chunks(h_ref.shape[0]):
        h1 = (h_ref[r, :]
              + jnp.dot(a_ref[r, :], wout_ref[:MLA_WIDTH, :], preferred_element_type=F32)
              + jnp.dot(s_ref[r, :], wout_ref[MLA_WIDTH:, :], preferred_element_type=F32))
        h1_ref[r, :] = h1
        xn = _rms(h1, gmoe_ref[...])
        xn_ref[r, :] = _pack_bf16_pairs(xn)
        x_hi = xn.astype(BF16)
        x_lo = (xn - x_hi.astype(F32)).astype(BF16)
        lg = (jnp.dot(x_hi, wrh_ref[...], preferred_element_type=F32)
              + (jnp.dot(x_lo, wrh_ref[...], preferred_element_type=F32)
                 + jnp.dot(x_hi, wrl_ref[...], preferred_element_type=F32))) + br_ref[...]
        gl = jnp.where(lane < N_EXPERT_GROUPS, lg, NEG)
        gmax = jnp.max(gl, axis=-1, keepdims=True)
        gsum = jnp.sum(jnp.exp(gl - gmax), axis=-1, keepdims=True)
        g_top_p = 1.0 / gsum
        gidx = jnp.min(jnp.where(gl == gmax, lane, big), axis=-1, keepdims=True)
        lo = ROUTE_E0 + EXPERTS_PER_GROUP * gidx
        sel = jnp.where((lane >= lo) & (lane < lo + EXPERTS_PER_GROUP), lg, NEG)
        m1 = jnp.max(sel, axis=-1, keepdims=True)
        i1 = jnp.min(jnp.where(sel == m1, lane, big), axis=-1, keepdims=True)
        sel2 = jnp.where(lane == i1, NEG, sel)
        m2 = jnp.max(sel2, axis=-1, keepdims=True)
        i2 = jnp.min(jnp.where(sel2 == m2, lane, big), axis=-1, keepdims=True)
        e21 = jnp.exp(m2 - m1)
        w1 = g_top_p / (1.0 + e21)
        w2 = g_top_p * e21 / (1.0 + e21)
        e1 = i1 - ROUTE_E0
        e2 = i2 - ROUTE_E0
        oh = jnp.where((lane == e1) | (lane == e2), 1.0, 0.0)
        before = cnt_ref[...] + jnp.dot(tril_ref[...], oh.astype(BF16), preferred_element_type=F32)
        r1 = jnp.sum(jnp.where(lane == e1, before, 0.0), axis=-1, keepdims=True)
        r2 = jnp.sum(jnp.where(lane == e2, before, 0.0), axis=-1, keepdims=True)
        cnt_ref[...] = cnt_ref[...] + jnp.sum(oh, axis=0, keepdims=True)
        route = jnp.where(lane == 0, e1, jnp.where(lane == 1, e2, jnp.where(lane == 2, r1, r2)))
        route_ref[r, :] = route.astype(I32)
        rw_ref[r, :] = jnp.where(lane == 0, w1, w2)


def _outproj(h, attn_n, ssm_tm, wts, tril, bsz, seq, tm):
    n_tok = bsz * seq
    nl = seq // tm
    row = lambda b, l: (b * nl + l, 0)
    return pl.pallas_call(
        _outproj_kernel,
        out_shape=(jax.ShapeDtypeStruct((n_tok, D_MODEL), F32),
                   jax.ShapeDtypeStruct((n_tok, PACKED), U32),
                   jax.ShapeDtypeStruct((n_tok, LANES), I32),
                   jax.ShapeDtypeStruct((n_tok, LANES), F32),
                   jax.ShapeDtypeStruct((1, LANES), F32)),
        grid=(bsz, nl),
        in_specs=[pl.BlockSpec((tm, D_MODEL), row),
                  pl.BlockSpec((tm, MLA_WIDTH), row),
                  pl.BlockSpec((tm, SSM_WIDTH), lambda b, l: (l, b)),
                  _full((D_MODEL, D_MODEL)), _full((1, D_MODEL)),
                  _full((D_MODEL, LANES)), _full((D_MODEL, LANES)), _full((1, LANES)),
                  _full(tril.shape)],
        out_specs=(pl.BlockSpec((tm, D_MODEL), row), pl.BlockSpec((tm, PACKED), row),
                   pl.BlockSpec((tm, LANES), row), pl.BlockSpec((tm, LANES), row),
                   _full((1, LANES))),
        compiler_params=_cparams(("arbitrary", "arbitrary")),
        name="outproj",
    )(h, attn_n, ssm_tm, wts["w_out"], wts["g_moe"], wts["w_r_hi"], wts["w_r_lo"], wts["b_r"], tril)


def _row_copy(src, src_row, dst, dst_row, sem):
    return pltpu.make_async_copy(src.at[pl.ds(src_row, 1)], dst.at[pl.ds(dst_row, 1)], sem)


def _dispatch_kernel(zflag_ref, dest_ref, xn_ref, xbuf_ref, zero_ref, sems, zsem, *, td, te, n_blocks):
    def zero_copy(b):
        return pltpu.make_async_copy(zero_ref, xbuf_ref.at[pl.ds(pl.multiple_of(b * te, te), te)], zsem)

    @pl.when(pl.program_id(0) == 0)
    def _():
        zero_ref[...] = jnp.zeros_like(zero_ref)

        def start(b, _):
            @pl.when(zflag_ref[b] != 0)
            def _():
                zero_copy(b).start()
            return 0

        def wait(b, _):
            @pl.when(zflag_ref[b] != 0)
            def _():
                zero_copy(b).wait()
            return 0

        lax.fori_loop(0, n_blocks, start, 0)
        lax.fori_loop(0, n_blocks, wait, 0)

    def issue(g, _):
        r0 = pl.multiple_of(g * SUBLANES, SUBLANES)
        for u in range(SUBLANES):
            for k in range(TOP_K):
                _row_copy(xn_ref, r0 + u, xbuf_ref, dest_ref[TOP_K * (r0 + u) + k],
                          sems.at[k]).start(priority=k)
        return 0

    lax.fori_loop(0, td // SUBLANES, issue, 0)
    for k in range(TOP_K):
        pltpu.make_async_copy(xn_ref, xbuf_ref.at[pl.ds(0, td)], sems.at[k]).wait()


def _dispatch(zflag, dest_flat, xn, n_slots, td, te):
    n_tok = xn.shape[0]
    n_blocks = n_slots // te
    grid_spec = pltpu.PrefetchScalarGridSpec(
        num_scalar_prefetch=1,
        grid=(n_tok // td,),
        in_specs=[pl.BlockSpec((TOP_K * td,), lambda i, zf: (i,), memory_space=pltpu.SMEM),
                  pl.BlockSpec((td, PACKED), lambda i, zf: (i, 0))],
        out_specs=pl.BlockSpec(memory_space=pl.ANY),
        scratch_shapes=[pltpu.VMEM((te, PACKED), U32), pltpu.SemaphoreType.DMA((TOP_K,)),
                        pltpu.SemaphoreType.DMA],
    )
    return pl.pallas_call(
        functools.partial(_dispatch_kernel, td=td, te=te, n_blocks=n_blocks),
        out_shape=jax.ShapeDtypeStruct((n_slots, PACKED), U32),
        grid_spec=grid_spec,
        compiler_params=_cparams(("arbitrary",)),
        name="dispatch",
    )(zflag, dest_flat, xn)


def _experts_kernel(be_ref, x_ref, wg_ref, wu_ref, wd_ref, y_ref):
    del be_ref
    xb = _unpack_bf16_pairs(x_ref[...]).astype(BF16)
    g = jnp.dot(xb, wg_ref[...].astype(BF16), preferred_element_type=F32)
    u = jnp.dot(xb, wu_ref[...].astype(BF16), preferred_element_type=F32)
    hdn = (g * _sigmoid(g) * u).astype(BF16)
    y_ref[...] = _pack_bf16_pairs(jnp.dot(hdn, wd_ref[...].astype(BF16), preferred_element_type=F32))


def _experts(block_e, xbuf, w_gate, w_up, w_down, layer, te):
    n_slots = xbuf.shape[0]
    grid_spec = pltpu.PrefetchScalarGridSpec(
        num_scalar_prefetch=1,
        grid=(n_slots // te,),
        in_specs=[pl.BlockSpec((te, PACKED), lambda i, be: (i, 0)),
                  pl.BlockSpec((None, None, D_MODEL, D_EXPERT), lambda i, be: (layer, be[i], 0, 0)),
                  pl.BlockSpec((None, None, D_MODEL, D_EXPERT), lambda i, be: (layer, be[i], 0, 0)),
                  pl.BlockSpec((None, None, D_EXPERT, D_MODEL), lambda i, be: (layer, be[i], 0, 0))],
        out_specs=pl.BlockSpec((te, PACKED), lambda i, be: (i, 0)),
    )
    return pl.pallas_call(
        _experts_kernel,
        out_shape=jax.ShapeDtypeStruct((n_slots, PACKED), U32),
        grid_spec=grid_spec,
        compiler_params=_cparams(("arbitrary",)),
        name="experts",
    )(block_e, xbuf, w_gate, w_up, w_down)


def _combine_kernel(dest_ref, dnext_ref, ybuf_ref, h1_ref, rw_ref, p_ref, gple_ref, wgate_ref, wproj_ref,
                    gfin_ref, o_ref, yv_ref, sems, *, tf, final):
    i = pl.program_id(0)
    slot = i % 2

    def gather(idx_ref, into):
        def issue(g, _):
            r0 = pl.multiple_of(g * SUBLANES, SUBLANES)
            for u in range(SUBLANES):
                for k in range(TOP_K):
                    _row_copy(ybuf_ref, idx_ref[TOP_K * (r0 + u) + k], yv_ref.at[into, k], r0 + u,
                              sems.at[into, k]).start(priority=k)
            return 0

        lax.fori_loop(0, tf // SUBLANES, issue, 0)

    @pl.when(i == 0)
    def _():
        gather(dest_ref, 0)

    @pl.when(i + 1 < pl.num_programs(0))
    def _():
        gather(dnext_ref, 1 - slot)

    for k in range(TOP_K):
        pltpu.make_async_copy(ybuf_ref.at[pl.ds(0, tf)], yv_ref.at[slot, k], sems.at[slot, k]).wait()
    rw = rw_ref[...]
    h2 = (h1_ref[...] + rw[:, 0:1] * _unpack_bf16_pairs(yv_ref[slot, 0])
          + rw[:, 1:2] * _unpack_bf16_pairs(yv_ref[slot, 1]))
    hn = _rms(h2, gple_ref[...]).astype(BF16)
    gate = _sigmoid(jnp.dot(hn, wgate_ref[...], preferred_element_type=F32))
    ple = jnp.dot(p_ref[...].astype(BF16), wproj_ref[...], preferred_element_type=F32)
    h3 = h2 + ple * gate
    if final:
        h3 = _rms(h3, gfin_ref[...])
    o_ref[...] = h3


def _combine(dest_flat, ybuf, h1, rw, p_all, layer, wts, g_final, tf, final):
    n_tok = h1.shape[0]
    n_steps = n_tok // tf
    row = lambda i: (i, 0)
    return pl.pallas_call(
        functools.partial(_combine_kernel, tf=tf, final=final),
        out_shape=jax.ShapeDtypeStruct((n_tok, D_MODEL), F32),
        grid=(n_steps,),
        in_specs=[pl.BlockSpec((TOP_K * tf,), lambda i: (i,), memory_space=pltpu.SMEM),
                  pl.BlockSpec((TOP_K * tf,), lambda i: (jnp.minimum(i + 1, n_steps - 1),),
                               memory_space=pltpu.SMEM),
                  pl.BlockSpec(memory_space=pl.ANY),
                  pl.BlockSpec((tf, D_MODEL), row), pl.BlockSpec((tf, LANES), row),
                  pl.BlockSpec((None, tf, PLE_DIM), lambda i: (layer, i, 0)),
                  _full((1, D_MODEL)), _full((D_MODEL, D_MODEL)), _full((PLE_DIM, D_MODEL)),
                  _full((1, D_MODEL))],
        out_specs=pl.BlockSpec((tf, D_MODEL), row),
        scratch_shapes=[pltpu.VMEM((2, TOP_K, tf, PACKED), U32), pltpu.SemaphoreType.DMA((2, TOP_K))],
        compiler_params=_cparams(("arbitrary",)),
        name="combine",
    )(dest_flat, dest_flat, ybuf, h1, rw, p_all, wts["g_ple"], wts["w_ple_gate"], wts["w_ple_proj"], g_final)


def _layer_weights(i, g_mix_norm, w_in, g_q_lat, w_q_up, g_kv_lat, w_kv_up,
                   ssm_a_re, ssm_a_im, ssm_b_re, ssm_b_im, ssm_c_re, ssm_c_im, ssm_d, ssm_log_step,
                   w_glu, b_glu, g_attn_out, g_ssm_out, w_out, g_moe_norm,
                   w_group_router, b_group_router, w_expert_router, b_expert_router,
                   g_ple_norm, w_ple_gate, w_ple_proj):
    off_kv, off_kr, off_ssm = Q_LORA, Q_LORA + KV_LORA, Q_LORA + KV_LORA + QK_ROPE
    wi = w_in[i]
    kr_block = jnp.zeros((D_MODEL, HEAD_PAD), F32).at[:, ROPE_LANE0:ROPE_LANE0 + QK_ROPE].set(wi[:, off_kr:off_ssm])
    w_in_p = jnp.concatenate([wi[:, :off_kr], kr_block, wi[:, off_ssm:]], axis=1)
    wq = w_q_up[i].reshape(Q_LORA, N_HEADS, QK_NOPE + QK_ROPE) * (ATTN_SCALE * math.log2(math.e))
    wq = jnp.pad(wq, ((0, 0), (0, 0), (0, HEAD_PAD - QK_NOPE - QK_ROPE))).reshape(Q_LORA, N_HEADS * HEAD_PAD)
    wkv = w_kv_up[i].reshape(KV_LORA, N_HEADS, QK_NOPE + V_HEAD)
    wk = jnp.pad(wkv[..., :QK_NOPE], ((0, 0), (0, 0), (0, HEAD_PAD - QK_NOPE))).reshape(KV_LORA, N_HEADS * HEAD_PAD)
    wv = jnp.pad(wkv[..., QK_NOPE:], ((0, 0), (0, 0), (0, HEAD_PAD - V_HEAD))).reshape(KV_LORA, N_HEADS * HEAD_PAD)
    lam = lax.complex(ssm_a_re[i], ssm_a_im[i])
    lam_bar = jnp.exp(lam * jnp.exp(ssm_log_step[i])[:, None])
    b_fac = (lam_bar - 1.0) / lam
    bc = lax.complex(ssm_b_re[i], ssm_b_im[i]) * b_fac[:, :, None]
    n_per = SSM_GROUPS // S5_SLABS

    def block_diag(blocks):
        s, n, a, b = blocks.shape
        eye = jnp.eye(n, dtype=blocks.dtype)
        return jnp.einsum("snab,nm->snamb", blocks, eye).reshape(s, n * a, n * b)

    b_hp_re = jnp.real(bc).transpose(0, 2, 1).reshape(S5_SLABS, n_per, SSM_CH, SSM_STATE)
    b_hp_im = jnp.imag(bc).transpose(0, 2, 1).reshape(S5_SLABS, n_per, SSM_CH, SSM_STATE)
    s5_wb = jnp.concatenate([block_diag(b_hp_re), block_diag(b_hp_im)], axis=-1)
    c_re = ssm_c_re[i].transpose(0, 2, 1).reshape(S5_SLABS, n_per, SSM_STATE, SSM_CH)
    c_im = ssm_c_im[i].transpose(0, 2, 1).reshape(S5_SLABS, n_per, SSM_STATE, SSM_CH)
    s5_wc = jnp.stack([block_diag(c_re), -block_diag(c_im)])
    w_r = jnp.zeros((D_MODEL, LANES), F32)
    w_r = w_r.at[:, :N_EXPERT_GROUPS].set(w_group_router[i])
    w_r = w_r.at[:, ROUTE_E0:ROUTE_E0 + N_EXPERTS].set(w_expert_router[i])
    w_r_hi = w_r.astype(BF16)
    w_r_lo = (w_r - w_r_hi.astype(F32)).astype(BF16)
    b_r = jnp.zeros((1, LANES), F32)
    b_r = b_r.at[0, :N_EXPERT_GROUPS].set(b_group_router[i])
    b_r = b_r.at[0, ROUTE_E0:ROUTE_E0 + N_EXPERTS].set(b_expert_router[i])
    return dict(
        g_mix=g_mix_norm[i][None], w_in=w_in_p.astype(BF16),
        g_q=g_q_lat[i][None], w_q=wq.astype(BF16),
        g_kv=g_kv_lat[i][None], w_kv=jnp.concatenate([wk, wv], axis=1).astype(BF16),
        s5_wb=s5_wb.astype(BF16), s5_wc=s5_wc.astype(BF16),
        s5_are=jnp.real(lam_bar).reshape(1, N_STATE), s5_aim=jnp.imag(lam_bar).reshape(1, N_STATE),
        s5_d=ssm_d[i].reshape(1, SSM_WIDTH), w_glu=w_glu[i].astype(BF16), b_glu=b_glu[i][None],
        g_attn=g_attn_out[i][None], g_ssm=g_ssm_out[i][None],
        w_out=w_out[i].astype(BF16), g_moe=g_moe_norm[i][None],
        w_r_hi=w_r_hi, w_r_lo=w_r_lo, b_r=b_r,
        g_ple=g_ple_norm[i][None], w_ple_gate=w_ple_gate[i].astype(BF16),
        w_ple_proj=w_ple_proj[i].astype(BF16),
    )


def _slot_layout(route, cnt, n_blocks, te):
    counts = cnt[0, :N_EXPERTS].astype(I32)
    pcounts = (counts + te - 1) // te * te
    pends = jnp.cumsum(pcounts)
    pstarts = pends - pcounts
    eid = route[:, :TOP_K]
    rank = route[:, TOP_K:2 * TOP_K]
    start = jnp.sum(jnp.where(eid[:, :, None] == jnp.arange(N_EXPERTS)[None, None, :],
                              pstarts[None, None, :], 0), axis=-1)
    dest = (start + rank).reshape(-1).astype(I32)
    blk0 = jnp.arange(n_blocks, dtype=I32) * te
    block_e = jnp.minimum(jnp.sum((pends[None, :] <= blk0[:, None]).astype(I32), axis=1), N_EXPERTS - 1)
    fill = jnp.clip((pstarts + counts)[block_e] - blk0, 0, te)
    zflag = (fill < te).astype(I32)
    return dest, block_e, zflag


def kernel(x, p, positions, g_mix_norm, w_in, g_q_lat, w_q_up, g_kv_lat, w_kv_up, ssm_a_re, ssm_a_im, ssm_b_re, ssm_b_im, ssm_c_re, ssm_c_im, ssm_d, ssm_log_step, w_glu, b_glu, g_attn_out, g_ssm_out, w_out, g_moe_norm, w_group_router, b_group_router, w_expert_router, b_expert_router, w_exp_gate, w_exp_up, w_exp_down, g_ple_norm, w_ple_gate, w_ple_proj, g_final):
    bsz, seq, _ = x.shape
    depth = p.shape[0]
    n_tok = bsz * seq
    t = _tiles(n_tok, seq)
    n_slots = TOP_K * n_tok + N_EXPERTS * t["te"]
    n_blocks = n_slots // t["te"]

    cosf, sinf = _rope_tables(positions.reshape(n_tok, 1), t["tm"])
    sub = min(SUB_ROWS, t["tm"])
    tril = jnp.tril(jnp.ones((sub, sub), F32), k=-1).astype(BF16)
    g_fin = g_final[None]

    h = x.reshape(n_tok, D_MODEL)
    for i in range(depth):
        wts = _layer_weights(i, g_mix_norm, w_in, g_q_lat, w_q_up, g_kv_lat, w_kv_up,
                             ssm_a_re, ssm_a_im, ssm_b_re, ssm_b_im, ssm_c_re, ssm_c_im, ssm_d,
                             ssm_log_step, w_glu, b_glu, g_attn_out, g_ssm_out, w_out, g_moe_norm,
                             w_group_router, b_group_router, w_expert_router, b_expert_router,
                             g_ple_norm, w_ple_gate, w_ple_proj)
        q, k, v, u_tm = _inproj(h, cosf, sinf, wts, bsz, seq, t["tm"])
        attn_n = _attention(q, k, v, wts["g_attn"], bsz, seq, t["tq"])
        ssm_n = _s5(u_tm.reshape(seq, bsz, SSM_WIDTH), wts, bsz, seq, t["lc"])
        h1, xn, route, rw, cnt = _outproj(h, attn_n, ssm_n.reshape(seq, bsz * SSM_WIDTH), wts, tril,
                                          bsz, seq, t["tm"])
        dest, block_e, zflag = _slot_layout(route, cnt, n_blocks, t["te"])
        xbuf = _dispatch(zflag, dest, xn, n_slots, t["td"], t["te"])
        ybuf = _experts(block_e, xbuf, w_exp_gate, w_exp_up, w_exp_down, i, t["te"])
        h = _combine(dest, ybuf, h1, rw, p.reshape(depth, n_tok, PLE_DIM), i, wts, g_fin, t["tf"],
                     final=(i == depth - 1))
    return h.reshape(bsz, seq, D_MODEL)
```

```python
import functools
import math

import jax
import jax.numpy as jnp
from jax import lax
from jax.experimental import pallas as pl
from jax.experimental.pallas import tpu as pltpu
from jax.experimental.pallas import tpu_sc as plsc

F32 = jnp.float32
BF16 = jnp.bfloat16
I32 = jnp.int32
U32 = jnp.uint32

D_MODEL = 1024
MLA_WIDTH = 512
SSM_WIDTH = 512
V_HEAD = 64
N_HEADS = 8
QK_NOPE = 64
QK_ROPE = 32
HALF_ROPE = QK_ROPE // 2
Q_LORA = 256
KV_LORA = 128
ROPE_BASE = 10000.0
ATTN_SCALE = 1.0 / math.sqrt(QK_NOPE + QK_ROPE)
SSM_CH = 16
SSM_GROUPS = 32
SSM_STATE = 64
N_EXPERT_GROUPS = 4
EXPERTS_PER_GROUP = 8
N_EXPERTS = 32
TOP_K = 2
D_EXPERT = 256
PLE_DIM = 256
EPS = 1e-6

LANES = 128
SUBLANES = 8
HEAD_PAD = 128
ROPE_LANE0 = QK_NOPE
N_STATE = SSM_GROUPS * SSM_STATE
PACKED = D_MODEL // 2
NEG = -1e30
VMEM_LIMIT = 56 * 1024 * 1024
SUB_ROWS = 512


def _tiles(n_tok, seq):
    return dict(
        tm=min(1024, seq),
        tq=min(512, seq),
        lc=min(32, seq),
        te=512,
        td=min(512, n_tok),
        tf=min(512, n_tok),
    )


def _rms(x, g):
    ms = jnp.mean(x * x, axis=-1, keepdims=True)
    return x * lax.rsqrt(ms + EPS) * g


def _sigmoid(x):
    return 1.0 / (1.0 + jnp.exp(-x))


def _pack_bf16_pairs(x):
    lo = lax.bitcast_convert_type(x[:, :PACKED].astype(BF16).astype(F32), U32)
    hi = lax.bitcast_convert_type(x[:, PACKED:].astype(BF16).astype(F32), U32)
    return (lo >> 16) | hi


def _unpack_bf16_pairs(w):
    lo = lax.bitcast_convert_type(w << 16, F32)
    hi = lax.bitcast_convert_type(w & jnp.uint32(0xFFFF0000), F32)
    return jnp.concatenate([lo, hi], axis=1)


def _cparams(sem):
    return pltpu.CompilerParams(dimension_semantics=sem, vmem_limit_bytes=VMEM_LIMIT)


def _full(shape):
    nd = len(shape)
    return pl.BlockSpec(shape, lambda *_: (0,) * nd)


def _rope_table_kernel(pos_ref, invf_ref, cos_ref, sin_ref):
    ang = pos_ref[...].astype(F32) * invf_ref[...]
    cos_ref[...] = jnp.cos(ang)
    sin_ref[...] = jnp.sin(ang)


def _rope_tables(pos_col, tm):
    n_tok = pos_col.shape[0]
    lane = jnp.arange(LANES)
    in_rope = (lane >= ROPE_LANE0) & (lane < ROPE_LANE0 + QK_ROPE)
    freq = ROPE_BASE ** (-((lane - ROPE_LANE0) % HALF_ROPE).astype(F32) / HALF_ROPE)
    invf = jnp.where(in_rope, freq, 0.0).astype(F32)[None, :]
    return pl.pallas_call(
        _rope_table_kernel,
        out_shape=(jax.ShapeDtypeStruct((n_tok, LANES), F32),) * 2,
        grid=(n_tok // tm,),
        in_specs=[pl.BlockSpec((tm, 1), lambda i: (i, 0)), _full((1, LANES))],
        out_specs=(pl.BlockSpec((tm, LANES), lambda i: (i, 0)),) * 2,
        compiler_params=_cparams(("parallel",)),
        name="rope_tables",
    )(pos_col, invf)


def _rope(x, cosf, s_lo, s_hi):
    n = x.shape[-1]
    return (x * cosf + pltpu.roll(x, n - HALF_ROPE, 1) * s_lo + pltpu.roll(x, HALF_ROPE, 1) * s_hi)


def _row_chunks(n_rows):
    sub = min(SUB_ROWS, n_rows)
    return [slice(c * sub, (c + 1) * sub) for c in range(n_rows // sub)]


def _inproj_kernel(h_ref, cos_ref, sin_ref, gmix_ref, win_ref, gq_ref, wq_ref, gkv_ref, wkv_ref,
                   q_ref, k_ref, v_ref, u_ref):
    lane = lax.broadcasted_iota(I32, (1, LANES), 1)
    ones_lane = jnp.where(lane == V_HEAD, 1.0, 0.0)
    for r in _row_chunks(h_ref.shape[0]):
        xn = _rms(h_ref[r, :], gmix_ref[...]).astype(BF16)
        z = jnp.dot(xn, win_ref[...], preferred_element_type=F32)
        cosf = cos_ref[r, :]
        sinf = sin_ref[r, :]
        s_lo = jnp.where((lane >= ROPE_LANE0) & (lane < ROPE_LANE0 + HALF_ROPE), -sinf, 0.0)
        s_hi = jnp.where((lane >= ROPE_LANE0 + HALF_ROPE) & (lane < ROPE_LANE0 + QK_ROPE), sinf, 0.0)
        qn = _rms(z[:, :Q_LORA], gq_ref[...]).astype(BF16)
        q = jnp.dot(qn, wq_ref[...], preferred_element_type=F32)
        kvn = _rms(z[:, Q_LORA:Q_LORA + KV_LORA], gkv_ref[...]).astype(BF16)
        kv = jnp.dot(kvn, wkv_ref[...], preferred_element_type=F32)
        kpe = _rope(z[:, Q_LORA + KV_LORA:Q_LORA + KV_LORA + HEAD_PAD], cosf, s_lo, s_hi)
        for hd in range(N_HEADS):
            sl = slice(hd * HEAD_PAD, (hd + 1) * HEAD_PAD)
            vsl = slice(N_HEADS * HEAD_PAD + hd * HEAD_PAD, N_HEADS * HEAD_PAD + (hd + 1) * HEAD_PAD)
            q_ref[r, sl] = _rope(q[:, sl], cosf, s_lo, s_hi).astype(BF16)
            k_ref[r, sl] = (kv[:, sl] + kpe).astype(BF16)
            v_ref[r, sl] = (kv[:, vsl] + ones_lane).astype(BF16)
        u_ref[r, :] = z[:, Q_LORA + KV_LORA + HEAD_PAD:]


def _inproj(h, cosf, sinf, wts, bsz, seq, tm):
    n_tok = bsz * seq
    nl = seq // tm
    row = lambda b, l: (b * nl + l, 0)
    qk_w = N_HEADS * HEAD_PAD
    return pl.pallas_call(
        _inproj_kernel,
        out_shape=(jax.ShapeDtypeStruct((n_tok, qk_w), BF16),
                   jax.ShapeDtypeStruct((n_tok, qk_w), BF16),
                   jax.ShapeDtypeStruct((n_tok, qk_w), BF16),
                   jax.ShapeDtypeStruct((seq, bsz * SSM_WIDTH), F32)),
        grid=(bsz, nl),
        in_specs=[pl.BlockSpec((tm, D_MODEL), row),
                  pl.BlockSpec((tm, LANES), row), pl.BlockSpec((tm, LANES), row),
                  _full((1, D_MODEL)), _full((D_MODEL, D_MODEL)),
                  _full((1, Q_LORA)), _full((Q_LORA, qk_w)),
                  _full((1, KV_LORA)), _full((KV_LORA, 2 * qk_w))],
        out_specs=(pl.BlockSpec((tm, qk_w), row), pl.BlockSpec((tm, qk_w), row),
                   pl.BlockSpec((tm, qk_w), row),
                   pl.BlockSpec((tm, SSM_WIDTH), lambda b, l: (l, b))),
        compiler_params=_cparams(("parallel", "parallel")),
        name="inproj",
    )(h, cosf, sinf, wts["g_mix"], wts["w_in"], wts["g_q"], wts["w_q"], wts["g_kv"], wts["w_kv"])


ATTN_RB = 32
ATTN_GROUP = 4


def _attn_kernel(q_ref, k_ref, v_ref, g_ref, o_ref, s_ref, p_ref, m_ref, al_ref, acc_ref, out_ref, *, tq):
    qi = pl.program_id(1)
    col = lax.broadcasted_iota(I32, (ATTN_RB, tq), 1)
    row = lax.broadcasted_iota(I32, (ATTN_RB, tq), 0)
    for pr in range(N_HEADS // ATTN_GROUP):
        heads = [ATTN_GROUP * pr + a for a in range(ATTN_GROUP)]
        m_ref[...] = jnp.full(m_ref.shape, NEG, F32)
        acc_ref[...] = jnp.zeros(acc_ref.shape, F32)

        def scores(j, slot, heads=heads):
            rows = pl.ds(pl.multiple_of(j * tq, tq), tq)
            for a, hd in enumerate(heads):
                hs = slice(hd * HEAD_PAD, (hd + 1) * HEAD_PAD)
                s_ref[slot, a] = lax.dot_general(q_ref[:, hs], k_ref[rows, hs], (((1,), (1,)), ((), ())),
                                                 preferred_element_type=F32)

        def softmax_pv(j, slot, masked, heads=heads):
            rows = pl.ds(pl.multiple_of(j * tq, tq), tq)
            for a in range(ATTN_GROUP):
                for c in range(tq // ATTN_RB):
                    r = slice(c * ATTN_RB, (c + 1) * ATTN_RB)
                    s = s_ref[slot, a, r, :]
                    if masked:
                        s = jnp.where(col <= row + c * ATTN_RB, s, NEG)
                    m_old = m_ref[a, r, :]
                    m_new = jnp.maximum(m_old, jnp.max(s, axis=-1, keepdims=True))
                    p_ref[a, r, :] = jnp.exp2(s - m_new).astype(BF16)
                    al_ref[a, r, :] = jnp.exp2(m_old - m_new)
                    m_ref[a, r, :] = m_new
            for a, hd in enumerate(heads):
                pv = jnp.dot(p_ref[a], v_ref[rows, hd * HEAD_PAD:(hd + 1) * HEAD_PAD],
                             preferred_element_type=F32)
                acc_ref[a] = al_ref[a] * acc_ref[a] + pv

        def two_tiles(jj, carry):
            t = 2 * jj
            scores(t + 1, 1)
            softmax_pv(t, 0, masked=False)
            scores(t + 2, 0)
            softmax_pv(t + 1, 1, masked=False)
            return carry

        scores(0, 0)
        lax.fori_loop(0, qi // 2, two_tiles, 0)

        @pl.when(qi % 2 == 0)
        def _():
            softmax_pv(qi, 0, masked=True)

        @pl.when(qi % 2 == 1)
        def _():
            scores(qi, 1)
            softmax_pv(qi - 1, 0, masked=False)
            softmax_pv(qi, 1, masked=True)

        for a, hd in enumerate(heads):
            acc = acc_ref[a]
            out_ref[:, hd * V_HEAD:(hd + 1) * V_HEAD] = acc[:, :V_HEAD] / acc[:, V_HEAD:V_HEAD + 1]
    o_ref[...] = _rms(out_ref[...], g_ref[...]).astype(BF16)


def _attention(q, k, v, g_attn, bsz, seq, tq):
    n_tok = bsz * seq
    nq = seq // tq
    qk_w = N_HEADS * HEAD_PAD
    return pl.pallas_call(
        functools.partial(_attn_kernel, tq=tq),
        out_shape=jax.ShapeDtypeStruct((n_tok, MLA_WIDTH), BF16),
        grid=(bsz, nq),
        in_specs=[pl.BlockSpec((tq, qk_w), lambda b, i: (b * nq + i, 0)),
                  pl.BlockSpec((seq, qk_w), lambda b, i: (b, 0)),
                  pl.BlockSpec((seq, qk_w), lambda b, i: (b, 0)),
                  _full((1, MLA_WIDTH))],
        out_specs=pl.BlockSpec((tq, MLA_WIDTH), lambda b, i: (b * nq + i, 0)),
        scratch_shapes=[pltpu.VMEM((2, ATTN_GROUP, tq, tq), F32), pltpu.VMEM((ATTN_GROUP, tq, tq), BF16),
                        pltpu.VMEM((ATTN_GROUP, tq, 1), F32), pltpu.VMEM((ATTN_GROUP, tq, 1), F32),
                        pltpu.VMEM((ATTN_GROUP, tq, HEAD_PAD), F32), pltpu.VMEM((tq, MLA_WIDTH), F32)],
        compiler_params=_cparams(("parallel", "parallel")),
        name="attn",
    )(q, k, v, g_attn)


S5_SLABS = 4
S5_SLAB_U = SSM_WIDTH // S5_SLABS
S5_SLAB_X = N_STATE // S5_SLABS


def _s5_kernel(u_ref, wb_ref, are_ref, aim_ref, wc_ref, d_ref, wglu_ref, bglu_ref, g_ref,
               o_ref, xr_ref, xi_ref, sr_ref, si_ref, *, lc, nb):
    @pl.when(pl.program_id(0) == 0)
    def _():
        xr_ref[...] = jnp.zeros_like(xr_ref)
        xi_ref[...] = jnp.zeros_like(xi_ref)

    rows = lc * nb
    u = u_ref[...].reshape(rows, SSM_WIDTH)
    ub = u.astype(BF16)
    def drive(j):
        bu = jnp.dot(ub[:, j * S5_SLAB_U:(j + 1) * S5_SLAB_U], wb_ref[j], preferred_element_type=F32)
        sr_ref[:, j * S5_SLAB_X:(j + 1) * S5_SLAB_X] = bu[:, :S5_SLAB_X]
        si_ref[:, j * S5_SLAB_X:(j + 1) * S5_SLAB_X] = bu[:, S5_SLAB_X:]

    ys = []
    drive(0)
    for c in range(S5_SLABS):
        if c + 1 < S5_SLABS:
            drive(c + 1)
        sl = slice(c * S5_SLAB_X, (c + 1) * S5_SLAB_X)
        ar = jnp.broadcast_to(are_ref[:, sl], (nb, S5_SLAB_X))
        ai = jnp.broadcast_to(aim_ref[:, sl], (nb, S5_SLAB_X))
        xr = xr_ref[:, sl]
        xi = xi_ref[:, sl]
        for t in range(lc):
            r = slice(t * nb, (t + 1) * nb)
            xr, xi = (ar * xr - ai * xi + sr_ref[r, sl], ar * xi + ai * xr + si_ref[r, sl])
            sr_ref[r, sl] = xr
            si_ref[r, sl] = xi
        xr_ref[:, sl] = xr
        xi_ref[:, sl] = xi
        ys.append(jnp.dot(sr_ref[:, sl].astype(BF16), wc_ref[0, c], preferred_element_type=F32)
                  + jnp.dot(si_ref[:, sl].astype(BF16), wc_ref[1, c], preferred_element_type=F32))
    y = jnp.concatenate(ys, axis=-1) + d_ref[...] * u
    y = 0.5 * y * (1.0 + jnp.tanh(math.sqrt(2.0 / math.pi) * (y + 0.044715 * (y * y * y))))
    gate = jnp.dot(y.astype(BF16), wglu_ref[...], preferred_element_type=F32) + bglu_ref[...]
    y = y * _sigmoid(gate)
    o_ref[...] = _rms(y, g_ref[...]).astype(BF16).reshape(lc, nb, SSM_WIDTH)


def _s5(u_tm, wts, bsz, seq, lc):
    rows = lc * bsz
    return pl.pallas_call(
        functools.partial(_s5_kernel, lc=lc, nb=bsz),
        out_shape=jax.ShapeDtypeStruct((seq, bsz, SSM_WIDTH), BF16),
        grid=(seq // lc,),
        in_specs=[pl.BlockSpec((lc, bsz, SSM_WIDTH), lambda i: (i, 0, 0)),
                  _full((S5_SLABS, S5_SLAB_U, 2 * S5_SLAB_X)),
                  _full((1, N_STATE)), _full((1, N_STATE)),
                  _full((2, S5_SLABS, S5_SLAB_X, S5_SLAB_U)),
                  _full((1, SSM_WIDTH)), _full((SSM_WIDTH, SSM_WIDTH)), _full((1, SSM_WIDTH)),
                  _full((1, SSM_WIDTH))],
        out_specs=pl.BlockSpec((lc, bsz, SSM_WIDTH), lambda i: (i, 0, 0)),
        scratch_shapes=[pltpu.VMEM((bsz, N_STATE), F32), pltpu.VMEM((bsz, N_STATE), F32),
                        pltpu.VMEM((rows, N_STATE), F32), pltpu.VMEM((rows, N_STATE), F32)],
        compiler_params=_cparams(("arbitrary",)),
        name="s5",
    )(u_tm, wts["s5_wb"], wts["s5_are"], wts["s5_aim"], wts["s5_wc"], wts["s5_d"],
      wts["w_glu"], wts["b_glu"], wts["g_ssm"])


ROUTE_E0 = N_EXPERT_GROUPS


def _outproj_kernel(h_ref, a_ref, s_ref, wout_ref, gmoe_ref, wrh_ref, wrl_ref, br_ref, tril_ref,
                    h1_ref, xn_ref, route_ref, rw_ref, cnt_ref):
    @pl.when((pl.program_id(0) == 0) & (pl.program_id(1) == 0))
    def _():
        cnt_ref[...] = jnp.zeros_like(cnt_ref)

    sub = tril_ref.shape[0]
    lane = lax.broadcasted_iota(I32, (sub, LANES), 1).astype(F32)
    big = float(LANES)
    for r in _row_chunks(h_ref.shape[0]):
        h1 = (h_ref[r, :]
              + jnp.dot(a_ref[r, :], wout_ref[:MLA_WIDTH, :], preferred_element_type=F32)
              + jnp.dot(s_ref[r, :], wout_ref[MLA_WIDTH:, :], preferred_element_type=F32))
        h1_ref[r, :] = h1
        xn = _rms(h1, gmoe_ref[...])
        xn_ref[r, :] = _pack_bf16_pairs(xn)
        x_hi = xn.astype(BF16)
        x_lo = (xn - x_hi.astype(F32)).astype(BF16)
        lg = (jnp.dot(x_hi, wrh_ref[...], preferred_element_type=F32)
              + (jnp.dot(x_lo, wrh_ref[...], preferred_element_type=F32)
                 + jnp.dot(x_hi, wrl_ref[...], preferred_element_type=F32))) + br_ref[...]
        gl = jnp.where(lane < N_EXPERT_GROUPS, lg, NEG)
        gmax = jnp.max(gl, axis=-1, keepdims=True)
        gsum = jnp.sum(jnp.exp(gl - gmax), axis=-1, keepdims=True)
        g_top_p = 1.0 / gsum
        gidx = jnp.min(jnp.where(gl == gmax, lane, big), axis=-1, keepdims=True)
        lo = ROUTE_E0 + EXPERTS_PER_GROUP * gidx
        sel = jnp.where((lane >= lo) & (lane < lo + EXPERTS_PER_GROUP), lg, NEG)
        m1 = jnp.max(sel, axis=-1, keepdims=True)
        i1 = jnp.min(jnp.where(sel == m1, lane, big), axis=-1, keepdims=True)
        sel2 = jnp.where(lane == i1, NEG, sel)
        m2 = jnp.max(sel2, axis=-1, keepdims=True)
        i2 = jnp.min(jnp.where(sel2 == m2, lane, big), axis=-1, keepdims=True)
        e21 = jnp.exp(m2 - m1)
        w1 = g_top_p / (1.0 + e21)
        w2 = g_top_p * e21 / (1.0 + e21)
        e1 = i1 - ROUTE_E0
        e2 = i2 - ROUTE_E0
        oh = jnp.where((lane == e1) | (lane == e2), 1.0, 0.0)
        before = cnt_ref[...] + jnp.dot(tril_ref[...], oh.astype(BF16), preferred_element_type=F32)
        r1 = jnp.sum(jnp.where(lane == e1, before, 0.0), axis=-1, keepdims=True)
        r2 = jnp.sum(jnp.where(lane == e2, before, 0.0), axis=-1, keepdims=True)
        cnt_ref[...] = cnt_ref[...] + jnp.sum(oh, axis=0, keepdims=True)
        route = jnp.where(lane == 0, e1, jnp.where(lane == 1, e2, jnp.where(lane == 2, r1, r2)))
        route_ref[r, :] = route.astype(I32)
        rw_ref[r, :] = jnp.where(lane == 0, w1, w2)


def _outproj(h, attn_n, ssm_tm, wts, tril, bsz, seq, tm):
    n_tok = bsz * seq
    nl = seq // tm
    row = lambda b, l: (b * nl + l, 0)
    return pl.pallas_call(
        _outproj_kernel,
        out_shape=(jax.ShapeDtypeStruct((n_tok, D_MODEL), F32),
                   jax.ShapeDtypeStruct((n_tok, PACKED), U32),
                   jax.ShapeDtypeStruct((n_tok, LANES), I32),
                   jax.ShapeDtypeStruct((n_tok, LANES), F32),
                   jax.ShapeDtypeStruct((1, LANES), F32)),
        grid=(bsz, nl),
        in_specs=[pl.BlockSpec((tm, D_MODEL), row),
                  pl.BlockSpec((tm, MLA_WIDTH), row),
                  pl.BlockSpec((tm, SSM_WIDTH), lambda b, l: (l, b)),
                  _full((D_MODEL, D_MODEL)), _full((1, D_MODEL)),
                  _full((D_MODEL, LANES)), _full((D_MODEL, LANES)), _full((1, LANES)),
                  _full(tril.shape)],
        out_specs=(pl.BlockSpec((tm, D_MODEL), row), pl.BlockSpec((tm, PACKED), row),
                   pl.BlockSpec((tm, LANES), row), pl.BlockSpec((tm, LANES), row),
                   _full((1, LANES))),
        compiler_params=_cparams(("arbitrary", "arbitrary")),
        name="outproj",
    )(h, attn_n, ssm_tm, wts["w_out"], wts["g_moe"], wts["w_r_hi"], wts["w_r_lo"], wts["b_r"], tril)


def _row_copy(src, src_row, dst, dst_row, sem):
    return pltpu.make_async_copy(src.at[pl.ds(src_row, 1)], dst.at[pl.ds(dst_row, 1)], sem)


def _dispatch_kernel(zflag_ref, dest_ref, xn_ref, xbuf_ref, zero_ref, sems, zsem, *, td, te, n_blocks):
    def zero_copy(b):
        return pltpu.make_async_copy(zero_ref, xbuf_ref.at[pl.ds(pl.multiple_of(b * te, te), te)], zsem)

    @pl.when(pl.program_id(0) == 0)
    def _():
        zero_ref[...] = jnp.zeros_like(zero_ref)

        def start(b, _):
            @pl.when(zflag_ref[b] != 0)
            def _():
                zero_copy(b).start()
            return 0

        def wait(b, _):
            @pl.when(zflag_ref[b] != 0)
            def _():
                zero_copy(b).wait()
            return 0

        lax.fori_loop(0, n_blocks, start, 0)
        lax.fori_loop(0, n_blocks, wait, 0)

    def issue(g, _):
        r0 = pl.multiple_of(g * SUBLANES, SUBLANES)
        for u in range(SUBLANES):
            for k in range(TOP_K):
                _row_copy(xn_ref, r0 + u, xbuf_ref, dest_ref[TOP_K * (r0 + u) + k],
                          sems.at[k]).start(priority=k)
        return 0

    lax.fori_loop(0, td // SUBLANES, issue, 0)
    for k in range(TOP_K):
        pltpu.make_async_copy(xn_ref, xbuf_ref.at[pl.ds(0, td)], sems.at[k]).wait()


def _dispatch(zflag, dest_flat, xn, n_slots, td, te):
    n_tok = xn.shape[0]
    n_blocks = n_slots // te
    grid_spec = pltpu.PrefetchScalarGridSpec(
        num_scalar_prefetch=1,
        grid=(n_tok // td,),
        in_specs=[pl.BlockSpec((TOP_K * td,), lambda i, zf: (i,), memory_space=pltpu.SMEM),
                  pl.BlockSpec((td, PACKED), lambda i, zf: (i, 0))],
        out_specs=pl.BlockSpec(memory_space=pl.ANY),
        scratch_shapes=[pltpu.VMEM((te, PACKED), U32), pltpu.SemaphoreType.DMA((TOP_K,)),
                        pltpu.SemaphoreType.DMA],
    )
    return pl.pallas_call(
        functools.partial(_dispatch_kernel, td=td, te=te, n_blocks=n_blocks),
        out_shape=jax.ShapeDtypeStruct((n_slots, PACKED), U32),
        grid_spec=grid_spec,
        compiler_params=_cparams(("arbitrary",)),
        name="dispatch",
    )(zflag, dest_flat, xn)


def _experts_kernel(be_ref, x_ref, wg_ref, wu_ref, wd_ref, y_ref):
    del be_ref
    xb = _unpack_bf16_pairs(x_ref[...]).astype(BF16)
    g = jnp.dot(xb, wg_ref[...].astype(BF16), preferred_element_type=F32)
    u = jnp.dot(xb, wu_ref[...].astype(BF16), preferred_element_type=F32)
    hdn = (g * _sigmoid(g) * u).astype(BF16)
    y_ref[...] = _pack_bf16_pairs(jnp.dot(hdn, wd_ref[...].astype(BF16), preferred_element_type=F32))


def _experts(block_e, xbuf, w_gate, w_up, w_down, layer, te):
    n_slots = xbuf.shape[0]
    grid_spec = pltpu.PrefetchScalarGridSpec(
        num_scalar_prefetch=1,
        grid=(n_slots // te,),
        in_specs=[pl.BlockSpec((te, PACKED), lambda i, be: (i, 0)),
                  pl.BlockSpec((None, None, D_MODEL, D_EXPERT), lambda i, be: (layer, be[i], 0, 0)),
                  pl.BlockSpec((None, None, D_MODEL, D_EXPERT), lambda i, be: (layer, be[i], 0, 0)),
                  pl.BlockSpec((None, None, D_EXPERT, D_MODEL), lambda i, be: (layer, be[i], 0, 0))],
        out_specs=pl.BlockSpec((te, PACKED), lambda i, be: (i, 0)),
    )
    return pl.pallas_call(
        _experts_kernel,
        out_shape=jax.ShapeDtypeStruct((n_slots, PACKED), U32),
        grid_spec=grid_spec,
        compiler_params=_cparams(("arbitrary",)),
        name="experts",
    )(block_e, xbuf, w_gate, w_up, w_down)


SC_CORES = 2
SC_SUBCORES = 16
SC_WORKERS = SC_CORES * SC_SUBCORES
SC_CHUNK = 64


def _sc_gather_rows(table, idx):
    n_rows = idx.shape[0]
    width = table.shape[1]
    per_w = n_rows // SC_WORKERS
    n_chunks = per_w // SC_CHUNK
    mesh = plsc.VectorSubcoreMesh(core_axis_name="c", subcore_axis_name="s",
                                  num_cores=SC_CORES, num_subcores=SC_SUBCORES)

    @functools.partial(
        pl.kernel, mesh=mesh,
        out_type=jax.ShapeDtypeStruct((n_rows, width), table.dtype),
        scratch_types=[pltpu.VMEM((per_w,), I32), pltpu.VMEM((SC_CHUNK, width), table.dtype),
                       pltpu.SemaphoreType.DMA],
        name="sc_gather_rows",
    )
    def gather_kernel(table_hbm, idx_hbm, out_hbm, idx_v, rows_v, sem):
        wid = lax.axis_index("s") * SC_CORES + lax.axis_index("c")
        base = wid * per_w
        pltpu.sync_copy(idx_hbm.at[pl.ds(base, per_w)], idx_v)

        @pl.loop(0, n_chunks)
        def _(j):
            off = pl.multiple_of(j * SC_CHUNK, SC_CHUNK)
            pltpu.async_copy(table_hbm.at[idx_v.at[pl.ds(off, SC_CHUNK)]], rows_v, sem).wait()
            pltpu.sync_copy(rows_v, out_hbm.at[pl.ds(base + off, SC_CHUNK)])

    return gather_kernel(table, idx)


def _combine_kernel(y0_ref, y1_ref, h1_ref, rw_ref, p_ref, gple_ref, wgate_ref, wproj_ref,
                    gfin_ref, o_ref, *, final):
    rw = rw_ref[...]
    h2 = (h1_ref[...] + rw[:, 0:1] * _unpack_bf16_pairs(y0_ref[...])
          + rw[:, 1:2] * _unpack_bf16_pairs(y1_ref[...]))
    hn = _rms(h2, gple_ref[...]).astype(BF16)
    gate = _sigmoid(jnp.dot(hn, wgate_ref[...], preferred_element_type=F32))
    ple = jnp.dot(p_ref[...].astype(BF16), wproj_ref[...], preferred_element_type=F32)
    h3 = h2 + ple * gate
    if final:
        h3 = _rms(h3, gfin_ref[...])
    o_ref[...] = h3


def _combine(dest_flat, ybuf, h1, rw, p_all, layer, wts, g_final, tf, final):
    n_tok = h1.shape[0]
    n_steps = n_tok // tf
    row = lambda i: (i, 0)
    dest_kmajor = dest_flat.reshape(n_tok, TOP_K).T.reshape(-1)
    yg = _sc_gather_rows(ybuf, dest_kmajor)
    return pl.pallas_call(
        functools.partial(_combine_kernel, final=final),
        out_shape=jax.ShapeDtypeStruct((n_tok, D_MODEL), F32),
        grid=(n_steps,),
        in_specs=[pl.BlockSpec((tf, PACKED), row),
                  pl.BlockSpec((tf, PACKED), lambda i: (n_steps + i, 0)),
                  pl.BlockSpec((tf, D_MODEL), row), pl.BlockSpec((tf, LANES), row),
                  pl.BlockSpec((None, tf, PLE_DIM), lambda i: (layer, i, 0)),
                  _full((1, D_MODEL)), _full((D_MODEL, D_MODEL)), _full((PLE_DIM, D_MODEL)),
                  _full((1, D_MODEL))],
        out_specs=pl.BlockSpec((tf, D_MODEL), row),
        compiler_params=_cparams(("parallel",)),
        name="combine",
    )(yg, yg, h1, rw, p_all, wts["g_ple"], wts["w_ple_gate"], wts["w_ple_proj"], g_final)


def _layer_weights(i, g_mix_norm, w_in, g_q_lat, w_q_up, g_kv_lat, w_kv_up,
                   ssm_a_re, ssm_a_im, ssm_b_re, ssm_b_im, ssm_c_re, ssm_c_im, ssm_d, ssm_log_step,
                   w_glu, b_glu, g_attn_out, g_ssm_out, w_out, g_moe_norm,
                   w_group_router, b_group_router, w_expert_router, b_expert_router,
                   g_ple_norm, w_ple_gate, w_ple_proj):
    off_kv, off_kr, off_ssm = Q_LORA, Q_LORA + KV_LORA, Q_LORA + KV_LORA + QK_ROPE
    wi = w_in[i]
    kr_block = jnp.zeros((D_MODEL, HEAD_PAD), F32).at[:, ROPE_LANE0:ROPE_LANE0 + QK_ROPE].set(wi[:, off_kr:off_ssm])
    w_in_p = jnp.concatenate([wi[:, :off_kr], kr_block, wi[:, off_ssm:]], axis=1)
    wq = w_q_up[i].reshape(Q_LORA, N_HEADS, QK_NOPE + QK_ROPE) * (ATTN_SCALE * math.log2(math.e))
    wq = jnp.pad(wq, ((0, 0), (0, 0), (0, HEAD_PAD - QK_NOPE - QK_ROPE))).reshape(Q_LORA, N_HEADS * HEAD_PAD)
    wkv = w_kv_up[i].reshape(KV_LORA, N_HEADS, QK_NOPE + V_HEAD)
    wk = jnp.pad(wkv[..., :QK_NOPE], ((0, 0), (0, 0), (0, HEAD_PAD - QK_NOPE))).reshape(KV_LORA, N_HEADS * HEAD_PAD)
    wv = jnp.pad(wkv[..., QK_NOPE:], ((0, 0), (0, 0), (0, HEAD_PAD - V_HEAD))).reshape(KV_LORA, N_HEADS * HEAD_PAD)
    lam = lax.complex(ssm_a_re[i], ssm_a_im[i])
    lam_bar = jnp.exp(lam * jnp.exp(ssm_log_step[i])[:, None])
    b_fac = (lam_bar - 1.0) / lam
    bc = lax.complex(ssm_b_re[i], ssm_b_im[i]) * b_fac[:, :, None]
    n_per = SSM_GROUPS // S5_SLABS

    def block_diag(blocks):
        s, n, a, b = blocks.shape
        eye = jnp.eye(n, dtype=blocks.dtype)
        return jnp.einsum("snab,nm->snamb", blocks, eye).reshape(s, n * a, n * b)

    b_hp_re = jnp.real(bc).transpose(0, 2, 1).reshape(S5_SLABS, n_per, SSM_CH, SSM_STATE)
    b_hp_im = jnp.imag(bc).transpose(0, 2, 1).reshape(S5_SLABS, n_per, SSM_CH, SSM_STATE)
    s5_wb = jnp.concatenate([block_diag(b_hp_re), block_diag(b_hp_im)], axis=-1)
    c_re = ssm_c_re[i].transpose(0, 2, 1).reshape(S5_SLABS, n_per, SSM_STATE, SSM_CH)
    c_im = ssm_c_im[i].transpose(0, 2, 1).reshape(S5_SLABS, n_per, SSM_STATE, SSM_CH)
    s5_wc = jnp.stack([block_diag(c_re), -block_diag(c_im)])
    w_r = jnp.zeros((D_MODEL, LANES), F32)
    w_r = w_r.at[:, :N_EXPERT_GROUPS].set(w_group_router[i])
    w_r = w_r.at[:, ROUTE_E0:ROUTE_E0 + N_EXPERTS].set(w_expert_router[i])
    w_r_hi = w_r.astype(BF16)
    w_r_lo = (w_r - w_r_hi.astype(F32)).astype(BF16)
    b_r = jnp.zeros((1, LANES), F32)
    b_r = b_r.at[0, :N_EXPERT_GROUPS].set(b_group_router[i])
    b_r = b_r.at[0, ROUTE_E0:ROUTE_E0 + N_EXPERTS].set(b_expert_router[i])
    return dict(
        g_mix=g_mix_norm[i][None], w_in=w_in_p.astype(BF16),
        g_q=g_q_lat[i][None], w_q=wq.astype(BF16),
        g_kv=g_kv_lat[i][None], w_kv=jnp.concatenate([wk, wv], axis=1).astype(BF16),
        s5_wb=s5_wb.astype(BF16), s5_wc=s5_wc.astype(BF16),
        s5_are=jnp.real(lam_bar).reshape(1, N_STATE), s5_aim=jnp.imag(lam_bar).reshape(1, N_STATE),
        s5_d=ssm_d[i].reshape(1, SSM_WIDTH), w_glu=w_glu[i].astype(BF16), b_glu=b_glu[i][None],
        g_attn=g_attn_out[i][None], g_ssm=g_ssm_out[i][None],
        w_out=w_out[i].astype(BF16), g_moe=g_moe_norm[i][None],
        w_r_hi=w_r_hi, w_r_lo=w_r_lo, b_r=b_r,
        g_ple=g_ple_norm[i][None], w_ple_gate=w_ple_gate[i].astype(BF16),
        w_ple_proj=w_ple_proj[i].astype(BF16),
    )


def _slot_layout(route, cnt, n_blocks, te):
    counts = cnt[0, :N_EXPERTS].astype(I32)
    pcounts = (counts + te - 1) // te * te
    pends = jnp.cumsum(pcounts)
    pstarts = pends - pcounts
    eid = route[:, :TOP_K]
    rank = route[:, TOP_K:2 * TOP_K]
    start = jnp.sum(jnp.where(eid[:, :, None] == jnp.arange(N_EXPERTS)[None, None, :],
                              pstarts[None, None, :], 0), axis=-1)
    dest = (start + rank).reshape(-1).astype(I32)
    blk0 = jnp.arange(n_blocks, dtype=I32) * te
    block_e = jnp.minimum(jnp.sum((pends[None, :] <= blk0[:, None]).astype(I32), axis=1), N_EXPERTS - 1)
    fill = jnp.clip((pstarts + counts)[block_e] - blk0, 0, te)
    zflag = (fill < te).astype(I32)
    return dest, block_e, zflag


def kernel(x, p, positions, g_mix_norm, w_in, g_q_lat, w_q_up, g_kv_lat, w_kv_up, ssm_a_re, ssm_a_im, ssm_b_re, ssm_b_im, ssm_c_re, ssm_c_im, ssm_d, ssm_log_step, w_glu, b_glu, g_attn_out, g_ssm_out, w_out, g_moe_norm, w_group_router, b_group_router, w_expert_router, b_expert_router, w_exp_gate, w_exp_up, w_exp_down, g_ple_norm, w_ple_gate, w_ple_proj, g_final):
    bsz, seq, _ = x.shape
    depth = p.shape[0]
    n_tok = bsz * seq
    t = _tiles(n_tok, seq)
    n_slots = TOP_K * n_tok + N_EXPERTS * t["te"]
    n_blocks = n_slots // t["te"]

    cosf, sinf = _rope_tables(positions.reshape(n_tok, 1), t["tm"])
    sub = min(SUB_ROWS, t["tm"])
    tril = jnp.tril(jnp.ones((sub, sub), F32), k=-1).astype(BF16)
    g_fin = g_final[None]

    h = x.reshape(n_tok, D_MODEL)
    for i in range(depth):
        wts = _layer_weights(i, g_mix_norm, w_in, g_q_lat, w_q_up, g_kv_lat, w_kv_up,
                             ssm_a_re, ssm_a_im, ssm_b_re, ssm_b_im, ssm_c_re, ssm_c_im, ssm_d,
                             ssm_log_step, w_glu, b_glu, g_attn_out, g_ssm_out, w_out, g_moe_norm,
                             w_group_router, b_group_router, w_expert_router, b_expert_router,
                             g_ple_norm, w_ple_gate, w_ple_proj)
        q, k, v, u_tm = _inproj(h, cosf, sinf, wts, bsz, seq, t["tm"])
        attn_n = _attention(q, k, v, wts["g_attn"], bsz, seq, t["tq"])
        ssm_n = _s5(u_tm.reshape(seq, bsz, SSM_WIDTH), wts, bsz, seq, t["lc"])
        h1, xn, route, rw, cnt = _outproj(h, attn_n, ssm_n.reshape(seq, bsz * SSM_WIDTH), wts, tril,
                                          bsz, seq, t["tm"])
        dest, block_e, zflag = _slot_layout(route, cnt, n_blocks, t["te"])
        xbuf = _dispatch(zflag, dest, xn, n_slots, t["td"], t["te"])
        ybuf = _experts(block_e, xbuf, w_exp_gate, w_exp_up, w_exp_down, i, t["te"])
        h = _combine(dest, ybuf, h1, rw, p.reshape(depth, n_tok, PLE_DIM), i, wts, g_fin, t["tf"],
                     final=(i == depth - 1))
    return h.reshape(bsz, seq, D_MODEL)
```

```python
import functools
import math

import jax
import jax.numpy as jnp
from jax import lax
from jax.experimental import pallas as pl
from jax.experimental.pallas import tpu as pltpu
from jax.experimental.pallas import tpu_sc as plsc

F32 = jnp.float32
BF16 = jnp.bfloat16
I32 = jnp.int32
U32 = jnp.uint32

D_MODEL = 1024
MLA_WIDTH = 512
SSM_WIDTH = 512
V_HEAD = 64
N_HEADS = 8
QK_NOPE = 64
QK_ROPE = 32
HALF_ROPE = QK_ROPE // 2
Q_LORA = 256
KV_LORA = 128
ROPE_BASE = 10000.0
ATTN_SCALE = 1.0 / math.sqrt(QK_NOPE + QK_ROPE)
SSM_CH = 16
SSM_GROUPS = 32
SSM_STATE = 64
N_EXPERT_GROUPS = 4
EXPERTS_PER_GROUP = 8
N_EXPERTS = 32
TOP_K = 2
D_EXPERT = 256
PLE_DIM = 256
EPS = 1e-6

LANES = 128
SUBLANES = 8
HEAD_PAD = 128
ROPE_LANE0 = QK_NOPE
N_STATE = SSM_GROUPS * SSM_STATE
PACKED = D_MODEL // 2
NEG = -1e30
VMEM_LIMIT = 56 * 1024 * 1024
SUB_ROWS = 512


def _tiles(n_tok, seq):
    return dict(
        tm=min(1024, seq),
        tq=min(512, seq),
        lc=min(32, seq),
        te=512,
        tf=min(512, n_tok),
    )


def _rms(x, g):
    ms = jnp.mean(x * x, axis=-1, keepdims=True)
    return x * lax.rsqrt(ms + EPS) * g


def _sigmoid(x):
    return 1.0 / (1.0 + jnp.exp(-x))


def _pack_bf16_pairs(x):
    lo = lax.bitcast_convert_type(x[:, :PACKED].astype(BF16).astype(F32), U32)
    hi = lax.bitcast_convert_type(x[:, PACKED:].astype(BF16).astype(F32), U32)
    return (lo >> 16) | hi


def _unpack_bf16_pairs(w):
    lo = lax.bitcast_convert_type(w << 16, F32)
    hi = lax.bitcast_convert_type(w & jnp.uint32(0xFFFF0000), F32)
    return jnp.concatenate([lo, hi], axis=1)


def _cparams(sem):
    return pltpu.CompilerParams(dimension_semantics=sem, vmem_limit_bytes=VMEM_LIMIT)


def _full(shape):
    nd = len(shape)
    return pl.BlockSpec(shape, lambda *_: (0,) * nd)


def _rope_table_kernel(pos_ref, invf_ref, cos_ref, sin_ref):
    ang = pos_ref[...].astype(F32) * invf_ref[...]
    cos_ref[...] = jnp.cos(ang)
    sin_ref[...] = jnp.sin(ang)


def _rope_tables(pos_col, tm):
    n_tok = pos_col.shape[0]
    lane = jnp.arange(LANES)
    in_rope = (lane >= ROPE_LANE0) & (lane < ROPE_LANE0 + QK_ROPE)
    freq = ROPE_BASE ** (-((lane - ROPE_LANE0) % HALF_ROPE).astype(F32) / HALF_ROPE)
    invf = jnp.where(in_rope, freq, 0.0).astype(F32)[None, :]
    return pl.pallas_call(
        _rope_table_kernel,
        out_shape=(jax.ShapeDtypeStruct((n_tok, LANES), F32),) * 2,
        grid=(n_tok // tm,),
        in_specs=[pl.BlockSpec((tm, 1), lambda i: (i, 0)), _full((1, LANES))],
        out_specs=(pl.BlockSpec((tm, LANES), lambda i: (i, 0)),) * 2,
        compiler_params=_cparams(("parallel",)),
        name="rope_tables",
    )(pos_col, invf)


def _rope(x, cosf, s_lo, s_hi):
    n = x.shape[-1]
    return (x * cosf + pltpu.roll(x, n - HALF_ROPE, 1) * s_lo + pltpu.roll(x, HALF_ROPE, 1) * s_hi)


def _row_chunks(n_rows):
    sub = min(SUB_ROWS, n_rows)
    return [slice(c * sub, (c + 1) * sub) for c in range(n_rows // sub)]


def _inproj_kernel(h_ref, cos_ref, sin_ref, gmix_ref, win_ref, gq_ref, wq_ref, gkv_ref, wkv_ref,
                   q_ref, k_ref, v_ref, u_ref):
    lane = lax.broadcasted_iota(I32, (1, LANES), 1)
    ones_lane = jnp.where(lane == V_HEAD, 1.0, 0.0)
    for r in _row_chunks(h_ref.shape[0]):
        xn = _rms(h_ref[r, :], gmix_ref[...]).astype(BF16)
        z = jnp.dot(xn, win_ref[...], preferred_element_type=F32)
        cosf = cos_ref[r, :]
        sinf = sin_ref[r, :]
        s_lo = jnp.where((lane >= ROPE_LANE0) & (lane < ROPE_LANE0 + HALF_ROPE), -sinf, 0.0)
        s_hi = jnp.where((lane >= ROPE_LANE0 + HALF_ROPE) & (lane < ROPE_LANE0 + QK_ROPE), sinf, 0.0)
        qn = _rms(z[:, :Q_LORA], gq_ref[...]).astype(BF16)
        q = jnp.dot(qn, wq_ref[...], preferred_element_type=F32)
        kvn = _rms(z[:, Q_LORA:Q_LORA + KV_LORA], gkv_ref[...]).astype(BF16)
        kv = jnp.dot(kvn, wkv_ref[...], preferred_element_type=F32)
        kpe = _rope(z[:, Q_LORA + KV_LORA:Q_LORA + KV_LORA + HEAD_PAD], cosf, s_lo, s_hi)
        for hd in range(N_HEADS):
            sl = slice(hd * HEAD_PAD, (hd + 1) * HEAD_PAD)
            vsl = slice(N_HEADS * HEAD_PAD + hd * HEAD_PAD, N_HEADS * HEAD_PAD + (hd + 1) * HEAD_PAD)
            q_ref[r, sl] = _rope(q[:, sl], cosf, s_lo, s_hi).astype(BF16)
            k_ref[r, sl] = (kv[:, sl] + kpe).astype(BF16)
            v_ref[r, sl] = (kv[:, vsl] + ones_lane).astype(BF16)
        u_ref[r, :] = z[:, Q_LORA + KV_LORA + HEAD_PAD:]


def _inproj(h, cosf, sinf, wts, bsz, seq, tm):
    n_tok = bsz * seq
    nl = seq // tm
    row = lambda b, l: (b * nl + l, 0)
    qk_w = N_HEADS * HEAD_PAD
    return pl.pallas_call(
        _inproj_kernel,
        out_shape=(jax.ShapeDtypeStruct((n_tok, qk_w), BF16),
                   jax.ShapeDtypeStruct((n_tok, qk_w), BF16),
                   jax.ShapeDtypeStruct((n_tok, qk_w), BF16),
                   jax.ShapeDtypeStruct((seq, bsz * SSM_WIDTH), F32)),
        grid=(bsz, nl),
        in_specs=[pl.BlockSpec((tm, D_MODEL), row),
                  pl.BlockSpec((tm, LANES), row), pl.BlockSpec((tm, LANES), row),
                  _full((1, D_MODEL)), _full((D_MODEL, D_MODEL)),
                  _full((1, Q_LORA)), _full((Q_LORA, qk_w)),
                  _full((1, KV_LORA)), _full((KV_LORA, 2 * qk_w))],
        out_specs=(pl.BlockSpec((tm, qk_w), row), pl.BlockSpec((tm, qk_w), row),
                   pl.BlockSpec((tm, qk_w), row),
                   pl.BlockSpec((tm, SSM_WIDTH), lambda b, l: (l, b))),
        compiler_params=_cparams(("parallel", "parallel")),
        name="inproj",
    )(h, cosf, sinf, wts["g_mix"], wts["w_in"], wts["g_q"], wts["w_q"], wts["g_kv"], wts["w_kv"])


ATTN_RB = 32
ATTN_GROUP = 4


def _attn_kernel(q_ref, k_ref, v_ref, g_ref, o_ref, s_ref, p_ref, m_ref, al_ref, acc_ref, out_ref, *, tq):
    qi = pl.program_id(1)
    col = lax.broadcasted_iota(I32, (ATTN_RB, tq), 1)
    row = lax.broadcasted_iota(I32, (ATTN_RB, tq), 0)
    for pr in range(N_HEADS // ATTN_GROUP):
        heads = [ATTN_GROUP * pr + a for a in range(ATTN_GROUP)]
        m_ref[...] = jnp.full(m_ref.shape, NEG, F32)
        acc_ref[...] = jnp.zeros(acc_ref.shape, F32)

        def scores(j, slot, heads=heads):
            rows = pl.ds(pl.multiple_of(j * tq, tq), tq)
            for a, hd in enumerate(heads):
                hs = slice(hd * HEAD_PAD, (hd + 1) * HEAD_PAD)
                s_ref[slot, a] = lax.dot_general(q_ref[:, hs], k_ref[rows, hs], (((1,), (1,)), ((), ())),
                                                 preferred_element_type=F32)

        def softmax_pv(j, slot, masked, heads=heads):
            rows = pl.ds(pl.multiple_of(j * tq, tq), tq)
            for a in range(ATTN_GROUP):
                for c in range(tq // ATTN_RB):
                    r = slice(c * ATTN_RB, (c + 1) * ATTN_RB)
                    s = s_ref[slot, a, r, :]
                    if masked:
                        s = jnp.where(col <= row + c * ATTN_RB, s, NEG)
                    m_old = m_ref[a, r, :]
                    m_new = jnp.maximum(m_old, jnp.max(s, axis=-1, keepdims=True))
                    p_ref[a, r, :] = jnp.exp2(s - m_new).astype(BF16)
                    al_ref[a, r, :] = jnp.exp2(m_old - m_new)
                    m_ref[a, r, :] = m_new
            for a, hd in enumerate(heads):
                pv = jnp.dot(p_ref[a], v_ref[rows, hd * HEAD_PAD:(hd + 1) * HEAD_PAD],
                             preferred_element_type=F32)
                acc_ref[a] = al_ref[a] * acc_ref[a] + pv

        def two_tiles(jj, carry):
            t = 2 * jj
            scores(t + 1, 1)
            softmax_pv(t, 0, masked=False)
            scores(t + 2, 0)
            softmax_pv(t + 1, 1, masked=False)
            return carry

        scores(0, 0)
        lax.fori_loop(0, qi // 2, two_tiles, 0)

        @pl.when(qi % 2 == 0)
        def _():
            softmax_pv(qi, 0, masked=True)

        @pl.when(qi % 2 == 1)
        def _():
            scores(qi, 1)
            softmax_pv(qi - 1, 0, masked=False)
            softmax_pv(qi, 1, masked=True)

        for a, hd in enumerate(heads):
            acc = acc_ref[a]
            out_ref[:, hd * V_HEAD:(hd + 1) * V_HEAD] = acc[:, :V_HEAD] / acc[:, V_HEAD:V_HEAD + 1]
    o_ref[...] = _rms(out_ref[...], g_ref[...]).astype(BF16)


def _attention(q, k, v, g_attn, bsz, seq, tq):
    n_tok = bsz * seq
    nq = seq // tq
    qk_w = N_HEADS * HEAD_PAD
    return pl.pallas_call(
        functools.partial(_attn_kernel, tq=tq),
        out_shape=jax.ShapeDtypeStruct((n_tok, MLA_WIDTH), BF16),
        grid=(bsz, nq),
        in_specs=[pl.BlockSpec((tq, qk_w), lambda b, i: (b * nq + i, 0)),
                  pl.BlockSpec((seq, qk_w), lambda b, i: (b, 0)),
                  pl.BlockSpec((seq, qk_w), lambda b, i: (b, 0)),
                  _full((1, MLA_WIDTH))],
        out_specs=pl.BlockSpec((tq, MLA_WIDTH), lambda b, i: (b * nq + i, 0)),
        scratch_shapes=[pltpu.VMEM((2, ATTN_GROUP, tq, tq), F32), pltpu.VMEM((ATTN_GROUP, tq, tq), BF16),
                        pltpu.VMEM((ATTN_GROUP, tq, 1), F32), pltpu.VMEM((ATTN_GROUP, tq, 1), F32),
                        pltpu.VMEM((ATTN_GROUP, tq, HEAD_PAD), F32), pltpu.VMEM((tq, MLA_WIDTH), F32)],
        compiler_params=_cparams(("parallel", "parallel")),
        name="attn",
    )(q, k, v, g_attn)


S5_SLABS = 4
S5_SLAB_U = SSM_WIDTH // S5_SLABS
S5_SLAB_X = N_STATE // S5_SLABS


def _s5_kernel(u_ref, wb_ref, are_ref, aim_ref, wc_ref, d_ref, wglu_ref, bglu_ref, g_ref,
               o_ref, xr_ref, xi_ref, sr_ref, si_ref, *, lc, nb):
    @pl.when(pl.program_id(0) == 0)
    def _():
        xr_ref[...] = jnp.zeros_like(xr_ref)
        xi_ref[...] = jnp.zeros_like(xi_ref)

    rows = lc * nb
    u = u_ref[...].reshape(rows, SSM_WIDTH)
    ub = u.astype(BF16)
    def drive(j):
        bu = jnp.dot(ub[:, j * S5_SLAB_U:(j + 1) * S5_SLAB_U], wb_ref[j], preferred_element_type=F32)
        sr_ref[:, j * S5_SLAB_X:(j + 1) * S5_SLAB_X] = bu[:, :S5_SLAB_X]
        si_ref[:, j * S5_SLAB_X:(j + 1) * S5_SLAB_X] = bu[:, S5_SLAB_X:]

    ys = []
    drive(0)
    for c in range(S5_SLABS):
        if c + 1 < S5_SLABS:
            drive(c + 1)
        sl = slice(c * S5_SLAB_X, (c + 1) * S5_SLAB_X)
        ar = jnp.broadcast_to(are_ref[:, sl], (nb, S5_SLAB_X))
        ai = jnp.broadcast_to(aim_ref[:, sl], (nb, S5_SLAB_X))
        xr = xr_ref[:, sl]
        xi = xi_ref[:, sl]
        for t in range(lc):
            r = slice(t * nb, (t + 1) * nb)
            xr, xi = (ar * xr - ai * xi + sr_ref[r, sl], ar * xi + ai * xr + si_ref[r, sl])
            sr_ref[r, sl] = xr
            si_ref[r, sl] = xi
        xr_ref[:, sl] = xr
        xi_ref[:, sl] = xi
        ys.append(jnp.dot(sr_ref[:, sl].astype(BF16), wc_ref[0, c], preferred_element_type=F32)
                  + jnp.dot(si_ref[:, sl].astype(BF16), wc_ref[1, c], preferred_element_type=F32))
    y = jnp.concatenate(ys, axis=-1) + d_ref[...] * u
    y = 0.5 * y * (1.0 + jnp.tanh(math.sqrt(2.0 / math.pi) * (y + 0.044715 * (y * y * y))))
    gate = jnp.dot(y.astype(BF16), wglu_ref[...], preferred_element_type=F32) + bglu_ref[...]
    y = y * _sigmoid(gate)
    o_ref[...] = _rms(y, g_ref[...]).astype(BF16).reshape(lc, nb, SSM_WIDTH)


def _s5(u_tm, wts, bsz, seq, lc):
    rows = lc * bsz
    return pl.pallas_call(
        functools.partial(_s5_kernel, lc=lc, nb=bsz),
        out_shape=jax.ShapeDtypeStruct((seq, bsz, SSM_WIDTH), BF16),
        grid=(seq // lc,),
        in_specs=[pl.BlockSpec((lc, bsz, SSM_WIDTH), lambda i: (i, 0, 0)),
                  _full((S5_SLABS, S5_SLAB_U, 2 * S5_SLAB_X)),
                  _full((1, N_STATE)), _full((1, N_STATE)),
                  _full((2, S5_SLABS, S5_SLAB_X, S5_SLAB_U)),
                  _full((1, SSM_WIDTH)), _full((SSM_WIDTH, SSM_WIDTH)), _full((1, SSM_WIDTH)),
                  _full((1, SSM_WIDTH))],
        out_specs=pl.BlockSpec((lc, bsz, SSM_WIDTH), lambda i: (i, 0, 0)),
        scratch_shapes=[pltpu.VMEM((bsz, N_STATE), F32), pltpu.VMEM((bsz, N_STATE), F32),
                        pltpu.VMEM((rows, N_STATE), F32), pltpu.VMEM((rows, N_STATE), F32)],
        compiler_params=_cparams(("arbitrary",)),
        name="s5",
    )(u_tm, wts["s5_wb"], wts["s5_are"], wts["s5_aim"], wts["s5_wc"], wts["s5_d"],
      wts["w_glu"], wts["b_glu"], wts["g_ssm"])


ROUTE_E0 = N_EXPERT_GROUPS


def _outproj_kernel(h_ref, a_ref, s_ref, wout_ref, gmoe_ref, wrh_ref, wrl_ref, br_ref, tril_ref,
                    h1_ref, xn_ref, route_ref, rw_ref, cnt_ref):
    @pl.when((pl.program_id(0) == 0) & (pl.program_id(1) == 0))
    def _():
        cnt_ref[...] = jnp.zeros_like(cnt_ref)

    sub = tril_ref.shape[0]
    lane = lax.broadcasted_iota(I32, (sub, LANES), 1).astype(F32)
    big = float(LANES)
    for r in _row_chunks(h_ref.shape[0]):
        h1 = (h_ref[r, :]
              + jnp.dot(a_ref[r, :], wout_ref[:MLA_WIDTH, :], preferred_element_type=F32)
              + jnp.dot(s_ref[r, :], wout_ref[MLA_WIDTH:, :], preferred_element_type=F32))
        h1_ref[r, :] = h1
        xn = _rms(h1, gmoe_ref[...])
        xn_ref[r, :] = _pack_bf16_pairs(xn)
        x_hi = xn.astype(BF16)
        x_lo = (xn - x_hi.astype(F32)).astype(BF16)
        lg = (jnp.dot(x_hi, wrh_ref[...], preferred_element_type=F32)
              + (jnp.dot(x_lo, wrh_ref[...], preferred_element_type=F32)
                 + jnp.dot(x_hi, wrl_ref[...], preferred_element_type=F32))) + br_ref[...]
        gl = jnp.where(lane < N_EXPERT_GROUPS, lg, NEG)
        gmax = jnp.max(gl, axis=-1, keepdims=True)
        gsum = jnp.sum(jnp.exp(gl - gmax), axis=-1, keepdims=True)
        g_top_p = 1.0 / gsum
        gidx = jnp.min(jnp.where(gl == gmax, lane, big), axis=-1, keepdims=True)
        lo = ROUTE_E0 + EXPERTS_PER_GROUP * gidx
        sel = jnp.where((lane >= lo) & (lane < lo + EXPERTS_PER_GROUP), lg, NEG)
        m1 = jnp.max(sel, axis=-1, keepdims=True)
        i1 = jnp.min(jnp.where(sel == m1, lane, big), axis=-1, keepdims=True)
        sel2 = jnp.where(lane == i1, NEG, sel)
        m2 = jnp.max(sel2, axis=-1, keepdims=True)
        i2 = jnp.min(jnp.where(sel2 == m2, lane, big), axis=-1, keepdims=True)
        e21 = jnp.exp(m2 - m1)
        w1 = g_top_p / (1.0 + e21)
        w2 = g_top_p * e21 / (1.0 + e21)
        e1 = i1 - ROUTE_E0
        e2 = i2 - ROUTE_E0
        oh = jnp.where((lane == e1) | (lane == e2), 1.0, 0.0)
        before = cnt_ref[...] + jnp.dot(tril_ref[...], oh.astype(BF16), preferred_element_type=F32)
        r1 = jnp.sum(jnp.where(lane == e1, before, 0.0), axis=-1, keepdims=True)
        r2 = jnp.sum(jnp.where(lane == e2, before, 0.0), axis=-1, keepdims=True)
        cnt_ref[...] = cnt_ref[...] + jnp.sum(oh, axis=0, keepdims=True)
        route = jnp.where(lane == 0, e1, jnp.where(lane == 1, e2, jnp.where(lane == 2, r1, r2)))
        route_ref[r, :] = route.astype(I32)
        rw_ref[r, :] = jnp.where(lane == 0, w1, w2)


def _outproj(h, attn_n, ssm_tm, wts, tril, bsz, seq, tm):
    n_tok = bsz * seq
    nl = seq // tm
    row = lambda b, l: (b * nl + l, 0)
    return pl.pallas_call(
        _outproj_kernel,
        out_shape=(jax.ShapeDtypeStruct((n_tok, D_MODEL), F32),
                   jax.ShapeDtypeStruct((n_tok, PACKED), U32),
                   jax.ShapeDtypeStruct((n_tok, LANES), I32),
                   jax.ShapeDtypeStruct((n_tok, LANES), F32),
                   jax.ShapeDtypeStruct((1, LANES), F32)),
        grid=(bsz, nl),
        in_specs=[pl.BlockSpec((tm, D_MODEL), row),
                  pl.BlockSpec((tm, MLA_WIDTH), row),
                  pl.BlockSpec((tm, SSM_WIDTH), lambda b, l: (l, b)),
                  _full((D_MODEL, D_MODEL)), _full((1, D_MODEL)),
                  _full((D_MODEL, LANES)), _full((D_MODEL, LANES)), _full((1, LANES)),
                  _full(tril.shape)],
        out_specs=(pl.BlockSpec((tm, D_MODEL), row), pl.BlockSpec((tm, PACKED), row),
                   pl.BlockSpec((tm, LANES), row), pl.BlockSpec((tm, LANES), row),
                   _full((1, LANES))),
        compiler_params=_cparams(("arbitrary", "arbitrary")),
        name="outproj",
    )(h, attn_n, ssm_tm, wts["w_out"], wts["g_moe"], wts["w_r_hi"], wts["w_r_lo"], wts["b_r"], tril)


SC_CORES = 2
SC_SUBCORES = 16
SC_WORKERS = SC_CORES * SC_SUBCORES
SC_CHUNK = 64


def _sc_mesh():
    return plsc.VectorSubcoreMesh(core_axis_name="c", subcore_axis_name="s",
                                  num_cores=SC_CORES, num_subcores=SC_SUBCORES)


def _sc_worker_id():
    return lax.axis_index("s") * SC_CORES + lax.axis_index("c")


def _dispatch(dest_k, pad_rows, xn, n_slots):
    n_tok, width = xn.shape
    n_pad = pad_rows.shape[0]
    per_w = n_tok // SC_WORKERS
    per_p = n_pad // SC_WORKERS

    @functools.partial(
        pl.kernel, mesh=_sc_mesh(),
        out_type=jax.ShapeDtypeStruct((n_slots, width), xn.dtype),
        scratch_types=[pltpu.VMEM((per_w,), I32), pltpu.VMEM((per_w,), I32), pltpu.VMEM((per_p,), I32),
                       pltpu.VMEM((SC_CHUNK, width), xn.dtype)],
        name="sc_dispatch_rows",
    )
    def scatter_kernel(xn_hbm, d0_hbm, d1_hbm, dp_hbm, out_hbm, i0_v, i1_v, ip_v, rows_v):
        wid = _sc_worker_id()
        base = wid * per_w
        pltpu.sync_copy(d0_hbm.at[pl.ds(base, per_w)], i0_v)
        pltpu.sync_copy(d1_hbm.at[pl.ds(base, per_w)], i1_v)
        pltpu.sync_copy(dp_hbm.at[pl.ds(wid * per_p, per_p)], ip_v)

        @pl.loop(0, per_w // SC_CHUNK)
        def _(j):
            off = pl.multiple_of(j * SC_CHUNK, SC_CHUNK)
            pltpu.sync_copy(xn_hbm.at[pl.ds(base + off, SC_CHUNK)], rows_v)
            pltpu.sync_copy(rows_v, out_hbm.at[i0_v.at[pl.ds(off, SC_CHUNK)]])
            pltpu.sync_copy(rows_v, out_hbm.at[i1_v.at[pl.ds(off, SC_CHUNK)]])

        @pl.loop(0, per_p // SC_CHUNK)
        def _(j):
            off = pl.multiple_of(j * SC_CHUNK, SC_CHUNK)
            pltpu.sync_copy(xn_hbm.at[pl.ds(base + off, SC_CHUNK)], rows_v)
            pltpu.sync_copy(rows_v, out_hbm.at[ip_v.at[pl.ds(off, SC_CHUNK)]])

    return scatter_kernel(xn, dest_k[0], dest_k[1], pad_rows)


def _experts_kernel(be_ref, x_ref, wg_ref, wu_ref, wd_ref, y_ref):
    del be_ref
    xb = _unpack_bf16_pairs(x_ref[...]).astype(BF16)
    g = jnp.dot(xb, wg_ref[...].astype(BF16), preferred_element_type=F32)
    u = jnp.dot(xb, wu_ref[...].astype(BF16), preferred_element_type=F32)
    hdn = (g * _sigmoid(g) * u).astype(BF16)
    y_ref[...] = _pack_bf16_pairs(jnp.dot(hdn, wd_ref[...].astype(BF16), preferred_element_type=F32))


def _experts(block_e, xbuf, w_gate, w_up, w_down, layer, te):
    n_slots = xbuf.shape[0]
    grid_spec = pltpu.PrefetchScalarGridSpec(
        num_scalar_prefetch=1,
        grid=(n_slots // te,),
        in_specs=[pl.BlockSpec((te, PACKED), lambda i, be: (i, 0)),
                  pl.BlockSpec((None, None, D_MODEL, D_EXPERT), lambda i, be: (layer, be[i], 0, 0)),
                  pl.BlockSpec((None, None, D_MODEL, D_EXPERT), lambda i, be: (layer, be[i], 0, 0)),
                  pl.BlockSpec((None, None, D_EXPERT, D_MODEL), lambda i, be: (layer, be[i], 0, 0))],
        out_specs=pl.BlockSpec((te, PACKED), lambda i, be: (i, 0)),
    )
    return pl.pallas_call(
        _experts_kernel,
        out_shape=jax.ShapeDtypeStruct((n_slots, PACKED), U32),
        grid_spec=grid_spec,
        compiler_params=_cparams(("arbitrary",)),
        name="experts",
    )(block_e, xbuf, w_gate, w_up, w_down)


def _sc_gather_rows(table, idx):
    n_rows = idx.shape[0]
    width = table.shape[1]
    per_w = n_rows // SC_WORKERS

    @functools.partial(
        pl.kernel, mesh=_sc_mesh(),
        out_type=jax.ShapeDtypeStruct((n_rows, width), table.dtype),
        scratch_types=[pltpu.VMEM((per_w,), I32), pltpu.VMEM((SC_CHUNK, width), table.dtype)],
        name="sc_gather_rows",
    )
    def gather_kernel(table_hbm, idx_hbm, out_hbm, idx_v, rows_v):
        base = _sc_worker_id() * per_w
        pltpu.sync_copy(idx_hbm.at[pl.ds(base, per_w)], idx_v)

        @pl.loop(0, per_w // SC_CHUNK)
        def _(j):
            off = pl.multiple_of(j * SC_CHUNK, SC_CHUNK)
            pltpu.sync_copy(table_hbm.at[idx_v.at[pl.ds(off, SC_CHUNK)]], rows_v)
            pltpu.sync_copy(rows_v, out_hbm.at[pl.ds(base + off, SC_CHUNK)])

    return gather_kernel(table, idx)


def _combine_kernel(y0_ref, y1_ref, h1_ref, rw_ref, p_ref, gple_ref, wgate_ref, wproj_ref,
                    gfin_ref, o_ref, *, final):
    rw = rw_ref[...]
    h2 = (h1_ref[...] + rw[:, 0:1] * _unpack_bf16_pairs(y0_ref[...])
          + rw[:, 1:2] * _unpack_bf16_pairs(y1_ref[...]))
    hn = _rms(h2, gple_ref[...]).astype(BF16)
    gate = _sigmoid(jnp.dot(hn, wgate_ref[...], preferred_element_type=F32))
    ple = jnp.dot(p_ref[...].astype(BF16), wproj_ref[...], preferred_element_type=F32)
    h3 = h2 + ple * gate
    if final:
        h3 = _rms(h3, gfin_ref[...])
    o_ref[...] = h3


def _combine(dest_k, ybuf, h1, rw, p_all, layer, wts, g_final, tf, final):
    n_tok = h1.shape[0]
    n_steps = n_tok // tf
    row = lambda i: (i, 0)
    yg = _sc_gather_rows(ybuf, dest_k.reshape(-1))
    return pl.pallas_call(
        functools.partial(_combine_kernel, final=final),
        out_shape=jax.ShapeDtypeStruct((n_tok, D_MODEL), F32),
        grid=(n_steps,),
        in_specs=[pl.BlockSpec((tf, PACKED), row),
                  pl.BlockSpec((tf, PACKED), lambda i: (n_steps + i, 0)),
                  pl.BlockSpec((tf, D_MODEL), row), pl.BlockSpec((tf, LANES), row),
                  pl.BlockSpec((None, tf, PLE_DIM), lambda i: (layer, i, 0)),
                  _full((1, D_MODEL)), _full((D_MODEL, D_MODEL)), _full((PLE_DIM, D_MODEL)),
                  _full((1, D_MODEL))],
        out_specs=pl.BlockSpec((tf, D_MODEL), row),
        compiler_params=_cparams(("parallel",)),
        name="combine",
    )(yg, yg, h1, rw, p_all, wts["g_ple"], wts["w_ple_gate"], wts["w_ple_proj"], g_final)


def _layer_weights(i, g_mix_norm, w_in, g_q_lat, w_q_up, g_kv_lat, w_kv_up,
                   ssm_a_re, ssm_a_im, ssm_b_re, ssm_b_im, ssm_c_re, ssm_c_im, ssm_d, ssm_log_step,
                   w_glu, b_glu, g_attn_out, g_ssm_out, w_out, g_moe_norm,
                   w_group_router, b_group_router, w_expert_router, b_expert_router,
                   g_ple_norm, w_ple_gate, w_ple_proj):
    off_kv, off_kr, off_ssm = Q_LORA, Q_LORA + KV_LORA, Q_LORA + KV_LORA + QK_ROPE
    wi = w_in[i]
    kr_block = jnp.zeros((D_MODEL, HEAD_PAD), F32).at[:, ROPE_LANE0:ROPE_LANE0 + QK_ROPE].set(wi[:, off_kr:off_ssm])
    w_in_p = jnp.concatenate([wi[:, :off_kr], kr_block, wi[:, off_ssm:]], axis=1)
    wq = w_q_up[i].reshape(Q_LORA, N_HEADS, QK_NOPE + QK_ROPE) * (ATTN_SCALE * math.log2(math.e))
    wq = jnp.pad(wq, ((0, 0), (0, 0), (0, HEAD_PAD - QK_NOPE - QK_ROPE))).reshape(Q_LORA, N_HEADS * HEAD_PAD)
    wkv = w_kv_up[i].reshape(KV_LORA, N_HEADS, QK_NOPE + V_HEAD)
    wk = jnp.pad(wkv[..., :QK_NOPE], ((0, 0), (0, 0), (0, HEAD_PAD - QK_NOPE))).reshape(KV_LORA, N_HEADS * HEAD_PAD)
    wv = jnp.pad(wkv[..., QK_NOPE:], ((0, 0), (0, 0), (0, HEAD_PAD - V_HEAD))).reshape(KV_LORA, N_HEADS * HEAD_PAD)
    lam = lax.complex(ssm_a_re[i], ssm_a_im[i])
    lam_bar = jnp.exp(lam * jnp.exp(ssm_log_step[i])[:, None])
    b_fac = (lam_bar - 1.0) / lam
    bc = lax.complex(ssm_b_re[i], ssm_b_im[i]) * b_fac[:, :, None]
    n_per = SSM_GROUPS // S5_SLABS

    def block_diag(blocks):
        s, n, a, b = blocks.shape
        eye = jnp.eye(n, dtype=blocks.dtype)
        return jnp.einsum("snab,nm->snamb", blocks, eye).reshape(s, n * a, n * b)

    b_hp_re = jnp.real(bc).transpose(0, 2, 1).reshape(S5_SLABS, n_per, SSM_CH, SSM_STATE)
    b_hp_im = jnp.imag(bc).transpose(0, 2, 1).reshape(S5_SLABS, n_per, SSM_CH, SSM_STATE)
    s5_wb = jnp.concatenate([block_diag(b_hp_re), block_diag(b_hp_im)], axis=-1)
    c_re = ssm_c_re[i].transpose(0, 2, 1).reshape(S5_SLABS, n_per, SSM_STATE, SSM_CH)
    c_im = ssm_c_im[i].transpose(0, 2, 1).reshape(S5_SLABS, n_per, SSM_STATE, SSM_CH)
    s5_wc = jnp.stack([block_diag(c_re), -block_diag(c_im)])
    w_r = jnp.zeros((D_MODEL, LANES), F32)
    w_r = w_r.at[:, :N_EXPERT_GROUPS].set(w_group_router[i])
    w_r = w_r.at[:, ROUTE_E0:ROUTE_E0 + N_EXPERTS].set(w_expert_router[i])
    w_r_hi = w_r.astype(BF16)
    w_r_lo = (w_r - w_r_hi.astype(F32)).astype(BF16)
    b_r = jnp.zeros((1, LANES), F32)
    b_r = b_r.at[0, :N_EXPERT_GROUPS].set(b_group_router[i])
    b_r = b_r.at[0, ROUTE_E0:ROUTE_E0 + N_EXPERTS].set(b_expert_router[i])
    return dict(
        g_mix=g_mix_norm[i][None], w_in=w_in_p.astype(BF16),
        g_q=g_q_lat[i][None], w_q=wq.astype(BF16),
        g_kv=g_kv_lat[i][None], w_kv=jnp.concatenate([wk, wv], axis=1).astype(BF16),
        s5_wb=s5_wb.astype(BF16), s5_wc=s5_wc.astype(BF16),
        s5_are=jnp.real(lam_bar).reshape(1, N_STATE), s5_aim=jnp.imag(lam_bar).reshape(1, N_STATE),
        s5_d=ssm_d[i].reshape(1, SSM_WIDTH), w_glu=w_glu[i].astype(BF16), b_glu=b_glu[i][None],
        g_attn=g_attn_out[i][None], g_ssm=g_ssm_out[i][None],
        w_out=w_out[i].astype(BF16), g_moe=g_moe_norm[i][None],
        w_r_hi=w_r_hi, w_r_lo=w_r_lo, b_r=b_r,
        g_ple=g_ple_norm[i][None], w_ple_gate=w_ple_gate[i].astype(BF16),
        w_ple_proj=w_ple_proj[i].astype(BF16),
    )


def _slot_layout(route, cnt, n_blocks, te):
    counts = cnt[0, :N_EXPERTS].astype(I32)
    pcounts = (counts + te - 1) // te * te
    pends = jnp.cumsum(pcounts)
    pstarts = pends - pcounts
    eid = route[:, :TOP_K]
    rank = route[:, TOP_K:2 * TOP_K]
    start = jnp.sum(jnp.where(eid[:, :, None] == jnp.arange(N_EXPERTS)[None, None, :],
                              pstarts[None, None, :], 0), axis=-1)
    dest_k = (start + rank).astype(I32).T
    blk0 = jnp.arange(n_blocks, dtype=I32) * te
    block_e = jnp.minimum(jnp.sum((pends[None, :] <= blk0[:, None]).astype(I32), axis=1), N_EXPERTS - 1)
    n_slots = n_blocks * te
    gap_start = jnp.concatenate([pstarts + counts, pends[-1:]])
    gap_len = jnp.concatenate([pcounts - counts, n_slots - pends[-1:]])
    gap_end = jnp.cumsum(gap_len)
    j = jnp.arange(n_slots - eid.size, dtype=I32)
    gap = jnp.sum((gap_end[None, :] <= j[:, None]).astype(I32), axis=1)
    onehot = gap[:, None] == jnp.arange(N_EXPERTS + 1)[None, :]
    pad_rows = j + jnp.sum(jnp.where(onehot, (gap_start - (gap_end - gap_len))[None, :], 0), axis=1)
    return dest_k, block_e, pad_rows.astype(I32)


def kernel(x, p, positions, g_mix_norm, w_in, g_q_lat, w_q_up, g_kv_lat, w_kv_up, ssm_a_re, ssm_a_im, ssm_b_re, ssm_b_im, ssm_c_re, ssm_c_im, ssm_d, ssm_log_step, w_glu, b_glu, g_attn_out, g_ssm_out, w_out, g_moe_norm, w_group_router, b_group_router, w_expert_router, b_expert_router, w_exp_gate, w_exp_up, w_exp_down, g_ple_norm, w_ple_gate, w_ple_proj, g_final):
    bsz, seq, _ = x.shape
    depth = p.shape[0]
    n_tok = bsz * seq
    t = _tiles(n_tok, seq)
    n_slots = TOP_K * n_tok + N_EXPERTS * t["te"]
    n_blocks = n_slots // t["te"]

    cosf, sinf = _rope_tables(positions.reshape(n_tok, 1), t["tm"])
    sub = min(SUB_ROWS, t["tm"])
    tril = jnp.tril(jnp.ones((sub, sub), F32), k=-1).astype(BF16)
    g_fin = g_final[None]

    h = x.reshape(n_tok, D_MODEL)
    for i in range(depth):
        wts = _layer_weights(i, g_mix_norm, w_in, g_q_lat, w_q_up, g_kv_lat, w_kv_up,
                             ssm_a_re, ssm_a_im, ssm_b_re, ssm_b_im, ssm_c_re, ssm_c_im, ssm_d,
                             ssm_log_step, w_glu, b_glu, g_attn_out, g_ssm_out, w_out, g_moe_norm,
                             w_group_router, b_group_router, w_expert_router, b_expert_router,
                             g_ple_norm, w_ple_gate, w_ple_proj)
        q, k, v, u_tm = _inproj(h, cosf, sinf, wts, bsz, seq, t["tm"])
        attn_n = _attention(q, k, v, wts["g_attn"], bsz, seq, t["tq"])
        ssm_n = _s5(u_tm.reshape(seq, bsz, SSM_WIDTH), wts, bsz, seq, t["lc"])
        h1, xn, route, rw, cnt = _outproj(h, attn_n, ssm_n.reshape(seq, bsz * SSM_WIDTH), wts, tril,
                                          bsz, seq, t["tm"])
        dest_k, block_e, pad_rows = _slot_layout(route, cnt, n_blocks, t["te"])
        xbuf = _dispatch(dest_k, pad_rows, xn, n_slots)
        ybuf = _experts(block_e, xbuf, w_exp_gate, w_exp_up, w_exp_down, i, t["te"])
        h = _combine(dest_k, ybuf, h1, rw, p.reshape(depth, n_tok, PLE_DIM), i, wts, g_fin, t["tf"],
                     final=(i == depth - 1))
    return h.reshape(bsz, seq, D_MODEL)
```

```python
import functools
import math

import jax
import jax.numpy as jnp
from jax import lax
from jax.experimental import pallas as pl
from jax.experimental.pallas import tpu as pltpu
from jax.experimental.pallas import tpu_sc as plsc

F32 = jnp.float32
BF16 = jnp.bfloat16
I32 = jnp.int32
U32 = jnp.uint32

D_MODEL = 1024
MLA_WIDTH = 512
SSM_WIDTH = 512
V_HEAD = 64
N_HEADS = 8
QK_NOPE = 64
QK_ROPE = 32
HALF_ROPE = QK_ROPE // 2
Q_LORA = 256
KV_LORA = 128
ROPE_BASE = 10000.0
ATTN_SCALE = 1.0 / math.sqrt(QK_NOPE + QK_ROPE)
SSM_CH = 16
SSM_GROUPS = 32
SSM_STATE = 64
N_EXPERT_GROUPS = 4
EXPERTS_PER_GROUP = 8
N_EXPERTS = 32
TOP_K = 2
D_EXPERT = 256
PLE_DIM = 256
EPS = 1e-6

LANES = 128
SUBLANES = 8
HEAD_PAD = 128
ROPE_LANE0 = QK_NOPE
N_STATE = SSM_GROUPS * SSM_STATE
PACKED = D_MODEL // 2
NEG = -1e30
VMEM_LIMIT = 56 * 1024 * 1024
SUB_ROWS = 512


def _tiles(n_tok, seq):
    return dict(
        tm=min(1024, seq),
        tq=min(512, seq),
        lc=min(32, seq),
        te=512,
        tf=min(512, n_tok),
    )


def _rms(x, g):
    ms = jnp.mean(x * x, axis=-1, keepdims=True)
    return x * lax.rsqrt(ms + EPS) * g


def _sigmoid(x):
    return 1.0 / (1.0 + jnp.exp(-x))


def _pack_bf16_pairs(x):
    lo = lax.bitcast_convert_type(x[:, :PACKED].astype(BF16).astype(F32), U32)
    hi = lax.bitcast_convert_type(x[:, PACKED:].astype(BF16).astype(F32), U32)
    return (lo >> 16) | hi


def _unpack_bf16_pairs(w):
    lo = lax.bitcast_convert_type(w << 16, F32)
    hi = lax.bitcast_convert_type(w & jnp.uint32(0xFFFF0000), F32)
    return jnp.concatenate([lo, hi], axis=1)


def _cparams(sem):
    return pltpu.CompilerParams(dimension_semantics=sem, vmem_limit_bytes=VMEM_LIMIT)


def _full(shape):
    nd = len(shape)
    return pl.BlockSpec(shape, lambda *_: (0,) * nd)


def _rope_table_kernel(pos_ref, invf_ref, cos_ref, sin_ref):
    ang = pos_ref[...].astype(F32) * invf_ref[...]
    cos_ref[...] = jnp.cos(ang)
    sin_ref[...] = jnp.sin(ang)


def _rope_tables(pos_col, tm):
    n_tok = pos_col.shape[0]
    lane = jnp.arange(LANES)
    in_rope = (lane >= ROPE_LANE0) & (lane < ROPE_LANE0 + QK_ROPE)
    freq = ROPE_BASE ** (-((lane - ROPE_LANE0) % HALF_ROPE).astype(F32) / HALF_ROPE)
    invf = jnp.where(in_rope, freq, 0.0).astype(F32)[None, :]
    return pl.pallas_call(
        _rope_table_kernel,
        out_shape=(jax.ShapeDtypeStruct((n_tok, LANES), F32),) * 2,
        grid=(n_tok // tm,),
        in_specs=[pl.BlockSpec((tm, 1), lambda i: (i, 0)), _full((1, LANES))],
        out_specs=(pl.BlockSpec((tm, LANES), lambda i: (i, 0)),) * 2,
        compiler_params=_cparams(("parallel",)),
        name="rope_tables",
    )(pos_col, invf)


def _rope(x, cosf, s_lo, s_hi):
    n = x.shape[-1]
    return (x * cosf + pltpu.roll(x, n - HALF_ROPE, 1) * s_lo + pltpu.roll(x, HALF_ROPE, 1) * s_hi)


def _row_chunks(n_rows):
    sub = min(SUB_ROWS, n_rows)
    return [slice(c * sub, (c + 1) * sub) for c in range(n_rows // sub)]


def _inproj_kernel(h_ref, cos_ref, sin_ref, gmix_ref, win_ref, gq_ref, wq_ref, gkv_ref, wkv_ref,
                   q_ref, k_ref, v_ref, u_ref):
    lane = lax.broadcasted_iota(I32, (1, LANES), 1)
    ones_lane = jnp.where(lane == V_HEAD, 1.0, 0.0)
    for r in _row_chunks(h_ref.shape[0]):
        xn = _rms(h_ref[r, :], gmix_ref[...]).astype(BF16)
        z = jnp.dot(xn, win_ref[...], preferred_element_type=F32)
        cosf = cos_ref[r, :]
        sinf = sin_ref[r, :]
        s_lo = jnp.where((lane >= ROPE_LANE0) & (lane < ROPE_LANE0 + HALF_ROPE), -sinf, 0.0)
        s_hi = jnp.where((lane >= ROPE_LANE0 + HALF_ROPE) & (lane < ROPE_LANE0 + QK_ROPE), sinf, 0.0)
        qn = _rms(z[:, :Q_LORA], gq_ref[...]).astype(BF16)
        q = jnp.dot(qn, wq_ref[...], preferred_element_type=F32)
        kvn = _rms(z[:, Q_LORA:Q_LORA + KV_LORA], gkv_ref[...]).astype(BF16)
        kv = jnp.dot(kvn, wkv_ref[...], preferred_element_type=F32)
        kpe = _rope(z[:, Q_LORA + KV_LORA:Q_LORA + KV_LORA + HEAD_PAD], cosf, s_lo, s_hi)
        for hd in range(N_HEADS):
            sl = slice(hd * HEAD_PAD, (hd + 1) * HEAD_PAD)
            vsl = slice(N_HEADS * HEAD_PAD + hd * HEAD_PAD, N_HEADS * HEAD_PAD + (hd + 1) * HEAD_PAD)
            q_ref[r, sl] = _rope(q[:, sl], cosf, s_lo, s_hi).astype(BF16)
            k_ref[r, sl] = (kv[:, sl] + kpe).astype(BF16)
            v_ref[r, sl] = (kv[:, vsl] + ones_lane).astype(BF16)
        u_ref[r, :] = z[:, Q_LORA + KV_LORA + HEAD_PAD:]


def _inproj(h, cosf, sinf, wts, bsz, seq, tm):
    n_tok = bsz * seq
    nl = seq // tm
    row = lambda b, l: (b * nl + l, 0)
    qk_w = N_HEADS * HEAD_PAD
    return pl.pallas_call(
        _inproj_kernel,
        out_shape=(jax.ShapeDtypeStruct((n_tok, qk_w), BF16),
                   jax.ShapeDtypeStruct((n_tok, qk_w), BF16),
                   jax.ShapeDtypeStruct((n_tok, qk_w), BF16),
                   jax.ShapeDtypeStruct((seq, bsz * SSM_WIDTH), F32)),
        grid=(bsz, nl),
        in_specs=[pl.BlockSpec((tm, D_MODEL), row),
                  pl.BlockSpec((tm, LANES), row), pl.BlockSpec((tm, LANES), row),
                  _full((1, D_MODEL)), _full((D_MODEL, D_MODEL)),
                  _full((1, Q_LORA)), _full((Q_LORA, qk_w)),
                  _full((1, KV_LORA)), _full((KV_LORA, 2 * qk_w))],
        out_specs=(pl.BlockSpec((tm, qk_w), row), pl.BlockSpec((tm, qk_w), row),
                   pl.BlockSpec((tm, qk_w), row),
                   pl.BlockSpec((tm, SSM_WIDTH), lambda b, l: (l, b))),
        compiler_params=_cparams(("parallel", "parallel")),
        name="inproj",
    )(h, cosf, sinf, wts["g_mix"], wts["w_in"], wts["g_q"], wts["w_q"], wts["g_kv"], wts["w_kv"])


ATTN_RB = 32
ATTN_GROUP = 4


def _attn_kernel(q_ref, k_ref, v_ref, g_ref, o_ref, s_ref, p_ref, m_ref, al_ref, acc_ref, out_ref, *, tq):
    qi = pl.program_id(1)
    col = lax.broadcasted_iota(I32, (ATTN_RB, tq), 1)
    row = lax.broadcasted_iota(I32, (ATTN_RB, tq), 0)
    for pr in range(N_HEADS // ATTN_GROUP):
        heads = [ATTN_GROUP * pr + a for a in range(ATTN_GROUP)]
        m_ref[...] = jnp.full(m_ref.shape, NEG, F32)
        acc_ref[...] = jnp.zeros(acc_ref.shape, F32)

        def scores(j, slot, heads=heads):
            rows = pl.ds(pl.multiple_of(j * tq, tq), tq)
            for a, hd in enumerate(heads):
                hs = slice(hd * HEAD_PAD, (hd + 1) * HEAD_PAD)
                s_ref[slot, a] = lax.dot_general(q_ref[:, hs], k_ref[rows, hs], (((1,), (1,)), ((), ())),
                                                 preferred_element_type=F32)

        def softmax_pv(j, slot, masked, heads=heads):
            rows = pl.ds(pl.multiple_of(j * tq, tq), tq)
            for a in range(ATTN_GROUP):
                for c in range(tq // ATTN_RB):
                    r = slice(c * ATTN_RB, (c + 1) * ATTN_RB)
                    s = s_ref[slot, a, r, :]
                    if masked:
                        s = jnp.where(col <= row + c * ATTN_RB, s, NEG)
                    m_old = m_ref[a, r, :]
                    m_new = jnp.maximum(m_old, jnp.max(s, axis=-1, keepdims=True))
                    p_ref[a, r, :] = jnp.exp2(s - m_new).astype(BF16)
                    al_ref[a, r, :] = jnp.exp2(m_old - m_new)
                    m_ref[a, r, :] = m_new
            for a, hd in enumerate(heads):
                pv = jnp.dot(p_ref[a], v_ref[rows, hd * HEAD_PAD:(hd + 1) * HEAD_PAD],
                             preferred_element_type=F32)
                acc_ref[a] = al_ref[a] * acc_ref[a] + pv

        def two_tiles(jj, carry):
            t = 2 * jj
            scores(t + 1, 1)
            softmax_pv(t, 0, masked=False)
            scores(t + 2, 0)
            softmax_pv(t + 1, 1, masked=False)
            return carry

        scores(0, 0)
        lax.fori_loop(0, qi // 2, two_tiles, 0)

        @pl.when(qi % 2 == 0)
        def _():
            softmax_pv(qi, 0, masked=True)

        @pl.when(qi % 2 == 1)
        def _():
            scores(qi, 1)
            softmax_pv(qi - 1, 0, masked=False)
            softmax_pv(qi, 1, masked=True)

        for a, hd in enumerate(heads):
            acc = acc_ref[a]
            out_ref[:, hd * V_HEAD:(hd + 1) * V_HEAD] = acc[:, :V_HEAD] / acc[:, V_HEAD:V_HEAD + 1]
    o_ref[...] = _rms(out_ref[...], g_ref[...]).astype(BF16)


def _attention(q, k, v, g_attn, bsz, seq, tq):
    n_tok = bsz * seq
    nq = seq // tq
    qk_w = N_HEADS * HEAD_PAD
    return pl.pallas_call(
        functools.partial(_attn_kernel, tq=tq),
        out_shape=jax.ShapeDtypeStruct((n_tok, MLA_WIDTH), BF16),
        grid=(bsz, nq),
        in_specs=[pl.BlockSpec((tq, qk_w), lambda b, i: (b * nq + i, 0)),
                  pl.BlockSpec((seq, qk_w), lambda b, i: (b, 0)),
                  pl.BlockSpec((seq, qk_w), lambda b, i: (b, 0)),
                  _full((1, MLA_WIDTH))],
        out_specs=pl.BlockSpec((tq, MLA_WIDTH), lambda b, i: (b * nq + i, 0)),
        scratch_shapes=[pltpu.VMEM((2, ATTN_GROUP, tq, tq), F32), pltpu.VMEM((ATTN_GROUP, tq, tq), BF16),
                        pltpu.VMEM((ATTN_GROUP, tq, 1), F32), pltpu.VMEM((ATTN_GROUP, tq, 1), F32),
                        pltpu.VMEM((ATTN_GROUP, tq, HEAD_PAD), F32), pltpu.VMEM((tq, MLA_WIDTH), F32)],
        compiler_params=_cparams(("parallel", "parallel")),
        name="attn",
    )(q, k, v, g_attn)


S5_SLABS = 4
S5_SLAB_U = SSM_WIDTH // S5_SLABS
S5_SLAB_X = N_STATE // S5_SLABS


def _s5_kernel(u_ref, wb_ref, are_ref, aim_ref, wc_ref, d_ref, wglu_ref, bglu_ref, g_ref, after_ref,
               o_ref, xr_ref, xi_ref, sr_ref, si_ref, *, lc, nb):
    del after_ref

    @pl.when(pl.program_id(0) == 0)
    def _():
        xr_ref[...] = jnp.zeros_like(xr_ref)
        xi_ref[...] = jnp.zeros_like(xi_ref)

    rows = lc * nb
    u = u_ref[...].reshape(rows, SSM_WIDTH)
    ub = u.astype(BF16)
    def drive(j):
        bu = jnp.dot(ub[:, j * S5_SLAB_U:(j + 1) * S5_SLAB_U], wb_ref[j], preferred_element_type=F32)
        sr_ref[:, j * S5_SLAB_X:(j + 1) * S5_SLAB_X] = bu[:, :S5_SLAB_X]
        si_ref[:, j * S5_SLAB_X:(j + 1) * S5_SLAB_X] = bu[:, S5_SLAB_X:]

    ys = []
    drive(0)
    for c in range(S5_SLABS):
        if c + 1 < S5_SLABS:
            drive(c + 1)
        sl = slice(c * S5_SLAB_X, (c + 1) * S5_SLAB_X)
        ar = jnp.broadcast_to(are_ref[:, sl], (nb, S5_SLAB_X))
        ai = jnp.broadcast_to(aim_ref[:, sl], (nb, S5_SLAB_X))
        xr = xr_ref[:, sl]
        xi = xi_ref[:, sl]
        for t in range(lc):
            r = slice(t * nb, (t + 1) * nb)
            xr, xi = (ar * xr - ai * xi + sr_ref[r, sl], ar * xi + ai * xr + si_ref[r, sl])
            sr_ref[r, sl] = xr
            si_ref[r, sl] = xi
        xr_ref[:, sl] = xr
        xi_ref[:, sl] = xi
        ys.append(jnp.dot(sr_ref[:, sl].astype(BF16), wc_ref[0, c], preferred_element_type=F32)
                  + jnp.dot(si_ref[:, sl].astype(BF16), wc_ref[1, c], preferred_element_type=F32))
    y = jnp.concatenate(ys, axis=-1) + d_ref[...] * u
    y = 0.5 * y * (1.0 + jnp.tanh(math.sqrt(2.0 / math.pi) * (y + 0.044715 * (y * y * y))))
    gate = jnp.dot(y.astype(BF16), wglu_ref[...], preferred_element_type=F32) + bglu_ref[...]
    y = y * _sigmoid(gate)
    o_ref[...] = _rms(y, g_ref[...]).astype(BF16).reshape(lc, nb, SSM_WIDTH)


def _s5(u_tm, wts, after, bsz, seq, lc):
    rows = lc * bsz
    return pl.pallas_call(
        functools.partial(_s5_kernel, lc=lc, nb=bsz),
        out_shape=jax.ShapeDtypeStruct((seq, bsz, SSM_WIDTH), BF16),
        grid=(seq // lc,),
        in_specs=[pl.BlockSpec((lc, bsz, SSM_WIDTH), lambda i: (i, 0, 0)),
                  _full((S5_SLABS, S5_SLAB_U, 2 * S5_SLAB_X)),
                  _full((1, N_STATE)), _full((1, N_STATE)),
                  _full((2, S5_SLABS, S5_SLAB_X, S5_SLAB_U)),
                  _full((1, SSM_WIDTH)), _full((SSM_WIDTH, SSM_WIDTH)), _full((1, SSM_WIDTH)),
                  _full((1, SSM_WIDTH)), pl.BlockSpec(memory_space=pl.ANY)],
        out_specs=pl.BlockSpec((lc, bsz, SSM_WIDTH), lambda i: (i, 0, 0)),
        scratch_shapes=[pltpu.VMEM((bsz, N_STATE), F32), pltpu.VMEM((bsz, N_STATE), F32),
                        pltpu.VMEM((rows, N_STATE), F32), pltpu.VMEM((rows, N_STATE), F32)],
        compiler_params=_cparams(("arbitrary",)),
        name="s5",
    )(u_tm, wts["s5_wb"], wts["s5_are"], wts["s5_aim"], wts["s5_wc"], wts["s5_d"],
      wts["w_glu"], wts["b_glu"], wts["g_ssm"], after)


ROUTE_E0 = N_EXPERT_GROUPS


def _outproj_kernel(h_ref, a_ref, s_ref, wout_ref, gmoe_ref, wrh_ref, wrl_ref, br_ref, tril_ref,
                    h1_ref, xn_ref, route_ref, rw_ref, cnt_ref):
    @pl.when((pl.program_id(0) == 0) & (pl.program_id(1) == 0))
    def _():
        cnt_ref[...] = jnp.zeros_like(cnt_ref)

    sub = tril_ref.shape[0]
    lane = lax.broadcasted_iota(I32, (sub, LANES), 1).astype(F32)
    big = float(LANES)
    for r in _row_chunks(h_ref.shape[0]):
        h1 = (h_ref[r, :]
              + jnp.dot(a_ref[r, :], wout_ref[:MLA_WIDTH, :], preferred_element_type=F32)
              + jnp.dot(s_ref[r, :], wout_ref[MLA_WIDTH:, :], preferred_element_type=F32))
        h1_ref[r, :] = h1
        xn = _rms(h1, gmoe_ref[...])
        xn_ref[r, :] = _pack_bf16_pairs(xn)
        x_hi = xn.astype(BF16)
        x_lo = (xn - x_hi.astype(F32)).astype(BF16)
        lg = (jnp.dot(x_hi, wrh_ref[...], preferred_element_type=F32)
              + (jnp.dot(x_lo, wrh_ref[...], preferred_element_type=F32)
                 + jnp.dot(x_hi, wrl_ref[...], preferred_element_type=F32))) + br_ref[...]
        gl = jnp.where(lane < N_EXPERT_GROUPS, lg, NEG)
        gmax = jnp.max(gl, axis=-1, keepdims=True)
        gsum = jnp.sum(jnp.exp(gl - gmax), axis=-1, keepdims=True)
        g_top_p = 1.0 / gsum
        gidx = jnp.min(jnp.where(gl == gmax, lane, big), axis=-1, keepdims=True)
        lo = ROUTE_E0 + EXPERTS_PER_GROUP * gidx
        sel = jnp.where((lane >= lo) & (lane < lo + EXPERTS_PER_GROUP), lg, NEG)
        m1 = jnp.max(sel, axis=-1, keepdims=True)
        i1 = jnp.min(jnp.where(sel == m1, lane, big), axis=-1, keepdims=True)
        sel2 = jnp.where(lane == i1, NEG, sel)
        m2 = jnp.max(sel2, axis=-1, keepdims=True)
        i2 = jnp.min(jnp.where(sel2 == m2, lane, big), axis=-1, keepdims=True)
        e21 = jnp.exp(m2 - m1)
        w1 = g_top_p / (1.0 + e21)
        w2 = g_top_p * e21 / (1.0 + e21)
        e1 = i1 - ROUTE_E0
        e2 = i2 - ROUTE_E0
        oh = jnp.where((lane == e1) | (lane == e2), 1.0, 0.0)
        before = cnt_ref[...] + jnp.dot(tril_ref[...], oh.astype(BF16), preferred_element_type=F32)
        r1 = jnp.sum(jnp.where(lane == e1, before, 0.0), axis=-1, keepdims=True)
        r2 = jnp.sum(jnp.where(lane == e2, before, 0.0), axis=-1, keepdims=True)
        cnt_ref[...] = cnt_ref[...] + jnp.sum(oh, axis=0, keepdims=True)
        route = jnp.where(lane == 0, e1, jnp.where(lane == 1, e2, jnp.where(lane == 2, r1, r2)))
        route_ref[:, r] = jnp.transpose(route)[:SUBLANES, :].astype(I32)
        rw_ref[r, :] = jnp.where(lane == 0, w1, w2)


def _outproj(h, attn_n, ssm_tm, wts, tril, bsz, seq, tm):
    n_tok = bsz * seq
    nl = seq // tm
    row = lambda b, l: (b * nl + l, 0)
    return pl.pallas_call(
        _outproj_kernel,
        out_shape=(jax.ShapeDtypeStruct((n_tok, D_MODEL), F32),
                   jax.ShapeDtypeStruct((n_tok, PACKED), U32),
                   jax.ShapeDtypeStruct((SUBLANES, n_tok), I32),
                   jax.ShapeDtypeStruct((n_tok, LANES), F32),
                   jax.ShapeDtypeStruct((1, LANES), F32)),
        grid=(bsz, nl),
        in_specs=[pl.BlockSpec((tm, D_MODEL), row),
                  pl.BlockSpec((tm, MLA_WIDTH), row),
                  pl.BlockSpec((tm, SSM_WIDTH), lambda b, l: (l, b)),
                  _full((D_MODEL, D_MODEL)), _full((1, D_MODEL)),
                  _full((D_MODEL, LANES)), _full((D_MODEL, LANES)), _full((1, LANES)),
                  _full(tril.shape)],
        out_specs=(pl.BlockSpec((tm, D_MODEL), row), pl.BlockSpec((tm, PACKED), row),
                   pl.BlockSpec((SUBLANES, tm), lambda b, l: (0, b * nl + l)),
                   pl.BlockSpec((tm, LANES), row),
                   _full((1, LANES))),
        compiler_params=_cparams(("arbitrary", "arbitrary")),
        name="outproj",
    )(h, attn_n, ssm_tm, wts["w_out"], wts["g_moe"], wts["w_r_hi"], wts["w_r_lo"], wts["b_r"], tril)


SC_CORES = 2
SC_SUBCORES = 16
SC_WORKERS = SC_CORES * SC_SUBCORES
SC_CHUNK = 64


def _sc_mesh():
    return plsc.VectorSubcoreMesh(core_axis_name="c", subcore_axis_name="s",
                                  num_cores=SC_CORES, num_subcores=SC_SUBCORES)


def _sc_worker_id():
    return lax.axis_index("s") * SC_CORES + lax.axis_index("c")


def _sc_two_slot_pipeline(n_chunks, load, stores):
    assert n_chunks % 2 == 0
    load(0, 0).start()

    @pl.loop(0, n_chunks // 2)
    def _(jj):
        for slot in (0, 1):
            j = 2 * jj + slot

            @pl.when(j >= 1)
            def _():
                for cp in stores(j - 1, 1 - slot):
                    cp.wait()

            @pl.when(j + 1 < n_chunks)
            def _():
                load(j + 1, 1 - slot).start()

            load(j, slot).wait()
            for cp in stores(j, slot):
                cp.start()

    for cp in stores(n_chunks - 1, 1):
        cp.wait()


def _sc_rows(j):
    return pl.ds(pl.multiple_of(j * SC_CHUNK, SC_CHUNK), SC_CHUNK)


def _dispatch(dest_k, pad_rows, xn, n_slots):
    n_tok, width = xn.shape
    n_pad = pad_rows.shape[0]
    per_w = n_tok // SC_WORKERS
    per_p = n_pad // SC_WORKERS
    assert per_p <= per_w

    @functools.partial(
        pl.kernel, mesh=_sc_mesh(),
        out_type=jax.ShapeDtypeStruct((n_slots, width), xn.dtype),
        scratch_types=[pltpu.VMEM((per_w,), I32), pltpu.VMEM((per_w,), I32), pltpu.VMEM((per_p,), I32),
                       pltpu.VMEM((2, SC_CHUNK, width), xn.dtype),
                       pltpu.SemaphoreType.DMA((2,)), pltpu.SemaphoreType.DMA((2, TOP_K))],
        name="sc_dispatch_rows",
    )
    def scatter_kernel(xn_hbm, d0_hbm, d1_hbm, dp_hbm, out_hbm, i0_v, i1_v, ip_v, rows_v, lsem, ssem):
        wid = _sc_worker_id()
        base = wid * per_w
        pltpu.sync_copy(d0_hbm.at[pl.ds(base, per_w)], i0_v)
        pltpu.sync_copy(d1_hbm.at[pl.ds(base, per_w)], i1_v)
        pltpu.sync_copy(dp_hbm.at[pl.ds(wid * per_p, per_p)], ip_v)

        def load(j, slot):
            return pltpu.make_async_copy(xn_hbm.at[pl.ds(base + j * SC_CHUNK, SC_CHUNK)],
                                         rows_v.at[slot], lsem.at[slot])

        def scatter(idx_v, k):
            return lambda j, slot: pltpu.make_async_copy(
                rows_v.at[slot], out_hbm.at[idx_v.at[_sc_rows(j)]], ssem.at[slot, k])

        to_k0, to_k1, to_pad = scatter(i0_v, 0), scatter(i1_v, 1), scatter(ip_v, 0)
        _sc_two_slot_pipeline(per_w // SC_CHUNK, load,
                              lambda j, slot: [to_k0(j, slot), to_k1(j, slot)])
        _sc_two_slot_pipeline(per_p // SC_CHUNK, load, lambda j, slot: [to_pad(j, slot)])

    return scatter_kernel(xn, dest_k[0], dest_k[1], pad_rows)


def _experts_kernel(be_ref, x_ref, wg_ref, wu_ref, wd_ref, y_ref):
    del be_ref
    xb = _unpack_bf16_pairs(x_ref[...]).astype(BF16)
    g = jnp.dot(xb, wg_ref[...].astype(BF16), preferred_element_type=F32)
    u = jnp.dot(xb, wu_ref[...].astype(BF16), preferred_element_type=F32)
    hdn = (g * _sigmoid(g) * u).astype(BF16)
    y_ref[...] = _pack_bf16_pairs(jnp.dot(hdn, wd_ref[...].astype(BF16), preferred_element_type=F32))


def _experts(block_e, xbuf, w_gate, w_up, w_down, layer, te):
    n_slots = xbuf.shape[0]
    grid_spec = pltpu.PrefetchScalarGridSpec(
        num_scalar_prefetch=1,
        grid=(n_slots // te,),
        in_specs=[pl.BlockSpec((te, PACKED), lambda i, be: (i, 0)),
                  pl.BlockSpec((None, None, D_MODEL, D_EXPERT), lambda i, be: (layer, be[i], 0, 0)),
                  pl.BlockSpec((None, None, D_MODEL, D_EXPERT), lambda i, be: (layer, be[i], 0, 0)),
                  pl.BlockSpec((None, None, D_EXPERT, D_MODEL), lambda i, be: (layer, be[i], 0, 0))],
        out_specs=pl.BlockSpec((te, PACKED), lambda i, be: (i, 0)),
    )
    return pl.pallas_call(
        _experts_kernel,
        out_shape=jax.ShapeDtypeStruct((n_slots, PACKED), U32),
        grid_spec=grid_spec,
        compiler_params=_cparams(("arbitrary",)),
        name="experts",
    )(block_e, xbuf, w_gate, w_up, w_down)


def _sc_gather_rows(table, idx):
    n_rows = idx.shape[0]
    width = table.shape[1]
    per_w = n_rows // SC_WORKERS

    @functools.partial(
        pl.kernel, mesh=_sc_mesh(),
        out_type=jax.ShapeDtypeStruct((n_rows, width), table.dtype),
        scratch_types=[pltpu.VMEM((per_w,), I32), pltpu.VMEM((2, SC_CHUNK, width), table.dtype),
                       pltpu.SemaphoreType.DMA((2,)), pltpu.SemaphoreType.DMA((2,))],
        name="sc_gather_rows",
    )
    def gather_kernel(table_hbm, idx_hbm, out_hbm, idx_v, rows_v, lsem, ssem):
        base = _sc_worker_id() * per_w
        pltpu.sync_copy(idx_hbm.at[pl.ds(base, per_w)], idx_v)

        def load(j, slot):
            return pltpu.make_async_copy(table_hbm.at[idx_v.at[_sc_rows(j)]], rows_v.at[slot],
                                         lsem.at[slot])

        def store(j, slot):
            return [pltpu.make_async_copy(rows_v.at[slot],
                                          out_hbm.at[pl.ds(base + j * SC_CHUNK, SC_CHUNK)],
                                          ssem.at[slot])]

        _sc_two_slot_pipeline(per_w // SC_CHUNK, load, store)

    return gather_kernel(table, idx)


def _combine_kernel(y0_ref, y1_ref, h1_ref, rw_ref, p_ref, gple_ref, wgate_ref, wproj_ref,
                    gfin_ref, o_ref, *, final):
    rw = rw_ref[...]
    h2 = (h1_ref[...] + rw[:, 0:1] * _unpack_bf16_pairs(y0_ref[...])
          + rw[:, 1:2] * _unpack_bf16_pairs(y1_ref[...]))
    hn = _rms(h2, gple_ref[...]).astype(BF16)
    gate = _sigmoid(jnp.dot(hn, wgate_ref[...], preferred_element_type=F32))
    ple = jnp.dot(p_ref[...].astype(BF16), wproj_ref[...], preferred_element_type=F32)
    h3 = h2 + ple * gate
    if final:
        h3 = _rms(h3, gfin_ref[...])
    o_ref[...] = h3


def _combine(dest_k, ybuf, h1, rw, p_all, layer, wts, g_final, tf, final):
    n_tok = h1.shape[0]
    n_steps = n_tok // tf
    row = lambda i: (i, 0)
    yg = _sc_gather_rows(ybuf, dest_k.reshape(-1))
    return pl.pallas_call(
        functools.partial(_combine_kernel, final=final),
        out_shape=jax.ShapeDtypeStruct((n_tok, D_MODEL), F32),
        grid=(n_steps,),
        in_specs=[pl.BlockSpec((tf, PACKED), row),
                  pl.BlockSpec((tf, PACKED), lambda i: (n_steps + i, 0)),
                  pl.BlockSpec((tf, D_MODEL), row), pl.BlockSpec((tf, LANES), row),
                  pl.BlockSpec((None, tf, PLE_DIM), lambda i: (layer, i, 0)),
                  _full((1, D_MODEL)), _full((D_MODEL, D_MODEL)), _full((PLE_DIM, D_MODEL)),
                  _full((1, D_MODEL))],
        out_specs=pl.BlockSpec((tf, D_MODEL), row),
        compiler_params=_cparams(("parallel",)),
        name="combine",
    )(yg, yg, h1, rw, p_all, wts["g_ple"], wts["w_ple_gate"], wts["w_ple_proj"], g_final)


def _layer_weights(i, g_mix_norm, w_in, g_q_lat, w_q_up, g_kv_lat, w_kv_up,
                   ssm_a_re, ssm_a_im, ssm_b_re, ssm_b_im, ssm_c_re, ssm_c_im, ssm_d, ssm_log_step,
                   w_glu, b_glu, g_attn_out, g_ssm_out, w_out, g_moe_norm,
                   w_group_router, b_group_router, w_expert_router, b_expert_router,
                   g_ple_norm, w_ple_gate, w_ple_proj):
    off_kv, off_kr, off_ssm = Q_LORA, Q_LORA + KV_LORA, Q_LORA + KV_LORA + QK_ROPE
    wi = w_in[i]
    kr_block = jnp.zeros((D_MODEL, HEAD_PAD), F32).at[:, ROPE_LANE0:ROPE_LANE0 + QK_ROPE].set(wi[:, off_kr:off_ssm])
    w_in_p = jnp.concatenate([wi[:, :off_kr], kr_block, wi[:, off_ssm:]], axis=1)
    wq = w_q_up[i].reshape(Q_LORA, N_HEADS, QK_NOPE + QK_ROPE) * (ATTN_SCALE * math.log2(math.e))
    wq = jnp.pad(wq, ((0, 0), (0, 0), (0, HEAD_PAD - QK_NOPE - QK_ROPE))).reshape(Q_LORA, N_HEADS * HEAD_PAD)
    wkv = w_kv_up[i].reshape(KV_LORA, N_HEADS, QK_NOPE + V_HEAD)
    wk = jnp.pad(wkv[..., :QK_NOPE], ((0, 0), (0, 0), (0, HEAD_PAD - QK_NOPE))).reshape(KV_LORA, N_HEADS * HEAD_PAD)
    wv = jnp.pad(wkv[..., QK_NOPE:], ((0, 0), (0, 0), (0, HEAD_PAD - V_HEAD))).reshape(KV_LORA, N_HEADS * HEAD_PAD)
    lam = lax.complex(ssm_a_re[i], ssm_a_im[i])
    lam_bar = jnp.exp(lam * jnp.exp(ssm_log_step[i])[:, None])
    b_fac = (lam_bar - 1.0) / lam
    bc = lax.complex(ssm_b_re[i], ssm_b_im[i]) * b_fac[:, :, None]
    n_per = SSM_GROUPS // S5_SLABS

    def block_diag(blocks):
        s, n, a, b = blocks.shape
        eye = jnp.eye(n, dtype=blocks.dtype)
        return jnp.einsum("snab,nm->snamb", blocks, eye).reshape(s, n * a, n * b)

    b_hp_re = jnp.real(bc).transpose(0, 2, 1).reshape(S5_SLABS, n_per, SSM_CH, SSM_STATE)
    b_hp_im = jnp.imag(bc).transpose(0, 2, 1).reshape(S5_SLABS, n_per, SSM_CH, SSM_STATE)
    s5_wb = jnp.concatenate([block_diag(b_hp_re), block_diag(b_hp_im)], axis=-1)
    c_re = ssm_c_re[i].transpose(0, 2, 1).reshape(S5_SLABS, n_per, SSM_STATE, SSM_CH)
    c_im = ssm_c_im[i].transpose(0, 2, 1).reshape(S5_SLABS, n_per, SSM_STATE, SSM_CH)
    s5_wc = jnp.stack([block_diag(c_re), -block_diag(c_im)])
    w_r = jnp.zeros((D_MODEL, LANES), F32)
    w_r = w_r.at[:, :N_EXPERT_GROUPS].set(w_group_router[i])
    w_r = w_r.at[:, ROUTE_E0:ROUTE_E0 + N_EXPERTS].set(w_expert_router[i])
    w_r_hi = w_r.astype(BF16)
    w_r_lo = (w_r - w_r_hi.astype(F32)).astype(BF16)
    b_r = jnp.zeros((1, LANES), F32)
    b_r = b_r.at[0, :N_EXPERT_GROUPS].set(b_group_router[i])
    b_r = b_r.at[0, ROUTE_E0:ROUTE_E0 + N_EXPERTS].set(b_expert_router[i])
    return dict(
        g_mix=g_mix_norm[i][None], w_in=w_in_p.astype(BF16),
        g_q=g_q_lat[i][None], w_q=wq.astype(BF16),
        g_kv=g_kv_lat[i][None], w_kv=jnp.concatenate([wk, wv], axis=1).astype(BF16),
        s5_wb=s5_wb.astype(BF16), s5_wc=s5_wc.astype(BF16),
        s5_are=jnp.real(lam_bar).reshape(1, N_STATE), s5_aim=jnp.imag(lam_bar).reshape(1, N_STATE),
        s5_d=ssm_d[i].reshape(1, SSM_WIDTH), w_glu=w_glu[i].astype(BF16), b_glu=b_glu[i][None],
        g_attn=g_attn_out[i][None], g_ssm=g_ssm_out[i][None],
        w_out=w_out[i].astype(BF16), g_moe=g_moe_norm[i][None],
        w_r_hi=w_r_hi, w_r_lo=w_r_lo, b_r=b_r,
        g_ple=g_ple_norm[i][None], w_ple_gate=w_ple_gate[i].astype(BF16),
        w_ple_proj=w_ple_proj[i].astype(BF16),
    )


def _slot_layout(route, cnt, n_blocks, te):
    counts = cnt[0, :N_EXPERTS].astype(I32)
    pcounts = (counts + te - 1) // te * te
    pends = jnp.cumsum(pcounts)
    pstarts = pends - pcounts
    eid = route[:TOP_K]
    rank = route[TOP_K:2 * TOP_K]
    start = jnp.sum(jnp.where(eid[:, :, None] == jnp.arange(N_EXPERTS)[None, None, :],
                              pstarts[None, None, :], 0), axis=-1)
    dest_k = (start + rank).astype(I32)
    blk0 = jnp.arange(n_blocks, dtype=I32) * te
    block_e = jnp.minimum(jnp.sum((pends[None, :] <= blk0[:, None]).astype(I32), axis=1), N_EXPERTS - 1)
    n_slots = n_blocks * te
    gap_start = jnp.concatenate([pstarts + counts, pends[-1:]])
    gap_len = jnp.concatenate([pcounts - counts, n_slots - pends[-1:]])
    gap_end = jnp.cumsum(gap_len)
    j = jnp.arange(n_slots - eid.size, dtype=I32)
    gap = jnp.sum((gap_end[None, :] <= j[:, None]).astype(I32), axis=1)
    onehot = gap[:, None] == jnp.arange(N_EXPERTS + 1)[None, :]
    pad_rows = j + jnp.sum(jnp.where(onehot, (gap_start - (gap_end - gap_len))[None, :], 0), axis=1)
    return dest_k, block_e, pad_rows.astype(I32)


def kernel(x, p, positions, g_mix_norm, w_in, g_q_lat, w_q_up, g_kv_lat, w_kv_up, ssm_a_re, ssm_a_im, ssm_b_re, ssm_b_im, ssm_c_re, ssm_c_im, ssm_d, ssm_log_step, w_glu, b_glu, g_attn_out, g_ssm_out, w_out, g_moe_norm, w_group_router, b_group_router, w_expert_router, b_expert_router, w_exp_gate, w_exp_up, w_exp_down, g_ple_norm, w_ple_gate, w_ple_proj, g_final):
    bsz, seq, _ = x.shape
    depth = p.shape[0]
    n_tok = bsz * seq
    t = _tiles(n_tok, seq)
    n_slots = TOP_K * n_tok + N_EXPERTS * t["te"]
    n_blocks = n_slots // t["te"]

    cosf, sinf = _rope_tables(positions.reshape(n_tok, 1), t["tm"])
    sub = min(SUB_ROWS, t["tm"])
    tril = jnp.tril(jnp.ones((sub, sub), F32), k=-1).astype(BF16)
    g_fin = g_final[None]

    h = x.reshape(n_tok, D_MODEL)
    for i in range(depth):
        wts = _layer_weights(i, g_mix_norm, w_in, g_q_lat, w_q_up, g_kv_lat, w_kv_up,
                             ssm_a_re, ssm_a_im, ssm_b_re, ssm_b_im, ssm_c_re, ssm_c_im, ssm_d,
                             ssm_log_step, w_glu, b_glu, g_attn_out, g_ssm_out, w_out, g_moe_norm,
                             w_group_router, b_group_router, w_expert_router, b_expert_router,
                             g_ple_norm, w_ple_gate, w_ple_proj)
        q, k, v, u_tm = _inproj(h, cosf, sinf, wts, bsz, seq, t["tm"])
        attn_n = _attention(q, k, v, wts["g_attn"], bsz, seq, t["tq"])
        ssm_n = _s5(u_tm.reshape(seq, bsz, SSM_WIDTH), wts, attn_n, bsz, seq, t["lc"])
        h1, xn, route, rw, cnt = _outproj(h, attn_n, ssm_n.reshape(seq, bsz * SSM_WIDTH), wts, tril,
                                          bsz, seq, t["tm"])
        dest_k, block_e, pad_rows = _slot_layout(route, cnt, n_blocks, t["te"])
        xbuf = _dispatch(dest_k, pad_rows, xn, n_slots)
        ybuf = _experts(block_e, xbuf, w_exp_gate, w_exp_up, w_exp_down, i, t["te"])
        h = _combine(dest_k, ybuf, h1, rw, p.reshape(depth, n_tok, PLE_DIM), i, wts, g_fin, t["tf"],
                     final=(i == depth - 1))
    return h.reshape(bsz, seq, D_MODEL)
```

```python
import functools
import math

import jax
import jax.numpy as jnp
from jax import lax
from jax.experimental import pallas as pl
from jax.experimental.pallas import tpu as pltpu
from jax.experimental.pallas import tpu_sc as plsc

F32 = jnp.float32
BF16 = jnp.bfloat16
I32 = jnp.int32
U32 = jnp.uint32

D_MODEL = 1024
MLA_WIDTH = 512
SSM_WIDTH = 512
V_HEAD = 64
N_HEADS = 8
QK_NOPE = 64
QK_ROPE = 32
HALF_ROPE = QK_ROPE // 2
Q_LORA = 256
KV_LORA = 128
ROPE_BASE = 10000.0
ATTN_SCALE = 1.0 / math.sqrt(QK_NOPE + QK_ROPE)
SSM_CH = 16
SSM_GROUPS = 32
SSM_STATE = 64
N_EXPERT_GROUPS = 4
EXPERTS_PER_GROUP = 8
N_EXPERTS = 32
TOP_K = 2
D_EXPERT = 256
PLE_DIM = 256
EPS = 1e-6

LANES = 128
SUBLANES = 8
HEAD_PAD = 128
ROPE_LANE0 = QK_NOPE
N_STATE = SSM_GROUPS * SSM_STATE
PACKED = D_MODEL // 2
NEG = -1e30
VMEM_LIMIT = 56 * 1024 * 1024
SUB_ROWS = 512
INPROJ_ROWS = 256
COMBINE_ROWS = 256
EXPERT_ROWS = 256


def _tiles(n_tok, seq):
    return dict(
        tm=min(1024, seq),
        tq=min(512, seq),
        lc=min(64, seq),
        te=512,
        tf=min(1024, n_tok),
    )


def _rms(x, g):
    ms = jnp.mean(x * x, axis=-1, keepdims=True)
    return x * lax.rsqrt(ms + EPS) * g


def _sigmoid(x):
    return 1.0 / (1.0 + jnp.exp(-x))


def _pack_bf16_pairs(x):
    lo = lax.bitcast_convert_type(x[:, :PACKED].astype(BF16).astype(F32), U32)
    hi = lax.bitcast_convert_type(x[:, PACKED:].astype(BF16).astype(F32), U32)
    return (lo >> 16) | hi


def _unpack_bf16_pairs(w):
    lo = lax.bitcast_convert_type(w << 16, F32)
    hi = lax.bitcast_convert_type(w & jnp.uint32(0xFFFF0000), F32)
    return jnp.concatenate([lo, hi], axis=1)


def _cparams(sem):
    return pltpu.CompilerParams(dimension_semantics=sem, vmem_limit_bytes=VMEM_LIMIT)


def _full(shape):
    nd = len(shape)
    return pl.BlockSpec(shape, lambda *_: (0,) * nd)


def _rope_table_kernel(pos_ref, invf_ref, cos_ref, sin_ref):
    ang = pos_ref[...].astype(F32) * invf_ref[...]
    cos_ref[...] = jnp.cos(ang)
    sin_ref[...] = jnp.sin(ang)


def _rope_tables(pos_col, tm):
    n_tok = pos_col.shape[0]
    lane = jnp.arange(LANES)
    in_rope = (lane >= ROPE_LANE0) & (lane < ROPE_LANE0 + QK_ROPE)
    freq = ROPE_BASE ** (-((lane - ROPE_LANE0) % HALF_ROPE).astype(F32) / HALF_ROPE)
    invf = jnp.where(in_rope, freq, 0.0).astype(F32)[None, :]
    return pl.pallas_call(
        _rope_table_kernel,
        out_shape=(jax.ShapeDtypeStruct((n_tok, LANES), F32),) * 2,
        grid=(n_tok // tm,),
        in_specs=[pl.BlockSpec((tm, 1), lambda i: (i, 0)), _full((1, LANES))],
        out_specs=(pl.BlockSpec((tm, LANES), lambda i: (i, 0)),) * 2,
        compiler_params=_cparams(("parallel",)),
        name="rope_tables",
    )(pos_col, invf)


def _rope(x, cosf, s_lo, s_hi):
    n = x.shape[-1]
    return (x * cosf + pltpu.roll(x, n - HALF_ROPE, 1) * s_lo + pltpu.roll(x, HALF_ROPE, 1) * s_hi)


def _row_chunks(n_rows, sub):
    sub = min(sub, n_rows)
    return [slice(c * sub, (c + 1) * sub) for c in range(n_rows // sub)]


def _inproj_kernel(h_ref, cos_ref, sin_ref, gmix_ref, win_ref, gq_ref, wq_ref, gkv_ref, wkv_ref,
                   q_ref, k_ref, v_ref, u_ref):
    lane = lax.broadcasted_iota(I32, (1, LANES), 1)
    ones_lane = jnp.where(lane == V_HEAD, 1.0, 0.0)
    chunks = _row_chunks(h_ref.shape[0], INPROJ_ROWS)
    zs = []
    for r in chunks:
        xn = _rms(h_ref[r, :], gmix_ref[...]).astype(BF16)
        zs.append(jnp.dot(xn, win_ref[...], preferred_element_type=F32))
    qs, kvs = [], []
    for z in zs:
        qn = _rms(z[:, :Q_LORA], gq_ref[...]).astype(BF16)
        qs.append(jnp.dot(qn, wq_ref[...], preferred_element_type=F32))
        kvn = _rms(z[:, Q_LORA:Q_LORA + KV_LORA], gkv_ref[...]).astype(BF16)
        kvs.append(jnp.dot(kvn, wkv_ref[...], preferred_element_type=F32))
    for r, z, q, kv in zip(chunks, zs, qs, kvs):
        cosf = cos_ref[r, :]
        sinf = sin_ref[r, :]
        s_lo = jnp.where((lane >= ROPE_LANE0) & (lane < ROPE_LANE0 + HALF_ROPE), -sinf, 0.0)
        s_hi = jnp.where((lane >= ROPE_LANE0 + HALF_ROPE) & (lane < ROPE_LANE0 + QK_ROPE), sinf, 0.0)
        kpe = _rope(z[:, Q_LORA + KV_LORA:Q_LORA + KV_LORA + HEAD_PAD], cosf, s_lo, s_hi)
        for hd in range(N_HEADS):
            sl = slice(hd * HEAD_PAD, (hd + 1) * HEAD_PAD)
            vsl = slice(N_HEADS * HEAD_PAD + hd * HEAD_PAD, N_HEADS * HEAD_PAD + (hd + 1) * HEAD_PAD)
            q_ref[r, sl] = _rope(q[:, sl], cosf, s_lo, s_hi).astype(BF16)
            k_ref[r, sl] = (kv[:, sl] + kpe).astype(BF16)
            v_ref[r, sl] = (kv[:, vsl] + ones_lane).astype(BF16)
        u_ref[r, :] = z[:, Q_LORA + KV_LORA + HEAD_PAD:]


def _inproj(h, cosf, sinf, wts, bsz, seq, tm):
    n_tok = bsz * seq
    nl = seq // tm
    row = lambda b, l: (b * nl + l, 0)
    qk_w = N_HEADS * HEAD_PAD
    return pl.pallas_call(
        _inproj_kernel,
        out_shape=(jax.ShapeDtypeStruct((n_tok, qk_w), BF16),
                   jax.ShapeDtypeStruct((n_tok, qk_w), BF16),
                   jax.ShapeDtypeStruct((n_tok, qk_w), BF16),
                   jax.ShapeDtypeStruct((seq, bsz * SSM_WIDTH), F32)),
        grid=(bsz, nl),
        in_specs=[pl.BlockSpec((tm, D_MODEL), row),
                  pl.BlockSpec((tm, LANES), row), pl.BlockSpec((tm, LANES), row),
                  _full((1, D_MODEL)), _full((D_MODEL, D_MODEL)),
                  _full((1, Q_LORA)), _full((Q_LORA, qk_w)),
                  _full((1, KV_LORA)), _full((KV_LORA, 2 * qk_w))],
        out_specs=(pl.BlockSpec((tm, qk_w), row), pl.BlockSpec((tm, qk_w), row),
                   pl.BlockSpec((tm, qk_w), row),
                   pl.BlockSpec((tm, SSM_WIDTH), lambda b, l: (l, b))),
        compiler_params=_cparams(("parallel", "parallel")),
        name="inproj",
    )(h, cosf, sinf, wts["g_mix"], wts["w_in"], wts["g_q"], wts["w_q"], wts["g_kv"], wts["w_kv"])


ATTN_RB = 32
ATTN_GROUP = 4


def _attn_kernel(q_ref, k_ref, v_ref, g_ref, o_ref, s_ref, p_ref, m_ref, al_ref, acc_ref, out_ref, *, tq):
    qi = pl.program_id(1)
    col = lax.broadcasted_iota(I32, (ATTN_RB, tq), 1)
    row = lax.broadcasted_iota(I32, (ATTN_RB, tq), 0)
    for pr in range(N_HEADS // ATTN_GROUP):
        heads = [ATTN_GROUP * pr + a for a in range(ATTN_GROUP)]
        m_ref[...] = jnp.full(m_ref.shape, NEG, F32)
        acc_ref[...] = jnp.zeros(acc_ref.shape, F32)

        def scores(j, slot, heads=heads):
            rows = pl.ds(pl.multiple_of(j * tq, tq), tq)
            for a, hd in enumerate(heads):
                hs = slice(hd * HEAD_PAD, (hd + 1) * HEAD_PAD)
                s_ref[slot, a] = lax.dot_general(q_ref[:, hs], k_ref[rows, hs], (((1,), (1,)), ((), ())),
                                                 preferred_element_type=F32)

        def softmax_pv(j, slot, masked, heads=heads):
            rows = pl.ds(pl.multiple_of(j * tq, tq), tq)
            for a in range(ATTN_GROUP):
                for c in range(tq // ATTN_RB):
                    r = slice(c * ATTN_RB, (c + 1) * ATTN_RB)
                    s = s_ref[slot, a, r, :]
                    if masked:
                        s = jnp.where(col <= row + c * ATTN_RB, s, NEG)
                    m_old = m_ref[a, r, :]
                    m_new = jnp.maximum(m_old, jnp.max(s, axis=-1, keepdims=True))
                    p_ref[a, r, :] = jnp.exp2(s - m_new).astype(BF16)
                    al_ref[a, r, :] = jnp.exp2(m_old - m_new)
                    m_ref[a, r, :] = m_new
            for a, hd in enumerate(heads):
                pv = jnp.dot(p_ref[a], v_ref[rows, hd * HEAD_PAD:(hd + 1) * HEAD_PAD],
                             preferred_element_type=F32)
                acc_ref[a] = al_ref[a] * acc_ref[a] + pv

        def two_tiles(jj, carry):
            t = 2 * jj
            scores(t + 1, 1)
            softmax_pv(t, 0, masked=False)
            scores(t + 2, 0)
            softmax_pv(t + 1, 1, masked=False)
            return carry

        scores(0, 0)
        lax.fori_loop(0, qi // 2, two_tiles, 0)

        @pl.when(qi % 2 == 0)
        def _():
            softmax_pv(qi, 0, masked=True)

        @pl.when(qi % 2 == 1)
        def _():
            scores(qi, 1)
            softmax_pv(qi - 1, 0, masked=False)
            softmax_pv(qi, 1, masked=True)

        for a, hd in enumerate(heads):
            acc = acc_ref[a]
            out_ref[:, hd * V_HEAD:(hd + 1) * V_HEAD] = acc[:, :V_HEAD] / acc[:, V_HEAD:V_HEAD + 1]
    o_ref[...] = _rms(out_ref[...], g_ref[...]).astype(BF16)


def _attention(q, k, v, g_attn, bsz, seq, tq):
    n_tok = bsz * seq
    nq = seq // tq
    qk_w = N_HEADS * HEAD_PAD
    return pl.pallas_call(
        functools.partial(_attn_kernel, tq=tq),
        out_shape=jax.ShapeDtypeStruct((n_tok, MLA_WIDTH), BF16),
        grid=(bsz, nq),
        in_specs=[pl.BlockSpec((tq, qk_w), lambda b, i: (b * nq + i, 0)),
                  pl.BlockSpec((seq, qk_w), lambda b, i: (b, 0)),
                  pl.BlockSpec((seq, qk_w), lambda b, i: (b, 0)),
                  _full((1, MLA_WIDTH))],
        out_specs=pl.BlockSpec((tq, MLA_WIDTH), lambda b, i: (b * nq + i, 0)),
        scratch_shapes=[pltpu.VMEM((2, ATTN_GROUP, tq, tq), F32), pltpu.VMEM((ATTN_GROUP, tq, tq), BF16),
                        pltpu.VMEM((ATTN_GROUP, tq, 1), F32), pltpu.VMEM((ATTN_GROUP, tq, 1), F32),
                        pltpu.VMEM((ATTN_GROUP, tq, HEAD_PAD), F32), pltpu.VMEM((tq, MLA_WIDTH), F32)],
        compiler_params=_cparams(("parallel", "parallel")),
        name="attn",
    )(q, k, v, g_attn)


S5_SLABS = 4
S5_SLAB_U = SSM_WIDTH // S5_SLABS
S5_SLAB_X = N_STATE // S5_SLABS


def _s5_kernel(u_ref, wb_ref, are_ref, aim_ref, wc_ref, d_ref, wglu_ref, bglu_ref, g_ref, after_ref,
               o_ref, xr_ref, xi_ref, sr_ref, si_ref, *, lc, nb):
    del after_ref

    @pl.when(pl.program_id(0) == 0)
    def _():
        xr_ref[...] = jnp.zeros_like(xr_ref)
        xi_ref[...] = jnp.zeros_like(xi_ref)

    rows = lc * nb
    u = u_ref[...].reshape(rows, SSM_WIDTH)
    ub = u.astype(BF16)
    def drive(j):
        bu = jnp.dot(ub[:, j * S5_SLAB_U:(j + 1) * S5_SLAB_U], wb_ref[j], preferred_element_type=F32)
        sr_ref[:, j * S5_SLAB_X:(j + 1) * S5_SLAB_X] = bu[:, :S5_SLAB_X]
        si_ref[:, j * S5_SLAB_X:(j + 1) * S5_SLAB_X] = bu[:, S5_SLAB_X:]

    ys = []
    drive(0)
    for c in range(S5_SLABS):
        if c + 1 < S5_SLABS:
            drive(c + 1)
        sl = slice(c * S5_SLAB_X, (c + 1) * S5_SLAB_X)
        ar = jnp.broadcast_to(are_ref[:, sl], (nb, S5_SLAB_X))
        ai = jnp.broadcast_to(aim_ref[:, sl], (nb, S5_SLAB_X))
        xr = xr_ref[:, sl]
        xi = xi_ref[:, sl]
        for t in range(lc):
            r = slice(t * nb, (t + 1) * nb)
            xr, xi = (ar * xr - ai * xi + sr_ref[r, sl], ar * xi + ai * xr + si_ref[r, sl])
            sr_ref[r, sl] = xr
            si_ref[r, sl] = xi
        xr_ref[:, sl] = xr
        xi_ref[:, sl] = xi
        ys.append(jnp.dot(sr_ref[:, sl].astype(BF16), wc_ref[0, c], preferred_element_type=F32)
                  + jnp.dot(si_ref[:, sl].astype(BF16), wc_ref[1, c], preferred_element_type=F32))
    y = jnp.concatenate(ys, axis=-1) + d_ref[...] * u
    y = 0.5 * y * (1.0 + jnp.tanh(math.sqrt(2.0 / math.pi) * (y + 0.044715 * (y * y * y))))
    gate = jnp.dot(y.astype(BF16), wglu_ref[...], preferred_element_type=F32) + bglu_ref[...]
    y = y * _sigmoid(gate)
    o_ref[...] = _rms(y, g_ref[...]).astype(BF16).reshape(lc, nb, SSM_WIDTH)


def _s5(u_tm, wts, after, bsz, seq, lc):
    rows = lc * bsz
    return pl.pallas_call(
        functools.partial(_s5_kernel, lc=lc, nb=bsz),
        out_shape=jax.ShapeDtypeStruct((seq, bsz, SSM_WIDTH), BF16),
        grid=(seq // lc,),
        in_specs=[pl.BlockSpec((lc, bsz, SSM_WIDTH), lambda i: (i, 0, 0)),
                  _full((S5_SLABS, S5_SLAB_U, 2 * S5_SLAB_X)),
                  _full((1, N_STATE)), _full((1, N_STATE)),
                  _full((2, S5_SLABS, S5_SLAB_X, S5_SLAB_U)),
                  _full((1, SSM_WIDTH)), _full((SSM_WIDTH, SSM_WIDTH)), _full((1, SSM_WIDTH)),
                  _full((1, SSM_WIDTH)), pl.BlockSpec(memory_space=pl.ANY)],
        out_specs=pl.BlockSpec((lc, bsz, SSM_WIDTH), lambda i: (i, 0, 0)),
        scratch_shapes=[pltpu.VMEM((bsz, N_STATE), F32), pltpu.VMEM((bsz, N_STATE), F32),
                        pltpu.VMEM((rows, N_STATE), F32), pltpu.VMEM((rows, N_STATE), F32)],
        compiler_params=_cparams(("arbitrary",)),
        name="s5",
    )(u_tm, wts["s5_wb"], wts["s5_are"], wts["s5_aim"], wts["s5_wc"], wts["s5_d"],
      wts["w_glu"], wts["b_glu"], wts["g_ssm"], after)


ROUTE_E0 = N_EXPERT_GROUPS


def _outproj_kernel(h_ref, a_ref, s_ref, wout_ref, gmoe_ref, wrh_ref, wrl_ref, br_ref, tril_ref,
                    h1_ref, xn_ref, route_ref, rw_ref, cnt_ref):
    @pl.when((pl.program_id(0) == 0) & (pl.program_id(1) == 0))
    def _():
        cnt_ref[...] = jnp.zeros_like(cnt_ref)

    sub = tril_ref.shape[0]
    lane = lax.broadcasted_iota(I32, (sub, LANES), 1).astype(F32)
    big = float(LANES)
    chunks = _row_chunks(h_ref.shape[0], sub)
    xns = []
    for r in chunks:
        h1 = (h_ref[r, :]
              + jnp.dot(a_ref[r, :], wout_ref[:MLA_WIDTH, :], preferred_element_type=F32)
              + jnp.dot(s_ref[r, :], wout_ref[MLA_WIDTH:, :], preferred_element_type=F32))
        h1_ref[r, :] = h1
        xns.append(_rms(h1, gmoe_ref[...]))
    lgs = []
    for r, xn in zip(chunks, xns):
        xn_ref[r, :] = _pack_bf16_pairs(xn)
        x_hi = xn.astype(BF16)
        x_lo = (xn - x_hi.astype(F32)).astype(BF16)
        lgs.append((jnp.dot(x_hi, wrh_ref[...], preferred_element_type=F32)
                    + (jnp.dot(x_lo, wrh_ref[...], preferred_element_type=F32)
                       + jnp.dot(x_hi, wrl_ref[...], preferred_element_type=F32))) + br_ref[...])
    for r, lg in zip(chunks, lgs):
        gl = jnp.where(lane < N_EXPERT_GROUPS, lg, NEG)
        gmax = jnp.max(gl, axis=-1, keepdims=True)
        gsum = jnp.sum(jnp.exp(gl - gmax), axis=-1, keepdims=True)
        g_top_p = 1.0 / gsum
        gidx = jnp.min(jnp.where(gl == gmax, lane, big), axis=-1, keepdims=True)
        lo = ROUTE_E0 + EXPERTS_PER_GROUP * gidx
        sel = jnp.where((lane >= lo) & (lane < lo + EXPERTS_PER_GROUP), lg, NEG)
        m1 = jnp.max(sel, axis=-1, keepdims=True)
        i1 = jnp.min(jnp.where(sel == m1, lane, big), axis=-1, keepdims=True)
        sel2 = jnp.where(lane == i1, NEG, sel)
        m2 = jnp.max(sel2, axis=-1, keepdims=True)
        i2 = jnp.min(jnp.where(sel2 == m2, lane, big), axis=-1, keepdims=True)
        e21 = jnp.exp(m2 - m1)
        w1 = g_top_p / (1.0 + e21)
        w2 = g_top_p * e21 / (1.0 + e21)
        e1 = i1 - ROUTE_E0
        e2 = i2 - ROUTE_E0
        oh = jnp.where((lane == e1) | (lane == e2), 1.0, 0.0)
        before = cnt_ref[...] + jnp.dot(tril_ref[...], oh.astype(BF16), preferred_element_type=F32)
        r1 = jnp.sum(jnp.where(lane == e1, before, 0.0), axis=-1, keepdims=True)
        r2 = jnp.sum(jnp.where(lane == e2, before, 0.0), axis=-1, keepdims=True)
        cnt_ref[...] = cnt_ref[...] + jnp.sum(oh, axis=0, keepdims=True)
        route = jnp.where(lane == 0, e1, jnp.where(lane == 1, e2, jnp.where(lane == 2, r1, r2)))
        route_ref[:, r] = jnp.transpose(route)[:SUBLANES, :].astype(I32)
        rw_ref[r, :] = jnp.where(lane == 0, w1, w2)


def _outproj(h, attn_n, ssm_tm, wts, tril, bsz, seq, tm):
    n_tok = bsz * seq
    nl = seq // tm
    row = lambda b, l: (b * nl + l, 0)
    return pl.pallas_call(
        _outproj_kernel,
        out_shape=(jax.ShapeDtypeStruct((n_tok, D_MODEL), F32),
                   jax.ShapeDtypeStruct((n_tok, PACKED), U32),
                   jax.ShapeDtypeStruct((SUBLANES, n_tok), I32),
                   jax.ShapeDtypeStruct((n_tok, LANES), F32),
                   jax.ShapeDtypeStruct((1, LANES), F32)),
        grid=(bsz, nl),
        in_specs=[pl.BlockSpec((tm, D_MODEL), row),
                  pl.BlockSpec((tm, MLA_WIDTH), row),
                  pl.BlockSpec((tm, SSM_WIDTH), lambda b, l: (l, b)),
                  _full((D_MODEL, D_MODEL)), _full((1, D_MODEL)),
                  _full((D_MODEL, LANES)), _full((D_MODEL, LANES)), _full((1, LANES)),
                  _full(tril.shape)],
        out_specs=(pl.BlockSpec((tm, D_MODEL), row), pl.BlockSpec((tm, PACKED), row),
                   pl.BlockSpec((SUBLANES, tm), lambda b, l: (0, b * nl + l)),
                   pl.BlockSpec((tm, LANES), row),
                   _full((1, LANES))),
        compiler_params=_cparams(("arbitrary", "arbitrary")),
        name="outproj",
    )(h, attn_n, ssm_tm, wts["w_out"], wts["g_moe"], wts["w_r_hi"], wts["w_r_lo"], wts["b_r"], tril)


SC_CORES = 2
SC_SUBCORES = 16
SC_WORKERS = SC_CORES * SC_SUBCORES
SC_CHUNK = 64


def _sc_mesh():
    return plsc.VectorSubcoreMesh(core_axis_name="c", subcore_axis_name="s",
                                  num_cores=SC_CORES, num_subcores=SC_SUBCORES)


def _sc_worker_id():
    return lax.axis_index("s") * SC_CORES + lax.axis_index("c")


def _sc_two_slot_pipeline(n_chunks, load, stores):
    assert n_chunks % 2 == 0
    load(0, 0).start()

    @pl.loop(0, n_chunks // 2)
    def _(jj):
        for slot in (0, 1):
            j = 2 * jj + slot

            @pl.when(j >= 1)
            def _():
                for cp in stores(j - 1, 1 - slot):
                    cp.wait()

            @pl.when(j + 1 < n_chunks)
            def _():
                load(j + 1, 1 - slot).start()

            load(j, slot).wait()
            for cp in stores(j, slot):
                cp.start()

    for cp in stores(n_chunks - 1, 1):
        cp.wait()


def _sc_rows(j):
    return pl.ds(pl.multiple_of(j * SC_CHUNK, SC_CHUNK), SC_CHUNK)


def _dispatch(dest_k, pad_rows, xn, n_slots):
    n_tok, width = xn.shape
    n_pad = pad_rows.shape[0]
    per_w = n_tok // SC_WORKERS
    per_p = n_pad // SC_WORKERS
    assert per_p <= per_w

    @functools.partial(
        pl.kernel, mesh=_sc_mesh(),
        out_type=jax.ShapeDtypeStruct((n_slots, width), xn.dtype),
        scratch_types=[pltpu.VMEM((per_w,), I32), pltpu.VMEM((per_w,), I32), pltpu.VMEM((per_p,), I32),
                       pltpu.VMEM((2, SC_CHUNK, width), xn.dtype),
                       pltpu.SemaphoreType.DMA((2,)), pltpu.SemaphoreType.DMA((2, TOP_K))],
        name="sc_dispatch_rows",
    )
    def scatter_kernel(xn_hbm, d0_hbm, d1_hbm, dp_hbm, out_hbm, i0_v, i1_v, ip_v, rows_v, lsem, ssem):
        wid = _sc_worker_id()
        base = wid * per_w
        pltpu.sync_copy(d0_hbm.at[pl.ds(base, per_w)], i0_v)
        pltpu.sync_copy(d1_hbm.at[pl.ds(base, per_w)], i1_v)
        pltpu.sync_copy(dp_hbm.at[pl.ds(wid * per_p, per_p)], ip_v)

        def load(j, slot):
            return pltpu.make_async_copy(xn_hbm.at[pl.ds(base + j * SC_CHUNK, SC_CHUNK)],
                                         rows_v.at[slot], lsem.at[slot])

        def scatter(idx_v, k):
            return lambda j, slot: pltpu.make_async_copy(
                rows_v.at[slot], out_hbm.at[idx_v.at[_sc_rows(j)]], ssem.at[slot, k])

        to_k0, to_k1, to_pad = scatter(i0_v, 0), scatter(i1_v, 1), scatter(ip_v, 0)
        _sc_two_slot_pipeline(per_w // SC_CHUNK, load,
                              lambda j, slot: [to_k0(j, slot), to_k1(j, slot)])
        _sc_two_slot_pipeline(per_p // SC_CHUNK, load, lambda j, slot: [to_pad(j, slot)])

    return scatter_kernel(xn, dest_k[0], dest_k[1], pad_rows)


def _experts_kernel(be_ref, x_ref, wg_ref, wu_ref, wd_ref, y_ref):
    del be_ref
    wg = wg_ref[...].astype(BF16)
    wu = wu_ref[...].astype(BF16)
    wd = wd_ref[...].astype(BF16)
    chunks = _row_chunks(x_ref.shape[0], EXPERT_ROWS)
    gus = []
    for r in chunks:
        xb = _unpack_bf16_pairs(x_ref[r, :]).astype(BF16)
        gus.append((jnp.dot(xb, wg, preferred_element_type=F32), jnp.dot(xb, wu, preferred_element_type=F32)))
    ys = []
    for g, u in gus:
        hdn = (g * _sigmoid(g) * u).astype(BF16)
        ys.append(jnp.dot(hdn, wd, preferred_element_type=F32))
    for r, y in zip(chunks, ys):
        y_ref[r, :] = _pack_bf16_pairs(y)


def _experts(block_e, xbuf, w_gate, w_up, w_down, layer, te):
    n_slots = xbuf.shape[0]
    grid_spec = pltpu.PrefetchScalarGridSpec(
        num_scalar_prefetch=1,
        grid=(n_slots // te,),
        in_specs=[pl.BlockSpec((te, PACKED), lambda i, be: (i, 0)),
                  pl.BlockSpec((None, None, D_MODEL, D_EXPERT), lambda i, be: (layer, be[i], 0, 0)),
                  pl.BlockSpec((None, None, D_MODEL, D_EXPERT), lambda i, be: (layer, be[i], 0, 0)),
                  pl.BlockSpec((None, None, D_EXPERT, D_MODEL), lambda i, be: (layer, be[i], 0, 0))],
        out_specs=pl.BlockSpec((te, PACKED), lambda i, be: (i, 0)),
    )
    return pl.pallas_call(
        _experts_kernel,
        out_shape=jax.ShapeDtypeStruct((n_slots, PACKED), U32),
        grid_spec=grid_spec,
        compiler_params=_cparams(("arbitrary",)),
        name="experts",
    )(block_e, xbuf, w_gate, w_up, w_down)


def _sc_gather_rows(table, idx):
    n_rows = idx.shape[0]
    width = table.shape[1]
    per_w = n_rows // SC_WORKERS

    @functools.partial(
        pl.kernel, mesh=_sc_mesh(),
        out_type=jax.ShapeDtypeStruct((n_rows, width), table.dtype),
        scratch_types=[pltpu.VMEM((per_w,), I32), pltpu.VMEM((2, SC_CHUNK, width), table.dtype),
                       pltpu.SemaphoreType.DMA((2,)), pltpu.SemaphoreType.DMA((2,))],
        name="sc_gather_rows",
    )
    def gather_kernel(table_hbm, idx_hbm, out_hbm, idx_v, rows_v, lsem, ssem):
        base = _sc_worker_id() * per_w
        pltpu.sync_copy(idx_hbm.at[pl.ds(base, per_w)], idx_v)

        def load(j, slot):
            return pltpu.make_async_copy(table_hbm.at[idx_v.at[_sc_rows(j)]], rows_v.at[slot],
                                         lsem.at[slot])

        def store(j, slot):
            return [pltpu.make_async_copy(rows_v.at[slot],
                                          out_hbm.at[pl.ds(base + j * SC_CHUNK, SC_CHUNK)],
                                          ssem.at[slot])]

        _sc_two_slot_pipeline(per_w // SC_CHUNK, load, store)

    return gather_kernel(table, idx)


def _combine_kernel(y0_ref, y1_ref, h1_ref, rw_ref, p_ref, gple_ref, wgate_ref, wproj_ref,
                    gfin_ref, o_ref, *, final):
    chunks = _row_chunks(h1_ref.shape[0], COMBINE_ROWS)
    h2s, gates, ples = [], [], []
    for r in chunks:
        rw = rw_ref[r, :]
        h2s.append(h1_ref[r, :] + rw[:, 0:1] * _unpack_bf16_pairs(y0_ref[r, :])
                   + rw[:, 1:2] * _unpack_bf16_pairs(y1_ref[r, :]))
    for r, h2 in zip(chunks, h2s):
        hn = _rms(h2, gple_ref[...]).astype(BF16)
        gates.append(jnp.dot(hn, wgate_ref[...], preferred_element_type=F32))
        ples.append(jnp.dot(p_ref[r, :].astype(BF16), wproj_ref[...], preferred_element_type=F32))
    for r, h2, gate, ple in zip(chunks, h2s, gates, ples):
        h3 = h2 + ple * _sigmoid(gate)
        if final:
            h3 = _rms(h3, gfin_ref[...])
        o_ref[r, :] = h3


def _combine(dest_k, ybuf, h1, rw, p_all, layer, wts, g_final, tf, final):
    n_tok = h1.shape[0]
    n_steps = n_tok // tf
    row = lambda i: (i, 0)
    yg = _sc_gather_rows(ybuf, dest_k.reshape(-1))
    return pl.pallas_call(
        functools.partial(_combine_kernel, final=final),
        out_shape=jax.ShapeDtypeStruct((n_tok, D_MODEL), F32),
        grid=(n_steps,),
        in_specs=[pl.BlockSpec((tf, PACKED), row),
                  pl.BlockSpec((tf, PACKED), lambda i: (n_steps + i, 0)),
                  pl.BlockSpec((tf, D_MODEL), row), pl.BlockSpec((tf, LANES), row),
                  pl.BlockSpec((None, tf, PLE_DIM), lambda i: (layer, i, 0)),
                  _full((1, D_MODEL)), _full((D_MODEL, D_MODEL)), _full((PLE_DIM, D_MODEL)),
                  _full((1, D_MODEL))],
        out_specs=pl.BlockSpec((tf, D_MODEL), row),
        compiler_params=_cparams(("parallel",)),
        name="combine",
    )(yg, yg, h1, rw, p_all, wts["g_ple"], wts["w_ple_gate"], wts["w_ple_proj"], g_final)


def _layer_weights(i, g_mix_norm, w_in, g_q_lat, w_q_up, g_kv_lat, w_kv_up,
                   ssm_a_re, ssm_a_im, ssm_b_re, ssm_b_im, ssm_c_re, ssm_c_im, ssm_d, ssm_log_step,
                   w_glu, b_glu, g_attn_out, g_ssm_out, w_out, g_moe_norm,
                   w_group_router, b_group_router, w_expert_router, b_expert_router,
                   g_ple_norm, w_ple_gate, w_ple_proj):
    off_kv, off_kr, off_ssm = Q_LORA, Q_LORA + KV_LORA, Q_LORA + KV_LORA + QK_ROPE
    wi = w_in[i]
    kr_block = jnp.zeros((D_MODEL, HEAD_PAD), F32).at[:, ROPE_LANE0:ROPE_LANE0 + QK_ROPE].set(wi[:, off_kr:off_ssm])
    w_in_p = jnp.concatenate([wi[:, :off_kr], kr_block, wi[:, off_ssm:]], axis=1)
    wq = w_q_up[i].reshape(Q_LORA, N_HEADS, QK_NOPE + QK_ROPE) * (ATTN_SCALE * math.log2(math.e))
    wq = jnp.pad(wq, ((0, 0), (0, 0), (0, HEAD_PAD - QK_NOPE - QK_ROPE))).reshape(Q_LORA, N_HEADS * HEAD_PAD)
    wkv = w_kv_up[i].reshape(KV_LORA, N_HEADS, QK_NOPE + V_HEAD)
    wk = jnp.pad(wkv[..., :QK_NOPE], ((0, 0), (0, 0), (0, HEAD_PAD - QK_NOPE))).reshape(KV_LORA, N_HEADS * HEAD_PAD)
    wv = jnp.pad(wkv[..., QK_NOPE:], ((0, 0), (0, 0), (0, HEAD_PAD - V_HEAD))).reshape(KV_LORA, N_HEADS * HEAD_PAD)
    lam = lax.complex(ssm_a_re[i], ssm_a_im[i])
    lam_bar = jnp.exp(lam * jnp.exp(ssm_log_step[i])[:, None])
    b_fac = (lam_bar - 1.0) / lam
    bc = lax.complex(ssm_b_re[i], ssm_b_im[i]) * b_fac[:, :, None]
    n_per = SSM_GROUPS // S5_SLABS

    def block_diag(blocks):
        s, n, a, b = blocks.shape
        eye = jnp.eye(n, dtype=blocks.dtype)
        return jnp.einsum("snab,nm->snamb", blocks, eye).reshape(s, n * a, n * b)

    b_hp_re = jnp.real(bc).transpose(0, 2, 1).reshape(S5_SLABS, n_per, SSM_CH, SSM_STATE)
    b_hp_im = jnp.imag(bc).transpose(0, 2, 1).reshape(S5_SLABS, n_per, SSM_CH, SSM_STATE)
    s5_wb = jnp.concatenate([block_diag(b_hp_re), block_diag(b_hp_im)], axis=-1)
    c_re = ssm_c_re[i].transpose(0, 2, 1).reshape(S5_SLABS, n_per, SSM_STATE, SSM_CH)
    c_im = ssm_c_im[i].transpose(0, 2, 1).reshape(S5_SLABS, n_per, SSM_STATE, SSM_CH)
    s5_wc = jnp.stack([block_diag(c_re), -block_diag(c_im)])
    w_r = jnp.zeros((D_MODEL, LANES), F32)
    w_r = w_r.at[:, :N_EXPERT_GROUPS].set(w_group_router[i])
    w_r = w_r.at[:, ROUTE_E0:ROUTE_E0 + N_EXPERTS].set(w_expert_router[i])
    w_r_hi = w_r.astype(BF16)
    w_r_lo = (w_r - w_r_hi.astype(F32)).astype(BF16)
    b_r = jnp.zeros((1, LANES), F32)
    b_r = b_r.at[0, :N_EXPERT_GROUPS].set(b_group_router[i])
    b_r = b_r.at[0, ROUTE_E0:ROUTE_E0 + N_EXPERTS].set(b_expert_router[i])
    return dict(
        g_mix=g_mix_norm[i][None], w_in=w_in_p.astype(BF16),
        g_q=g_q_lat[i][None], w_q=wq.astype(BF16),
        g_kv=g_kv_lat[i][None], w_kv=jnp.concatenate([wk, wv], axis=1).astype(BF16),
        s5_wb=s5_wb.astype(BF16), s5_wc=s5_wc.astype(BF16),
        s5_are=jnp.real(lam_bar).reshape(1, N_STATE), s5_aim=jnp.imag(lam_bar).reshape(1, N_STATE),
        s5_d=ssm_d[i].reshape(1, SSM_WIDTH), w_glu=w_glu[i].astype(BF16), b_glu=b_glu[i][None],
        g_attn=g_attn_out[i][None], g_ssm=g_ssm_out[i][None],
        w_out=w_out[i].astype(BF16), g_moe=g_moe_norm[i][None],
        w_r_hi=w_r_hi, w_r_lo=w_r_lo, b_r=b_r,
        g_ple=g_ple_norm[i][None], w_ple_gate=w_ple_gate[i].astype(BF16),
        w_ple_proj=w_ple_proj[i].astype(BF16),
    )


def _slot_layout(route, cnt, n_blocks, te):
    counts = cnt[0, :N_EXPERTS].astype(I32)
    pcounts = (counts + te - 1) // te * te
    pends = jnp.cumsum(pcounts)
    pstarts = pends - pcounts
    eid = route[:TOP_K]
    rank = route[TOP_K:2 * TOP_K]
    start = jnp.sum(jnp.where(eid[:, :, None] == jnp.arange(N_EXPERTS)[None, None, :],
                              pstarts[None, None, :], 0), axis=-1)
    dest_k = (start + rank).astype(I32)
    blk0 = jnp.arange(n_blocks, dtype=I32) * te
    block_e = jnp.minimum(jnp.sum((pends[None, :] <= blk0[:, None]).astype(I32), axis=1), N_EXPERTS - 1)
    n_slots = n_blocks * te
    gap_start = jnp.concatenate([pstarts + counts, pends[-1:]])
    gap_len = jnp.concatenate([pcounts - counts, n_slots - pends[-1:]])
    gap_end = jnp.cumsum(gap_len)
    j = jnp.arange(n_slots - eid.size, dtype=I32)
    gap = jnp.sum((gap_end[None, :] <= j[:, None]).astype(I32), axis=1)
    onehot = gap[:, None] == jnp.arange(N_EXPERTS + 1)[None, :]
    pad_rows = j + jnp.sum(jnp.where(onehot, (gap_start - (gap_end - gap_len))[None, :], 0), axis=1)
    return dest_k, block_e, pad_rows.astype(I32)


def kernel(x, p, positions, g_mix_norm, w_in, g_q_lat, w_q_up, g_kv_lat, w_kv_up, ssm_a_re, ssm_a_im, ssm_b_re, ssm_b_im, ssm_c_re, ssm_c_im, ssm_d, ssm_log_step, w_glu, b_glu, g_attn_out, g_ssm_out, w_out, g_moe_norm, w_group_router, b_group_router, w_expert_router, b_expert_router, w_exp_gate, w_exp_up, w_exp_down, g_ple_norm, w_ple_gate, w_ple_proj, g_final):
    bsz, seq, _ = x.shape
    depth = p.shape[0]
    n_tok = bsz * seq
    t = _tiles(n_tok, seq)
    n_slots = TOP_K * n_tok + N_EXPERTS * t["te"]
    n_blocks = n_slots // t["te"]

    cosf, sinf = _rope_tables(positions.reshape(n_tok, 1), t["tm"])
    sub = min(SUB_ROWS, t["tm"])
    tril = jnp.tril(jnp.ones((sub, sub), F32), k=-1).astype(BF16)
    g_fin = g_final[None]

    h = x.reshape(n_tok, D_MODEL)
    for i in range(depth):
        wts = _layer_weights(i, g_mix_norm, w_in, g_q_lat, w_q_up, g_kv_lat, w_kv_up,
                             ssm_a_re, ssm_a_im, ssm_b_re, ssm_b_im, ssm_c_re, ssm_c_im, ssm_d,
                             ssm_log_step, w_glu, b_glu, g_attn_out, g_ssm_out, w_out, g_moe_norm,
                             w_group_router, b_group_router, w_expert_router, b_expert_router,
                             g_ple_norm, w_ple_gate, w_ple_proj)
        q, k, v, u_tm = _inproj(h, cosf, sinf, wts, bsz, seq, t["tm"])
        attn_n = _attention(q, k, v, wts["g_attn"], bsz, seq, t["tq"])
        ssm_n = _s5(u_tm.reshape(seq, bsz, SSM_WIDTH), wts, attn_n, bsz, seq, t["lc"])
        h1, xn, route, rw, cnt = _outproj(h, attn_n, ssm_n.reshape(seq, bsz * SSM_WIDTH), wts, tril,
                                          bsz, seq, t["tm"])
        dest_k, block_e, pad_rows = _slot_layout(route, cnt, n_blocks, t["te"])
        xbuf = _dispatch(dest_k, pad_rows, xn, n_slots)
        ybuf = _experts(block_e, xbuf, w_exp_gate, w_exp_up, w_exp_down, i, t["te"])
        h = _combine(dest_k, ybuf, h1, rw, p.reshape(depth, n_tok, PLE_DIM), i, wts, g_fin, t["tf"],
                     final=(i == depth - 1))
    return h.reshape(bsz, seq, D_MODEL)
```

```python
import functools
import math

import jax
import jax.numpy as jnp
from jax import lax
from jax.experimental import pallas as pl
from jax.experimental.pallas import tpu as pltpu
from jax.experimental.pallas import tpu_sc as plsc

F32 = jnp.float32
BF16 = jnp.bfloat16
I32 = jnp.int32
U32 = jnp.uint32

D_MODEL = 1024
MLA_WIDTH = 512
SSM_WIDTH = 512
V_HEAD = 64
N_HEADS = 8
QK_NOPE = 64
QK_ROPE = 32
HALF_ROPE = QK_ROPE // 2
Q_LORA = 256
KV_LORA = 128
ROPE_BASE = 10000.0
ATTN_SCALE = 1.0 / math.sqrt(QK_NOPE + QK_ROPE)
SSM_CH = 16
SSM_GROUPS = 32
SSM_STATE = 64
N_EXPERT_GROUPS = 4
EXPERTS_PER_GROUP = 8
N_EXPERTS = 32
TOP_K = 2
D_EXPERT = 256
PLE_DIM = 256
EPS = 1e-6

LANES = 128
SUBLANES = 8
HEAD_PAD = 128
ROPE_LANE0 = QK_NOPE
N_STATE = SSM_GROUPS * SSM_STATE
PACKED = D_MODEL // 2
NEG = -1e30
VMEM_LIMIT = 56 * 1024 * 1024
SUB_ROWS = 512
INPROJ_ROWS = 256
COMBINE_ROWS = 256
EXPERT_ROWS = 256
S5_ROWS = 256


def _tiles(n_tok, seq):
    return dict(
        tm=min(1024, seq),
        tq=min(512, seq),
        lc=min(64, seq),
        te=512,
        tf=min(1024, n_tok),
    )


def _rms(x, g):
    ms = jnp.mean(x * x, axis=-1, keepdims=True)
    return x * lax.rsqrt(ms + EPS) * g


def _sigmoid(x):
    return 1.0 / (1.0 + jnp.exp(-x))


def _pack_bf16_pairs(x):
    lo = lax.bitcast_convert_type(x[:, :PACKED].astype(BF16).astype(F32), U32)
    hi = lax.bitcast_convert_type(x[:, PACKED:].astype(BF16).astype(F32), U32)
    return (lo >> 16) | hi


def _unpack_bf16_pairs(w):
    lo = lax.bitcast_convert_type(w << 16, F32)
    hi = lax.bitcast_convert_type(w & jnp.uint32(0xFFFF0000), F32)
    return jnp.concatenate([lo, hi], axis=1)


def _cparams(sem):
    return pltpu.CompilerParams(dimension_semantics=sem, vmem_limit_bytes=VMEM_LIMIT)


def _full(shape):
    nd = len(shape)
    return pl.BlockSpec(shape, lambda *_: (0,) * nd)


def _rope_table_kernel(pos_ref, invf_ref, cos_ref, sin_ref):
    ang = pos_ref[...].astype(F32) * invf_ref[...]
    lane = lax.broadcasted_iota(I32, ang.shape, 1)
    hi_half = lane >= ROPE_LANE0 + HALF_ROPE
    c = jnp.cos(jnp.where(hi_half & (lane < ROPE_LANE0 + QK_ROPE), ang - 0.5 * math.pi, ang))
    cos_ref[...] = jnp.where(hi_half, pltpu.roll(c, HALF_ROPE, 1), c)
    sin_ref[...] = jnp.where(hi_half, c, pltpu.roll(c, LANES - HALF_ROPE, 1))


def _rope_tables(pos_col, tm):
    n_tok = pos_col.shape[0]
    lane = jnp.arange(LANES)
    in_rope = (lane >= ROPE_LANE0) & (lane < ROPE_LANE0 + QK_ROPE)
    freq = ROPE_BASE ** (-((lane - ROPE_LANE0) % HALF_ROPE).astype(F32) / HALF_ROPE)
    invf = jnp.where(in_rope, freq, 0.0).astype(F32)[None, :]
    return pl.pallas_call(
        _rope_table_kernel,
        out_shape=(jax.ShapeDtypeStruct((n_tok, LANES), F32),) * 2,
        grid=(n_tok // tm,),
        in_specs=[pl.BlockSpec((tm, 1), lambda i: (i, 0)), _full((1, LANES))],
        out_specs=(pl.BlockSpec((tm, LANES), lambda i: (i, 0)),) * 2,
        compiler_params=_cparams(("parallel",)),
        name="rope_tables",
    )(pos_col, invf)


def _rope(x, cosf, s_lo, s_hi):
    n = x.shape[-1]
    return (x * cosf + pltpu.roll(x, n - HALF_ROPE, 1) * s_lo + pltpu.roll(x, HALF_ROPE, 1) * s_hi)


def _row_chunks(n_rows, sub):
    sub = min(sub, n_rows)
    return [slice(c * sub, (c + 1) * sub) for c in range(n_rows // sub)]


def _inproj_kernel(h_ref, cos_ref, sin_ref, gmix_ref, win_ref, gq_ref, wq_ref, gkv_ref, wkv_ref,
                   q_ref, k_ref, v_ref, u_ref):
    lane = lax.broadcasted_iota(I32, (1, LANES), 1)
    ones_lane = jnp.where(lane == V_HEAD, 1.0, 0.0)
    chunks = _row_chunks(h_ref.shape[0], INPROJ_ROWS)
    zs = []
    for r in chunks:
        xn = _rms(h_ref[r, :], gmix_ref[...]).astype(BF16)
        zs.append(jnp.dot(xn, win_ref[...], preferred_element_type=F32))
    qs, kvs = [], []
    for z in zs:
        qn = _rms(z[:, :Q_LORA], gq_ref[...]).astype(BF16)
        qs.append(jnp.dot(qn, wq_ref[...], preferred_element_type=F32))
        kvn = _rms(z[:, Q_LORA:Q_LORA + KV_LORA], gkv_ref[...]).astype(BF16)
        kvs.append(jnp.dot(kvn, wkv_ref[...], preferred_element_type=F32))
    for r, z, q, kv in zip(chunks, zs, qs, kvs):
        cosf = cos_ref[r, :]
        sinf = sin_ref[r, :]
        s_lo = jnp.where((lane >= ROPE_LANE0) & (lane < ROPE_LANE0 + HALF_ROPE), -sinf, 0.0)
        s_hi = jnp.where((lane >= ROPE_LANE0 + HALF_ROPE) & (lane < ROPE_LANE0 + QK_ROPE), sinf, 0.0)
        kpe = _rope(z[:, Q_LORA + KV_LORA:Q_LORA + KV_LORA + HEAD_PAD], cosf, s_lo, s_hi)
        for hd in range(N_HEADS):
            sl = slice(hd * HEAD_PAD, (hd + 1) * HEAD_PAD)
            vsl = slice(N_HEADS * HEAD_PAD + hd * HEAD_PAD, N_HEADS * HEAD_PAD + (hd + 1) * HEAD_PAD)
            q_ref[r, sl] = _rope(q[:, sl], cosf, s_lo, s_hi).astype(BF16)
            k_ref[r, sl] = (kv[:, sl] + kpe).astype(BF16)
            v_ref[r, sl] = (kv[:, vsl] + ones_lane).astype(BF16)
        u_ref[r, :] = z[:, Q_LORA + KV_LORA + HEAD_PAD:]


def _inproj(h, cosf, sinf, wts, bsz, seq, tm):
    n_tok = bsz * seq
    nl = seq // tm
    row = lambda b, l: (b * nl + l, 0)
    qk_w = N_HEADS * HEAD_PAD
    return pl.pallas_call(
        _inproj_kernel,
        out_shape=(jax.ShapeDtypeStruct((n_tok, qk_w), BF16),
                   jax.ShapeDtypeStruct((n_tok, qk_w), BF16),
                   jax.ShapeDtypeStruct((n_tok, qk_w), BF16),
                   jax.ShapeDtypeStruct((seq, bsz * SSM_WIDTH), F32)),
        grid=(bsz, nl),
        in_specs=[pl.BlockSpec((tm, D_MODEL), row),
                  pl.BlockSpec((tm, LANES), row), pl.BlockSpec((tm, LANES), row),
                  _full((1, D_MODEL)), _full((D_MODEL, D_MODEL)),
                  _full((1, Q_LORA)), _full((Q_LORA, qk_w)),
                  _full((1, KV_LORA)), _full((KV_LORA, 2 * qk_w))],
        out_specs=(pl.BlockSpec((tm, qk_w), row), pl.BlockSpec((tm, qk_w), row),
                   pl.BlockSpec((tm, qk_w), row),
                   pl.BlockSpec((tm, SSM_WIDTH), lambda b, l: (l, b))),
        compiler_params=_cparams(("parallel", "parallel")),
        name="inproj",
    )(h, cosf, sinf, wts["g_mix"], wts["w_in"], wts["g_q"], wts["w_q"], wts["g_kv"], wts["w_kv"])


ATTN_RB = 32
ATTN_GROUP = 4


def _attn_kernel(q_ref, k_ref, v_ref, g_ref, o_ref, s_ref, p_ref, m_ref, al_ref, acc_ref, out_ref, *, tq):
    qi = pl.program_id(1)
    col = lax.broadcasted_iota(I32, (ATTN_RB, tq), 1)
    row = lax.broadcasted_iota(I32, (ATTN_RB, tq), 0)
    for pr in range(N_HEADS // ATTN_GROUP):
        heads = [ATTN_GROUP * pr + a for a in range(ATTN_GROUP)]
        m_ref[...] = jnp.full(m_ref.shape, NEG, F32)
        acc_ref[...] = jnp.zeros(acc_ref.shape, F32)

        def scores(j, slot, heads=heads):
            rows = pl.ds(pl.multiple_of(j * tq, tq), tq)
            for a, hd in enumerate(heads):
                hs = slice(hd * HEAD_PAD, (hd + 1) * HEAD_PAD)
                s_ref[slot, a] = lax.dot_general(q_ref[:, hs], k_ref[rows, hs], (((1,), (1,)), ((), ())),
                                                 preferred_element_type=F32)

        def softmax_pv(j, slot, masked, heads=heads):
            rows = pl.ds(pl.multiple_of(j * tq, tq), tq)
            for a in range(ATTN_GROUP):
                for c in range(tq // ATTN_RB):
                    r = slice(c * ATTN_RB, (c + 1) * ATTN_RB)
                    s = s_ref[slot, a, r, :]
                    if masked:
                        s = jnp.where(col <= row + c * ATTN_RB, s, NEG)
                    m_old = m_ref[a, r, :]
                    m_new = jnp.maximum(m_old, jnp.max(s, axis=-1, keepdims=True))
                    p_ref[a, r, :] = jnp.exp2(s - m_new).astype(BF16)
                    al_ref[a, r, :] = jnp.exp2(m_old - m_new)
                    m_ref[a, r, :] = m_new
            for a, hd in enumerate(heads):
                pv = jnp.dot(p_ref[a], v_ref[rows, hd * HEAD_PAD:(hd + 1) * HEAD_PAD],
                             preferred_element_type=F32)
                acc_ref[a] = al_ref[a] * acc_ref[a] + pv

        def two_tiles(jj, carry):
            t = 2 * jj
            scores(t + 1, 1)
            softmax_pv(t, 0, masked=False)
            scores(t + 2, 0)
            softmax_pv(t + 1, 1, masked=False)
            return carry

        scores(0, 0)
        lax.fori_loop(0, qi // 2, two_tiles, 0)

        @pl.when(qi % 2 == 0)
        def _():
            softmax_pv(qi, 0, masked=True)

        @pl.when(qi % 2 == 1)
        def _():
            scores(qi, 1)
            softmax_pv(qi - 1, 0, masked=False)
            softmax_pv(qi, 1, masked=True)

        for a, hd in enumerate(heads):
            acc = acc_ref[a]
            out_ref[:, hd * V_HEAD:(hd + 1) * V_HEAD] = acc[:, :V_HEAD] / acc[:, V_HEAD:V_HEAD + 1]
    o_ref[...] = _rms(out_ref[...], g_ref[...]).astype(BF16)


def _attention(q, k, v, g_attn, bsz, seq, tq):
    n_tok = bsz * seq
    nq = seq // tq
    qk_w = N_HEADS * HEAD_PAD
    return pl.pallas_call(
        functools.partial(_attn_kernel, tq=tq),
        out_shape=jax.ShapeDtypeStruct((n_tok, MLA_WIDTH), BF16),
        grid=(bsz, nq),
        in_specs=[pl.BlockSpec((tq, qk_w), lambda b, i: (b * nq + i, 0)),
                  pl.BlockSpec((seq, qk_w), lambda b, i: (b, 0)),
                  pl.BlockSpec((seq, qk_w), lambda b, i: (b, 0)),
                  _full((1, MLA_WIDTH))],
        out_specs=pl.BlockSpec((tq, MLA_WIDTH), lambda b, i: (b * nq + i, 0)),
        scratch_shapes=[pltpu.VMEM((2, ATTN_GROUP, tq, tq), F32), pltpu.VMEM((ATTN_GROUP, tq, tq), BF16),
                        pltpu.VMEM((ATTN_GROUP, tq, 1), F32), pltpu.VMEM((ATTN_GROUP, tq, 1), F32),
                        pltpu.VMEM((ATTN_GROUP, tq, HEAD_PAD), F32), pltpu.VMEM((tq, MLA_WIDTH), F32)],
        compiler_params=_cparams(("parallel", "parallel")),
        name="attn",
    )(q, k, v, g_attn)


S5_SLABS = 4
S5_SLAB_U = SSM_WIDTH // S5_SLABS
S5_SLAB_X = N_STATE // S5_SLABS


def _s5_kernel(u_ref, wb_ref, are_ref, aim_ref, wc_ref, d_ref, wglu_ref, bglu_ref, g_ref, after_ref,
               o_ref, xr_ref, xi_ref, sr_ref, si_ref, *, lc, nb):
    del after_ref

    @pl.when(pl.program_id(0) == 0)
    def _():
        xr_ref[...] = jnp.zeros_like(xr_ref)
        xi_ref[...] = jnp.zeros_like(xi_ref)

    rows = lc * nb
    u = u_ref[...].reshape(rows, SSM_WIDTH)
    ub = u.astype(BF16)
    def drive(j):
        bu = jnp.dot(ub[:, j * S5_SLAB_U:(j + 1) * S5_SLAB_U], wb_ref[j], preferred_element_type=F32)
        sr_ref[:, j * S5_SLAB_X:(j + 1) * S5_SLAB_X] = bu[:, :S5_SLAB_X]
        si_ref[:, j * S5_SLAB_X:(j + 1) * S5_SLAB_X] = bu[:, S5_SLAB_X:]

    ys = []
    drive(0)
    for c in range(S5_SLABS):
        if c + 1 < S5_SLABS:
            drive(c + 1)
        sl = slice(c * S5_SLAB_X, (c + 1) * S5_SLAB_X)
        ar = jnp.broadcast_to(are_ref[:, sl], (nb, S5_SLAB_X))
        ai = jnp.broadcast_to(aim_ref[:, sl], (nb, S5_SLAB_X))
        xr = xr_ref[:, sl]
        xi = xi_ref[:, sl]
        for t in range(lc):
            r = slice(t * nb, (t + 1) * nb)
            xr, xi = (ar * xr - ai * xi + sr_ref[r, sl], ar * xi + ai * xr + si_ref[r, sl])
            sr_ref[r, sl] = xr
            si_ref[r, sl] = xi
        xr_ref[:, sl] = xr
        xi_ref[:, sl] = xi
        ys.append(jnp.dot(sr_ref[:, sl].astype(BF16), wc_ref[0, c], preferred_element_type=F32)
                  + jnp.dot(si_ref[:, sl].astype(BF16), wc_ref[1, c], preferred_element_type=F32))
    y_all = jnp.concatenate(ys, axis=-1) + d_ref[...] * u
    chunks = _row_chunks(rows, S5_ROWS)
    ygs, gates = [], []
    for r in chunks:
        y = y_all[r, :]
        ygs.append(0.5 * y * (1.0 + jnp.tanh(math.sqrt(2.0 / math.pi) * (y + 0.044715 * (y * y * y)))))
    for yg in ygs:
        gates.append(jnp.dot(yg.astype(BF16), wglu_ref[...], preferred_element_type=F32) + bglu_ref[...])
    for r, yg, gate in zip(chunks, ygs, gates):
        out = _rms(yg * _sigmoid(gate), g_ref[...]).astype(BF16)
        o_ref[r.start // nb:r.stop // nb] = out.reshape((r.stop - r.start) // nb, nb, SSM_WIDTH)


def _s5(u_tm, wts, after, bsz, seq, lc):
    rows = lc * bsz
    return pl.pallas_call(
        functools.partial(_s5_kernel, lc=lc, nb=bsz),
        out_shape=jax.ShapeDtypeStruct((seq, bsz, SSM_WIDTH), BF16),
        grid=(seq // lc,),
        in_specs=[pl.BlockSpec((lc, bsz, SSM_WIDTH), lambda i: (i, 0, 0)),
                  _full((S5_SLABS, S5_SLAB_U, 2 * S5_SLAB_X)),
                  _full((1, N_STATE)), _full((1, N_STATE)),
                  _full((2, S5_SLABS, S5_SLAB_X, S5_SLAB_U)),
                  _full((1, SSM_WIDTH)), _full((SSM_WIDTH, SSM_WIDTH)), _full((1, SSM_WIDTH)),
                  _full((1, SSM_WIDTH)), pl.BlockSpec(memory_space=pl.ANY)],
        out_specs=pl.BlockSpec((lc, bsz, SSM_WIDTH), lambda i: (i, 0, 0)),
        scratch_shapes=[pltpu.VMEM((bsz, N_STATE), F32), pltpu.VMEM((bsz, N_STATE), F32),
                        pltpu.VMEM((rows, N_STATE), F32), pltpu.VMEM((rows, N_STATE), F32)],
        compiler_params=_cparams(("arbitrary",)),
        name="s5",
    )(u_tm, wts["s5_wb"], wts["s5_are"], wts["s5_aim"], wts["s5_wc"], wts["s5_d"],
      wts["w_glu"], wts["b_glu"], wts["g_ssm"], after)


ROUTE_E0 = N_EXPERT_GROUPS


def _outproj_kernel(h_ref, a_ref, s_ref, wout_ref, gmoe_ref, wrh_ref, wrl_ref, br_ref, tril_ref,
                    h1_ref, xn_ref, route_ref, rw_ref, cnt_ref):
    @pl.when((pl.program_id(0) == 0) & (pl.program_id(1) == 0))
    def _():
        cnt_ref[...] = jnp.zeros_like(cnt_ref)

    sub = tril_ref.shape[0]
    lane = lax.broadcasted_iota(I32, (sub, LANES), 1).astype(F32)
    big = float(LANES)
    chunks = _row_chunks(h_ref.shape[0], sub)
    xns = []
    for r in chunks:
        h1 = (h_ref[r, :]
              + jnp.dot(a_ref[r, :], wout_ref[:MLA_WIDTH, :], preferred_element_type=F32)
              + jnp.dot(s_ref[r, :], wout_ref[MLA_WIDTH:, :], preferred_element_type=F32))
        h1_ref[r, :] = h1
        xns.append(_rms(h1, gmoe_ref[...]))
    lgs = []
    for r, xn in zip(chunks, xns):
        xn_ref[r, :] = _pack_bf16_pairs(xn)
        x_hi = xn.astype(BF16)
        x_lo = (xn - x_hi.astype(F32)).astype(BF16)
        lgs.append((jnp.dot(x_hi, wrh_ref[...], preferred_element_type=F32)
                    + (jnp.dot(x_lo, wrh_ref[...], preferred_element_type=F32)
                       + jnp.dot(x_hi, wrl_ref[...], preferred_element_type=F32))) + br_ref[...])
    for r, lg in zip(chunks, lgs):
        gl = jnp.where(lane < N_EXPERT_GROUPS, lg, NEG)
        gmax = jnp.max(gl, axis=-1, keepdims=True)
        gsum = jnp.sum(jnp.exp(gl - gmax), axis=-1, keepdims=True)
        g_top_p = 1.0 / gsum
        gidx = jnp.min(jnp.where(gl == gmax, lane, big), axis=-1, keepdims=True)
        lo = ROUTE_E0 + EXPERTS_PER_GROUP * gidx
        sel = jnp.where((lane >= lo) & (lane < lo + EXPERTS_PER_GROUP), lg, NEG)
        m1 = jnp.max(sel, axis=-1, keepdims=True)
        i1 = jnp.min(jnp.where(sel == m1, lane, big), axis=-1, keepdims=True)
        sel2 = jnp.where(lane == i1, NEG, sel)
        m2 = jnp.max(sel2, axis=-1, keepdims=True)
        i2 = jnp.min(jnp.where(sel2 == m2, lane, big), axis=-1, keepdims=True)
        e21 = jnp.exp(m2 - m1)
        w1 = g_top_p / (1.0 + e21)
        w2 = g_top_p * e21 / (1.0 + e21)
        e1 = i1 - ROUTE_E0
        e2 = i2 - ROUTE_E0
        oh = jnp.where((lane == e1) | (lane == e2), 1.0, 0.0)
        before = cnt_ref[...] + jnp.dot(tril_ref[...], oh.astype(BF16), preferred_element_type=F32)
        r1 = jnp.sum(jnp.where(lane == e1, before, 0.0), axis=-1, keepdims=True)
        r2 = jnp.sum(jnp.where(lane == e2, before, 0.0), axis=-1, keepdims=True)
        cnt_ref[...] = cnt_ref[...] + jnp.sum(oh, axis=0, keepdims=True)
        route = jnp.where(lane == 0, e1, jnp.where(lane == 1, e2, jnp.where(lane == 2, r1, r2)))
        route_ref[:, r] = jnp.transpose(route)[:SUBLANES, :].astype(I32)
        rw_ref[r, :] = jnp.where(lane == 0, w1, w2)


def _outproj(h, attn_n, ssm_tm, wts, tril, bsz, seq, tm):
    n_tok = bsz * seq
    nl = seq // tm
    row = lambda b, l: (b * nl + l, 0)
    return pl.pallas_call(
        _outproj_kernel,
        out_shape=(jax.ShapeDtypeStruct((n_tok, D_MODEL), F32),
                   jax.ShapeDtypeStruct((n_tok, PACKED), U32),
                   jax.ShapeDtypeStruct((SUBLANES, n_tok), I32),
                   jax.ShapeDtypeStruct((n_tok, LANES), F32),
                   jax.ShapeDtypeStruct((1, LANES), F32)),
        grid=(bsz, nl),
        in_specs=[pl.BlockSpec((tm, D_MODEL), row),
                  pl.BlockSpec((tm, MLA_WIDTH), row),
                  pl.BlockSpec((tm, SSM_WIDTH), lambda b, l: (l, b)),
                  _full((D_MODEL, D_MODEL)), _full((1, D_MODEL)),
                  _full((D_MODEL, LANES)), _full((D_MODEL, LANES)), _full((1, LANES)),
                  _full(tril.shape)],
        out_specs=(pl.BlockSpec((tm, D_MODEL), row), pl.BlockSpec((tm, PACKED), row),
                   pl.BlockSpec((SUBLANES, tm), lambda b, l: (0, b * nl + l)),
                   pl.BlockSpec((tm, LANES), row),
                   _full((1, LANES))),
        compiler_params=_cparams(("arbitrary", "arbitrary")),
        name="outproj",
    )(h, attn_n, ssm_tm, wts["w_out"], wts["g_moe"], wts["w_r_hi"], wts["w_r_lo"], wts["b_r"], tril)


SC_CORES = 2
SC_SUBCORES = 16
SC_WORKERS = SC_CORES * SC_SUBCORES
SC_CHUNK = 64


def _sc_mesh():
    return plsc.VectorSubcoreMesh(core_axis_name="c", subcore_axis_name="s",
                                  num_cores=SC_CORES, num_subcores=SC_SUBCORES)


def _sc_worker_id():
    return lax.axis_index("s") * SC_CORES + lax.axis_index("c")


def _sc_two_slot_pipeline(n_chunks, load, stores):
    assert n_chunks % 2 == 0
    load(0, 0).start()

    @pl.loop(0, n_chunks // 2)
    def _(jj):
        for slot in (0, 1):
            j = 2 * jj + slot

            @pl.when(j >= 1)
            def _():
                for cp in stores(j - 1, 1 - slot):
                    cp.wait()

            @pl.when(j + 1 < n_chunks)
            def _():
                load(j + 1, 1 - slot).start()

            load(j, slot).wait()
            for cp in stores(j, slot):
                cp.start()

    for cp in stores(n_chunks - 1, 1):
        cp.wait()


def _sc_rows(j):
    return pl.ds(pl.multiple_of(j * SC_CHUNK, SC_CHUNK), SC_CHUNK)


def _dispatch(dest_k, pad_rows, xn, n_slots):
    n_tok, width = xn.shape
    n_pad = pad_rows.shape[0]
    per_w = n_tok // SC_WORKERS
    per_p = n_pad // SC_WORKERS
    assert per_p <= per_w

    @functools.partial(
        pl.kernel, mesh=_sc_mesh(),
        out_type=jax.ShapeDtypeStruct((n_slots, width), xn.dtype),
        scratch_types=[pltpu.VMEM((per_w,), I32), pltpu.VMEM((per_w,), I32), pltpu.VMEM((per_p,), I32),
                       pltpu.VMEM((2, SC_CHUNK, width), xn.dtype),
                       pltpu.SemaphoreType.DMA((2,)), pltpu.SemaphoreType.DMA((2, TOP_K))],
        name="sc_dispatch_rows",
    )
    def scatter_kernel(xn_hbm, d0_hbm, d1_hbm, dp_hbm, out_hbm, i0_v, i1_v, ip_v, rows_v, lsem, ssem):
        wid = _sc_worker_id()
        base = wid * per_w
        pltpu.sync_copy(d0_hbm.at[pl.ds(base, per_w)], i0_v)
        pltpu.sync_copy(d1_hbm.at[pl.ds(base, per_w)], i1_v)
        pltpu.sync_copy(dp_hbm.at[pl.ds(wid * per_p, per_p)], ip_v)

        def load(j, slot):
            return pltpu.make_async_copy(xn_hbm.at[pl.ds(base + j * SC_CHUNK, SC_CHUNK)],
                                         rows_v.at[slot], lsem.at[slot])

        def scatter(idx_v, k):
            return lambda j, slot: pltpu.make_async_copy(
                rows_v.at[slot], out_hbm.at[idx_v.at[_sc_rows(j)]], ssem.at[slot, k])

        to_k0, to_k1, to_pad = scatter(i0_v, 0), scatter(i1_v, 1), scatter(ip_v, 0)
        _sc_two_slot_pipeline(per_w // SC_CHUNK, load,
                              lambda j, slot: [to_k0(j, slot), to_k1(j, slot)])
        _sc_two_slot_pipeline(per_p // SC_CHUNK, load, lambda j, slot: [to_pad(j, slot)])

    return scatter_kernel(xn, dest_k[0], dest_k[1], pad_rows)


def _experts_kernel(be_ref, nused_ref, x_ref, wg_ref, wu_ref, wd_ref, y_ref):
    del be_ref

    @pl.when(pl.program_id(0) >= nused_ref[0])
    def _():
        y_ref[...] = jnp.zeros_like(y_ref)

    @pl.when(pl.program_id(0) < nused_ref[0])
    def _():
        wg = wg_ref[...].astype(BF16)
        wu = wu_ref[...].astype(BF16)
        wd = wd_ref[...].astype(BF16)
        chunks = _row_chunks(x_ref.shape[0], EXPERT_ROWS)
        gus = []
        for r in chunks:
            xb = _unpack_bf16_pairs(x_ref[r, :]).astype(BF16)
            gus.append((jnp.dot(xb, wg, preferred_element_type=F32),
                        jnp.dot(xb, wu, preferred_element_type=F32)))
        ys = []
        for g, u in gus:
            hdn = (g * _sigmoid(g) * u).astype(BF16)
            ys.append(jnp.dot(hdn, wd, preferred_element_type=F32))
        for r, y in zip(chunks, ys):
            y_ref[r, :] = _pack_bf16_pairs(y)


def _experts(block_e, n_used, xbuf, w_gate, w_up, w_down, layer, te):
    n_slots = xbuf.shape[0]
    grid_spec = pltpu.PrefetchScalarGridSpec(
        num_scalar_prefetch=2,
        grid=(n_slots // te,),
        in_specs=[pl.BlockSpec((te, PACKED), lambda i, be, nu: (i, 0)),
                  pl.BlockSpec((None, None, D_MODEL, D_EXPERT), lambda i, be, nu: (layer, be[i], 0, 0)),
                  pl.BlockSpec((None, None, D_MODEL, D_EXPERT), lambda i, be, nu: (layer, be[i], 0, 0)),
                  pl.BlockSpec((None, None, D_EXPERT, D_MODEL), lambda i, be, nu: (layer, be[i], 0, 0))],
        out_specs=pl.BlockSpec((te, PACKED), lambda i, be, nu: (i, 0)),
    )
    return pl.pallas_call(
        _experts_kernel,
        out_shape=jax.ShapeDtypeStruct((n_slots, PACKED), U32),
        grid_spec=grid_spec,
        compiler_params=_cparams(("arbitrary",)),
        name="experts",
    )(block_e, n_used, xbuf, w_gate, w_up, w_down)


def _sc_gather_rows(table, idx):
    n_rows = idx.shape[0]
    width = table.shape[1]
    per_w = n_rows // SC_WORKERS

    @functools.partial(
        pl.kernel, mesh=_sc_mesh(),
        out_type=jax.ShapeDtypeStruct((n_rows, width), table.dtype),
        scratch_types=[pltpu.VMEM((per_w,), I32), pltpu.VMEM((2, SC_CHUNK, width), table.dtype),
                       pltpu.SemaphoreType.DMA((2,)), pltpu.SemaphoreType.DMA((2,))],
        name="sc_gather_rows",
    )
    def gather_kernel(table_hbm, idx_hbm, out_hbm, idx_v, rows_v, lsem, ssem):
        base = _sc_worker_id() * per_w
        pltpu.sync_copy(idx_hbm.at[pl.ds(base, per_w)], idx_v)

        def load(j, slot):
            return pltpu.make_async_copy(table_hbm.at[idx_v.at[_sc_rows(j)]], rows_v.at[slot],
                                         lsem.at[slot])

        def store(j, slot):
            return [pltpu.make_async_copy(rows_v.at[slot],
                                          out_hbm.at[pl.ds(base + j * SC_CHUNK, SC_CHUNK)],
                                          ssem.at[slot])]

        _sc_two_slot_pipeline(per_w // SC_CHUNK, load, store)

    return gather_kernel(table, idx)


def _combine_kernel(y0_ref, y1_ref, h1_ref, rw_ref, p_ref, gple_ref, wgate_ref, wproj_ref,
                    gfin_ref, o_ref, *, final):
    chunks = _row_chunks(h1_ref.shape[0], COMBINE_ROWS)
    h2s, gates, ples = [], [], []
    for r in chunks:
        rw = rw_ref[r, :]
        h2s.append(h1_ref[r, :] + rw[:, 0:1] * _unpack_bf16_pairs(y0_ref[r, :])
                   + rw[:, 1:2] * _unpack_bf16_pairs(y1_ref[r, :]))
    for r, h2 in zip(chunks, h2s):
        hn = _rms(h2, gple_ref[...]).astype(BF16)
        gates.append(jnp.dot(hn, wgate_ref[...], preferred_element_type=F32))
        ples.append(jnp.dot(p_ref[r, :].astype(BF16), wproj_ref[...], preferred_element_type=F32))
    for r, h2, gate, ple in zip(chunks, h2s, gates, ples):
        h3 = h2 + ple * _sigmoid(gate)
        if final:
            h3 = _rms(h3, gfin_ref[...])
        o_ref[r, :] = h3


def _combine(dest_k, ybuf, h1, rw, p_all, layer, wts, g_final, tf, final):
    n_tok = h1.shape[0]
    n_steps = n_tok // tf
    row = lambda i: (i, 0)
    yg = _sc_gather_rows(ybuf, dest_k.reshape(-1))
    return pl.pallas_call(
        functools.partial(_combine_kernel, final=final),
        out_shape=jax.ShapeDtypeStruct((n_tok, D_MODEL), F32),
        grid=(n_steps,),
        in_specs=[pl.BlockSpec((tf, PACKED), row),
                  pl.BlockSpec((tf, PACKED), lambda i: (n_steps + i, 0)),
                  pl.BlockSpec((tf, D_MODEL), row), pl.BlockSpec((tf, LANES), row),
                  pl.BlockSpec((None, tf, PLE_DIM), lambda i: (layer, i, 0)),
                  _full((1, D_MODEL)), _full((D_MODEL, D_MODEL)), _full((PLE_DIM, D_MODEL)),
                  _full((1, D_MODEL))],
        out_specs=pl.BlockSpec((tf, D_MODEL), row),
        compiler_params=_cparams(("parallel",)),
        name="combine",
    )(yg, yg, h1, rw, p_all, wts["g_ple"], wts["w_ple_gate"], wts["w_ple_proj"], g_final)


def _layer_weights(i, g_mix_norm, w_in, g_q_lat, w_q_up, g_kv_lat, w_kv_up,
                   ssm_a_re, ssm_a_im, ssm_b_re, ssm_b_im, ssm_c_re, ssm_c_im, ssm_d, ssm_log_step,
                   w_glu, b_glu, g_attn_out, g_ssm_out, w_out, g_moe_norm,
                   w_group_router, b_group_router, w_expert_router, b_expert_router,
                   g_ple_norm, w_ple_gate, w_ple_proj):
    off_kv, off_kr, off_ssm = Q_LORA, Q_LORA + KV_LORA, Q_LORA + KV_LORA + QK_ROPE
    wi = w_in[i]
    kr_block = jnp.zeros((D_MODEL, HEAD_PAD), F32).at[:, ROPE_LANE0:ROPE_LANE0 + QK_ROPE].set(wi[:, off_kr:off_ssm])
    w_in_p = jnp.concatenate([wi[:, :off_kr], kr_block, wi[:, off_ssm:]], axis=1)
    wq = w_q_up[i].reshape(Q_LORA, N_HEADS, QK_NOPE + QK_ROPE) * (ATTN_SCALE * math.log2(math.e))
    wq = jnp.pad(wq, ((0, 0), (0, 0), (0, HEAD_PAD - QK_NOPE - QK_ROPE))).reshape(Q_LORA, N_HEADS * HEAD_PAD)
    wkv = w_kv_up[i].reshape(KV_LORA, N_HEADS, QK_NOPE + V_HEAD)
    wk = jnp.pad(wkv[..., :QK_NOPE], ((0, 0), (0, 0), (0, HEAD_PAD - QK_NOPE))).reshape(KV_LORA, N_HEADS * HEAD_PAD)
    wv = jnp.pad(wkv[..., QK_NOPE:], ((0, 0), (0, 0), (0, HEAD_PAD - V_HEAD))).reshape(KV_LORA, N_HEADS * HEAD_PAD)
    lam = lax.complex(ssm_a_re[i], ssm_a_im[i])
    lam_bar = jnp.exp(lam * jnp.exp(ssm_log_step[i])[:, None])
    b_fac = (lam_bar - 1.0) / lam
    bc = lax.complex(ssm_b_re[i], ssm_b_im[i]) * b_fac[:, :, None]
    n_per = SSM_GROUPS // S5_SLABS

    def block_diag(blocks):
        s, n, a, b = blocks.shape
        eye = jnp.eye(n, dtype=blocks.dtype)
        return jnp.einsum("snab,nm->snamb", blocks, eye).reshape(s, n * a, n * b)

    b_hp_re = jnp.real(bc).transpose(0, 2, 1).reshape(S5_SLABS, n_per, SSM_CH, SSM_STATE)
    b_hp_im = jnp.imag(bc).transpose(0, 2, 1).reshape(S5_SLABS, n_per, SSM_CH, SSM_STATE)
    s5_wb = jnp.concatenate([block_diag(b_hp_re), block_diag(b_hp_im)], axis=-1)
    c_re = ssm_c_re[i].transpose(0, 2, 1).reshape(S5_SLABS, n_per, SSM_STATE, SSM_CH)
    c_im = ssm_c_im[i].transpose(0, 2, 1).reshape(S5_SLABS, n_per, SSM_STATE, SSM_CH)
    s5_wc = jnp.stack([block_diag(c_re), -block_diag(c_im)])
    w_r = jnp.zeros((D_MODEL, LANES), F32)
    w_r = w_r.at[:, :N_EXPERT_GROUPS].set(w_group_router[i])
    w_r = w_r.at[:, ROUTE_E0:ROUTE_E0 + N_EXPERTS].set(w_expert_router[i])
    w_r_hi = w_r.astype(BF16)
    w_r_lo = (w_r - w_r_hi.astype(F32)).astype(BF16)
    b_r = jnp.zeros((1, LANES), F32)
    b_r = b_r.at[0, :N_EXPERT_GROUPS].set(b_group_router[i])
    b_r = b_r.at[0, ROUTE_E0:ROUTE_E0 + N_EXPERTS].set(b_expert_router[i])
    return dict(
        g_mix=g_mix_norm[i][None], w_in=w_in_p.astype(BF16),
        g_q=g_q_lat[i][None], w_q=wq.astype(BF16),
        g_kv=g_kv_lat[i][None], w_kv=jnp.concatenate([wk, wv], axis=1).astype(BF16),
        s5_wb=s5_wb.astype(BF16), s5_wc=s5_wc.astype(BF16),
        s5_are=jnp.real(lam_bar).reshape(1, N_STATE), s5_aim=jnp.imag(lam_bar).reshape(1, N_STATE),
        s5_d=ssm_d[i].reshape(1, SSM_WIDTH), w_glu=w_glu[i].astype(BF16), b_glu=b_glu[i][None],
        g_attn=g_attn_out[i][None], g_ssm=g_ssm_out[i][None],
        w_out=w_out[i].astype(BF16), g_moe=g_moe_norm[i][None],
        w_r_hi=w_r_hi, w_r_lo=w_r_lo, b_r=b_r,
        g_ple=g_ple_norm[i][None], w_ple_gate=w_ple_gate[i].astype(BF16),
        w_ple_proj=w_ple_proj[i].astype(BF16),
    )


def _slot_layout(route, cnt, n_blocks, te):
    counts = cnt[0, :N_EXPERTS].astype(I32)
    pcounts = (counts + te - 1) // te * te
    pends = jnp.cumsum(pcounts)
    pstarts = pends - pcounts
    eid = route[:TOP_K]
    rank = route[TOP_K:2 * TOP_K]
    start = jnp.sum(jnp.where(eid[:, :, None] == jnp.arange(N_EXPERTS)[None, None, :],
                              pstarts[None, None, :], 0), axis=-1)
    dest_k = (start + rank).astype(I32)
    blk0 = jnp.arange(n_blocks, dtype=I32) * te
    block_e = jnp.minimum(jnp.sum((pends[None, :] <= blk0[:, None]).astype(I32), axis=1), N_EXPERTS - 1)
    n_slots = n_blocks * te
    gap_start = jnp.concatenate([pstarts + counts, pends[-1:]])
    gap_len = jnp.concatenate([pcounts - counts, n_slots - pends[-1:]])
    gap_end = jnp.cumsum(gap_len)
    j = jnp.arange(n_slots - eid.size, dtype=I32)
    gap = jnp.sum((gap_end[None, :] <= j[:, None]).astype(I32), axis=1)
    onehot = gap[:, None] == jnp.arange(N_EXPERTS + 1)[None, :]
    pad_rows = j + jnp.sum(jnp.where(onehot, (gap_start - (gap_end - gap_len))[None, :], 0), axis=1)
    n_used = (pends[-1:] // te).astype(I32)
    return dest_k, block_e, n_used, pad_rows.astype(I32)


def kernel(x, p, positions, g_mix_norm, w_in, g_q_lat, w_q_up, g_kv_lat, w_kv_up, ssm_a_re, ssm_a_im, ssm_b_re, ssm_b_im, ssm_c_re, ssm_c_im, ssm_d, ssm_log_step, w_glu, b_glu, g_attn_out, g_ssm_out, w_out, g_moe_norm, w_group_router, b_group_router, w_expert_router, b_expert_router, w_exp_gate, w_exp_up, w_exp_down, g_ple_norm, w_ple_gate, w_ple_proj, g_final):
    bsz, seq, _ = x.shape
    depth = p.shape[0]
    n_tok = bsz * seq
    t = _tiles(n_tok, seq)
    n_slots = TOP_K * n_tok + N_EXPERTS * t["te"]
    n_blocks = n_slots // t["te"]

    cosf, sinf = _rope_tables(positions.reshape(n_tok, 1), t["tm"])
    sub = min(SUB_ROWS, t["tm"])
    tril = jnp.tril(jnp.ones((sub, sub), F32), k=-1).astype(BF16)
    g_fin = g_final[None]

    h = x.reshape(n_tok, D_MODEL)
    for i in range(depth):
        wts = _layer_weights(i, g_mix_norm, w_in, g_q_lat, w_q_up, g_kv_lat, w_kv_up,
                             ssm_a_re, ssm_a_im, ssm_b_re, ssm_b_im, ssm_c_re, ssm_c_im, ssm_d,
                             ssm_log_step, w_glu, b_glu, g_attn_out, g_ssm_out, w_out, g_moe_norm,
                             w_group_router, b_group_router, w_expert_router, b_expert_router,
                             g_ple_norm, w_ple_gate, w_ple_proj)
        q, k, v, u_tm = _inproj(h, cosf, sinf, wts, bsz, seq, t["tm"])
        attn_n = _attention(q, k, v, wts["g_attn"], bsz, seq, t["tq"])
        ssm_n = _s5(u_tm.reshape(seq, bsz, SSM_WIDTH), wts, attn_n, bsz, seq, t["lc"])
        h1, xn, route, rw, cnt = _outproj(h, attn_n, ssm_n.reshape(seq, bsz * SSM_WIDTH), wts, tril,
                                          bsz, seq, t["tm"])
        dest_k, block_e, n_used, pad_rows = _slot_layout(route, cnt, n_blocks, t["te"])
        xbuf = _dispatch(dest_k, pad_rows, xn, n_slots)
        ybuf = _experts(block_e, n_used, xbuf, w_exp_gate, w_exp_up, w_exp_down, i, t["te"])
        h = _combine(dest_k, ybuf, h1, rw, p.reshape(depth, n_tok, PLE_DIM), i, wts, g_fin, t["tf"],
                     final=(i == depth - 1))
    return h.reshape(bsz, seq, D_MODEL)
```

```python
import functools
import math

import jax
import jax.numpy as jnp
from jax import lax
from jax.experimental import pallas as pl
from jax.experimental.pallas import tpu as pltpu
from jax.experimental.pallas import tpu_sc as plsc

F32 = jnp.float32
BF16 = jnp.bfloat16
I32 = jnp.int32
U32 = jnp.uint32

D_MODEL = 1024
MLA_WIDTH = 512
SSM_WIDTH = 512
V_HEAD = 64
N_HEADS = 8
QK_NOPE = 64
QK_ROPE = 32
HALF_ROPE = QK_ROPE // 2
Q_LORA = 256
KV_LORA = 128
ROPE_BASE = 10000.0
ATTN_SCALE = 1.0 / math.sqrt(QK_NOPE + QK_ROPE)
SSM_CH = 16
SSM_GROUPS = 32
SSM_STATE = 64
N_EXPERT_GROUPS = 4
EXPERTS_PER_GROUP = 8
N_EXPERTS = 32
TOP_K = 2
D_EXPERT = 256
PLE_DIM = 256
EPS = 1e-6

LANES = 128
SUBLANES = 8
HEAD_PAD = 128
ROPE_LANE0 = QK_NOPE
N_STATE = SSM_GROUPS * SSM_STATE
PACKED = D_MODEL // 2
NEG = -1e30
VMEM_LIMIT = 56 * 1024 * 1024
SUB_ROWS = 512
INPROJ_ROWS = 256
COMBINE_ROWS = 256
EXPERT_ROWS = 256
S5_ROWS = 256


def _tiles(n_tok, seq):
    return dict(
        tm=min(1024, seq),
        tq=min(512, seq),
        lc=min(64, seq),
        te=512,
        tf=min(1024, n_tok),
    )


def _rms(x, g):
    ms = jnp.mean(x * x, axis=-1, keepdims=True)
    return x * lax.rsqrt(ms + EPS) * g


def _sigmoid(x):
    return 1.0 / (1.0 + jnp.exp(-x))


def _pack_bf16_pairs(x):
    lo = lax.bitcast_convert_type(x[:, :PACKED].astype(BF16).astype(F32), U32)
    hi = lax.bitcast_convert_type(x[:, PACKED:].astype(BF16).astype(F32), U32)
    return (lo >> 16) | hi


def _unpack_bf16_pairs(w):
    lo = lax.bitcast_convert_type(w << 16, F32)
    hi = lax.bitcast_convert_type(w & jnp.uint32(0xFFFF0000), F32)
    return jnp.concatenate([lo, hi], axis=1)


def _cparams(sem):
    return pltpu.CompilerParams(dimension_semantics=sem, vmem_limit_bytes=VMEM_LIMIT)


def _full(shape):
    nd = len(shape)
    return pl.BlockSpec(shape, lambda *_: (0,) * nd)


def _rope_table_kernel(pos_ref, invf_ref, cos_ref, sin_ref):
    ang = pos_ref[...].astype(F32) * invf_ref[...]
    cos_ref[...] = jnp.cos(ang)
    sin_ref[...] = jnp.sin(ang)


def _rope_tables(pos_col, tm):
    n_tok = pos_col.shape[0]
    lane = jnp.arange(LANES)
    in_rope = (lane >= ROPE_LANE0) & (lane < ROPE_LANE0 + QK_ROPE)
    freq = ROPE_BASE ** (-((lane - ROPE_LANE0) % HALF_ROPE).astype(F32) / HALF_ROPE)
    invf = jnp.where(in_rope, freq, 0.0).astype(F32)[None, :]
    return pl.pallas_call(
        _rope_table_kernel,
        out_shape=(jax.ShapeDtypeStruct((n_tok, LANES), F32),) * 2,
        grid=(n_tok // tm,),
        in_specs=[pl.BlockSpec((tm, 1), lambda i: (i, 0)), _full((1, LANES))],
        out_specs=(pl.BlockSpec((tm, LANES), lambda i: (i, 0)),) * 2,
        compiler_params=_cparams(("parallel",)),
        name="rope_tables",
    )(pos_col, invf)


def _rope(x, cosf, s_lo, s_hi):
    n = x.shape[-1]
    return (x * cosf + pltpu.roll(x, n - HALF_ROPE, 1) * s_lo + pltpu.roll(x, HALF_ROPE, 1) * s_hi)


def _row_chunks(n_rows, sub):
    sub = min(sub, n_rows)
    return [slice(c * sub, (c + 1) * sub) for c in range(n_rows // sub)]


def _inproj_kernel(h_ref, cos_ref, sin_ref, gmix_ref, win_ref, gq_ref, wqt_ref, gkv_ref, wk_ref, wvt_ref,
                   qt_ref, k_ref, vt_ref, u_ref):
    lane = lax.broadcasted_iota(I32, (1, LANES), 1)
    feat = lax.broadcasted_iota(I32, (N_HEADS * HEAD_PAD, 1), 0)
    ones_row = jnp.where(feat % HEAD_PAD == V_HEAD, 1.0, 0.0)
    tile = qt_ref.shape[-1]
    chunks = _row_chunks(h_ref.shape[0], INPROJ_ROWS)
    zs = []
    for r in chunks:
        xn = _rms(h_ref[r, :], gmix_ref[...]).astype(BF16)
        zs.append(jnp.dot(xn, win_ref[...], preferred_element_type=F32))
    qts, ks, vts = [], [], []
    for z in zs:
        qn_t = jnp.transpose(_rms(z[:, :Q_LORA], gq_ref[...])).astype(BF16)
        qts.append(jnp.dot(wqt_ref[...], qn_t, preferred_element_type=F32))
        kvn = _rms(z[:, Q_LORA:Q_LORA + KV_LORA], gkv_ref[...])
        ks.append(jnp.dot(kvn.astype(BF16), wk_ref[...], preferred_element_type=F32))
        vts.append(jnp.dot(wvt_ref[...], jnp.transpose(kvn).astype(BF16),
                           preferred_element_type=F32))
    for r, z, qt, k, vt in zip(chunks, zs, qts, ks, vts):
        cosf = cos_ref[r, :]
        sinf = sin_ref[r, :]
        s_lo = jnp.where((lane >= ROPE_LANE0) & (lane < ROPE_LANE0 + HALF_ROPE), -sinf, 0.0)
        s_hi = jnp.where((lane >= ROPE_LANE0 + HALF_ROPE) & (lane < ROPE_LANE0 + QK_ROPE), sinf, 0.0)
        kpe = _rope(z[:, Q_LORA + KV_LORA:Q_LORA + KV_LORA + HEAD_PAD], cosf, s_lo, s_hi)
        cos_t = jnp.transpose(cosf)[ROPE_LANE0:ROPE_LANE0 + HALF_ROPE, :]
        sin_t = jnp.transpose(sinf)[ROPE_LANE0:ROPE_LANE0 + HALF_ROPE, :]
        cols = slice(r.start % tile, r.start % tile + (r.stop - r.start))
        for hd in range(N_HEADS):
            sl = slice(hd * HEAD_PAD, (hd + 1) * HEAD_PAD)
            k_ref[r, sl] = (k[:, sl] + kpe).astype(BF16)
            x1 = qt[hd * HEAD_PAD + ROPE_LANE0:hd * HEAD_PAD + ROPE_LANE0 + HALF_ROPE, :]
            x2 = qt[hd * HEAD_PAD + ROPE_LANE0 + HALF_ROPE:hd * HEAD_PAD + ROPE_LANE0 + QK_ROPE, :]
            qt_ref[r.start // tile, hd * HEAD_PAD:hd * HEAD_PAD + ROPE_LANE0, cols] = (
                qt[hd * HEAD_PAD:hd * HEAD_PAD + ROPE_LANE0, :].astype(BF16))
            qt_ref[r.start // tile, hd * HEAD_PAD + ROPE_LANE0:hd * HEAD_PAD + ROPE_LANE0 + QK_ROPE, cols] = (
                jnp.concatenate([x1 * cos_t - x2 * sin_t, x1 * sin_t + x2 * cos_t], axis=0).astype(BF16))
            qt_ref[r.start // tile, hd * HEAD_PAD + ROPE_LANE0 + QK_ROPE:(hd + 1) * HEAD_PAD, cols] = (
                qt[hd * HEAD_PAD + ROPE_LANE0 + QK_ROPE:(hd + 1) * HEAD_PAD, :].astype(BF16))
        vt_ref[r.start // tile, :, cols] = (vt + ones_row).astype(BF16)
        u_ref[r, :] = z[:, Q_LORA + KV_LORA + HEAD_PAD:]


def _inproj(h, cosf, sinf, wts, bsz, seq, tm, tq):
    n_tok = bsz * seq
    nl = seq // tm
    row = lambda b, l: (b * nl + l, 0)
    tile3 = lambda b, l: (b * nl + l, 0, 0)
    qk_w = N_HEADS * HEAD_PAD
    return pl.pallas_call(
        _inproj_kernel,
        out_shape=(jax.ShapeDtypeStruct((n_tok // tq, qk_w, tq), BF16),
                   jax.ShapeDtypeStruct((n_tok, qk_w), BF16),
                   jax.ShapeDtypeStruct((n_tok // tq, qk_w, tq), BF16),
                   jax.ShapeDtypeStruct((seq, bsz * SSM_WIDTH), F32)),
        grid=(bsz, nl),
        in_specs=[pl.BlockSpec((tm, D_MODEL), row),
                  pl.BlockSpec((tm, LANES), row), pl.BlockSpec((tm, LANES), row),
                  _full((1, D_MODEL)), _full((D_MODEL, D_MODEL)),
                  _full((1, Q_LORA)), _full((qk_w, Q_LORA)),
                  _full((1, KV_LORA)), _full((KV_LORA, qk_w)), _full((qk_w, KV_LORA))],
        out_specs=(pl.BlockSpec((tm // tq, qk_w, tq), tile3), pl.BlockSpec((tm, qk_w), row),
                   pl.BlockSpec((tm // tq, qk_w, tq), tile3),
                   pl.BlockSpec((tm, SSM_WIDTH), lambda b, l: (l, b))),
        compiler_params=_cparams(("parallel", "parallel")),
        name="inproj",
    )(h, cosf, sinf, wts["g_mix"], wts["w_in"], wts["g_q"], wts["w_q_t"], wts["g_kv"], wts["w_k"],
      wts["w_v_t"])


ATTN_RB = 32
ATTN_GROUP = 4


def _attn_kernel(qt_ref, k_ref, vt_ref, g_ref, o_ref, s_ref, p_ref, m_ref, acc_ref, out_ref, *, tq):
    qi = pl.program_id(1)
    key = lax.broadcasted_iota(I32, (ATTN_RB, tq), 0)
    qry = lax.broadcasted_iota(I32, (ATTN_RB, tq), 1)
    n_chunks = tq // ATTN_RB
    for pr in range(N_HEADS // ATTN_GROUP):
        heads = [ATTN_GROUP * pr + a for a in range(ATTN_GROUP)]
        m_ref[...] = jnp.full(m_ref.shape, NEG, F32)
        acc_ref[...] = jnp.zeros(acc_ref.shape, F32)

        def scores(j, slot, heads=heads):
            rows = pl.ds(pl.multiple_of(j * tq, tq), tq)
            for a, hd in enumerate(heads):
                hs = slice(hd * HEAD_PAD, (hd + 1) * HEAD_PAD)
                s_ref[slot, a] = jnp.dot(k_ref[rows, hs], qt_ref[hs, :], preferred_element_type=F32)

        def softmax_pv(j, slot, masked, heads=heads):
            alphas = []
            for a in range(ATTN_GROUP):
                def chunk(c, slot=slot, a=a):
                    s = s_ref[slot, a, c * ATTN_RB:(c + 1) * ATTN_RB, :]
                    if masked:
                        s = jnp.where(key + c * ATTN_RB <= qry, s, NEG)
                    return s

                top = chunk(0)
                for c in range(1, n_chunks):
                    top = jnp.maximum(top, chunk(c))
                m_old = m_ref[a]
                m_new = jnp.maximum(m_old, jnp.max(top, axis=0, keepdims=True))
                for c in range(n_chunks):
                    p_ref[a, c * ATTN_RB:(c + 1) * ATTN_RB, :] = jnp.exp2(chunk(c) - m_new).astype(BF16)
                alphas.append(jnp.exp2(m_old - m_new))
                m_ref[a] = m_new
            for a, hd in enumerate(heads):
                pv = jnp.dot(vt_ref[j, hd * HEAD_PAD:(hd + 1) * HEAD_PAD, :], p_ref[a],
                             preferred_element_type=F32)
                acc_ref[a] = alphas[a] * acc_ref[a] + pv

        def two_tiles(jj, carry):
            t = 2 * jj
            scores(t + 1, 1)
            softmax_pv(t, 0, masked=False)
            scores(t + 2, 0)
            softmax_pv(t + 1, 1, masked=False)
            return carry

        scores(0, 0)
        lax.fori_loop(0, qi // 2, two_tiles, 0)

        @pl.when(qi % 2 == 0)
        def _():
            softmax_pv(qi, 0, masked=True)

        @pl.when(qi % 2 == 1)
        def _():
            scores(qi, 1)
            softmax_pv(qi - 1, 0, masked=False)
            softmax_pv(qi, 1, masked=True)

        for a, hd in enumerate(heads):
            acc = acc_ref[a]
            out_ref[hd * V_HEAD:(hd + 1) * V_HEAD, :] = acc[:V_HEAD, :] / acc[V_HEAD:V_HEAD + 1, :]
    o_ref[...] = _rms(jnp.transpose(out_ref[...]), g_ref[...]).astype(BF16)


def _attention(qt, k, vt, g_attn, bsz, seq, tq):
    n_tok = bsz * seq
    nq = seq // tq
    qk_w = N_HEADS * HEAD_PAD
    return pl.pallas_call(
        functools.partial(_attn_kernel, tq=tq),
        out_shape=jax.ShapeDtypeStruct((n_tok, MLA_WIDTH), BF16),
        grid=(bsz, nq),
        in_specs=[pl.BlockSpec((None, qk_w, tq), lambda b, i: (b * nq + i, 0, 0)),
                  pl.BlockSpec((seq, qk_w), lambda b, i: (b, 0)),
                  pl.BlockSpec((nq, qk_w, tq), lambda b, i: (b, 0, 0)),
                  _full((1, MLA_WIDTH))],
        out_specs=pl.BlockSpec((tq, MLA_WIDTH), lambda b, i: (b * nq + i, 0)),
        scratch_shapes=[pltpu.VMEM((2, ATTN_GROUP, tq, tq), F32), pltpu.VMEM((ATTN_GROUP, tq, tq), BF16),
                        pltpu.VMEM((ATTN_GROUP, 1, tq), F32),
                        pltpu.VMEM((ATTN_GROUP, HEAD_PAD, tq), F32), pltpu.VMEM((MLA_WIDTH, tq), F32)],
        compiler_params=_cparams(("parallel", "parallel")),
        name="attn",
    )(qt, k, vt, g_attn)


S5_SLABS = 4
S5_SLAB_U = SSM_WIDTH // S5_SLABS
S5_SLAB_X = N_STATE // S5_SLABS


def _s5_kernel(u_ref, wb_ref, are_ref, aim_ref, wc_ref, d_ref, wglu_ref, bglu_ref, g_ref, after_ref,
               o_ref, xr_ref, xi_ref, sr_ref, si_ref, *, lc, nb):
    del after_ref

    @pl.when(pl.program_id(0) == 0)
    def _():
        xr_ref[...] = jnp.zeros_like(xr_ref)
        xi_ref[...] = jnp.zeros_like(xi_ref)

    rows = lc * nb
    u = u_ref[...].reshape(rows, SSM_WIDTH)
    ub = u.astype(BF16)
    def drive(j):
        bu = jnp.dot(ub[:, j * S5_SLAB_U:(j + 1) * S5_SLAB_U], wb_ref[j], preferred_element_type=F32)
        sr_ref[:, j * S5_SLAB_X:(j + 1) * S5_SLAB_X] = bu[:, :S5_SLAB_X]
        si_ref[:, j * S5_SLAB_X:(j + 1) * S5_SLAB_X] = bu[:, S5_SLAB_X:]

    ys = []
    drive(0)
    for c in range(S5_SLABS):
        if c + 1 < S5_SLABS:
            drive(c + 1)
        sl = slice(c * S5_SLAB_X, (c + 1) * S5_SLAB_X)
        ar = jnp.broadcast_to(are_ref[:, sl], (nb, S5_SLAB_X))
        ai = jnp.broadcast_to(aim_ref[:, sl], (nb, S5_SLAB_X))
        xr = xr_ref[:, sl]
        xi = xi_ref[:, sl]
        for t in range(lc):
            r = slice(t * nb, (t + 1) * nb)
            xr, xi = (ar * xr - ai * xi + sr_ref[r, sl], ar * xi + ai * xr + si_ref[r, sl])
            sr_ref[r, sl] = xr
            si_ref[r, sl] = xi
        xr_ref[:, sl] = xr
        xi_ref[:, sl] = xi
        ys.append(jnp.dot(sr_ref[:, sl].astype(BF16), wc_ref[0, c], preferred_element_type=F32)
                  + jnp.dot(si_ref[:, sl].astype(BF16), wc_ref[1, c], preferred_element_type=F32))
    y_all = jnp.concatenate(ys, axis=-1) + d_ref[...] * u
    chunks = _row_chunks(rows, S5_ROWS)
    ygs, gates = [], []
    for r in chunks:
        y = y_all[r, :]
        ygs.append(0.5 * y * (1.0 + jnp.tanh(math.sqrt(2.0 / math.pi) * (y + 0.044715 * (y * y * y)))))
    for yg in ygs:
        gates.append(jnp.dot(yg.astype(BF16), wglu_ref[...], preferred_element_type=F32) + bglu_ref[...])
    for r, yg, gate in zip(chunks, ygs, gates):
        out = _rms(yg * _sigmoid(gate), g_ref[...]).astype(BF16)
        o_ref[r.start // nb:r.stop // nb] = out.reshape((r.stop - r.start) // nb, nb, SSM_WIDTH)


def _s5(u_tm, wts, after, bsz, seq, lc):
    rows = lc * bsz
    return pl.pallas_call(
        functools.partial(_s5_kernel, lc=lc, nb=bsz),
        out_shape=jax.ShapeDtypeStruct((seq, bsz, SSM_WIDTH), BF16),
        grid=(seq // lc,),
        in_specs=[pl.BlockSpec((lc, bsz, SSM_WIDTH), lambda i: (i, 0, 0)),
                  _full((S5_SLABS, S5_SLAB_U, 2 * S5_SLAB_X)),
                  _full((1, N_STATE)), _full((1, N_STATE)),
                  _full((2, S5_SLABS, S5_SLAB_X, S5_SLAB_U)),
                  _full((1, SSM_WIDTH)), _full((SSM_WIDTH, SSM_WIDTH)), _full((1, SSM_WIDTH)),
                  _full((1, SSM_WIDTH)), pl.BlockSpec(memory_space=pl.ANY)],
        out_specs=pl.BlockSpec((lc, bsz, SSM_WIDTH), lambda i: (i, 0, 0)),
        scratch_shapes=[pltpu.VMEM((bsz, N_STATE), F32), pltpu.VMEM((bsz, N_STATE), F32),
                        pltpu.VMEM((rows, N_STATE), F32), pltpu.VMEM((rows, N_STATE), F32)],
        compiler_params=_cparams(("arbitrary",)),
        name="s5",
    )(u_tm, wts["s5_wb"], wts["s5_are"], wts["s5_aim"], wts["s5_wc"], wts["s5_d"],
      wts["w_glu"], wts["b_glu"], wts["g_ssm"], after)


ROUTE_E0 = N_EXPERT_GROUPS


def _outproj_kernel(h_ref, a_ref, s_ref, wout_ref, gmoe_ref, wrh_ref, wrl_ref, br_ref, tril_ref,
                    h1_ref, xn_ref, route_ref, rw_ref, cnt_ref):
    @pl.when((pl.program_id(0) == 0) & (pl.program_id(1) == 0))
    def _():
        cnt_ref[...] = jnp.zeros_like(cnt_ref)

    sub = tril_ref.shape[0]
    lane = lax.broadcasted_iota(I32, (sub, LANES), 1).astype(F32)
    big = float(LANES)
    chunks = _row_chunks(h_ref.shape[0], sub)
    xns = []
    for r in chunks:
        h1 = (h_ref[r, :]
              + jnp.dot(a_ref[r, :], wout_ref[:MLA_WIDTH, :], preferred_element_type=F32)
              + jnp.dot(s_ref[r, :], wout_ref[MLA_WIDTH:, :], preferred_element_type=F32))
        h1_ref[r, :] = h1
        xns.append(_rms(h1, gmoe_ref[...]))
    lgs = []
    for r, xn in zip(chunks, xns):
        xn_ref[r, :] = _pack_bf16_pairs(xn)
        x_hi = xn.astype(BF16)
        x_lo = (xn - x_hi.astype(F32)).astype(BF16)
        lgs.append((jnp.dot(x_hi, wrh_ref[...], preferred_element_type=F32)
                    + (jnp.dot(x_lo, wrh_ref[...], preferred_element_type=F32)
                       + jnp.dot(x_hi, wrl_ref[...], preferred_element_type=F32))) + br_ref[...])
    for r, lg in zip(chunks, lgs):
        gl = jnp.where(lane < N_EXPERT_GROUPS, lg, NEG)
        gmax = jnp.max(gl, axis=-1, keepdims=True)
        gsum = jnp.sum(jnp.exp(gl - gmax), axis=-1, keepdims=True)
        g_top_p = 1.0 / gsum
        gidx = jnp.min(jnp.where(gl == gmax, lane, big), axis=-1, keepdims=True)
        lo = ROUTE_E0 + EXPERTS_PER_GROUP * gidx
        sel = jnp.where((lane >= lo) & (lane < lo + EXPERTS_PER_GROUP), lg, NEG)
        m1 = jnp.max(sel, axis=-1, keepdims=True)
        i1 = jnp.min(jnp.where(sel == m1, lane, big), axis=-1, keepdims=True)
        sel2 = jnp.where(lane == i1, NEG, sel)
        m2 = jnp.max(sel2, axis=-1, keepdims=True)
        i2 = jnp.min(jnp.where(sel2 == m2, lane, big), axis=-1, keepdims=True)
        e21 = jnp.exp(m2 - m1)
        w1 = g_top_p / (1.0 + e21)
        w2 = g_top_p * e21 / (1.0 + e21)
        e1 = i1 - ROUTE_E0
        e2 = i2 - ROUTE_E0
        oh = jnp.where((lane == e1) | (lane == e2), 1.0, 0.0)
        before = cnt_ref[...] + jnp.dot(tril_ref[...], oh.astype(BF16), preferred_element_type=F32)
        r1 = jnp.sum(jnp.where(lane == e1, before, 0.0), axis=-1, keepdims=True)
        r2 = jnp.sum(jnp.where(lane == e2, before, 0.0), axis=-1, keepdims=True)
        cnt_ref[...] = cnt_ref[...] + jnp.sum(oh, axis=0, keepdims=True)
        route = jnp.where(lane == 0, e1, jnp.where(lane == 1, e2, jnp.where(lane == 2, r1, r2)))
        route_ref[:, r] = jnp.transpose(route)[:SUBLANES, :].astype(I32)
        rw_ref[r, :] = jnp.where(lane == 0, w1, w2)


def _outproj(h, attn_n, ssm_tm, wts, tril, bsz, seq, tm):
    n_tok = bsz * seq
    nl = seq // tm
    row = lambda b, l: (b * nl + l, 0)
    return pl.pallas_call(
        _outproj_kernel,
        out_shape=(jax.ShapeDtypeStruct((n_tok, D_MODEL), F32),
                   jax.ShapeDtypeStruct((n_tok, PACKED), U32),
                   jax.ShapeDtypeStruct((SUBLANES, n_tok), I32),
                   jax.ShapeDtypeStruct((n_tok, LANES), F32),
                   jax.ShapeDtypeStruct((1, LANES), F32)),
        grid=(bsz, nl),
        in_specs=[pl.BlockSpec((tm, D_MODEL), row),
                  pl.BlockSpec((tm, MLA_WIDTH), row),
                  pl.BlockSpec((tm, SSM_WIDTH), lambda b, l: (l, b)),
                  _full((D_MODEL, D_MODEL)), _full((1, D_MODEL)),
                  _full((D_MODEL, LANES)), _full((D_MODEL, LANES)), _full((1, LANES)),
                  _full(tril.shape)],
        out_specs=(pl.BlockSpec((tm, D_MODEL), row), pl.BlockSpec((tm, PACKED), row),
                   pl.BlockSpec((SUBLANES, tm), lambda b, l: (0, b * nl + l)),
                   pl.BlockSpec((tm, LANES), row),
                   _full((1, LANES))),
        compiler_params=_cparams(("arbitrary", "arbitrary")),
        name="outproj",
    )(h, attn_n, ssm_tm, wts["w_out"], wts["g_moe"], wts["w_r_hi"], wts["w_r_lo"], wts["b_r"], tril)


SC_CORES = 2
SC_SUBCORES = 16
SC_WORKERS = SC_CORES * SC_SUBCORES
SC_CHUNK = 64


def _sc_mesh():
    return plsc.VectorSubcoreMesh(core_axis_name="c", subcore_axis_name="s",
                                  num_cores=SC_CORES, num_subcores=SC_SUBCORES)


def _sc_worker_id():
    return lax.axis_index("s") * SC_CORES + lax.axis_index("c")


def _sc_two_slot_pipeline(n_chunks, load, stores):
    assert n_chunks % 2 == 0
    load(0, 0).start()

    @pl.loop(0, n_chunks // 2)
    def _(jj):
        for slot in (0, 1):
            j = 2 * jj + slot

            @pl.when(j >= 1)
            def _():
                for cp in stores(j - 1, 1 - slot):
                    cp.wait()

            @pl.when(j + 1 < n_chunks)
            def _():
                load(j + 1, 1 - slot).start()

            load(j, slot).wait()
            for cp in stores(j, slot):
                cp.start()

    for cp in stores(n_chunks - 1, 1):
        cp.wait()


def _sc_rows(j):
    return pl.ds(pl.multiple_of(j * SC_CHUNK, SC_CHUNK), SC_CHUNK)


def _dispatch(dest_k, pad_rows, xn, n_slots):
    n_tok, width = xn.shape
    n_pad = pad_rows.shape[0]
    per_w = n_tok // SC_WORKERS
    per_p = n_pad // SC_WORKERS
    assert per_p <= per_w

    @functools.partial(
        pl.kernel, mesh=_sc_mesh(),
        out_type=jax.ShapeDtypeStruct((n_slots, width), xn.dtype),
        scratch_types=[pltpu.VMEM((per_w,), I32), pltpu.VMEM((per_w,), I32), pltpu.VMEM((per_p,), I32),
                       pltpu.VMEM((2, SC_CHUNK, width), xn.dtype),
                       pltpu.SemaphoreType.DMA((2,)), pltpu.SemaphoreType.DMA((2, TOP_K))],
        name="sc_dispatch_rows",
    )
    def scatter_kernel(xn_hbm, d0_hbm, d1_hbm, dp_hbm, out_hbm, i0_v, i1_v, ip_v, rows_v, lsem, ssem):
        wid = _sc_worker_id()
        base = wid * per_w
        pltpu.sync_copy(d0_hbm.at[pl.ds(base, per_w)], i0_v)
        pltpu.sync_copy(d1_hbm.at[pl.ds(base, per_w)], i1_v)
        pltpu.sync_copy(dp_hbm.at[pl.ds(wid * per_p, per_p)], ip_v)

        def load(j, slot):
            return pltpu.make_async_copy(xn_hbm.at[pl.ds(base + j * SC_CHUNK, SC_CHUNK)],
                                         rows_v.at[slot], lsem.at[slot])

        def scatter(idx_v, k):
            return lambda j, slot: pltpu.make_async_copy(
                rows_v.at[slot], out_hbm.at[idx_v.at[_sc_rows(j)]], ssem.at[slot, k])

        to_k0, to_k1, to_pad = scatter(i0_v, 0), scatter(i1_v, 1), scatter(ip_v, 0)
        _sc_two_slot_pipeline(per_w // SC_CHUNK, load,
                              lambda j, slot: [to_k0(j, slot), to_k1(j, slot)])
        _sc_two_slot_pipeline(per_p // SC_CHUNK, load, lambda j, slot: [to_pad(j, slot)])

    return scatter_kernel(xn, dest_k[0], dest_k[1], pad_rows)


def _experts_kernel(be_ref, nused_ref, x_ref, wg_ref, wu_ref, wd_ref, y_ref):
    del be_ref

    @pl.when(pl.program_id(0) >= nused_ref[0])
    def _():
        y_ref[...] = jnp.zeros_like(y_ref)

    @pl.when(pl.program_id(0) < nused_ref[0])
    def _():
        wg = wg_ref[...].astype(BF16)
        wu = wu_ref[...].astype(BF16)
        wd = wd_ref[...].astype(BF16)
        chunks = _row_chunks(x_ref.shape[0], EXPERT_ROWS)
        gus = []
        for r in chunks:
            xb = _unpack_bf16_pairs(x_ref[r, :]).astype(BF16)
            gus.append((jnp.dot(xb, wg, preferred_element_type=F32),
                        jnp.dot(xb, wu, preferred_element_type=F32)))
        ys = []
        for g, u in gus:
            hdn = (g * _sigmoid(g) * u).astype(BF16)
            ys.append(jnp.dot(hdn, wd, preferred_element_type=F32))
        for r, y in zip(chunks, ys):
            y_ref[r, :] = _pack_bf16_pairs(y)


def _experts(block_e, n_used, xbuf, w_gate, w_up, w_down, layer, te):
    n_slots = xbuf.shape[0]
    grid_spec = pltpu.PrefetchScalarGridSpec(
        num_scalar_prefetch=2,
        grid=(n_slots // te,),
        in_specs=[pl.BlockSpec((te, PACKED), lambda i, be, nu: (i, 0)),
                  pl.BlockSpec((None, None, D_MODEL, D_EXPERT), lambda i, be, nu: (layer, be[i], 0, 0)),
                  pl.BlockSpec((None, None, D_MODEL, D_EXPERT), lambda i, be, nu: (layer, be[i], 0, 0)),
                  pl.BlockSpec((None, None, D_EXPERT, D_MODEL), lambda i, be, nu: (layer, be[i], 0, 0))],
        out_specs=pl.BlockSpec((te, PACKED), lambda i, be, nu: (i, 0)),
    )
    return pl.pallas_call(
        _experts_kernel,
        out_shape=jax.ShapeDtypeStruct((n_slots, PACKED), U32),
        grid_spec=grid_spec,
        compiler_params=_cparams(("arbitrary",)),
        name="experts",
    )(block_e, n_used, xbuf, w_gate, w_up, w_down)


def _sc_gather_rows(table, idx):
    n_rows = idx.shape[0]
    width = table.shape[1]
    per_w = n_rows // SC_WORKERS

    @functools.partial(
        pl.kernel, mesh=_sc_mesh(),
        out_type=jax.ShapeDtypeStruct((n_rows, width), table.dtype),
        scratch_types=[pltpu.VMEM((per_w,), I32), pltpu.VMEM((2, SC_CHUNK, width), table.dtype),
                       pltpu.SemaphoreType.DMA((2,)), pltpu.SemaphoreType.DMA((2,))],
        name="sc_gather_rows",
    )
    def gather_kernel(table_hbm, idx_hbm, out_hbm, idx_v, rows_v, lsem, ssem):
        base = _sc_worker_id() * per_w
        pltpu.sync_copy(idx_hbm.at[pl.ds(base, per_w)], idx_v)

        def load(j, slot):
            return pltpu.make_async_copy(table_hbm.at[idx_v.at[_sc_rows(j)]], rows_v.at[slot],
                                         lsem.at[slot])

        def store(j, slot):
            return [pltpu.make_async_copy(rows_v.at[slot],
                                          out_hbm.at[pl.ds(base + j * SC_CHUNK, SC_CHUNK)],
                                          ssem.at[slot])]

        _sc_two_slot_pipeline(per_w // SC_CHUNK, load, store)

    return gather_kernel(table, idx)


def _combine_kernel(y0_ref, y1_ref, h1_ref, rw_ref, p_ref, gple_ref, wgate_ref, wproj_ref,
                    gfin_ref, o_ref, *, final):
    chunks = _row_chunks(h1_ref.shape[0], COMBINE_ROWS)
    h2s, gates, ples = [], [], []
    for r in chunks:
        rw = rw_ref[r, :]
        h2s.append(h1_ref[r, :] + rw[:, 0:1] * _unpack_bf16_pairs(y0_ref[r, :])
                   + rw[:, 1:2] * _unpack_bf16_pairs(y1_ref[r, :]))
    for r, h2 in zip(chunks, h2s):
        hn = _rms(h2, gple_ref[...]).astype(BF16)
        gates.append(jnp.dot(hn, wgate_ref[...], preferred_element_type=F32))
        ples.append(jnp.dot(p_ref[r, :].astype(BF16), wproj_ref[...], preferred_element_type=F32))
    for r, h2, gate, ple in zip(chunks, h2s, gates, ples):
        h3 = h2 + ple * _sigmoid(gate)
        if final:
            h3 = _rms(h3, gfin_ref[...])
        o_ref[r, :] = h3


def _combine(dest_k, ybuf, h1, rw, p_all, layer, wts, g_final, tf, final):
    n_tok = h1.shape[0]
    n_steps = n_tok // tf
    row = lambda i: (i, 0)
    yg = _sc_gather_rows(ybuf, dest_k.reshape(-1))
    return pl.pallas_call(
        functools.partial(_combine_kernel, final=final),
        out_shape=jax.ShapeDtypeStruct((n_tok, D_MODEL), F32),
        grid=(n_steps,),
        in_specs=[pl.BlockSpec((tf, PACKED), row),
                  pl.BlockSpec((tf, PACKED), lambda i: (n_steps + i, 0)),
                  pl.BlockSpec((tf, D_MODEL), row), pl.BlockSpec((tf, LANES), row),
                  pl.BlockSpec((None, tf, PLE_DIM), lambda i: (layer, i, 0)),
                  _full((1, D_MODEL)), _full((D_MODEL, D_MODEL)), _full((PLE_DIM, D_MODEL)),
                  _full((1, D_MODEL))],
        out_specs=pl.BlockSpec((tf, D_MODEL), row),
        compiler_params=_cparams(("parallel",)),
        name="combine",
    )(yg, yg, h1, rw, p_all, wts["g_ple"], wts["w_ple_gate"], wts["w_ple_proj"], g_final)


def _layer_weights(i, g_mix_norm, w_in, g_q_lat, w_q_up, g_kv_lat, w_kv_up,
                   ssm_a_re, ssm_a_im, ssm_b_re, ssm_b_im, ssm_c_re, ssm_c_im, ssm_d, ssm_log_step,
                   w_glu, b_glu, g_attn_out, g_ssm_out, w_out, g_moe_norm,
                   w_group_router, b_group_router, w_expert_router, b_expert_router,
                   g_ple_norm, w_ple_gate, w_ple_proj):
    off_kv, off_kr, off_ssm = Q_LORA, Q_LORA + KV_LORA, Q_LORA + KV_LORA + QK_ROPE
    wi = w_in[i]
    kr_block = jnp.zeros((D_MODEL, HEAD_PAD), F32).at[:, ROPE_LANE0:ROPE_LANE0 + QK_ROPE].set(wi[:, off_kr:off_ssm])
    w_in_p = jnp.concatenate([wi[:, :off_kr], kr_block, wi[:, off_ssm:]], axis=1)
    wq = w_q_up[i].reshape(Q_LORA, N_HEADS, QK_NOPE + QK_ROPE) * (ATTN_SCALE * math.log2(math.e))
    wq = jnp.pad(wq, ((0, 0), (0, 0), (0, HEAD_PAD - QK_NOPE - QK_ROPE))).reshape(Q_LORA, N_HEADS * HEAD_PAD)
    wkv = w_kv_up[i].reshape(KV_LORA, N_HEADS, QK_NOPE + V_HEAD)
    wk = jnp.pad(wkv[..., :QK_NOPE], ((0, 0), (0, 0), (0, HEAD_PAD - QK_NOPE))).reshape(KV_LORA, N_HEADS * HEAD_PAD)
    wv = jnp.pad(wkv[..., QK_NOPE:], ((0, 0), (0, 0), (0, HEAD_PAD - V_HEAD))).reshape(KV_LORA, N_HEADS * HEAD_PAD)
    lam = lax.complex(ssm_a_re[i], ssm_a_im[i])
    lam_bar = jnp.exp(lam * jnp.exp(ssm_log_step[i])[:, None])
    b_fac = (lam_bar - 1.0) / lam
    bc = lax.complex(ssm_b_re[i], ssm_b_im[i]) * b_fac[:, :, None]
    n_per = SSM_GROUPS // S5_SLABS

    def block_diag(blocks):
        s, n, a, b = blocks.shape
        eye = jnp.eye(n, dtype=blocks.dtype)
        return jnp.einsum("snab,nm->snamb", blocks, eye).reshape(s, n * a, n * b)

    b_hp_re = jnp.real(bc).transpose(0, 2, 1).reshape(S5_SLABS, n_per, SSM_CH, SSM_STATE)
    b_hp_im = jnp.imag(bc).transpose(0, 2, 1).reshape(S5_SLABS, n_per, SSM_CH, SSM_STATE)
    s5_wb = jnp.concatenate([block_diag(b_hp_re), block_diag(b_hp_im)], axis=-1)
    c_re = ssm_c_re[i].transpose(0, 2, 1).reshape(S5_SLABS, n_per, SSM_STATE, SSM_CH)
    c_im = ssm_c_im[i].transpose(0, 2, 1).reshape(S5_SLABS, n_per, SSM_STATE, SSM_CH)
    s5_wc = jnp.stack([block_diag(c_re), -block_diag(c_im)])
    w_r = jnp.zeros((D_MODEL, LANES), F32)
    w_r = w_r.at[:, :N_EXPERT_GROUPS].set(w_group_router[i])
    w_r = w_r.at[:, ROUTE_E0:ROUTE_E0 + N_EXPERTS].set(w_expert_router[i])
    w_r_hi = w_r.astype(BF16)
    w_r_lo = (w_r - w_r_hi.astype(F32)).astype(BF16)
    b_r = jnp.zeros((1, LANES), F32)
    b_r = b_r.at[0, :N_EXPERT_GROUPS].set(b_group_router[i])
    b_r = b_r.at[0, ROUTE_E0:ROUTE_E0 + N_EXPERTS].set(b_expert_router[i])
    return dict(
        g_mix=g_mix_norm[i][None], w_in=w_in_p.astype(BF16),
        g_q=g_q_lat[i][None], w_q_t=wq.T.astype(BF16),
        g_kv=g_kv_lat[i][None], w_k=wk.astype(BF16), w_v_t=wv.T.astype(BF16),
        s5_wb=s5_wb.astype(BF16), s5_wc=s5_wc.astype(BF16),
        s5_are=jnp.real(lam_bar).reshape(1, N_STATE), s5_aim=jnp.imag(lam_bar).reshape(1, N_STATE),
        s5_d=ssm_d[i].reshape(1, SSM_WIDTH), w_glu=w_glu[i].astype(BF16), b_glu=b_glu[i][None],
        g_attn=g_attn_out[i][None], g_ssm=g_ssm_out[i][None],
        w_out=w_out[i].astype(BF16), g_moe=g_moe_norm[i][None],
        w_r_hi=w_r_hi, w_r_lo=w_r_lo, b_r=b_r,
        g_ple=g_ple_norm[i][None], w_ple_gate=w_ple_gate[i].astype(BF16),
        w_ple_proj=w_ple_proj[i].astype(BF16),
    )


def _slot_layout(route, cnt, n_blocks, te):
    counts = cnt[0, :N_EXPERTS].astype(I32)
    pcounts = (counts + te - 1) // te * te
    pends = jnp.cumsum(pcounts)
    pstarts = pends - pcounts
    eid = route[:TOP_K]
    rank = route[TOP_K:2 * TOP_K]
    start = jnp.sum(jnp.where(eid[:, :, None] == jnp.arange(N_EXPERTS)[None, None, :],
                              pstarts[None, None, :], 0), axis=-1)
    dest_k = (start + rank).astype(I32)
    blk0 = jnp.arange(n_blocks, dtype=I32) * te
    block_e = jnp.minimum(jnp.sum((pends[None, :] <= blk0[:, None]).astype(I32), axis=1), N_EXPERTS - 1)
    n_slots = n_blocks * te
    gap_start = jnp.concatenate([pstarts + counts, pends[-1:]])
    gap_len = jnp.concatenate([pcounts - counts, n_slots - pends[-1:]])
    gap_end = jnp.cumsum(gap_len)
    j = jnp.arange(n_slots - eid.size, dtype=I32)
    gap = jnp.sum((gap_end[None, :] <= j[:, None]).astype(I32), axis=1)
    onehot = gap[:, None] == jnp.arange(N_EXPERTS + 1)[None, :]
    pad_rows = j + jnp.sum(jnp.where(onehot, (gap_start - (gap_end - gap_len))[None, :], 0), axis=1)
    n_used = (pends[-1:] // te).astype(I32)
    return dest_k, block_e, n_used, pad_rows.astype(I32)


def kernel(x, p, positions, g_mix_norm, w_in, g_q_lat, w_q_up, g_kv_lat, w_kv_up, ssm_a_re, ssm_a_im, ssm_b_re, ssm_b_im, ssm_c_re, ssm_c_im, ssm_d, ssm_log_step, w_glu, b_glu, g_attn_out, g_ssm_out, w_out, g_moe_norm, w_group_router, b_group_router, w_expert_router, b_expert_router, w_exp_gate, w_exp_up, w_exp_down, g_ple_norm, w_ple_gate, w_ple_proj, g_final):
    bsz, seq, _ = x.shape
    depth = p.shape[0]
    n_tok = bsz * seq
    t = _tiles(n_tok, seq)
    n_slots = TOP_K * n_tok + N_EXPERTS * t["te"]
    n_blocks = n_slots // t["te"]

    cosf, sinf = _rope_tables(positions.reshape(n_tok, 1), t["tm"])
    sub = min(SUB_ROWS, t["tm"])
    tril = jnp.tril(jnp.ones((sub, sub), F32), k=-1).astype(BF16)
    g_fin = g_final[None]

    h = x.reshape(n_tok, D_MODEL)
    for i in range(depth):
        wts = _layer_weights(i, g_mix_norm, w_in, g_q_lat, w_q_up, g_kv_lat, w_kv_up,
                             ssm_a_re, ssm_a_im, ssm_b_re, ssm_b_im, ssm_c_re, ssm_c_im, ssm_d,
                             ssm_log_step, w_glu, b_glu, g_attn_out, g_ssm_out, w_out, g_moe_norm,
                             w_group_router, b_group_router, w_expert_router, b_expert_router,
                             g_ple_norm, w_ple_gate, w_ple_proj)
        qt, k, vt, u_tm = _inproj(h, cosf, sinf, wts, bsz, seq, t["tm"], t["tq"])
        attn_n = _attention(qt, k, vt, wts["g_attn"], bsz, seq, t["tq"])
        ssm_n = _s5(u_tm.reshape(seq, bsz, SSM_WIDTH), wts, attn_n, bsz, seq, t["lc"])
        h1, xn, route, rw, cnt = _outproj(h, attn_n, ssm_n.reshape(seq, bsz * SSM_WIDTH), wts, tril,
                                          bsz, seq, t["tm"])
        dest_k, block_e, n_used, pad_rows = _slot_layout(route, cnt, n_blocks, t["te"])
        xbuf = _dispatch(dest_k, pad_rows, xn, n_slots)
        ybuf = _experts(block_e, n_used, xbuf, w_exp_gate, w_exp_up, w_exp_down, i, t["te"])
        h = _combine(dest_k, ybuf, h1, rw, p.reshape(depth, n_tok, PLE_DIM), i, wts, g_fin, t["tf"],
                     final=(i == depth - 1))
    return h.reshape(bsz, seq, D_MODEL)
```

```python
import functools
import math

import jax
import jax.numpy as jnp
from jax import lax
from jax.experimental import pallas as pl
from jax.experimental.pallas import tpu as pltpu
from jax.experimental.pallas import tpu_sc as plsc

F32 = jnp.float32
BF16 = jnp.bfloat16
I32 = jnp.int32
U32 = jnp.uint32

D_MODEL = 1024
MLA_WIDTH = 512
SSM_WIDTH = 512
V_HEAD = 64
N_HEADS = 8
QK_NOPE = 64
QK_ROPE = 32
HALF_ROPE = QK_ROPE // 2
Q_LORA = 256
KV_LORA = 128
ROPE_BASE = 10000.0
ATTN_SCALE = 1.0 / math.sqrt(QK_NOPE + QK_ROPE)
SSM_CH = 16
SSM_GROUPS = 32
SSM_STATE = 64
N_EXPERT_GROUPS = 4
EXPERTS_PER_GROUP = 8
N_EXPERTS = 32
TOP_K = 2
D_EXPERT = 256
PLE_DIM = 256
EPS = 1e-6

LANES = 128
SUBLANES = 8
HEAD_PAD = 128
ROPE_LANE0 = QK_NOPE
N_STATE = SSM_GROUPS * SSM_STATE
PACKED = D_MODEL // 2
NEG = -1e30
VMEM_LIMIT = 56 * 1024 * 1024
SUB_ROWS = 512
INPROJ_ROWS = 256
COMBINE_ROWS = 256
EXPERT_ROWS = 256
S5_ROWS = 256


def _tiles(n_tok, seq):
    return dict(
        tm=min(1024, seq),
        tq=min(512, seq),
        lc=min(64, seq),
        te=512,
        tf=min(1024, n_tok),
    )


def _rms(x, g):
    ms = jnp.mean(x * x, axis=-1, keepdims=True)
    return x * lax.rsqrt(ms + EPS) * g


def _sigmoid(x):
    return 1.0 / (1.0 + jnp.exp(-x))


def _pack_bf16_pairs(x):
    lo = lax.bitcast_convert_type(x[:, :PACKED].astype(BF16).astype(F32), U32)
    hi = lax.bitcast_convert_type(x[:, PACKED:].astype(BF16).astype(F32), U32)
    return (lo >> 16) | hi


def _unpack_bf16_pairs(w):
    lo = lax.bitcast_convert_type(w << 16, F32)
    hi = lax.bitcast_convert_type(w & jnp.uint32(0xFFFF0000), F32)
    return jnp.concatenate([lo, hi], axis=1)


def _cparams(sem):
    return pltpu.CompilerParams(dimension_semantics=sem, vmem_limit_bytes=VMEM_LIMIT)


def _full(shape):
    nd = len(shape)
    return pl.BlockSpec(shape, lambda *_: (0,) * nd)


def _rope_table_kernel(pos_ref, invf_ref, cos_ref, sin_ref):
    ang = pos_ref[...].astype(F32) * invf_ref[...]
    cos_ref[...] = jnp.cos(ang)
    sin_ref[...] = jnp.sin(ang)


def _rope_tables(pos_col, tm):
    n_tok = pos_col.shape[0]
    lane = jnp.arange(LANES)
    in_rope = (lane >= ROPE_LANE0) & (lane < ROPE_LANE0 + QK_ROPE)
    freq = ROPE_BASE ** (-((lane - ROPE_LANE0) % HALF_ROPE).astype(F32) / HALF_ROPE)
    invf = jnp.where(in_rope, freq, 0.0).astype(F32)[None, :]
    return pl.pallas_call(
        _rope_table_kernel,
        out_shape=(jax.ShapeDtypeStruct((n_tok, LANES), F32),) * 2,
        grid=(n_tok // tm,),
        in_specs=[pl.BlockSpec((tm, 1), lambda i: (i, 0)), _full((1, LANES))],
        out_specs=(pl.BlockSpec((tm, LANES), lambda i: (i, 0)),) * 2,
        compiler_params=_cparams(("parallel",)),
        name="rope_tables",
    )(pos_col, invf)


def _rope(x, cosf, s_lo, s_hi):
    n = x.shape[-1]
    return (x * cosf + pltpu.roll(x, n - HALF_ROPE, 1) * s_lo + pltpu.roll(x, HALF_ROPE, 1) * s_hi)


def _row_chunks(n_rows, sub):
    sub = min(sub, n_rows)
    return [slice(c * sub, (c + 1) * sub) for c in range(n_rows // sub)]


def _inproj_kernel(h_ref, cos_ref, sin_ref, gmix_ref, win_ref, gq_ref, wqt_ref, gkv_ref, wk_ref, wvt_ref,
                   qt_ref, k_ref, vt_ref, u_ref):
    lane = lax.broadcasted_iota(I32, (1, LANES), 1)
    feat = lax.broadcasted_iota(I32, (N_HEADS * HEAD_PAD, 1), 0)
    ones_row = jnp.where(feat % HEAD_PAD == V_HEAD, 1.0, 0.0)
    tile = qt_ref.shape[-1]
    chunks = _row_chunks(h_ref.shape[0], INPROJ_ROWS)
    zs = []
    for r in chunks:
        xn = _rms(h_ref[r, :], gmix_ref[...]).astype(BF16)
        zs.append(jnp.dot(xn, win_ref[...], preferred_element_type=F32))
    qts, ks, vts = [], [], []
    for z in zs:
        qn_t = jnp.transpose(_rms(z[:, :Q_LORA], gq_ref[...])).astype(BF16)
        qts.append(jnp.dot(wqt_ref[...], qn_t, preferred_element_type=F32))
        kvn = _rms(z[:, Q_LORA:Q_LORA + KV_LORA], gkv_ref[...])
        ks.append(jnp.dot(kvn.astype(BF16), wk_ref[...], preferred_element_type=F32))
        vts.append(jnp.dot(wvt_ref[...], jnp.transpose(kvn).astype(BF16),
                           preferred_element_type=F32))
    for r, z, qt, k, vt in zip(chunks, zs, qts, ks, vts):
        cosf = cos_ref[r, :]
        sinf = sin_ref[r, :]
        s_lo = jnp.where((lane >= ROPE_LANE0) & (lane < ROPE_LANE0 + HALF_ROPE), -sinf, 0.0)
        s_hi = jnp.where((lane >= ROPE_LANE0 + HALF_ROPE) & (lane < ROPE_LANE0 + QK_ROPE), sinf, 0.0)
        kpe = _rope(z[:, Q_LORA + KV_LORA:Q_LORA + KV_LORA + HEAD_PAD], cosf, s_lo, s_hi)
        cos_t = jnp.transpose(cosf)[ROPE_LANE0:ROPE_LANE0 + HALF_ROPE, :]
        sin_t = jnp.transpose(sinf)[ROPE_LANE0:ROPE_LANE0 + HALF_ROPE, :]
        cols = slice(r.start % tile, r.start % tile + (r.stop - r.start))
        for hd in range(N_HEADS):
            sl = slice(hd * HEAD_PAD, (hd + 1) * HEAD_PAD)
            k_ref[r, sl] = (k[:, sl] + kpe).astype(BF16)
            x1 = qt[hd * HEAD_PAD + ROPE_LANE0:hd * HEAD_PAD + ROPE_LANE0 + HALF_ROPE, :]
            x2 = qt[hd * HEAD_PAD + ROPE_LANE0 + HALF_ROPE:hd * HEAD_PAD + ROPE_LANE0 + QK_ROPE, :]
            qt_ref[r.start // tile, hd * HEAD_PAD:hd * HEAD_PAD + ROPE_LANE0, cols] = (
                qt[hd * HEAD_PAD:hd * HEAD_PAD + ROPE_LANE0, :].astype(BF16))
            qt_ref[r.start // tile, hd * HEAD_PAD + ROPE_LANE0:hd * HEAD_PAD + ROPE_LANE0 + QK_ROPE, cols] = (
                jnp.concatenate([x1 * cos_t - x2 * sin_t, x1 * sin_t + x2 * cos_t], axis=0).astype(BF16))
            qt_ref[r.start // tile, hd * HEAD_PAD + ROPE_LANE0 + QK_ROPE:(hd + 1) * HEAD_PAD, cols] = (
                qt[hd * HEAD_PAD + ROPE_LANE0 + QK_ROPE:(hd + 1) * HEAD_PAD, :].astype(BF16))
        vt_ref[r.start // tile, :, cols] = (vt + ones_row).astype(BF16)
        u_ref[r, :] = z[:, Q_LORA + KV_LORA + HEAD_PAD:]


def _inproj(h, cosf, sinf, wts, bsz, seq, tm, tq):
    n_tok = bsz * seq
    nl = seq // tm
    row = lambda b, l: (b * nl + l, 0)
    tile3 = lambda b, l: (b * nl + l, 0, 0)
    qk_w = N_HEADS * HEAD_PAD
    return pl.pallas_call(
        _inproj_kernel,
        out_shape=(jax.ShapeDtypeStruct((n_tok // tq, qk_w, tq), BF16),
                   jax.ShapeDtypeStruct((n_tok, qk_w), BF16),
                   jax.ShapeDtypeStruct((n_tok // tq, qk_w, tq), BF16),
                   jax.ShapeDtypeStruct((seq, bsz * SSM_WIDTH), F32)),
        grid=(bsz, nl),
        in_specs=[pl.BlockSpec((tm, D_MODEL), row),
                  pl.BlockSpec((tm, LANES), row), pl.BlockSpec((tm, LANES), row),
                  _full((1, D_MODEL)), _full((D_MODEL, D_MODEL)),
                  _full((1, Q_LORA)), _full((qk_w, Q_LORA)),
                  _full((1, KV_LORA)), _full((KV_LORA, qk_w)), _full((qk_w, KV_LORA))],
        out_specs=(pl.BlockSpec((tm // tq, qk_w, tq), tile3), pl.BlockSpec((tm, qk_w), row),
                   pl.BlockSpec((tm // tq, qk_w, tq), tile3),
                   pl.BlockSpec((tm, SSM_WIDTH), lambda b, l: (l, b))),
        compiler_params=_cparams(("parallel", "parallel")),
        name="inproj",
    )(h, cosf, sinf, wts["g_mix"], wts["w_in"], wts["g_q"], wts["w_q_t"], wts["g_kv"], wts["w_k"],
      wts["w_v_t"])


ATTN_RB = 16
ATTN_GROUP = 4

def _attn_kernel(qt_ref, k_ref, vt_ref, g_ref, o_ref, s_ref, p_ref, m_ref, acc_ref, out_ref, *, tq):
    qi = pl.program_id(1)
    key = lax.broadcasted_iota(I32, (ATTN_RB, tq), 0)
    qry = lax.broadcasted_iota(I32, (ATTN_RB, tq), 1)
    n_chunks = tq // ATTN_RB
    for pr in range(N_HEADS // ATTN_GROUP):
        heads = [ATTN_GROUP * pr + a for a in range(ATTN_GROUP)]
        m_ref[...] = jnp.full(m_ref.shape, NEG, F32)
        acc_ref[...] = jnp.zeros(acc_ref.shape, F32)

        def scores(j, slot, heads=heads):
            rows = pl.ds(pl.multiple_of(j * tq, tq), tq)
            for a, hd in enumerate(heads):
                hs = slice(hd * HEAD_PAD, (hd + 1) * HEAD_PAD)
                s_ref[slot, a] = jnp.dot(k_ref[rows, hs], qt_ref[hs, :], preferred_element_type=F32)

        def softmax_pv(j, slot, masked, heads=heads):
            alphas = []
            for a in range(ATTN_GROUP):
                def chunk(c, slot=slot, a=a):
                    s = s_ref[slot, a, c * ATTN_RB:(c + 1) * ATTN_RB, :]
                    if masked:
                        s = jnp.where(key + c * ATTN_RB <= qry, s, NEG)
                    return s

                top = chunk(0)
                for c in range(1, n_chunks):
                    top = jnp.maximum(top, chunk(c))
                m_old = m_ref[a]
                m_new = jnp.maximum(m_old, jnp.max(top, axis=0, keepdims=True))
                for c in range(n_chunks):
                    p_ref[a, c * ATTN_RB:(c + 1) * ATTN_RB, :] = jnp.exp2(chunk(c) - m_new).astype(BF16)
                alphas.append(jnp.exp2(m_old - m_new))
                m_ref[a] = m_new
            for a, hd in enumerate(heads):
                pv = jnp.dot(vt_ref[j, hd * HEAD_PAD:(hd + 1) * HEAD_PAD, :], p_ref[a],
                             preferred_element_type=F32)
                acc_ref[a] = alphas[a] * acc_ref[a] + pv

        def two_tiles(jj, carry):
            t = 2 * jj
            scores(t, 0)
            scores(t + 1, 1)
            softmax_pv(t, 0, masked=False)
            softmax_pv(t + 1, 1, masked=False)
            return carry

        lax.fori_loop(0, qi // 2, two_tiles, 0)

        @pl.when(qi % 2 == 0)
        def _():
            scores(qi, 0)
            softmax_pv(qi, 0, masked=True)

        @pl.when(qi % 2 == 1)
        def _():
            scores(qi - 1, 0)
            scores(qi, 1)
            softmax_pv(qi - 1, 0, masked=False)
            softmax_pv(qi, 1, masked=True)

        for a, hd in enumerate(heads):
            acc = acc_ref[a]
            out_ref[hd * V_HEAD:(hd + 1) * V_HEAD, :] = acc[:V_HEAD, :] / acc[V_HEAD:V_HEAD + 1, :]
    o_ref[...] = _rms(jnp.transpose(out_ref[...]), g_ref[...]).astype(BF16)


def _attention(qt, k, vt, g_attn, bsz, seq, tq):
    n_tok = bsz * seq
    nq = seq // tq
    qk_w = N_HEADS * HEAD_PAD
    return pl.pallas_call(
        functools.partial(_attn_kernel, tq=tq),
        out_shape=jax.ShapeDtypeStruct((n_tok, MLA_WIDTH), BF16),
        grid=(bsz, nq),
        in_specs=[pl.BlockSpec((None, qk_w, tq), lambda b, i: (b * nq + i, 0, 0)),
                  pl.BlockSpec((seq, qk_w), lambda b, i: (b, 0)),
                  pl.BlockSpec((nq, qk_w, tq), lambda b, i: (b, 0, 0)),
                  _full((1, MLA_WIDTH))],
        out_specs=pl.BlockSpec((tq, MLA_WIDTH), lambda b, i: (b * nq + i, 0)),
        scratch_shapes=[pltpu.VMEM((2, ATTN_GROUP, tq, tq), F32), pltpu.VMEM((ATTN_GROUP, tq, tq), BF16),
                        pltpu.VMEM((ATTN_GROUP, 1, tq), F32),
                        pltpu.VMEM((ATTN_GROUP, HEAD_PAD, tq), F32), pltpu.VMEM((MLA_WIDTH, tq), F32)],
        compiler_params=_cparams(("parallel", "parallel")),
        name="attn",
    )(qt, k, vt, g_attn)


S5_SLABS = 4
S5_SLAB_U = SSM_WIDTH // S5_SLABS
S5_SLAB_X = N_STATE // S5_SLABS


def _s5_kernel(u_ref, wb_ref, are_ref, aim_ref, wc_ref, d_ref, wglu_ref, bglu_ref, g_ref, after_ref,
               o_ref, xr_ref, xi_ref, sr_ref, si_ref, *, lc, nb):
    del after_ref

    @pl.when(pl.program_id(0) == 0)
    def _():
        xr_ref[...] = jnp.zeros_like(xr_ref)
        xi_ref[...] = jnp.zeros_like(xi_ref)

    rows = lc * nb
    u = u_ref[...].reshape(rows, SSM_WIDTH)
    ub = u.astype(BF16)
    def drive(j):
        bu = jnp.dot(ub[:, j * S5_SLAB_U:(j + 1) * S5_SLAB_U], wb_ref[j], preferred_element_type=F32)
        sr_ref[:, j * S5_SLAB_X:(j + 1) * S5_SLAB_X] = bu[:, :S5_SLAB_X]
        si_ref[:, j * S5_SLAB_X:(j + 1) * S5_SLAB_X] = bu[:, S5_SLAB_X:]

    ys = []
    drive(0)
    for c in range(S5_SLABS):
        if c + 1 < S5_SLABS:
            drive(c + 1)
        sl = slice(c * S5_SLAB_X, (c + 1) * S5_SLAB_X)
        ar = jnp.broadcast_to(are_ref[:, sl], (nb, S5_SLAB_X))
        ai = jnp.broadcast_to(aim_ref[:, sl], (nb, S5_SLAB_X))
        xr = xr_ref[:, sl]
        xi = xi_ref[:, sl]
        for t in range(lc):
            r = slice(t * nb, (t + 1) * nb)
            xr, xi = (ar * xr - ai * xi + sr_ref[r, sl], ar * xi + ai * xr + si_ref[r, sl])
            sr_ref[r, sl] = xr
            si_ref[r, sl] = xi
        xr_ref[:, sl] = xr
        xi_ref[:, sl] = xi
        ys.append(jnp.dot(sr_ref[:, sl].astype(BF16), wc_ref[0, c], preferred_element_type=F32)
                  + jnp.dot(si_ref[:, sl].astype(BF16), wc_ref[1, c], preferred_element_type=F32))
    y_all = jnp.concatenate(ys, axis=-1) + d_ref[...] * u
    chunks = _row_chunks(rows, S5_ROWS)
    ygs, gates = [], []
    for r in chunks:
        y = y_all[r, :]
        ygs.append(0.5 * y * (1.0 + jnp.tanh(math.sqrt(2.0 / math.pi) * (y + 0.044715 * (y * y * y)))))
    for yg in ygs:
        gates.append(jnp.dot(yg.astype(BF16), wglu_ref[...], preferred_element_type=F32) + bglu_ref[...])
    for r, yg, gate in zip(chunks, ygs, gates):
        out = _rms(yg * _sigmoid(gate), g_ref[...]).astype(BF16)
        o_ref[r.start // nb:r.stop // nb] = out.reshape((r.stop - r.start) // nb, nb, SSM_WIDTH)


def _s5(u_tm, wts, after, bsz, seq, lc):
    rows = lc * bsz
    return pl.pallas_call(
        functools.partial(_s5_kernel, lc=lc, nb=bsz),
        out_shape=jax.ShapeDtypeStruct((seq, bsz, SSM_WIDTH), BF16),
        grid=(seq // lc,),
        in_specs=[pl.BlockSpec((lc, bsz, SSM_WIDTH), lambda i: (i, 0, 0)),
                  _full((S5_SLABS, S5_SLAB_U, 2 * S5_SLAB_X)),
                  _full((1, N_STATE)), _full((1, N_STATE)),
                  _full((2, S5_SLABS, S5_SLAB_X, S5_SLAB_U)),
                  _full((1, SSM_WIDTH)), _full((SSM_WIDTH, SSM_WIDTH)), _full((1, SSM_WIDTH)),
                  _full((1, SSM_WIDTH)), pl.BlockSpec(memory_space=pl.ANY)],
        out_specs=pl.BlockSpec((lc, bsz, SSM_WIDTH), lambda i: (i, 0, 0)),
        scratch_shapes=[pltpu.VMEM((bsz, N_STATE), F32), pltpu.VMEM((bsz, N_STATE), F32),
                        pltpu.VMEM((rows, N_STATE), F32), pltpu.VMEM((rows, N_STATE), F32)],
        compiler_params=_cparams(("arbitrary",)),
        name="s5",
    )(u_tm, wts["s5_wb"], wts["s5_are"], wts["s5_aim"], wts["s5_wc"], wts["s5_d"],
      wts["w_glu"], wts["b_glu"], wts["g_ssm"], after)


ROUTE_E0 = N_EXPERT_GROUPS


def _outproj_kernel(h_ref, a_ref, s_ref, wout_ref, gmoe_ref, wrh_ref, wrl_ref, br_ref, tril_ref,
                    h1_ref, xn_ref, route_ref, rw_ref, cnt_ref):
    @pl.when((pl.program_id(0) == 0) & (pl.program_id(1) == 0))
    def _():
        cnt_ref[...] = jnp.zeros_like(cnt_ref)

    sub = tril_ref.shape[0]
    lane = lax.broadcasted_iota(I32, (sub, LANES), 1).astype(F32)
    big = float(LANES)
    chunks = _row_chunks(h_ref.shape[0], sub)
    xns = []
    for r in chunks:
        h1 = (h_ref[r, :]
              + jnp.dot(a_ref[r, :], wout_ref[:MLA_WIDTH, :], preferred_element_type=F32)
              + jnp.dot(s_ref[r, :], wout_ref[MLA_WIDTH:, :], preferred_element_type=F32))
        h1_ref[r, :] = h1
        xns.append(_rms(h1, gmoe_ref[...]))
    lgs = []
    for r, xn in zip(chunks, xns):
        xn_ref[r, :] = _pack_bf16_pairs(xn)
        x_hi = xn.astype(BF16)
        x_lo = (xn - x_hi.astype(F32)).astype(BF16)
        lgs.append((jnp.dot(x_hi, wrh_ref[...], preferred_element_type=F32)
                    + (jnp.dot(x_lo, wrh_ref[...], preferred_element_type=F32)
                       + jnp.dot(x_hi, wrl_ref[...], preferred_element_type=F32))) + br_ref[...])
    for r, lg in zip(chunks, lgs):
        gl = jnp.where(lane < N_EXPERT_GROUPS, lg, NEG)
        gmax = jnp.max(gl, axis=-1, keepdims=True)
        gsum = jnp.sum(jnp.exp(gl - gmax), axis=-1, keepdims=True)
        g_top_p = 1.0 / gsum
        gidx = jnp.min(jnp.where(gl == gmax, lane, big), axis=-1, keepdims=True)
        lo = ROUTE_E0 + EXPERTS_PER_GROUP * gidx
        sel = jnp.where((lane >= lo) & (lane < lo + EXPERTS_PER_GROUP), lg, NEG)
        m1 = jnp.max(sel, axis=-1, keepdims=True)
        i1 = jnp.min(jnp.where(sel == m1, lane, big), axis=-1, keepdims=True)
        sel2 = jnp.where(lane == i1, NEG, sel)
        m2 = jnp.max(sel2, axis=-1, keepdims=True)
        i2 = jnp.min(jnp.where(sel2 == m2, lane, big), axis=-1, keepdims=True)
        e21 = jnp.exp(m2 - m1)
        w1 = g_top_p / (1.0 + e21)
        w2 = g_top_p * e21 / (1.0 + e21)
        e1 = i1 - ROUTE_E0
        e2 = i2 - ROUTE_E0
        oh = jnp.where((lane == e1) | (lane == e2), 1.0, 0.0)
        before = cnt_ref[...] + jnp.dot(tril_ref[...], oh.astype(BF16), preferred_element_type=F32)
        r1 = jnp.sum(jnp.where(lane == e1, before, 0.0), axis=-1, keepdims=True)
        r2 = jnp.sum(jnp.where(lane == e2, before, 0.0), axis=-1, keepdims=True)
        cnt_ref[...] = cnt_ref[...] + jnp.sum(oh, axis=0, keepdims=True)
        route = jnp.where(lane == 0, e1, jnp.where(lane == 1, e2, jnp.where(lane == 2, r1, r2)))
        route_ref[:, r] = jnp.transpose(route)[:SUBLANES, :].astype(I32)
        rw_ref[r, :] = jnp.where(lane == 0, w1, w2)


def _outproj(h, attn_n, ssm_tm, wts, tril, bsz, seq, tm):
    n_tok = bsz * seq
    nl = seq // tm
    row = lambda b, l: (b * nl + l, 0)
    return pl.pallas_call(
        _outproj_kernel,
        out_shape=(jax.ShapeDtypeStruct((n_tok, D_MODEL), F32),
                   jax.ShapeDtypeStruct((n_tok, PACKED), U32),
                   jax.ShapeDtypeStruct((SUBLANES, n_tok), I32),
                   jax.ShapeDtypeStruct((n_tok, LANES), F32),
                   jax.ShapeDtypeStruct((1, LANES), F32)),
        grid=(bsz, nl),
        in_specs=[pl.BlockSpec((tm, D_MODEL), row),
                  pl.BlockSpec((tm, MLA_WIDTH), row),
                  pl.BlockSpec((tm, SSM_WIDTH), lambda b, l: (l, b)),
                  _full((D_MODEL, D_MODEL)), _full((1, D_MODEL)),
                  _full((D_MODEL, LANES)), _full((D_MODEL, LANES)), _full((1, LANES)),
                  _full(tril.shape)],
        out_specs=(pl.BlockSpec((tm, D_MODEL), row), pl.BlockSpec((tm, PACKED), row),
                   pl.BlockSpec((SUBLANES, tm), lambda b, l: (0, b * nl + l)),
                   pl.BlockSpec((tm, LANES), row),
                   _full((1, LANES))),
        compiler_params=_cparams(("arbitrary", "arbitrary")),
        name="outproj",
    )(h, attn_n, ssm_tm, wts["w_out"], wts["g_moe"], wts["w_r_hi"], wts["w_r_lo"], wts["b_r"], tril)


SC_CORES = 2
SC_SUBCORES = 16
SC_WORKERS = SC_CORES * SC_SUBCORES
SC_CHUNK = 64


def _sc_mesh():
    return plsc.VectorSubcoreMesh(core_axis_name="c", subcore_axis_name="s",
                                  num_cores=SC_CORES, num_subcores=SC_SUBCORES)


def _sc_worker_id():
    return lax.axis_index("s") * SC_CORES + lax.axis_index("c")


def _sc_two_slot_pipeline(n_chunks, load, stores):
    assert n_chunks % 2 == 0
    load(0, 0).start()

    @pl.loop(0, n_chunks // 2)
    def _(jj):
        for slot in (0, 1):
            j = 2 * jj + slot

            @pl.when(j >= 1)
            def _():
                for cp in stores(j - 1, 1 - slot):
                    cp.wait()

            @pl.when(j + 1 < n_chunks)
            def _():
                load(j + 1, 1 - slot).start()

            load(j, slot).wait()
            for cp in stores(j, slot):
                cp.start()

    for cp in stores(n_chunks - 1, 1):
        cp.wait()


def _sc_rows(j):
    return pl.ds(pl.multiple_of(j * SC_CHUNK, SC_CHUNK), SC_CHUNK)


def _dispatch(dest_k, pad_rows, xn, n_slots):
    n_tok, width = xn.shape
    n_pad = pad_rows.shape[0]
    per_w = n_tok // SC_WORKERS
    per_p = n_pad // SC_WORKERS
    assert per_p <= per_w

    @functools.partial(
        pl.kernel, mesh=_sc_mesh(),
        out_type=jax.ShapeDtypeStruct((n_slots, width), xn.dtype),
        scratch_types=[pltpu.VMEM((per_w,), I32), pltpu.VMEM((per_w,), I32), pltpu.VMEM((per_p,), I32),
                       pltpu.VMEM((2, SC_CHUNK, width), xn.dtype),
                       pltpu.SemaphoreType.DMA((2,)), pltpu.SemaphoreType.DMA((2, TOP_K))],
        name="sc_dispatch_rows",
    )
    def scatter_kernel(xn_hbm, d0_hbm, d1_hbm, dp_hbm, out_hbm, i0_v, i1_v, ip_v, rows_v, lsem, ssem):
        wid = _sc_worker_id()
        base = wid * per_w
        pltpu.sync_copy(d0_hbm.at[pl.ds(base, per_w)], i0_v)
        pltpu.sync_copy(d1_hbm.at[pl.ds(base, per_w)], i1_v)
        pltpu.sync_copy(dp_hbm.at[pl.ds(wid * per_p, per_p)], ip_v)

        def load(j, slot):
            return pltpu.make_async_copy(xn_hbm.at[pl.ds(base + j * SC_CHUNK, SC_CHUNK)],
                                         rows_v.at[slot], lsem.at[slot])

        def scatter(idx_v, k):
            return lambda j, slot: pltpu.make_async_copy(
                rows_v.at[slot], out_hbm.at[idx_v.at[_sc_rows(j)]], ssem.at[slot, k])

        to_k0, to_k1, to_pad = scatter(i0_v, 0), scatter(i1_v, 1), scatter(ip_v, 0)
        _sc_two_slot_pipeline(per_w // SC_CHUNK, load,
                              lambda j, slot: [to_k0(j, slot), to_k1(j, slot)])
        _sc_two_slot_pipeline(per_p // SC_CHUNK, load, lambda j, slot: [to_pad(j, slot)])

    return scatter_kernel(xn, dest_k[0], dest_k[1], pad_rows)


def _experts_kernel(be_ref, nused_ref, x_ref, wg_ref, wu_ref, wd_ref, y_ref):
    del be_ref

    @pl.when(pl.program_id(0) >= nused_ref[0])
    def _():
        y_ref[...] = jnp.zeros_like(y_ref)

    @pl.when(pl.program_id(0) < nused_ref[0])
    def _():
        wg = wg_ref[...].astype(BF16)
        wu = wu_ref[...].astype(BF16)
        wd = wd_ref[...].astype(BF16)
        chunks = _row_chunks(x_ref.shape[0], EXPERT_ROWS)
        gus = []
        for r in chunks:
            xb = _unpack_bf16_pairs(x_ref[r, :]).astype(BF16)
            gus.append((jnp.dot(xb, wg, preferred_element_type=F32),
                        jnp.dot(xb, wu, preferred_element_type=F32)))
        ys = []
        for g, u in gus:
            hdn = (g * _sigmoid(g) * u).astype(BF16)
            ys.append(jnp.dot(hdn, wd, preferred_element_type=F32))
        for r, y in zip(chunks, ys):
            y_ref[r, :] = _pack_bf16_pairs(y)


def _experts(block_e, n_used, xbuf, w_gate, w_up, w_down, layer, te):
    n_slots = xbuf.shape[0]
    grid_spec = pltpu.PrefetchScalarGridSpec(
        num_scalar_prefetch=2,
        grid=(n_slots // te,),
        in_specs=[pl.BlockSpec((te, PACKED), lambda i, be, nu: (i, 0)),
                  pl.BlockSpec((None, None, D_MODEL, D_EXPERT), lambda i, be, nu: (layer, be[i], 0, 0)),
                  pl.BlockSpec((None, None, D_MODEL, D_EXPERT), lambda i, be, nu: (layer, be[i], 0, 0)),
                  pl.BlockSpec((None, None, D_EXPERT, D_MODEL), lambda i, be, nu: (layer, be[i], 0, 0))],
        out_specs=pl.BlockSpec((te, PACKED), lambda i, be, nu: (i, 0)),
    )
    return pl.pallas_call(
        _experts_kernel,
        out_shape=jax.ShapeDtypeStruct((n_slots, PACKED), U32),
        grid_spec=grid_spec,
        compiler_params=_cparams(("arbitrary",)),
        name="experts",
    )(block_e, n_used, xbuf, w_gate, w_up, w_down)


def _sc_gather_rows(table, idx):
    n_rows = idx.shape[0]
    width = table.shape[1]
    per_w = n_rows // SC_WORKERS

    @functools.partial(
        pl.kernel, mesh=_sc_mesh(),
        out_type=jax.ShapeDtypeStruct((n_rows, width), table.dtype),
        scratch_types=[pltpu.VMEM((per_w,), I32), pltpu.VMEM((2, SC_CHUNK, width), table.dtype),
                       pltpu.SemaphoreType.DMA((2,)), pltpu.SemaphoreType.DMA((2,))],
        name="sc_gather_rows",
    )
    def gather_kernel(table_hbm, idx_hbm, out_hbm, idx_v, rows_v, lsem, ssem):
        base = _sc_worker_id() * per_w
        pltpu.sync_copy(idx_hbm.at[pl.ds(base, per_w)], idx_v)

        def load(j, slot):
            return pltpu.make_async_copy(table_hbm.at[idx_v.at[_sc_rows(j)]], rows_v.at[slot],
                                         lsem.at[slot])

        def store(j, slot):
            return [pltpu.make_async_copy(rows_v.at[slot],
                                          out_hbm.at[pl.ds(base + j * SC_CHUNK, SC_CHUNK)],
                                          ssem.at[slot])]

        _sc_two_slot_pipeline(per_w // SC_CHUNK, load, store)

    return gather_kernel(table, idx)


def _combine_kernel(y0_ref, y1_ref, h1_ref, rw_ref, p_ref, gple_ref, wgate_ref, wproj_ref,
                    gfin_ref, o_ref, *, final):
    chunks = _row_chunks(h1_ref.shape[0], COMBINE_ROWS)
    h2s, gates, ples = [], [], []
    for r in chunks:
        rw = rw_ref[r, :]
        h2s.append(h1_ref[r, :] + rw[:, 0:1] * _unpack_bf16_pairs(y0_ref[r, :])
                   + rw[:, 1:2] * _unpack_bf16_pairs(y1_ref[r, :]))
    for r, h2 in zip(chunks, h2s):
        hn = _rms(h2, gple_ref[...]).astype(BF16)
        gates.append(jnp.dot(hn, wgate_ref[...], preferred_element_type=F32))
        ples.append(jnp.dot(p_ref[r, :].astype(BF16), wproj_ref[...], preferred_element_type=F32))
    for r, h2, gate, ple in zip(chunks, h2s, gates, ples):
        h3 = h2 + ple * _sigmoid(gate)
        if final:
            h3 = _rms(h3, gfin_ref[...])
        o_ref[r, :] = h3


def _combine(dest_k, ybuf, h1, rw, p_all, layer, wts, g_final, tf, final):
    n_tok = h1.shape[0]
    n_steps = n_tok // tf
    row = lambda i: (i, 0)
    yg = _sc_gather_rows(ybuf, dest_k.reshape(-1))
    return pl.pallas_call(
        functools.partial(_combine_kernel, final=final),
        out_shape=jax.ShapeDtypeStruct((n_tok, D_MODEL), F32),
        grid=(n_steps,),
        in_specs=[pl.BlockSpec((tf, PACKED), row),
                  pl.BlockSpec((tf, PACKED), lambda i: (n_steps + i, 0)),
                  pl.BlockSpec((tf, D_MODEL), row), pl.BlockSpec((tf, LANES), row),
                  pl.BlockSpec((None, tf, PLE_DIM), lambda i: (layer, i, 0)),
                  _full((1, D_MODEL)), _full((D_MODEL, D_MODEL)), _full((PLE_DIM, D_MODEL)),
                  _full((1, D_MODEL))],
        out_specs=pl.BlockSpec((tf, D_MODEL), row),
        compiler_params=_cparams(("parallel",)),
        name="combine",
    )(yg, yg, h1, rw, p_all, wts["g_ple"], wts["w_ple_gate"], wts["w_ple_proj"], g_final)


def _layer_weights(i, g_mix_norm, w_in, g_q_lat, w_q_up, g_kv_lat, w_kv_up,
                   ssm_a_re, ssm_a_im, ssm_b_re, ssm_b_im, ssm_c_re, ssm_c_im, ssm_d, ssm_log_step,
                   w_glu, b_glu, g_attn_out, g_ssm_out, w_out, g_moe_norm,
                   w_group_router, b_group_router, w_expert_router, b_expert_router,
                   g_ple_norm, w_ple_gate, w_ple_proj):
    off_kv, off_kr, off_ssm = Q_LORA, Q_LORA + KV_LORA, Q_LORA + KV_LORA + QK_ROPE
    wi = w_in[i]
    kr_block = jnp.zeros((D_MODEL, HEAD_PAD), F32).at[:, ROPE_LANE0:ROPE_LANE0 + QK_ROPE].set(wi[:, off_kr:off_ssm])
    w_in_p = jnp.concatenate([wi[:, :off_kr], kr_block, wi[:, off_ssm:]], axis=1)
    wq = w_q_up[i].reshape(Q_LORA, N_HEADS, QK_NOPE + QK_ROPE) * (ATTN_SCALE * math.log2(math.e))
    wq = jnp.pad(wq, ((0, 0), (0, 0), (0, HEAD_PAD - QK_NOPE - QK_ROPE))).reshape(Q_LORA, N_HEADS * HEAD_PAD)
    wkv = w_kv_up[i].reshape(KV_LORA, N_HEADS, QK_NOPE + V_HEAD)
    wk = jnp.pad(wkv[..., :QK_NOPE], ((0, 0), (0, 0), (0, HEAD_PAD - QK_NOPE))).reshape(KV_LORA, N_HEADS * HEAD_PAD)
    wv = jnp.pad(wkv[..., QK_NOPE:], ((0, 0), (0, 0), (0, HEAD_PAD - V_HEAD))).reshape(KV_LORA, N_HEADS * HEAD_PAD)
    lam = lax.complex(ssm_a_re[i], ssm_a_im[i])
    lam_bar = jnp.exp(lam * jnp.exp(ssm_log_step[i])[:, None])
    b_fac = (lam_bar - 1.0) / lam
    bc = lax.complex(ssm_b_re[i], ssm_b_im[i]) * b_fac[:, :, None]
    n_per = SSM_GROUPS // S5_SLABS

    def block_diag(blocks):
        s, n, a, b = blocks.shape
        eye = jnp.eye(n, dtype=blocks.dtype)
        return jnp.einsum("snab,nm->snamb", blocks, eye).reshape(s, n * a, n * b)

    b_hp_re = jnp.real(bc).transpose(0, 2, 1).reshape(S5_SLABS, n_per, SSM_CH, SSM_STATE)
    b_hp_im = jnp.imag(bc).transpose(0, 2, 1).reshape(S5_SLABS, n_per, SSM_CH, SSM_STATE)
    s5_wb = jnp.concatenate([block_diag(b_hp_re), block_diag(b_hp_im)], axis=-1)
    c_re = ssm_c_re[i].transpose(0, 2, 1).reshape(S5_SLABS, n_per, SSM_STATE, SSM_CH)
    c_im = ssm_c_im[i].transpose(0, 2, 1).reshape(S5_SLABS, n_per, SSM_STATE, SSM_CH)
    s5_wc = jnp.stack([block_diag(c_re), -block_diag(c_im)])
    w_r = jnp.zeros((D_MODEL, LANES), F32)
    w_r = w_r.at[:, :N_EXPERT_GROUPS].set(w_group_router[i])
    w_r = w_r.at[:, ROUTE_E0:ROUTE_E0 + N_EXPERTS].set(w_expert_router[i])
    w_r_hi = w_r.astype(BF16)
    w_r_lo = (w_r - w_r_hi.astype(F32)).astype(BF16)
    b_r = jnp.zeros((1, LANES), F32)
    b_r = b_r.at[0, :N_EXPERT_GROUPS].set(b_group_router[i])
    b_r = b_r.at[0, ROUTE_E0:ROUTE_E0 + N_EXPERTS].set(b_expert_router[i])
    return dict(
        g_mix=g_mix_norm[i][None], w_in=w_in_p.astype(BF16),
        g_q=g_q_lat[i][None], w_q_t=wq.T.astype(BF16),
        g_kv=g_kv_lat[i][None], w_k=wk.astype(BF16), w_v_t=wv.T.astype(BF16),
        s5_wb=s5_wb.astype(BF16), s5_wc=s5_wc.astype(BF16),
        s5_are=jnp.real(lam_bar).reshape(1, N_STATE), s5_aim=jnp.imag(lam_bar).reshape(1, N_STATE),
        s5_d=ssm_d[i].reshape(1, SSM_WIDTH), w_glu=w_glu[i].astype(BF16), b_glu=b_glu[i][None],
        g_attn=g_attn_out[i][None], g_ssm=g_ssm_out[i][None],
        w_out=w_out[i].astype(BF16), g_moe=g_moe_norm[i][None],
        w_r_hi=w_r_hi, w_r_lo=w_r_lo, b_r=b_r,
        g_ple=g_ple_norm[i][None], w_ple_gate=w_ple_gate[i].astype(BF16),
        w_ple_proj=w_ple_proj[i].astype(BF16),
    )


def _slot_layout(route, cnt, n_blocks, te):
    counts = cnt[0, :N_EXPERTS].astype(I32)
    pcounts = (counts + te - 1) // te * te
    pends = jnp.cumsum(pcounts)
    pstarts = pends - pcounts
    eid = route[:TOP_K]
    rank = route[TOP_K:2 * TOP_K]
    start = jnp.sum(jnp.where(eid[:, :, None] == jnp.arange(N_EXPERTS)[None, None, :],
                              pstarts[None, None, :], 0), axis=-1)
    dest_k = (start + rank).astype(I32)
    blk0 = jnp.arange(n_blocks, dtype=I32) * te
    block_e = jnp.minimum(jnp.sum((pends[None, :] <= blk0[:, None]).astype(I32), axis=1), N_EXPERTS - 1)
    n_slots = n_blocks * te
    gap_start = jnp.concatenate([pstarts + counts, pends[-1:]])
    gap_len = jnp.concatenate([pcounts - counts, n_slots - pends[-1:]])
    gap_end = jnp.cumsum(gap_len)
    j = jnp.arange(n_slots - eid.size, dtype=I32)
    gap = jnp.sum((gap_end[None, :] <= j[:, None]).astype(I32), axis=1)
    onehot = gap[:, None] == jnp.arange(N_EXPERTS + 1)[None, :]
    pad_rows = j + jnp.sum(jnp.where(onehot, (gap_start - (gap_end - gap_len))[None, :], 0), axis=1)
    n_used = (pends[-1:] // te).astype(I32)
    return dest_k, block_e, n_used, pad_rows.astype(I32)


def kernel(x, p, positions, g_mix_norm, w_in, g_q_lat, w_q_up, g_kv_lat, w_kv_up, ssm_a_re, ssm_a_im, ssm_b_re, ssm_b_im, ssm_c_re, ssm_c_im, ssm_d, ssm_log_step, w_glu, b_glu, g_attn_out, g_ssm_out, w_out, g_moe_norm, w_group_router, b_group_router, w_expert_router, b_expert_router, w_exp_gate, w_exp_up, w_exp_down, g_ple_norm, w_ple_gate, w_ple_proj, g_final):
    bsz, seq, _ = x.shape
    depth = p.shape[0]
    n_tok = bsz * seq
    t = _tiles(n_tok, seq)
    n_slots = TOP_K * n_tok + N_EXPERTS * t["te"]
    n_blocks = n_slots // t["te"]

    cosf, sinf = _rope_tables(positions.reshape(n_tok, 1), t["tm"])
    sub = min(SUB_ROWS, t["tm"])
    tril = jnp.tril(jnp.ones((sub, sub), F32), k=-1).astype(BF16)
    g_fin = g_final[None]

    h = x.reshape(n_tok, D_MODEL)
    for i in range(depth):
        wts = _layer_weights(i, g_mix_norm, w_in, g_q_lat, w_q_up, g_kv_lat, w_kv_up,
                             ssm_a_re, ssm_a_im, ssm_b_re, ssm_b_im, ssm_c_re, ssm_c_im, ssm_d,
                             ssm_log_step, w_glu, b_glu, g_attn_out, g_ssm_out, w_out, g_moe_norm,
                             w_group_router, b_group_router, w_expert_router, b_expert_router,
                             g_ple_norm, w_ple_gate, w_ple_proj)
        qt, k, vt, u_tm = _inproj(h, cosf, sinf, wts, bsz, seq, t["tm"], t["tq"])
        attn_n = _attention(qt, k, vt, wts["g_attn"], bsz, seq, t["tq"])
        ssm_n = _s5(u_tm.reshape(seq, bsz, SSM_WIDTH), wts, attn_n, bsz, seq, t["lc"])
        h1, xn, route, rw, cnt = _outproj(h, attn_n, ssm_n.reshape(seq, bsz * SSM_WIDTH), wts, tril,
                                          bsz, seq, t["tm"])
        dest_k, block_e, n_used, pad_rows = _slot_layout(route, cnt, n_blocks, t["te"])
        xbuf = _dispatch(dest_k, pad_rows, xn, n_slots)
        ybuf = _experts(block_e, n_used, xbuf, w_exp_gate, w_exp_up, w_exp_down, i, t["te"])
        h = _combine(dest_k, ybuf, h1, rw, p.reshape(depth, n_tok, PLE_DIM), i, wts, g_fin, t["tf"],
                     final=(i == depth - 1))
    return h.reshape(bsz, seq, D_MODEL)
```

```python
import functools
import math

import jax
import jax.numpy as jnp
from jax import lax
from jax.experimental import pallas as pl
from jax.experimental.pallas import tpu as pltpu
from jax.experimental.pallas import tpu_sc as plsc

F32 = jnp.float32
BF16 = jnp.bfloat16
I32 = jnp.int32
U32 = jnp.uint32

D_MODEL = 1024
MLA_WIDTH = 512
SSM_WIDTH = 512
V_HEAD = 64
N_HEADS = 8
QK_NOPE = 64
QK_ROPE = 32
HALF_ROPE = QK_ROPE // 2
Q_LORA = 256
KV_LORA = 128
ROPE_BASE = 10000.0
ATTN_SCALE = 1.0 / math.sqrt(QK_NOPE + QK_ROPE)
SSM_CH = 16
SSM_GROUPS = 32
SSM_STATE = 64
N_EXPERT_GROUPS = 4
EXPERTS_PER_GROUP = 8
N_EXPERTS = 32
TOP_K = 2
D_EXPERT = 256
PLE_DIM = 256
EPS = 1e-6

LANES = 128
SUBLANES = 8
HEAD_PAD = 128
ROPE_LANE0 = QK_NOPE
N_STATE = SSM_GROUPS * SSM_STATE
PACKED = D_MODEL // 2
NEG = -1e30
VMEM_LIMIT = 56 * 1024 * 1024
SUB_ROWS = 512
INPROJ_ROWS = 256
COMBINE_ROWS = 256
EXPERT_ROWS = 256
S5_ROWS = 256


def _tiles(n_tok, seq):
    return dict(
        tm=min(1024, seq),
        tq=min(512, seq),
        lc=min(64, seq),
        te=512,
        tf=min(1024, n_tok),
    )


def _rms(x, g):
    ms = jnp.mean(x * x, axis=-1, keepdims=True)
    return x * lax.rsqrt(ms + EPS) * g


def _sigmoid(x):
    return 1.0 / (1.0 + jnp.exp(-x))


def _pack_bf16_pairs(x):
    lo = lax.bitcast_convert_type(x[:, :PACKED].astype(BF16).astype(F32), U32)
    hi = lax.bitcast_convert_type(x[:, PACKED:].astype(BF16).astype(F32), U32)
    return (lo >> 16) | hi


def _unpack_bf16_pairs(w):
    lo = lax.bitcast_convert_type(w << 16, F32)
    hi = lax.bitcast_convert_type(w & jnp.uint32(0xFFFF0000), F32)
    return jnp.concatenate([lo, hi], axis=1)


def _cparams(sem):
    return pltpu.CompilerParams(dimension_semantics=sem, vmem_limit_bytes=VMEM_LIMIT)


def _full(shape):
    nd = len(shape)
    return pl.BlockSpec(shape, lambda *_: (0,) * nd)


def _rope_table_kernel(pos_ref, invf_ref, cos_ref, sin_ref):
    ang = pos_ref[...].astype(F32) * invf_ref[...]
    cos_ref[...] = jnp.cos(ang)
    sin_ref[...] = jnp.sin(ang)


def _rope_tables(pos_col, tm):
    n_tok = pos_col.shape[0]
    lane = jnp.arange(LANES)
    in_rope = (lane >= ROPE_LANE0) & (lane < ROPE_LANE0 + QK_ROPE)
    freq = ROPE_BASE ** (-((lane - ROPE_LANE0) % HALF_ROPE).astype(F32) / HALF_ROPE)
    invf = jnp.where(in_rope, freq, 0.0).astype(F32)[None, :]
    return pl.pallas_call(
        _rope_table_kernel,
        out_shape=(jax.ShapeDtypeStruct((n_tok, LANES), F32),) * 2,
        grid=(n_tok // tm,),
        in_specs=[pl.BlockSpec((tm, 1), lambda i: (i, 0)), _full((1, LANES))],
        out_specs=(pl.BlockSpec((tm, LANES), lambda i: (i, 0)),) * 2,
        compiler_params=_cparams(("parallel",)),
        name="rope_tables",
    )(pos_col, invf)


def _rope(x, cosf, s_lo, s_hi):
    n = x.shape[-1]
    return (x * cosf + pltpu.roll(x, n - HALF_ROPE, 1) * s_lo + pltpu.roll(x, HALF_ROPE, 1) * s_hi)


def _row_chunks(n_rows, sub):
    sub = min(sub, n_rows)
    return [slice(c * sub, (c + 1) * sub) for c in range(n_rows // sub)]


def _inproj_kernel(h_ref, cos_ref, sin_ref, gmix_ref, win_ref, gq_ref, wqt_ref, gkv_ref, wk_ref, wvt_ref,
                   qt_ref, k_ref, vt_ref, u_ref):
    lane = lax.broadcasted_iota(I32, (1, LANES), 1)
    feat = lax.broadcasted_iota(I32, (N_HEADS * HEAD_PAD, 1), 0)
    ones_row = jnp.where(feat % HEAD_PAD == V_HEAD, 1.0, 0.0)
    tile = qt_ref.shape[-1]
    chunks = _row_chunks(h_ref.shape[0], INPROJ_ROWS)
    zs = []
    for r in chunks:
        xn = _rms(h_ref[r, :], gmix_ref[...]).astype(BF16)
        zs.append(jnp.dot(xn, win_ref[...], preferred_element_type=F32))
    qts, ks, vts = [], [], []
    for z in zs:
        qn_t = jnp.transpose(_rms(z[:, :Q_LORA], gq_ref[...])).astype(BF16)
        qts.append(jnp.dot(wqt_ref[...], qn_t, preferred_element_type=F32))
        kvn = _rms(z[:, Q_LORA:Q_LORA + KV_LORA], gkv_ref[...])
        ks.append(jnp.dot(kvn.astype(BF16), wk_ref[...], preferred_element_type=F32))
        vts.append(jnp.dot(wvt_ref[...], jnp.transpose(kvn).astype(BF16),
                           preferred_element_type=F32))
    for r, z, qt, k, vt in zip(chunks, zs, qts, ks, vts):
        cosf = cos_ref[r, :]
        sinf = sin_ref[r, :]
        s_lo = jnp.where((lane >= ROPE_LANE0) & (lane < ROPE_LANE0 + HALF_ROPE), -sinf, 0.0)
        s_hi = jnp.where((lane >= ROPE_LANE0 + HALF_ROPE) & (lane < ROPE_LANE0 + QK_ROPE), sinf, 0.0)
        kpe = _rope(z[:, Q_LORA + KV_LORA:Q_LORA + KV_LORA + HEAD_PAD], cosf, s_lo, s_hi)
        cos_t = jnp.transpose(cosf)[ROPE_LANE0:ROPE_LANE0 + HALF_ROPE, :]
        sin_t = jnp.transpose(sinf)[ROPE_LANE0:ROPE_LANE0 + HALF_ROPE, :]
        cols = slice(r.start % tile, r.start % tile + (r.stop - r.start))
        for hd in range(N_HEADS):
            sl = slice(hd * HEAD_PAD, (hd + 1) * HEAD_PAD)
            k_ref[r, sl] = (k[:, sl] + kpe).astype(BF16)
            x1 = qt[hd * HEAD_PAD + ROPE_LANE0:hd * HEAD_PAD + ROPE_LANE0 + HALF_ROPE, :]
            x2 = qt[hd * HEAD_PAD + ROPE_LANE0 + HALF_ROPE:hd * HEAD_PAD + ROPE_LANE0 + QK_ROPE, :]
            qt_ref[r.start // tile, hd * HEAD_PAD:hd * HEAD_PAD + ROPE_LANE0, cols] = (
                qt[hd * HEAD_PAD:hd * HEAD_PAD + ROPE_LANE0, :].astype(BF16))
            qt_ref[r.start // tile, hd * HEAD_PAD + ROPE_LANE0:hd * HEAD_PAD + ROPE_LANE0 + QK_ROPE, cols] = (
                jnp.concatenate([x1 * cos_t - x2 * sin_t, x1 * sin_t + x2 * cos_t], axis=0).astype(BF16))
            qt_ref[r.start // tile, hd * HEAD_PAD + ROPE_LANE0 + QK_ROPE:(hd + 1) * HEAD_PAD, cols] = (
                qt[hd * HEAD_PAD + ROPE_LANE0 + QK_ROPE:(hd + 1) * HEAD_PAD, :].astype(BF16))
        vt_ref[r.start // tile, :, cols] = (vt + ones_row).astype(BF16)
        u_ref[r, :] = z[:, Q_LORA + KV_LORA + HEAD_PAD:]


def _inproj(h, cosf, sinf, wts, bsz, seq, tm, tq):
    n_tok = bsz * seq
    nl = seq // tm
    row = lambda b, l: (b * nl + l, 0)
    tile3 = lambda b, l: (b * nl + l, 0, 0)
    qk_w = N_HEADS * HEAD_PAD
    return pl.pallas_call(
        _inproj_kernel,
        out_shape=(jax.ShapeDtypeStruct((n_tok // tq, qk_w, tq), BF16),
                   jax.ShapeDtypeStruct((n_tok, qk_w), BF16),
                   jax.ShapeDtypeStruct((n_tok // tq, qk_w, tq), BF16),
                   jax.ShapeDtypeStruct((seq, bsz * SSM_WIDTH), F32)),
        grid=(bsz, nl),
        in_specs=[pl.BlockSpec((tm, D_MODEL), row),
                  pl.BlockSpec((tm, LANES), row), pl.BlockSpec((tm, LANES), row),
                  _full((1, D_MODEL)), _full((D_MODEL, D_MODEL)),
                  _full((1, Q_LORA)), _full((qk_w, Q_LORA)),
                  _full((1, KV_LORA)), _full((KV_LORA, qk_w)), _full((qk_w, KV_LORA))],
        out_specs=(pl.BlockSpec((tm // tq, qk_w, tq), tile3), pl.BlockSpec((tm, qk_w), row),
                   pl.BlockSpec((tm // tq, qk_w, tq), tile3),
                   pl.BlockSpec((tm, SSM_WIDTH), lambda b, l: (l, b))),
        compiler_params=_cparams(("parallel", "parallel")),
        name="inproj",
    )(h, cosf, sinf, wts["g_mix"], wts["w_in"], wts["g_q"], wts["w_q_t"], wts["g_kv"], wts["w_k"],
      wts["w_v_t"])


ATTN_RB = 16
ATTN_GROUP = 4

def _attn_kernel(qt_ref, k_ref, vt_ref, g_ref, o_ref, s_ref, p_ref, m_ref, acc_ref, out_ref, *, tq):
    qi = pl.program_id(1)
    key = lax.broadcasted_iota(I32, (ATTN_RB, tq), 0)
    qry = lax.broadcasted_iota(I32, (ATTN_RB, tq), 1)
    n_chunks = tq // ATTN_RB
    for pr in range(N_HEADS // ATTN_GROUP):
        heads = [ATTN_GROUP * pr + a for a in range(ATTN_GROUP)]
        m_ref[...] = jnp.full(m_ref.shape, NEG, F32)
        acc_ref[...] = jnp.zeros(acc_ref.shape, F32)

        def scores(j, slot, heads=heads):
            rows = pl.ds(pl.multiple_of(j * tq, tq), tq)
            for a, hd in enumerate(heads):
                hs = slice(hd * HEAD_PAD, (hd + 1) * HEAD_PAD)
                s_ref[slot, a] = jnp.dot(k_ref[rows, hs], qt_ref[hs, :], preferred_element_type=F32)

        def softmax_pv(j, slot, masked, heads=heads):
            alphas = []
            for a in range(ATTN_GROUP):
                def chunk(c, slot=slot, a=a):
                    s = s_ref[slot, a, c * ATTN_RB:(c + 1) * ATTN_RB, :]
                    if masked:
                        s = jnp.where(key + c * ATTN_RB <= qry, s, NEG)
                    return s

                top = chunk(0)
                for c in range(1, n_chunks):
                    top = jnp.maximum(top, chunk(c))
                m_old = m_ref[a]
                m_new = jnp.maximum(m_old, jnp.max(top, axis=0, keepdims=True))
                for c in range(n_chunks):
                    p_ref[a, c * ATTN_RB:(c + 1) * ATTN_RB, :] = jnp.exp2(chunk(c) - m_new).astype(BF16)
                alphas.append(jnp.exp2(m_old - m_new))
                m_ref[a] = m_new
            for a, hd in enumerate(heads):
                pv = jnp.dot(vt_ref[j, hd * HEAD_PAD:(hd + 1) * HEAD_PAD, :], p_ref[a],
                             preferred_element_type=F32)
                acc_ref[a] = alphas[a] * acc_ref[a] + pv

        def two_tiles(jj, carry):
            t = 2 * jj
            scores(t, 0)
            scores(t + 1, 1)
            softmax_pv(t, 0, masked=False)
            softmax_pv(t + 1, 1, masked=False)
            return carry

        lax.fori_loop(0, qi // 2, two_tiles, 0)

        @pl.when(qi % 2 == 0)
        def _():
            scores(qi, 0)
            softmax_pv(qi, 0, masked=True)

        @pl.when(qi % 2 == 1)
        def _():
            scores(qi - 1, 0)
            scores(qi, 1)
            softmax_pv(qi - 1, 0, masked=False)
            softmax_pv(qi, 1, masked=True)

        for a, hd in enumerate(heads):
            acc = acc_ref[a]
            out_ref[hd * V_HEAD:(hd + 1) * V_HEAD, :] = acc[:V_HEAD, :] / acc[V_HEAD:V_HEAD + 1, :]
    o_ref[...] = _rms(jnp.transpose(out_ref[...]), g_ref[...]).astype(BF16)


def _attention(qt, k, vt, g_attn, bsz, seq, tq):
    n_tok = bsz * seq
    nq = seq // tq
    qk_w = N_HEADS * HEAD_PAD
    return pl.pallas_call(
        functools.partial(_attn_kernel, tq=tq),
        out_shape=jax.ShapeDtypeStruct((n_tok, MLA_WIDTH), BF16),
        grid=(bsz, nq),
        in_specs=[pl.BlockSpec((None, qk_w, tq), lambda b, i: (b * nq + i, 0, 0)),
                  pl.BlockSpec((seq, qk_w), lambda b, i: (b, 0)),
                  pl.BlockSpec((nq, qk_w, tq), lambda b, i: (b, 0, 0)),
                  _full((1, MLA_WIDTH))],
        out_specs=pl.BlockSpec((tq, MLA_WIDTH), lambda b, i: (b * nq + i, 0)),
        scratch_shapes=[pltpu.VMEM((2, ATTN_GROUP, tq, tq), F32), pltpu.VMEM((ATTN_GROUP, tq, tq), BF16),
                        pltpu.VMEM((ATTN_GROUP, 1, tq), F32),
                        pltpu.VMEM((ATTN_GROUP, HEAD_PAD, tq), F32), pltpu.VMEM((MLA_WIDTH, tq), F32)],
        compiler_params=_cparams(("parallel", "parallel")),
        name="attn",
    )(qt, k, vt, g_attn)


S5_SLABS = 4
S5_SLAB_U = SSM_WIDTH // S5_SLABS
S5_SLAB_X = N_STATE // S5_SLABS


def _s5_kernel(u_ref, wb_ref, are_ref, aim_ref, wc_ref, d_ref, wglu_ref, bglu_ref, g_ref, after_ref,
               o_ref, xr_ref, xi_ref, sr_ref, si_ref, *, lc, nb):
    del after_ref

    @pl.when(pl.program_id(0) == 0)
    def _():
        xr_ref[...] = jnp.zeros_like(xr_ref)
        xi_ref[...] = jnp.zeros_like(xi_ref)

    rows = lc * nb
    u = u_ref[...].reshape(rows, SSM_WIDTH)
    ub = u.astype(BF16)
    def drive(j):
        bu = jnp.dot(ub[:, j * S5_SLAB_U:(j + 1) * S5_SLAB_U], wb_ref[j], preferred_element_type=F32)
        sr_ref[:, j * S5_SLAB_X:(j + 1) * S5_SLAB_X] = bu[:, :S5_SLAB_X]
        si_ref[:, j * S5_SLAB_X:(j + 1) * S5_SLAB_X] = bu[:, S5_SLAB_X:]

    ys = []
    drive(0)
    for c in range(S5_SLABS):
        if c + 1 < S5_SLABS:
            drive(c + 1)
        sl = slice(c * S5_SLAB_X, (c + 1) * S5_SLAB_X)
        ar = jnp.broadcast_to(are_ref[:, sl], (nb, S5_SLAB_X))
        ai = jnp.broadcast_to(aim_ref[:, sl], (nb, S5_SLAB_X))
        xr = xr_ref[:, sl]
        xi = xi_ref[:, sl]
        for t in range(lc):
            r = slice(t * nb, (t + 1) * nb)
            xr, xi = (ar * xr - ai * xi + sr_ref[r, sl], ar * xi + ai * xr + si_ref[r, sl])
            sr_ref[r, sl] = xr
            si_ref[r, sl] = xi
        xr_ref[:, sl] = xr
        xi_ref[:, sl] = xi
        ys.append(jnp.dot(sr_ref[:, sl].astype(BF16), wc_ref[0, c], preferred_element_type=F32)
                  + jnp.dot(si_ref[:, sl].astype(BF16), wc_ref[1, c], preferred_element_type=F32))
    y_all = jnp.concatenate(ys, axis=-1) + d_ref[...] * u
    chunks = _row_chunks(rows, S5_ROWS)
    ygs, gates = [], []
    for r in chunks:
        y = y_all[r, :]
        ygs.append(0.5 * y * (1.0 + jnp.tanh(math.sqrt(2.0 / math.pi) * (y + 0.044715 * (y * y * y)))))
    for yg in ygs:
        gates.append(jnp.dot(yg.astype(BF16), wglu_ref[...], preferred_element_type=F32) + bglu_ref[...])
    for r, yg, gate in zip(chunks, ygs, gates):
        out = _rms(yg * _sigmoid(gate), g_ref[...]).astype(BF16)
        o_ref[r.start // nb:r.stop // nb] = out.reshape((r.stop - r.start) // nb, nb, SSM_WIDTH)


def _s5(u_tm, wts, after, bsz, seq, lc):
    rows = lc * bsz
    return pl.pallas_call(
        functools.partial(_s5_kernel, lc=lc, nb=bsz),
        out_shape=jax.ShapeDtypeStruct((seq, bsz, SSM_WIDTH), BF16),
        grid=(seq // lc,),
        in_specs=[pl.BlockSpec((lc, bsz, SSM_WIDTH), lambda i: (i, 0, 0)),
                  _full((S5_SLABS, S5_SLAB_U, 2 * S5_SLAB_X)),
                  _full((1, N_STATE)), _full((1, N_STATE)),
                  _full((2, S5_SLABS, S5_SLAB_X, S5_SLAB_U)),
                  _full((1, SSM_WIDTH)), _full((SSM_WIDTH, SSM_WIDTH)), _full((1, SSM_WIDTH)),
                  _full((1, SSM_WIDTH)), pl.BlockSpec(memory_space=pl.ANY)],
        out_specs=pl.BlockSpec((lc, bsz, SSM_WIDTH), lambda i: (i, 0, 0)),
        scratch_shapes=[pltpu.VMEM((bsz, N_STATE), F32), pltpu.VMEM((bsz, N_STATE), F32),
                        pltpu.VMEM((rows, N_STATE), F32), pltpu.VMEM((rows, N_STATE), F32)],
        compiler_params=_cparams(("arbitrary",)),
        name="s5",
    )(u_tm, wts["s5_wb"], wts["s5_are"], wts["s5_aim"], wts["s5_wc"], wts["s5_d"],
      wts["w_glu"], wts["b_glu"], wts["g_ssm"], after)


ROUTE_E0 = N_EXPERT_GROUPS


def _outproj_kernel(h_ref, a_ref, s_ref, wout_ref, gmoe_ref, wrh_ref, wrl_ref, br_ref, tril_ref,
                    h1_ref, xn_ref, route_ref, rw_ref, cnt_ref):
    @pl.when((pl.program_id(0) == 0) & (pl.program_id(1) == 0))
    def _():
        cnt_ref[...] = jnp.zeros_like(cnt_ref)

    sub = tril_ref.shape[0]
    lane = lax.broadcasted_iota(I32, (sub, LANES), 1).astype(F32)
    big = float(LANES)
    chunks = _row_chunks(h_ref.shape[0], sub)
    xns = []
    for r in chunks:
        h1 = (h_ref[r, :]
              + jnp.dot(a_ref[r, :], wout_ref[:MLA_WIDTH, :], preferred_element_type=F32)
              + jnp.dot(s_ref[r, :], wout_ref[MLA_WIDTH:, :], preferred_element_type=F32))
        h1_ref[r, :] = h1
        xns.append(_rms(h1, gmoe_ref[...]))
    lgs = []
    for r, xn in zip(chunks, xns):
        xn_ref[r, :] = _pack_bf16_pairs(xn)
        x_hi = xn.astype(BF16)
        x_lo = (xn - x_hi.astype(F32)).astype(BF16)
        lgs.append((jnp.dot(x_hi, wrh_ref[...], preferred_element_type=F32)
                    + (jnp.dot(x_lo, wrh_ref[...], preferred_element_type=F32)
                       + jnp.dot(x_hi, wrl_ref[...], preferred_element_type=F32))) + br_ref[...])
    for r, lg in zip(chunks, lgs):
        gl = jnp.where(lane < N_EXPERT_GROUPS, lg, NEG)
        gmax = jnp.max(gl, axis=-1, keepdims=True)
        gsum = jnp.sum(jnp.exp(gl - gmax), axis=-1, keepdims=True)
        g_top_p = 1.0 / gsum
        gidx = jnp.min(jnp.where(gl == gmax, lane, big), axis=-1, keepdims=True)
        lo = ROUTE_E0 + EXPERTS_PER_GROUP * gidx
        sel = jnp.where((lane >= lo) & (lane < lo + EXPERTS_PER_GROUP), lg, NEG)
        m1 = jnp.max(sel, axis=-1, keepdims=True)
        i1 = jnp.min(jnp.where(sel == m1, lane, big), axis=-1, keepdims=True)
        sel2 = jnp.where(lane == i1, NEG, sel)
        m2 = jnp.max(sel2, axis=-1, keepdims=True)
        i2 = jnp.min(jnp.where(sel2 == m2, lane, big), axis=-1, keepdims=True)
        e21 = jnp.exp(m2 - m1)
        w1 = g_top_p / (1.0 + e21)
        w2 = g_top_p * e21 / (1.0 + e21)
        e1 = i1 - ROUTE_E0
        e2 = i2 - ROUTE_E0
        oh = jnp.where((lane == e1) | (lane == e2), 1.0, 0.0)
        before = cnt_ref[...] + jnp.dot(tril_ref[...], oh.astype(BF16), preferred_element_type=F32)
        r1 = jnp.sum(jnp.where(lane == e1, before, 0.0), axis=-1, keepdims=True)
        r2 = jnp.sum(jnp.where(lane == e2, before, 0.0), axis=-1, keepdims=True)
        cnt_ref[...] = cnt_ref[...] + jnp.sum(oh, axis=0, keepdims=True)
        route = jnp.where(lane == 0, e1, jnp.where(lane == 1, e2, jnp.where(lane == 2, r1, r2)))
        route_ref[:, r] = jnp.transpose(route)[:SUBLANES, :].astype(I32)
        rw_ref[r, :] = jnp.where(lane == 0, w1, w2)


def _outproj(h, attn_n, ssm_tm, wts, tril, bsz, seq, tm):
    n_tok = bsz * seq
    nl = seq // tm
    row = lambda b, l: (b * nl + l, 0)
    return pl.pallas_call(
        _outproj_kernel,
        out_shape=(jax.ShapeDtypeStruct((n_tok, D_MODEL), F32),
                   jax.ShapeDtypeStruct((n_tok, PACKED), U32),
                   jax.ShapeDtypeStruct((SUBLANES, n_tok), I32),
                   jax.ShapeDtypeStruct((n_tok, LANES), F32),
                   jax.ShapeDtypeStruct((1, LANES), F32)),
        grid=(bsz, nl),
        in_specs=[pl.BlockSpec((tm, D_MODEL), row),
                  pl.BlockSpec((tm, MLA_WIDTH), row),
                  pl.BlockSpec((tm, SSM_WIDTH), lambda b, l: (l, b)),
                  _full((D_MODEL, D_MODEL)), _full((1, D_MODEL)),
                  _full((D_MODEL, LANES)), _full((D_MODEL, LANES)), _full((1, LANES)),
                  _full(tril.shape)],
        out_specs=(pl.BlockSpec((tm, D_MODEL), row), pl.BlockSpec((tm, PACKED), row),
                   pl.BlockSpec((SUBLANES, tm), lambda b, l: (0, b * nl + l)),
                   pl.BlockSpec((tm, LANES), row),
                   _full((1, LANES))),
        compiler_params=_cparams(("arbitrary", "arbitrary")),
        name="outproj",
    )(h, attn_n, ssm_tm, wts["w_out"], wts["g_moe"], wts["w_r_hi"], wts["w_r_lo"], wts["b_r"], tril)


SC_CORES = 2
SC_SUBCORES = 16
SC_WORKERS = SC_CORES * SC_SUBCORES
SC_CHUNK = 64


def _sc_mesh():
    return plsc.VectorSubcoreMesh(core_axis_name="c", subcore_axis_name="s",
                                  num_cores=SC_CORES, num_subcores=SC_SUBCORES)


def _sc_worker_id():
    return lax.axis_index("s") * SC_CORES + lax.axis_index("c")


def _sc_two_slot_pipeline(n_chunks, load, stores):
    assert n_chunks % 2 == 0
    load(0, 0).start()

    @pl.loop(0, n_chunks // 2)
    def _(jj):
        for slot in (0, 1):
            j = 2 * jj + slot

            @pl.when(j >= 1)
            def _():
                for cp in stores(j - 1, 1 - slot):
                    cp.wait()

            @pl.when(j + 1 < n_chunks)
            def _():
                load(j + 1, 1 - slot).start()

            load(j, slot).wait()
            for cp in stores(j, slot):
                cp.start()

    for cp in stores(n_chunks - 1, 1):
        cp.wait()


def _sc_rows(j):
    return pl.ds(pl.multiple_of(j * SC_CHUNK, SC_CHUNK), SC_CHUNK)


def _dispatch(dest_k, pad_rows, xn, n_slots):
    n_tok, width = xn.shape
    n_pad = pad_rows.shape[0]
    per_w = n_tok // SC_WORKERS
    per_p = n_pad // SC_WORKERS
    assert per_p <= per_w

    @functools.partial(
        pl.kernel, mesh=_sc_mesh(),
        out_type=jax.ShapeDtypeStruct((n_slots, width), xn.dtype),
        scratch_types=[pltpu.VMEM((per_w,), I32), pltpu.VMEM((per_w,), I32), pltpu.VMEM((per_p,), I32),
                       pltpu.VMEM((2, SC_CHUNK, width), xn.dtype),
                       pltpu.SemaphoreType.DMA((2,)), pltpu.SemaphoreType.DMA((2, TOP_K))],
        name="sc_dispatch_rows",
    )
    def scatter_kernel(xn_hbm, d0_hbm, d1_hbm, dp_hbm, out_hbm, i0_v, i1_v, ip_v, rows_v, lsem, ssem):
        wid = _sc_worker_id()
        base = wid * per_w
        pltpu.sync_copy(d0_hbm.at[pl.ds(base, per_w)], i0_v)
        pltpu.sync_copy(d1_hbm.at[pl.ds(base, per_w)], i1_v)
        pltpu.sync_copy(dp_hbm.at[pl.ds(wid * per_p, per_p)], ip_v)

        def load(j, slot):
            return pltpu.make_async_copy(xn_hbm.at[pl.ds(base + j * SC_CHUNK, SC_CHUNK)],
                                         rows_v.at[slot], lsem.at[slot])

        def scatter(idx_v, k):
            return lambda j, slot: pltpu.make_async_copy(
                rows_v.at[slot], out_hbm.at[idx_v.at[_sc_rows(j)]], ssem.at[slot, k])

        to_k0, to_k1, to_pad = scatter(i0_v, 0), scatter(i1_v, 1), scatter(ip_v, 0)
        _sc_two_slot_pipeline(per_w // SC_CHUNK, load,
                              lambda j, slot: [to_k0(j, slot), to_k1(j, slot)])
        _sc_two_slot_pipeline(per_p // SC_CHUNK, load, lambda j, slot: [to_pad(j, slot)])

    return scatter_kernel(xn, dest_k[0], dest_k[1], pad_rows)


EXPERT_BLOCKS = 2


def _experts_kernel(be_ref, nused_ref, x_ref, *refs):
    del be_ref
    w_refs, y_ref = refs[:-1], refs[-1]
    te = x_ref.shape[0] // EXPERT_BLOCKS
    for b in range(EXPERT_BLOCKS):
        blk = pl.program_id(0) * EXPERT_BLOCKS + b
        wg_ref, wu_ref, wd_ref = w_refs[3 * b:3 * b + 3]
        rows = slice(b * te, (b + 1) * te)

        @pl.when(blk >= nused_ref[0])
        def _(rows=rows):
            y_ref[rows, :] = jnp.zeros((te, PACKED), U32)

        @pl.when(blk < nused_ref[0])
        def _(rows=rows, wg_ref=wg_ref, wu_ref=wu_ref, wd_ref=wd_ref):
            wg = wg_ref[...].astype(BF16)
            wu = wu_ref[...].astype(BF16)
            wd = wd_ref[...].astype(BF16)
            chunks = [slice(rows.start + r.start, rows.start + r.stop) for r in _row_chunks(te, EXPERT_ROWS)]
            gus = []
            for r in chunks:
                xb = _unpack_bf16_pairs(x_ref[r, :]).astype(BF16)
                gus.append((jnp.dot(xb, wg, preferred_element_type=F32),
                            jnp.dot(xb, wu, preferred_element_type=F32)))
            ys = []
            for g, u in gus:
                hdn = (g * _sigmoid(g) * u).astype(BF16)
                ys.append(jnp.dot(hdn, wd, preferred_element_type=F32))
            for r, y in zip(chunks, ys):
                y_ref[r, :] = _pack_bf16_pairs(y)


def _experts(block_e, n_used, xbuf, w_gate, w_up, w_down, layer, te):
    n_slots = xbuf.shape[0]
    step_rows = EXPERT_BLOCKS * te

    def weights(b):
        expert = lambda i, be, nu: (layer, be[EXPERT_BLOCKS * i + b], 0, 0)
        return [pl.BlockSpec((None, None, D_MODEL, D_EXPERT), expert),
                pl.BlockSpec((None, None, D_MODEL, D_EXPERT), expert),
                pl.BlockSpec((None, None, D_EXPERT, D_MODEL), expert)]

    grid_spec = pltpu.PrefetchScalarGridSpec(
        num_scalar_prefetch=2,
        grid=(n_slots // step_rows,),
        in_specs=[pl.BlockSpec((step_rows, PACKED), lambda i, be, nu: (i, 0))]
        + [spec for b in range(EXPERT_BLOCKS) for spec in weights(b)],
        out_specs=pl.BlockSpec((step_rows, PACKED), lambda i, be, nu: (i, 0)),
    )
    return pl.pallas_call(
        _experts_kernel,
        out_shape=jax.ShapeDtypeStruct((n_slots, PACKED), U32),
        grid_spec=grid_spec,
        compiler_params=_cparams(("arbitrary",)),
        name="experts",
    )(block_e, n_used, xbuf, *([w_gate, w_up, w_down] * EXPERT_BLOCKS))


def _sc_gather_rows(table, idx):
    n_rows = idx.shape[0]
    width = table.shape[1]
    per_w = n_rows // SC_WORKERS

    @functools.partial(
        pl.kernel, mesh=_sc_mesh(),
        out_type=jax.ShapeDtypeStruct((n_rows, width), table.dtype),
        scratch_types=[pltpu.VMEM((per_w,), I32), pltpu.VMEM((2, SC_CHUNK, width), table.dtype),
                       pltpu.SemaphoreType.DMA((2,)), pltpu.SemaphoreType.DMA((2,))],
        name="sc_gather_rows",
    )
    def gather_kernel(table_hbm, idx_hbm, out_hbm, idx_v, rows_v, lsem, ssem):
        base = _sc_worker_id() * per_w
        pltpu.sync_copy(idx_hbm.at[pl.ds(base, per_w)], idx_v)

        def load(j, slot):
            return pltpu.make_async_copy(table_hbm.at[idx_v.at[_sc_rows(j)]], rows_v.at[slot],
                                         lsem.at[slot])

        def store(j, slot):
            return [pltpu.make_async_copy(rows_v.at[slot],
                                          out_hbm.at[pl.ds(base + j * SC_CHUNK, SC_CHUNK)],
                                          ssem.at[slot])]

        _sc_two_slot_pipeline(per_w // SC_CHUNK, load, store)

    return gather_kernel(table, idx)


def _combine_kernel(y0_ref, y1_ref, h1_ref, rw_ref, p_ref, gple_ref, wgate_ref, wproj_ref,
                    gfin_ref, o_ref, *, final):
    chunks = _row_chunks(h1_ref.shape[0], COMBINE_ROWS)
    h2s, gates, ples = [], [], []
    for r in chunks:
        rw = rw_ref[r, :]
        h2s.append(h1_ref[r, :] + rw[:, 0:1] * _unpack_bf16_pairs(y0_ref[r, :])
                   + rw[:, 1:2] * _unpack_bf16_pairs(y1_ref[r, :]))
    for r, h2 in zip(chunks, h2s):
        hn = _rms(h2, gple_ref[...]).astype(BF16)
        gates.append(jnp.dot(hn, wgate_ref[...], preferred_element_type=F32))
        ples.append(jnp.dot(p_ref[r, :].astype(BF16), wproj_ref[...], preferred_element_type=F32))
    for r, h2, gate, ple in zip(chunks, h2s, gates, ples):
        h3 = h2 + ple * _sigmoid(gate)
        if final:
            h3 = _rms(h3, gfin_ref[...])
        o_ref[r, :] = h3


def _combine(dest_k, ybuf, h1, rw, p_all, layer, wts, g_final, tf, final):
    n_tok = h1.shape[0]
    n_steps = n_tok // tf
    row = lambda i: (i, 0)
    yg = _sc_gather_rows(ybuf, dest_k.reshape(-1))
    return pl.pallas_call(
        functools.partial(_combine_kernel, final=final),
        out_shape=jax.ShapeDtypeStruct((n_tok, D_MODEL), F32),
        grid=(n_steps,),
        in_specs=[pl.BlockSpec((tf, PACKED), row),
                  pl.BlockSpec((tf, PACKED), lambda i: (n_steps + i, 0)),
                  pl.BlockSpec((tf, D_MODEL), row), pl.BlockSpec((tf, LANES), row),
                  pl.BlockSpec((None, tf, PLE_DIM), lambda i: (layer, i, 0)),
                  _full((1, D_MODEL)), _full((D_MODEL, D_MODEL)), _full((PLE_DIM, D_MODEL)),
                  _full((1, D_MODEL))],
        out_specs=pl.BlockSpec((tf, D_MODEL), row),
        compiler_params=_cparams(("parallel",)),
        name="combine",
    )(yg, yg, h1, rw, p_all, wts["g_ple"], wts["w_ple_gate"], wts["w_ple_proj"], g_final)


def _layer_weights(i, g_mix_norm, w_in, g_q_lat, w_q_up, g_kv_lat, w_kv_up,
                   ssm_a_re, ssm_a_im, ssm_b_re, ssm_b_im, ssm_c_re, ssm_c_im, ssm_d, ssm_log_step,
                   w_glu, b_glu, g_attn_out, g_ssm_out, w_out, g_moe_norm,
                   w_group_router, b_group_router, w_expert_router, b_expert_router,
                   g_ple_norm, w_ple_gate, w_ple_proj):
    off_kv, off_kr, off_ssm = Q_LORA, Q_LORA + KV_LORA, Q_LORA + KV_LORA + QK_ROPE
    wi = w_in[i]
    kr_block = jnp.zeros((D_MODEL, HEAD_PAD), F32).at[:, ROPE_LANE0:ROPE_LANE0 + QK_ROPE].set(wi[:, off_kr:off_ssm])
    w_in_p = jnp.concatenate([wi[:, :off_kr], kr_block, wi[:, off_ssm:]], axis=1)
    wq = w_q_up[i].reshape(Q_LORA, N_HEADS, QK_NOPE + QK_ROPE) * (ATTN_SCALE * math.log2(math.e))
    wq = jnp.pad(wq, ((0, 0), (0, 0), (0, HEAD_PAD - QK_NOPE - QK_ROPE))).reshape(Q_LORA, N_HEADS * HEAD_PAD)
    wkv = w_kv_up[i].reshape(KV_LORA, N_HEADS, QK_NOPE + V_HEAD)
    wk = jnp.pad(wkv[..., :QK_NOPE], ((0, 0), (0, 0), (0, HEAD_PAD - QK_NOPE))).reshape(KV_LORA, N_HEADS * HEAD_PAD)
    wv = jnp.pad(wkv[..., QK_NOPE:], ((0, 0), (0, 0), (0, HEAD_PAD - V_HEAD))).reshape(KV_LORA, N_HEADS * HEAD_PAD)
    lam = lax.complex(ssm_a_re[i], ssm_a_im[i])
    lam_bar = jnp.exp(lam * jnp.exp(ssm_log_step[i])[:, None])
    b_fac = (lam_bar - 1.0) / lam
    bc = lax.complex(ssm_b_re[i], ssm_b_im[i]) * b_fac[:, :, None]
    n_per = SSM_GROUPS // S5_SLABS

    def block_diag(blocks):
        s, n, a, b = blocks.shape
        eye = jnp.eye(n, dtype=blocks.dtype)
        return jnp.einsum("snab,nm->snamb", blocks, eye).reshape(s, n * a, n * b)

    b_hp_re = jnp.real(bc).transpose(0, 2, 1).reshape(S5_SLABS, n_per, SSM_CH, SSM_STATE)
    b_hp_im = jnp.imag(bc).transpose(0, 2, 1).reshape(S5_SLABS, n_per, SSM_CH, SSM_STATE)
    s5_wb = jnp.concatenate([block_diag(b_hp_re), block_diag(b_hp_im)], axis=-1)
    c_re = ssm_c_re[i].transpose(0, 2, 1).reshape(S5_SLABS, n_per, SSM_STATE, SSM_CH)
    c_im = ssm_c_im[i].transpose(0, 2, 1).reshape(S5_SLABS, n_per, SSM_STATE, SSM_CH)
    s5_wc = jnp.stack([block_diag(c_re), -block_diag(c_im)])
    w_r = jnp.zeros((D_MODEL, LANES), F32)
    w_r = w_r.at[:, :N_EXPERT_GROUPS].set(w_group_router[i])
    w_r = w_r.at[:, ROUTE_E0:ROUTE_E0 + N_EXPERTS].set(w_expert_router[i])
    w_r_hi = w_r.astype(BF16)
    w_r_lo = (w_r - w_r_hi.astype(F32)).astype(BF16)
    b_r = jnp.zeros((1, LANES), F32)
    b_r = b_r.at[0, :N_EXPERT_GROUPS].set(b_group_router[i])
    b_r = b_r.at[0, ROUTE_E0:ROUTE_E0 + N_EXPERTS].set(b_expert_router[i])
    return dict(
        g_mix=g_mix_norm[i][None], w_in=w_in_p.astype(BF16),
        g_q=g_q_lat[i][None], w_q_t=wq.T.astype(BF16),
        g_kv=g_kv_lat[i][None], w_k=wk.astype(BF16), w_v_t=wv.T.astype(BF16),
        s5_wb=s5_wb.astype(BF16), s5_wc=s5_wc.astype(BF16),
        s5_are=jnp.real(lam_bar).reshape(1, N_STATE), s5_aim=jnp.imag(lam_bar).reshape(1, N_STATE),
        s5_d=ssm_d[i].reshape(1, SSM_WIDTH), w_glu=w_glu[i].astype(BF16), b_glu=b_glu[i][None],
        g_attn=g_attn_out[i][None], g_ssm=g_ssm_out[i][None],
        w_out=w_out[i].astype(BF16), g_moe=g_moe_norm[i][None],
        w_r_hi=w_r_hi, w_r_lo=w_r_lo, b_r=b_r,
        g_ple=g_ple_norm[i][None], w_ple_gate=w_ple_gate[i].astype(BF16),
        w_ple_proj=w_ple_proj[i].astype(BF16),
    )


def _slot_layout(route, cnt, n_blocks, te):
    counts = cnt[0, :N_EXPERTS].astype(I32)
    pcounts = (counts + te - 1) // te * te
    pends = jnp.cumsum(pcounts)
    pstarts = pends - pcounts
    eid = route[:TOP_K]
    rank = route[TOP_K:2 * TOP_K]
    start = jnp.sum(jnp.where(eid[:, :, None] == jnp.arange(N_EXPERTS)[None, None, :],
                              pstarts[None, None, :], 0), axis=-1)
    dest_k = (start + rank).astype(I32)
    blk0 = jnp.arange(n_blocks, dtype=I32) * te
    block_e = jnp.minimum(jnp.sum((pends[None, :] <= blk0[:, None]).astype(I32), axis=1), N_EXPERTS - 1)
    n_slots = n_blocks * te
    gap_start = jnp.concatenate([pstarts + counts, pends[-1:]])
    gap_len = jnp.concatenate([pcounts - counts, n_slots - pends[-1:]])
    gap_end = jnp.cumsum(gap_len)
    j = jnp.arange(n_slots - eid.size, dtype=I32)
    gap = jnp.sum((gap_end[None, :] <= j[:, None]).astype(I32), axis=1)
    onehot = gap[:, None] == jnp.arange(N_EXPERTS + 1)[None, :]
    pad_rows = j + jnp.sum(jnp.where(onehot, (gap_start - (gap_end - gap_len))[None, :], 0), axis=1)
    n_used = (pends[-1:] // te).astype(I32)
    return dest_k, block_e, n_used, pad_rows.astype(I32)


def kernel(x, p, positions, g_mix_norm, w_in, g_q_lat, w_q_up, g_kv_lat, w_kv_up, ssm_a_re, ssm_a_im, ssm_b_re, ssm_b_im, ssm_c_re, ssm_c_im, ssm_d, ssm_log_step, w_glu, b_glu, g_attn_out, g_ssm_out, w_out, g_moe_norm, w_group_router, b_group_router, w_expert_router, b_expert_router, w_exp_gate, w_exp_up, w_exp_down, g_ple_norm, w_ple_gate, w_ple_proj, g_final):
    bsz, seq, _ = x.shape
    depth = p.shape[0]
    n_tok = bsz * seq
    t = _tiles(n_tok, seq)
    n_slots = TOP_K * n_tok + N_EXPERTS * t["te"]
    n_blocks = n_slots // t["te"]

    cosf, sinf = _rope_tables(positions.reshape(n_tok, 1), t["tm"])
    sub = min(SUB_ROWS, t["tm"])
    tril = jnp.tril(jnp.ones((sub, sub), F32), k=-1).astype(BF16)
    g_fin = g_final[None]

    h = x.reshape(n_tok, D_MODEL)
    for i in range(depth):
        wts = _layer_weights(i, g_mix_norm, w_in, g_q_lat, w_q_up, g_kv_lat, w_kv_up,
                             ssm_a_re, ssm_a_im, ssm_b_re, ssm_b_im, ssm_c_re, ssm_c_im, ssm_d,
                             ssm_log_step, w_glu, b_glu, g_attn_out, g_ssm_out, w_out, g_moe_norm,
                             w_group_router, b_group_router, w_expert_router, b_expert_router,
                             g_ple_norm, w_ple_gate, w_ple_proj)
        qt, k, vt, u_tm = _inproj(h, cosf, sinf, wts, bsz, seq, t["tm"], t["tq"])
        attn_n = _attention(qt, k, vt, wts["g_attn"], bsz, seq, t["tq"])
        ssm_n = _s5(u_tm.reshape(seq, bsz, SSM_WIDTH), wts, attn_n, bsz, seq, t["lc"])
        h1, xn, route, rw, cnt = _outproj(h, attn_n, ssm_n.reshape(seq, bsz * SSM_WIDTH), wts, tril,
                                          bsz, seq, t["tm"])
        dest_k, block_e, n_used, pad_rows = _slot_layout(route, cnt, n_blocks, t["te"])
        xbuf = _dispatch(dest_k, pad_rows, xn, n_slots)
        ybuf = _experts(block_e, n_used, xbuf, w_exp_gate, w_exp_up, w_exp_down, i, t["te"])
        h = _combine(dest_k, ybuf, h1, rw, p.reshape(depth, n_tok, PLE_DIM), i, wts, g_fin, t["tf"],
                     final=(i == depth - 1))
    return h.reshape(bsz, seq, D_MODEL)
```

```python
import functools
import math

import jax
import jax.numpy as jnp
from jax import lax
from jax.experimental import pallas as pl
from jax.experimental.pallas import tpu as pltpu
from jax.experimental.pallas import tpu_sc as plsc

F32 = jnp.float32
BF16 = jnp.bfloat16
I32 = jnp.int32
U32 = jnp.uint32

D_MODEL = 1024
MLA_WIDTH = 512
SSM_WIDTH = 512
V_HEAD = 64
N_HEADS = 8
QK_NOPE = 64
QK_ROPE = 32
HALF_ROPE = QK_ROPE // 2
Q_LORA = 256
KV_LORA = 128
ROPE_BASE = 10000.0
ATTN_SCALE = 1.0 / math.sqrt(QK_NOPE + QK_ROPE)
SSM_CH = 16
SSM_GROUPS = 32
SSM_STATE = 64
N_EXPERT_GROUPS = 4
EXPERTS_PER_GROUP = 8
N_EXPERTS = 32
TOP_K = 2
D_EXPERT = 256
PLE_DIM = 256
EPS = 1e-6

LANES = 128
SUBLANES = 8
HEAD_PAD = 128
ROPE_LANE0 = QK_NOPE
N_STATE = SSM_GROUPS * SSM_STATE
PACKED = D_MODEL // 2
NEG = -1e30
VMEM_LIMIT = 56 * 1024 * 1024
SUB_ROWS = 512
INPROJ_ROWS = 256
COMBINE_ROWS = 256
EXPERT_ROWS = 256
S5_ROWS = 256


def _tiles(n_tok, seq):
    return dict(
        tm=min(1024, seq),
        tq=min(512, seq),
        lc=min(64, seq),
        te=512,
        tf=min(1024, n_tok),
    )


def _rms(x, g):
    ms = jnp.mean(x * x, axis=-1, keepdims=True)
    return x * lax.rsqrt(ms + EPS) * g


def _sigmoid(x):
    return 1.0 / (1.0 + jnp.exp(-x))


def _pack_bf16_pairs(x):
    lo = lax.bitcast_convert_type(x[:, :PACKED].astype(BF16).astype(F32), U32)
    hi = lax.bitcast_convert_type(x[:, PACKED:].astype(BF16).astype(F32), U32)
    return (lo >> 16) | hi


def _unpack_bf16_pairs(w):
    lo = lax.bitcast_convert_type(w << 16, F32)
    hi = lax.bitcast_convert_type(w & jnp.uint32(0xFFFF0000), F32)
    return jnp.concatenate([lo, hi], axis=1)


def _cparams(sem):
    return pltpu.CompilerParams(dimension_semantics=sem, vmem_limit_bytes=VMEM_LIMIT)


def _full(shape):
    nd = len(shape)
    return pl.BlockSpec(shape, lambda *_: (0,) * nd)


def _rope_table_kernel(pos_ref, invf_ref, cos_ref, sin_ref):
    ang = pos_ref[...].astype(F32) * invf_ref[...]
    cos_ref[...] = jnp.cos(ang)
    sin_ref[...] = jnp.sin(ang)


def _rope_tables(pos_col, tm):
    n_tok = pos_col.shape[0]
    lane = jnp.arange(LANES)
    in_rope = (lane >= ROPE_LANE0) & (lane < ROPE_LANE0 + QK_ROPE)
    freq = ROPE_BASE ** (-((lane - ROPE_LANE0) % HALF_ROPE).astype(F32) / HALF_ROPE)
    invf = jnp.where(in_rope, freq, 0.0).astype(F32)[None, :]
    return pl.pallas_call(
        _rope_table_kernel,
        out_shape=(jax.ShapeDtypeStruct((n_tok, LANES), F32),) * 2,
        grid=(n_tok // tm,),
        in_specs=[pl.BlockSpec((tm, 1), lambda i: (i, 0)), _full((1, LANES))],
        out_specs=(pl.BlockSpec((tm, LANES), lambda i: (i, 0)),) * 2,
        compiler_params=_cparams(("parallel",)),
        name="rope_tables",
    )(pos_col, invf)


def _rope(x, cosf, s_lo, s_hi):
    n = x.shape[-1]
    return (x * cosf + pltpu.roll(x, n - HALF_ROPE, 1) * s_lo + pltpu.roll(x, HALF_ROPE, 1) * s_hi)


def _row_chunks(n_rows, sub):
    sub = min(sub, n_rows)
    return [slice(c * sub, (c + 1) * sub) for c in range(n_rows // sub)]


def _inproj_kernel(h_ref, cos_ref, sin_ref, gmix_ref, win_ref, gq_ref, wqt_ref, gkv_ref, wk_ref, wvt_ref,
                   qt_ref, k_ref, vt_ref, u_ref):
    lane = lax.broadcasted_iota(I32, (1, LANES), 1)
    feat = lax.broadcasted_iota(I32, (N_HEADS * HEAD_PAD, 1), 0)
    ones_row = jnp.where(feat % HEAD_PAD == V_HEAD, 1.0, 0.0)
    tile = qt_ref.shape[-1]
    chunks = _row_chunks(h_ref.shape[0], INPROJ_ROWS)
    zs = []
    for r in chunks:
        xn = _rms(h_ref[r, :], gmix_ref[...]).astype(BF16)
        zs.append(jnp.dot(xn, win_ref[...], preferred_element_type=F32))
    qts, ks, vts = [], [], []
    for z in zs:
        qn_t = jnp.transpose(_rms(z[:, :Q_LORA], gq_ref[...])).astype(BF16)
        qts.append(jnp.dot(wqt_ref[...], qn_t, preferred_element_type=F32))
        kvn = _rms(z[:, Q_LORA:Q_LORA + KV_LORA], gkv_ref[...])
        ks.append(jnp.dot(kvn.astype(BF16), wk_ref[...], preferred_element_type=F32))
        vts.append(jnp.dot(wvt_ref[...], jnp.transpose(kvn).astype(BF16),
                           preferred_element_type=F32))
    for r, z, qt, k, vt in zip(chunks, zs, qts, ks, vts):
        cosf = cos_ref[r, :]
        sinf = sin_ref[r, :]
        s_lo = jnp.where((lane >= ROPE_LANE0) & (lane < ROPE_LANE0 + HALF_ROPE), -sinf, 0.0)
        s_hi = jnp.where((lane >= ROPE_LANE0 + HALF_ROPE) & (lane < ROPE_LANE0 + QK_ROPE), sinf, 0.0)
        kpe = _rope(z[:, Q_LORA + KV_LORA:Q_LORA + KV_LORA + HEAD_PAD], cosf, s_lo, s_hi)
        cos_t = jnp.transpose(cosf)[ROPE_LANE0:ROPE_LANE0 + HALF_ROPE, :]
        sin_t = jnp.transpose(sinf)[ROPE_LANE0:ROPE_LANE0 + HALF_ROPE, :]
        cols = slice(r.start % tile, r.start % tile + (r.stop - r.start))
        for hd in range(N_HEADS):
            sl = slice(hd * HEAD_PAD, (hd + 1) * HEAD_PAD)
            k_ref[r, sl] = (k[:, sl] + kpe).astype(BF16)
            x1 = qt[hd * HEAD_PAD + ROPE_LANE0:hd * HEAD_PAD + ROPE_LANE0 + HALF_ROPE, :]
            x2 = qt[hd * HEAD_PAD + ROPE_LANE0 + HALF_ROPE:hd * HEAD_PAD + ROPE_LANE0 + QK_ROPE, :]
            qt_ref[r.start // tile, hd * HEAD_PAD:hd * HEAD_PAD + ROPE_LANE0, cols] = (
                qt[hd * HEAD_PAD:hd * HEAD_PAD + ROPE_LANE0, :].astype(BF16))
            qt_ref[r.start // tile, hd * HEAD_PAD + ROPE_LANE0:hd * HEAD_PAD + ROPE_LANE0 + QK_ROPE, cols] = (
                jnp.concatenate([x1 * cos_t - x2 * sin_t, x1 * sin_t + x2 * cos_t], axis=0).astype(BF16))
            qt_ref[r.start // tile, hd * HEAD_PAD + ROPE_LANE0 + QK_ROPE:(hd + 1) * HEAD_PAD, cols] = (
                qt[hd * HEAD_PAD + ROPE_LANE0 + QK_ROPE:(hd + 1) * HEAD_PAD, :].astype(BF16))
        vt_ref[r.start // tile, :, cols] = (vt + ones_row).astype(BF16)
        u_ref[r, :] = z[:, Q_LORA + KV_LORA + HEAD_PAD:]


def _inproj(h, cosf, sinf, wts, bsz, seq, tm, tq):
    n_tok = bsz * seq
    nl = seq // tm
    row = lambda b, l: (b * nl + l, 0)
    tile3 = lambda b, l: (b * nl + l, 0, 0)
    qk_w = N_HEADS * HEAD_PAD
    return pl.pallas_call(
        _inproj_kernel,
        out_shape=(jax.ShapeDtypeStruct((n_tok // tq, qk_w, tq), BF16),
                   jax.ShapeDtypeStruct((n_tok, qk_w), BF16),
                   jax.ShapeDtypeStruct((n_tok // tq, qk_w, tq), BF16),
                   jax.ShapeDtypeStruct((seq, bsz * SSM_WIDTH), F32)),
        grid=(bsz, nl),
        in_specs=[pl.BlockSpec((tm, D_MODEL), row),
                  pl.BlockSpec((tm, LANES), row), pl.BlockSpec((tm, LANES), row),
                  _full((1, D_MODEL)), _full((D_MODEL, D_MODEL)),
                  _full((1, Q_LORA)), _full((qk_w, Q_LORA)),
                  _full((1, KV_LORA)), _full((KV_LORA, qk_w)), _full((qk_w, KV_LORA))],
        out_specs=(pl.BlockSpec((tm // tq, qk_w, tq), tile3), pl.BlockSpec((tm, qk_w), row),
                   pl.BlockSpec((tm // tq, qk_w, tq), tile3),
                   pl.BlockSpec((tm, SSM_WIDTH), lambda b, l: (l, b))),
        compiler_params=_cparams(("parallel", "parallel")),
        name="inproj",
    )(h, cosf, sinf, wts["g_mix"], wts["w_in"], wts["g_q"], wts["w_q_t"], wts["g_kv"], wts["w_k"],
      wts["w_v_t"])


ATTN_RB = 16
ATTN_GROUP = 8

def _attn_kernel(qt_ref, k_ref, vt_ref, g_ref, o_ref, s_ref, p_ref, m_ref, acc_ref, out_ref, *, tq):
    qi = pl.program_id(1)
    key = lax.broadcasted_iota(I32, (ATTN_RB, tq), 0)
    qry = lax.broadcasted_iota(I32, (ATTN_RB, tq), 1)
    n_chunks = tq // ATTN_RB
    for pr in range(N_HEADS // ATTN_GROUP):
        heads = [ATTN_GROUP * pr + a for a in range(ATTN_GROUP)]
        m_ref[...] = jnp.full(m_ref.shape, NEG, F32)
        acc_ref[...] = jnp.zeros(acc_ref.shape, F32)

        def scores(j, slot, heads=heads):
            rows = pl.ds(pl.multiple_of(j * tq, tq), tq)
            for a, hd in enumerate(heads):
                hs = slice(hd * HEAD_PAD, (hd + 1) * HEAD_PAD)
                s_ref[slot, a] = jnp.dot(k_ref[rows, hs], qt_ref[hs, :], preferred_element_type=F32)

        def softmax_pv(j, slot, masked, heads=heads):
            alphas = []
            for a in range(ATTN_GROUP):
                def chunk(c, slot=slot, a=a):
                    s = s_ref[slot, a, c * ATTN_RB:(c + 1) * ATTN_RB, :]
                    if masked:
                        s = jnp.where(key + c * ATTN_RB <= qry, s, NEG)
                    return s

                top = chunk(0)
                for c in range(1, n_chunks):
                    top = jnp.maximum(top, chunk(c))
                m_old = m_ref[a]
                m_new = jnp.maximum(m_old, jnp.max(top, axis=0, keepdims=True))
                for c in range(n_chunks):
                    p_ref[a, c * ATTN_RB:(c + 1) * ATTN_RB, :] = jnp.exp2(chunk(c) - m_new).astype(BF16)
                alphas.append(jnp.exp2(m_old - m_new))
                m_ref[a] = m_new
            for a, hd in enumerate(heads):
                pv = jnp.dot(vt_ref[j, hd * HEAD_PAD:(hd + 1) * HEAD_PAD, :], p_ref[a],
                             preferred_element_type=F32)
                acc_ref[a] = alphas[a] * acc_ref[a] + pv

        def two_tiles(jj, carry):
            t = 2 * jj
            scores(t, 0)
            scores(t + 1, 1)
            softmax_pv(t, 0, masked=False)
            softmax_pv(t + 1, 1, masked=False)
            return carry

        lax.fori_loop(0, qi // 2, two_tiles, 0)

        @pl.when(qi % 2 == 0)
        def _():
            scores(qi, 0)
            softmax_pv(qi, 0, masked=True)

        @pl.when(qi % 2 == 1)
        def _():
            scores(qi - 1, 0)
            scores(qi, 1)
            softmax_pv(qi - 1, 0, masked=False)
            softmax_pv(qi, 1, masked=True)

        for a, hd in enumerate(heads):
            acc = acc_ref[a]
            out_ref[hd * V_HEAD:(hd + 1) * V_HEAD, :] = acc[:V_HEAD, :] / acc[V_HEAD:V_HEAD + 1, :]
    o_ref[...] = _rms(jnp.transpose(out_ref[...]), g_ref[...]).astype(BF16)


def _attention(qt, k, vt, g_attn, bsz, seq, tq):
    n_tok = bsz * seq
    nq = seq // tq
    qk_w = N_HEADS * HEAD_PAD
    return pl.pallas_call(
        functools.partial(_attn_kernel, tq=tq),
        out_shape=jax.ShapeDtypeStruct((n_tok, MLA_WIDTH), BF16),
        grid=(bsz, nq),
        in_specs=[pl.BlockSpec((None, qk_w, tq), lambda b, i: (b * nq + i, 0, 0)),
                  pl.BlockSpec((seq, qk_w), lambda b, i: (b, 0)),
                  pl.BlockSpec((nq, qk_w, tq), lambda b, i: (b, 0, 0)),
                  _full((1, MLA_WIDTH))],
        out_specs=pl.BlockSpec((tq, MLA_WIDTH), lambda b, i: (b * nq + i, 0)),
        scratch_shapes=[pltpu.VMEM((2, ATTN_GROUP, tq, tq), F32), pltpu.VMEM((ATTN_GROUP, tq, tq), BF16),
                        pltpu.VMEM((ATTN_GROUP, 1, tq), F32),
                        pltpu.VMEM((ATTN_GROUP, HEAD_PAD, tq), F32), pltpu.VMEM((MLA_WIDTH, tq), F32)],
        compiler_params=_cparams(("parallel", "parallel")),
        name="attn",
    )(qt, k, vt, g_attn)


S5_SLABS = 4
S5_SLAB_U = SSM_WIDTH // S5_SLABS
S5_SLAB_X = N_STATE // S5_SLABS


def _s5_kernel(u_ref, wb_ref, are_ref, aim_ref, wc_ref, d_ref, wglu_ref, bglu_ref, g_ref, after_ref,
               o_ref, xr_ref, xi_ref, sr_ref, si_ref, *, lc, nb):
    del after_ref

    @pl.when(pl.program_id(0) == 0)
    def _():
        xr_ref[...] = jnp.zeros_like(xr_ref)
        xi_ref[...] = jnp.zeros_like(xi_ref)

    rows = lc * nb
    u = u_ref[...].reshape(rows, SSM_WIDTH)
    ub = u.astype(BF16)
    def drive(j):
        bu = jnp.dot(ub[:, j * S5_SLAB_U:(j + 1) * S5_SLAB_U], wb_ref[j], preferred_element_type=F32)
        sr_ref[:, j * S5_SLAB_X:(j + 1) * S5_SLAB_X] = bu[:, :S5_SLAB_X]
        si_ref[:, j * S5_SLAB_X:(j + 1) * S5_SLAB_X] = bu[:, S5_SLAB_X:]

    ys = []
    drive(0)
    for c in range(S5_SLABS):
        if c + 1 < S5_SLABS:
            drive(c + 1)
        sl = slice(c * S5_SLAB_X, (c + 1) * S5_SLAB_X)
        ar = jnp.broadcast_to(are_ref[:, sl], (nb, S5_SLAB_X))
        ai = jnp.broadcast_to(aim_ref[:, sl], (nb, S5_SLAB_X))
        xr = xr_ref[:, sl]
        xi = xi_ref[:, sl]
        for t in range(lc):
            r = slice(t * nb, (t + 1) * nb)
            xr, xi = (ar * xr - ai * xi + sr_ref[r, sl], ar * xi + ai * xr + si_ref[r, sl])
            sr_ref[r, sl] = xr
            si_ref[r, sl] = xi
        xr_ref[:, sl] = xr
        xi_ref[:, sl] = xi
        ys.append(jnp.dot(sr_ref[:, sl].astype(BF16), wc_ref[0, c], preferred_element_type=F32)
                  + jnp.dot(si_ref[:, sl].astype(BF16), wc_ref[1, c], preferred_element_type=F32))
    y_all = jnp.concatenate(ys, axis=-1) + d_ref[...] * u
    chunks = _row_chunks(rows, S5_ROWS)
    ygs, gates = [], []
    for r in chunks:
        y = y_all[r, :]
        ygs.append(0.5 * y * (1.0 + jnp.tanh(math.sqrt(2.0 / math.pi) * (y + 0.044715 * (y * y * y)))))
    for yg in ygs:
        gates.append(jnp.dot(yg.astype(BF16), wglu_ref[...], preferred_element_type=F32) + bglu_ref[...])
    for r, yg, gate in zip(chunks, ygs, gates):
        out = _rms(yg * _sigmoid(gate), g_ref[...]).astype(BF16)
        o_ref[r.start // nb:r.stop // nb] = out.reshape((r.stop - r.start) // nb, nb, SSM_WIDTH)


def _s5(u_tm, wts, after, bsz, seq, lc):
    rows = lc * bsz
    return pl.pallas_call(
        functools.partial(_s5_kernel, lc=lc, nb=bsz),
        out_shape=jax.ShapeDtypeStruct((seq, bsz, SSM_WIDTH), BF16),
        grid=(seq // lc,),
        in_specs=[pl.BlockSpec((lc, bsz, SSM_WIDTH), lambda i: (i, 0, 0)),
                  _full((S5_SLABS, S5_SLAB_U, 2 * S5_SLAB_X)),
                  _full((1, N_STATE)), _full((1, N_STATE)),
                  _full((2, S5_SLABS, S5_SLAB_X, S5_SLAB_U)),
                  _full((1, SSM_WIDTH)), _full((SSM_WIDTH, SSM_WIDTH)), _full((1, SSM_WIDTH)),
                  _full((1, SSM_WIDTH)), pl.BlockSpec(memory_space=pl.ANY)],
        out_specs=pl.BlockSpec((lc, bsz, SSM_WIDTH), lambda i: (i, 0, 0)),
        scratch_shapes=[pltpu.VMEM((bsz, N_STATE), F32), pltpu.VMEM((bsz, N_STATE), F32),
                        pltpu.VMEM((rows, N_STATE), F32), pltpu.VMEM((rows, N_STATE), F32)],
        compiler_params=_cparams(("arbitrary",)),
        name="s5",
    )(u_tm, wts["s5_wb"], wts["s5_are"], wts["s5_aim"], wts["s5_wc"], wts["s5_d"],
      wts["w_glu"], wts["b_glu"], wts["g_ssm"], after)


ROUTE_E0 = N_EXPERT_GROUPS


def _outproj_kernel(h_ref, a_ref, s_ref, wout_ref, gmoe_ref, wrh_ref, wrl_ref, br_ref, tril_ref,
                    h1_ref, xn_ref, route_ref, rw_ref, cnt_ref):
    @pl.when((pl.program_id(0) == 0) & (pl.program_id(1) == 0))
    def _():
        cnt_ref[...] = jnp.zeros_like(cnt_ref)

    sub = tril_ref.shape[0]
    lane = lax.broadcasted_iota(I32, (sub, LANES), 1).astype(F32)
    big = float(LANES)
    chunks = _row_chunks(h_ref.shape[0], sub)
    xns = []
    for r in chunks:
        h1 = (h_ref[r, :]
              + jnp.dot(a_ref[r, :], wout_ref[:MLA_WIDTH, :], preferred_element_type=F32)
              + jnp.dot(s_ref[r, :], wout_ref[MLA_WIDTH:, :], preferred_element_type=F32))
        h1_ref[r, :] = h1
        xns.append(_rms(h1, gmoe_ref[...]))
    lgs = []
    for r, xn in zip(chunks, xns):
        xn_ref[r, :] = _pack_bf16_pairs(xn)
        x_hi = xn.astype(BF16)
        x_lo = (xn - x_hi.astype(F32)).astype(BF16)
        lgs.append((jnp.dot(x_hi, wrh_ref[...], preferred_element_type=F32)
                    + (jnp.dot(x_lo, wrh_ref[...], preferred_element_type=F32)
                       + jnp.dot(x_hi, wrl_ref[...], preferred_element_type=F32))) + br_ref[...])
    for r, lg in zip(chunks, lgs):
        gl = jnp.where(lane < N_EXPERT_GROUPS, lg, NEG)
        gmax = jnp.max(gl, axis=-1, keepdims=True)
        gsum = jnp.sum(jnp.exp(gl - gmax), axis=-1, keepdims=True)
        g_top_p = 1.0 / gsum
        gidx = jnp.min(jnp.where(gl == gmax, lane, big), axis=-1, keepdims=True)
        lo = ROUTE_E0 + EXPERTS_PER_GROUP * gidx
        sel = jnp.where((lane >= lo) & (lane < lo + EXPERTS_PER_GROUP), lg, NEG)
        m1 = jnp.max(sel, axis=-1, keepdims=True)
        i1 = jnp.min(jnp.where(sel == m1, lane, big), axis=-1, keepdims=True)
        sel2 = jnp.where(lane == i1, NEG, sel)
        m2 = jnp.max(sel2, axis=-1, keepdims=True)
        i2 = jnp.min(jnp.where(sel2 == m2, lane, big), axis=-1, keepdims=True)
        e21 = jnp.exp(m2 - m1)
        w1 = g_top_p / (1.0 + e21)
        w2 = g_top_p * e21 / (1.0 + e21)
        e1 = i1 - ROUTE_E0
        e2 = i2 - ROUTE_E0
        oh = jnp.where((lane == e1) | (lane == e2), 1.0, 0.0)
        before = cnt_ref[...] + jnp.dot(tril_ref[...], oh.astype(BF16), preferred_element_type=F32)
        r1 = jnp.sum(jnp.where(lane == e1, before, 0.0), axis=-1, keepdims=True)
        r2 = jnp.sum(jnp.where(lane == e2, before, 0.0), axis=-1, keepdims=True)
        cnt_ref[...] = cnt_ref[...] + jnp.sum(oh, axis=0, keepdims=True)
        route = jnp.where(lane == 0, e1, jnp.where(lane == 1, e2, jnp.where(lane == 2, r1, r2)))
        route_ref[:, r] = jnp.transpose(route)[:SUBLANES, :].astype(I32)
        rw_ref[r, :] = jnp.where(lane == 0, w1, w2)


def _outproj(h, attn_n, ssm_tm, wts, tril, bsz, seq, tm):
    n_tok = bsz * seq
    nl = seq // tm
    row = lambda b, l: (b * nl + l, 0)
    return pl.pallas_call(
        _outproj_kernel,
        out_shape=(jax.ShapeDtypeStruct((n_tok, D_MODEL), F32),
                   jax.ShapeDtypeStruct((n_tok, PACKED), U32),
                   jax.ShapeDtypeStruct((SUBLANES, n_tok), I32),
                   jax.ShapeDtypeStruct((n_tok, LANES), F32),
                   jax.ShapeDtypeStruct((1, LANES), F32)),
        grid=(bsz, nl),
        in_specs=[pl.BlockSpec((tm, D_MODEL), row),
                  pl.BlockSpec((tm, MLA_WIDTH), row),
                  pl.BlockSpec((tm, SSM_WIDTH), lambda b, l: (l, b)),
                  _full((D_MODEL, D_MODEL)), _full((1, D_MODEL)),
                  _full((D_MODEL, LANES)), _full((D_MODEL, LANES)), _full((1, LANES)),
                  _full(tril.shape)],
        out_specs=(pl.BlockSpec((tm, D_MODEL), row), pl.BlockSpec((tm, PACKED), row),
                   pl.BlockSpec((SUBLANES, tm), lambda b, l: (0, b * nl + l)),
                   pl.BlockSpec((tm, LANES), row),
                   _full((1, LANES))),
        compiler_params=_cparams(("arbitrary", "arbitrary")),
        name="outproj",
    )(h, attn_n, ssm_tm, wts["w_out"], wts["g_moe"], wts["w_r_hi"], wts["w_r_lo"], wts["b_r"], tril)


SC_CORES = 2
SC_SUBCORES = 16
SC_WORKERS = SC_CORES * SC_SUBCORES
SC_CHUNK = 64


def _sc_mesh():
    return plsc.VectorSubcoreMesh(core_axis_name="c", subcore_axis_name="s",
                                  num_cores=SC_CORES, num_subcores=SC_SUBCORES)


def _sc_worker_id():
    return lax.axis_index("s") * SC_CORES + lax.axis_index("c")


def _sc_two_slot_pipeline(n_chunks, load, stores):
    assert n_chunks % 2 == 0
    load(0, 0).start()

    @pl.loop(0, n_chunks // 2)
    def _(jj):
        for slot in (0, 1):
            j = 2 * jj + slot

            @pl.when(j >= 1)
            def _():
                for cp in stores(j - 1, 1 - slot):
                    cp.wait()

            @pl.when(j + 1 < n_chunks)
            def _():
                load(j + 1, 1 - slot).start()

            load(j, slot).wait()
            for cp in stores(j, slot):
                cp.start()

    for cp in stores(n_chunks - 1, 1):
        cp.wait()


def _sc_rows(j):
    return pl.ds(pl.multiple_of(j * SC_CHUNK, SC_CHUNK), SC_CHUNK)


def _dispatch(dest_k, pad_rows, xn, n_slots):
    n_tok, width = xn.shape
    n_pad = pad_rows.shape[0]
    per_w = n_tok // SC_WORKERS
    per_p = n_pad // SC_WORKERS
    assert per_p <= per_w

    @functools.partial(
        pl.kernel, mesh=_sc_mesh(),
        out_type=jax.ShapeDtypeStruct((n_slots, width), xn.dtype),
        scratch_types=[pltpu.VMEM((per_w,), I32), pltpu.VMEM((per_w,), I32), pltpu.VMEM((per_p,), I32),
                       pltpu.VMEM((2, SC_CHUNK, width), xn.dtype),
                       pltpu.SemaphoreType.DMA((2,)), pltpu.SemaphoreType.DMA((2, TOP_K))],
        name="sc_dispatch_rows",
    )
    def scatter_kernel(xn_hbm, d0_hbm, d1_hbm, dp_hbm, out_hbm, i0_v, i1_v, ip_v, rows_v, lsem, ssem):
        wid = _sc_worker_id()
        base = wid * per_w
        pltpu.sync_copy(d0_hbm.at[pl.ds(base, per_w)], i0_v)
        pltpu.sync_copy(d1_hbm.at[pl.ds(base, per_w)], i1_v)
        pltpu.sync_copy(dp_hbm.at[pl.ds(wid * per_p, per_p)], ip_v)

        def load(j, slot):
            return pltpu.make_async_copy(xn_hbm.at[pl.ds(base + j * SC_CHUNK, SC_CHUNK)],
                                         rows_v.at[slot], lsem.at[slot])

        def scatter(idx_v, k):
            return lambda j, slot: pltpu.make_async_copy(
                rows_v.at[slot], out_hbm.at[idx_v.at[_sc_rows(j)]], ssem.at[slot, k])

        to_k0, to_k1, to_pad = scatter(i0_v, 0), scatter(i1_v, 1), scatter(ip_v, 0)
        _sc_two_slot_pipeline(per_w // SC_CHUNK, load,
                              lambda j, slot: [to_k0(j, slot), to_k1(j, slot)])
        _sc_two_slot_pipeline(per_p // SC_CHUNK, load, lambda j, slot: [to_pad(j, slot)])

    return scatter_kernel(xn, dest_k[0], dest_k[1], pad_rows)


EXPERT_BLOCKS = 2


def _experts_kernel(be_ref, nused_ref, x_ref, *refs):
    del be_ref
    w_refs, y_ref = refs[:-1], refs[-1]
    te = x_ref.shape[0] // EXPERT_BLOCKS
    for b in range(EXPERT_BLOCKS):
        blk = pl.program_id(0) * EXPERT_BLOCKS + b
        wg_ref, wu_ref, wd_ref = w_refs[3 * b:3 * b + 3]
        rows = slice(b * te, (b + 1) * te)

        @pl.when(blk >= nused_ref[0])
        def _(rows=rows):
            y_ref[rows, :] = jnp.zeros((te, PACKED), U32)

        @pl.when(blk < nused_ref[0])
        def _(rows=rows, wg_ref=wg_ref, wu_ref=wu_ref, wd_ref=wd_ref):
            wg = wg_ref[...].astype(BF16)
            wu = wu_ref[...].astype(BF16)
            wd = wd_ref[...].astype(BF16)
            chunks = [slice(rows.start + r.start, rows.start + r.stop) for r in _row_chunks(te, EXPERT_ROWS)]
            gus = []
            for r in chunks:
                xb = _unpack_bf16_pairs(x_ref[r, :]).astype(BF16)
                gus.append((jnp.dot(xb, wg, preferred_element_type=F32),
                            jnp.dot(xb, wu, preferred_element_type=F32)))
            ys = []
            for g, u in gus:
                hdn = (g * _sigmoid(g) * u).astype(BF16)
                ys.append(jnp.dot(hdn, wd, preferred_element_type=F32))
            for r, y in zip(chunks, ys):
                y_ref[r, :] = _pack_bf16_pairs(y)


def _experts(block_e, n_used, xbuf, w_gate, w_up, w_down, layer, te):
    n_slots = xbuf.shape[0]
    step_rows = EXPERT_BLOCKS * te

    def weights(b):
        expert = lambda i, be, nu: (layer, be[EXPERT_BLOCKS * i + b], 0, 0)
        return [pl.BlockSpec((None, None, D_MODEL, D_EXPERT), expert),
                pl.BlockSpec((None, None, D_MODEL, D_EXPERT), expert),
                pl.BlockSpec((None, None, D_EXPERT, D_MODEL), expert)]

    grid_spec = pltpu.PrefetchScalarGridSpec(
        num_scalar_prefetch=2,
        grid=(n_slots // step_rows,),
        in_specs=[pl.BlockSpec((step_rows, PACKED), lambda i, be, nu: (i, 0))]
        + [spec for b in range(EXPERT_BLOCKS) for spec in weights(b)],
        out_specs=pl.BlockSpec((step_rows, PACKED), lambda i, be, nu: (i, 0)),
    )
    return pl.pallas_call(
        _experts_kernel,
        out_shape=jax.ShapeDtypeStruct((n_slots, PACKED), U32),
        grid_spec=grid_spec,
        compiler_params=_cparams(("arbitrary",)),
        name="experts",
    )(block_e, n_used, xbuf, *([w_gate, w_up, w_down] * EXPERT_BLOCKS))


def _sc_gather_rows(table, idx):
    n_rows = idx.shape[0]
    width = table.shape[1]
    per_w = n_rows // SC_WORKERS

    @functools.partial(
        pl.kernel, mesh=_sc_mesh(),
        out_type=jax.ShapeDtypeStruct((n_rows, width), table.dtype),
        scratch_types=[pltpu.VMEM((per_w,), I32), pltpu.VMEM((2, SC_CHUNK, width), table.dtype),
                       pltpu.SemaphoreType.DMA((2,)), pltpu.SemaphoreType.DMA((2,))],
        name="sc_gather_rows",
    )
    def gather_kernel(table_hbm, idx_hbm, out_hbm, idx_v, rows_v, lsem, ssem):
        base = _sc_worker_id() * per_w
        pltpu.sync_copy(idx_hbm.at[pl.ds(base, per_w)], idx_v)

        def load(j, slot):
            return pltpu.make_async_copy(table_hbm.at[idx_v.at[_sc_rows(j)]], rows_v.at[slot],
                                         lsem.at[slot])

        def store(j, slot):
            return [pltpu.make_async_copy(rows_v.at[slot],
                                          out_hbm.at[pl.ds(base + j * SC_CHUNK, SC_CHUNK)],
                                          ssem.at[slot])]

        _sc_two_slot_pipeline(per_w // SC_CHUNK, load, store)

    return gather_kernel(table, idx)


def _combine_kernel(y0_ref, y1_ref, h1_ref, rw_ref, p_ref, gple_ref, wgate_ref, wproj_ref,
                    gfin_ref, o_ref, *, final):
    chunks = _row_chunks(h1_ref.shape[0], COMBINE_ROWS)
    h2s, gates, ples = [], [], []
    for r in chunks:
        rw = rw_ref[r, :]
        h2s.append(h1_ref[r, :] + rw[:, 0:1] * _unpack_bf16_pairs(y0_ref[r, :])
                   + rw[:, 1:2] * _unpack_bf16_pairs(y1_ref[r, :]))
    for r, h2 in zip(chunks, h2s):
        hn = _rms(h2, gple_ref[...]).astype(BF16)
        gates.append(jnp.dot(hn, wgate_ref[...], preferred_element_type=F32))
        ples.append(jnp.dot(p_ref[r, :].astype(BF16), wproj_ref[...], preferred_element_type=F32))
    for r, h2, gate, ple in zip(chunks, h2s, gates, ples):
        h3 = h2 + ple * _sigmoid(gate)
        if final:
            h3 = _rms(h3, gfin_ref[...])
        o_ref[r, :] = h3


def _combine(dest_k, ybuf, h1, rw, p_all, layer, wts, g_final, tf, final):
    n_tok = h1.shape[0]
    n_steps = n_tok // tf
    row = lambda i: (i, 0)
    yg = _sc_gather_rows(ybuf, dest_k.reshape(-1))
    return pl.pallas_call(
        functools.partial(_combine_kernel, final=final),
        out_shape=jax.ShapeDtypeStruct((n_tok, D_MODEL), F32),
        grid=(n_steps,),
        in_specs=[pl.BlockSpec((tf, PACKED), row),
                  pl.BlockSpec((tf, PACKED), lambda i: (n_steps + i, 0)),
                  pl.BlockSpec((tf, D_MODEL), row), pl.BlockSpec((tf, LANES), row),
                  pl.BlockSpec((None, tf, PLE_DIM), lambda i: (layer, i, 0)),
                  _full((1, D_MODEL)), _full((D_MODEL, D_MODEL)), _full((PLE_DIM, D_MODEL)),
                  _full((1, D_MODEL))],
        out_specs=pl.BlockSpec((tf, D_MODEL), row),
        compiler_params=_cparams(("parallel",)),
        name="combine",
    )(yg, yg, h1, rw, p_all, wts["g_ple"], wts["w_ple_gate"], wts["w_ple_proj"], g_final)


def _layer_weights(i, g_mix_norm, w_in, g_q_lat, w_q_up, g_kv_lat, w_kv_up,
                   ssm_a_re, ssm_a_im, ssm_b_re, ssm_b_im, ssm_c_re, ssm_c_im, ssm_d, ssm_log_step,
                   w_glu, b_glu, g_attn_out, g_ssm_out, w_out, g_moe_norm,
                   w_group_router, b_group_router, w_expert_router, b_expert_router,
                   g_ple_norm, w_ple_gate, w_ple_proj):
    off_kv, off_kr, off_ssm = Q_LORA, Q_LORA + KV_LORA, Q_LORA + KV_LORA + QK_ROPE
    wi = w_in[i]
    kr_block = jnp.zeros((D_MODEL, HEAD_PAD), F32).at[:, ROPE_LANE0:ROPE_LANE0 + QK_ROPE].set(wi[:, off_kr:off_ssm])
    w_in_p = jnp.concatenate([wi[:, :off_kr], kr_block, wi[:, off_ssm:]], axis=1)
    wq = w_q_up[i].reshape(Q_LORA, N_HEADS, QK_NOPE + QK_ROPE) * (ATTN_SCALE * math.log2(math.e))
    wq = jnp.pad(wq, ((0, 0), (0, 0), (0, HEAD_PAD - QK_NOPE - QK_ROPE))).reshape(Q_LORA, N_HEADS * HEAD_PAD)
    wkv = w_kv_up[i].reshape(KV_LORA, N_HEADS, QK_NOPE + V_HEAD)
    wk = jnp.pad(wkv[..., :QK_NOPE], ((0, 0), (0, 0), (0, HEAD_PAD - QK_NOPE))).reshape(KV_LORA, N_HEADS * HEAD_PAD)
    wv = jnp.pad(wkv[..., QK_NOPE:], ((0, 0), (0, 0), (0, HEAD_PAD - V_HEAD))).reshape(KV_LORA, N_HEADS * HEAD_PAD)
    lam = lax.complex(ssm_a_re[i], ssm_a_im[i])
    lam_bar = jnp.exp(lam * jnp.exp(ssm_log_step[i])[:, None])
    b_fac = (lam_bar - 1.0) / lam
    bc = lax.complex(ssm_b_re[i], ssm_b_im[i]) * b_fac[:, :, None]
    n_per = SSM_GROUPS // S5_SLABS

    def block_diag(blocks):
        s, n, a, b = blocks.shape
        eye = jnp.eye(n, dtype=blocks.dtype)
        return jnp.einsum("snab,nm->snamb", blocks, eye).reshape(s, n * a, n * b)

    b_hp_re = jnp.real(bc).transpose(0, 2, 1).reshape(S5_SLABS, n_per, SSM_CH, SSM_STATE)
    b_hp_im = jnp.imag(bc).transpose(0, 2, 1).reshape(S5_SLABS, n_per, SSM_CH, SSM_STATE)
    s5_wb = jnp.concatenate([block_diag(b_hp_re), block_diag(b_hp_im)], axis=-1)
    c_re = ssm_c_re[i].transpose(0, 2, 1).reshape(S5_SLABS, n_per, SSM_STATE, SSM_CH)
    c_im = ssm_c_im[i].transpose(0, 2, 1).reshape(S5_SLABS, n_per, SSM_STATE, SSM_CH)
    s5_wc = jnp.stack([block_diag(c_re), -block_diag(c_im)])
    w_r = jnp.zeros((D_MODEL, LANES), F32)
    w_r = w_r.at[:, :N_EXPERT_GROUPS].set(w_group_router[i])
    w_r = w_r.at[:, ROUTE_E0:ROUTE_E0 + N_EXPERTS].set(w_expert_router[i])
    w_r_hi = w_r.astype(BF16)
    w_r_lo = (w_r - w_r_hi.astype(F32)).astype(BF16)
    b_r = jnp.zeros((1, LANES), F32)
    b_r = b_r.at[0, :N_EXPERT_GROUPS].set(b_group_router[i])
    b_r = b_r.at[0, ROUTE_E0:ROUTE_E0 + N_EXPERTS].set(b_expert_router[i])
    return dict(
        g_mix=g_mix_norm[i][None], w_in=w_in_p.astype(BF16),
        g_q=g_q_lat[i][None], w_q_t=wq.T.astype(BF16),
        g_kv=g_kv_lat[i][None], w_k=wk.astype(BF16), w_v_t=wv.T.astype(BF16),
        s5_wb=s5_wb.astype(BF16), s5_wc=s5_wc.astype(BF16),
        s5_are=jnp.real(lam_bar).reshape(1, N_STATE), s5_aim=jnp.imag(lam_bar).reshape(1, N_STATE),
        s5_d=ssm_d[i].reshape(1, SSM_WIDTH), w_glu=w_glu[i].astype(BF16), b_glu=b_glu[i][None],
        g_attn=g_attn_out[i][None], g_ssm=g_ssm_out[i][None],
        w_out=w_out[i].astype(BF16), g_moe=g_moe_norm[i][None],
        w_r_hi=w_r_hi, w_r_lo=w_r_lo, b_r=b_r,
        g_ple=g_ple_norm[i][None], w_ple_gate=w_ple_gate[i].astype(BF16),
        w_ple_proj=w_ple_proj[i].astype(BF16),
    )


def _slot_layout(route, cnt, n_blocks, te):
    counts = cnt[0, :N_EXPERTS].astype(I32)
    pcounts = (counts + te - 1) // te * te
    pends = jnp.cumsum(pcounts)
    pstarts = pends - pcounts
    eid = route[:TOP_K]
    rank = route[TOP_K:2 * TOP_K]
    start = jnp.sum(jnp.where(eid[:, :, None] == jnp.arange(N_EXPERTS)[None, None, :],
                              pstarts[None, None, :], 0), axis=-1)
    dest_k = (start + rank).astype(I32)
    blk0 = jnp.arange(n_blocks, dtype=I32) * te
    block_e = jnp.minimum(jnp.sum((pends[None, :] <= blk0[:, None]).astype(I32), axis=1), N_EXPERTS - 1)
    n_slots = n_blocks * te
    gap_start = jnp.concatenate([pstarts + counts, pends[-1:]])
    gap_len = jnp.concatenate([pcounts - counts, n_slots - pends[-1:]])
    gap_end = jnp.cumsum(gap_len)
    j = jnp.arange(n_slots - eid.size, dtype=I32)
    gap = jnp.sum((gap_end[None, :] <= j[:, None]).astype(I32), axis=1)
    onehot = gap[:, None] == jnp.arange(N_EXPERTS + 1)[None, :]
    pad_rows = j + jnp.sum(jnp.where(onehot, (gap_start - (gap_end - gap_len))[None, :], 0), axis=1)
    n_used = (pends[-1:] // te).astype(I32)
    return dest_k, block_e, n_used, pad_rows.astype(I32)


def kernel(x, p, positions, g_mix_norm, w_in, g_q_lat, w_q_up, g_kv_lat, w_kv_up, ssm_a_re, ssm_a_im, ssm_b_re, ssm_b_im, ssm_c_re, ssm_c_im, ssm_d, ssm_log_step, w_glu, b_glu, g_attn_out, g_ssm_out, w_out, g_moe_norm, w_group_router, b_group_router, w_expert_router, b_expert_router, w_exp_gate, w_exp_up, w_exp_down, g_ple_norm, w_ple_gate, w_ple_proj, g_final):
    bsz, seq, _ = x.shape
    depth = p.shape[0]
    n_tok = bsz * seq
    t = _tiles(n_tok, seq)
    n_slots = TOP_K * n_tok + N_EXPERTS * t["te"]
    n_blocks = n_slots // t["te"]

    cosf, sinf = _rope_tables(positions.reshape(n_tok, 1), t["tm"])
    sub = min(SUB_ROWS, t["tm"])
    tril = jnp.tril(jnp.ones((sub, sub), F32), k=-1).astype(BF16)
    g_fin = g_final[None]

    h = x.reshape(n_tok, D_MODEL)
    for i in range(depth):
        wts = _layer_weights(i, g_mix_norm, w_in, g_q_lat, w_q_up, g_kv_lat, w_kv_up,
                             ssm_a_re, ssm_a_im, ssm_b_re, ssm_b_im, ssm_c_re, ssm_c_im, ssm_d,
                             ssm_log_step, w_glu, b_glu, g_attn_out, g_ssm_out, w_out, g_moe_norm,
                             w_group_router, b_group_router, w_expert_router, b_expert_router,
                             g_ple_norm, w_ple_gate, w_ple_proj)
        qt, k, vt, u_tm = _inproj(h, cosf, sinf, wts, bsz, seq, t["tm"], t["tq"])
        attn_n = _attention(qt, k, vt, wts["g_attn"], bsz, seq, t["tq"])
        ssm_n = _s5(u_tm.reshape(seq, bsz, SSM_WIDTH), wts, attn_n, bsz, seq, t["lc"])
        h1, xn, route, rw, cnt = _outproj(h, attn_n, ssm_n.reshape(seq, bsz * SSM_WIDTH), wts, tril,
                                          bsz, seq, t["tm"])
        dest_k, block_e, n_used, pad_rows = _slot_layout(route, cnt, n_blocks, t["te"])
        xbuf = _dispatch(dest_k, pad_rows, xn, n_slots)
        ybuf = _experts(block_e, n_used, xbuf, w_exp_gate, w_exp_up, w_exp_down, i, t["te"])
        h = _combine(dest_k, ybuf, h1, rw, p.reshape(depth, n_tok, PLE_DIM), i, wts, g_fin, t["tf"],
                     final=(i == depth - 1))
    return h.reshape(bsz, seq, D_MODEL)
```

```python
import functools
import math

import jax
import jax.numpy as jnp
from jax import lax
from jax.experimental import pallas as pl
from jax.experimental.pallas import tpu as pltpu
from jax.experimental.pallas import tpu_sc as plsc

F32 = jnp.float32
BF16 = jnp.bfloat16
I32 = jnp.int32
U32 = jnp.uint32

D_MODEL = 1024
MLA_WIDTH = 512
SSM_WIDTH = 512
V_HEAD = 64
N_HEADS = 8
QK_NOPE = 64
QK_ROPE = 32
HALF_ROPE = QK_ROPE // 2
Q_LORA = 256
KV_LORA = 128
ROPE_BASE = 10000.0
ATTN_SCALE = 1.0 / math.sqrt(QK_NOPE + QK_ROPE)
SSM_CH = 16
SSM_GROUPS = 32
SSM_STATE = 64
N_EXPERT_GROUPS = 4
EXPERTS_PER_GROUP = 8
N_EXPERTS = 32
TOP_K = 2
D_EXPERT = 256
PLE_DIM = 256
EPS = 1e-6

LANES = 128
SUBLANES = 8
HEAD_PAD = 128
ROPE_LANE0 = QK_NOPE
N_STATE = SSM_GROUPS * SSM_STATE
PACKED = D_MODEL // 2
NEG = -1e30
VMEM_LIMIT = 56 * 1024 * 1024
SUB_ROWS = 512
INPROJ_ROWS = 256
COMBINE_ROWS = 256
EXPERT_ROWS = 256
S5_ROWS = 256


def _tiles(n_tok, seq):
    return dict(
        tm=min(1024, seq),
        tq=min(512, seq),
        lc=min(64, seq),
        te=512,
        tf=min(1024, n_tok),
    )


def _rms(x, g):
    ms = jnp.mean(x * x, axis=-1, keepdims=True)
    return x * lax.rsqrt(ms + EPS) * g


def _sigmoid(x):
    return 1.0 / (1.0 + jnp.exp(-x))


def _pack_bf16_pairs(x):
    lo = lax.bitcast_convert_type(x[:, :PACKED].astype(BF16).astype(F32), U32)
    hi = lax.bitcast_convert_type(x[:, PACKED:].astype(BF16).astype(F32), U32)
    return (lo >> 16) | hi


def _unpack_bf16_pairs(w):
    lo = lax.bitcast_convert_type(w << 16, F32)
    hi = lax.bitcast_convert_type(w & jnp.uint32(0xFFFF0000), F32)
    return jnp.concatenate([lo, hi], axis=1)


def _cparams(sem):
    return pltpu.CompilerParams(dimension_semantics=sem, vmem_limit_bytes=VMEM_LIMIT)


def _full(shape):
    nd = len(shape)
    return pl.BlockSpec(shape, lambda *_: (0,) * nd)


def _rope_table_kernel(pos_ref, invf_ref, cos_ref, sin_ref):
    ang = pos_ref[...].astype(F32) * invf_ref[...]
    cos_ref[...] = jnp.cos(ang)
    sin_ref[...] = jnp.sin(ang)


def _rope_tables(pos_col, tm):
    n_tok = pos_col.shape[0]
    lane = jnp.arange(LANES)
    in_rope = (lane >= ROPE_LANE0) & (lane < ROPE_LANE0 + QK_ROPE)
    freq = ROPE_BASE ** (-((lane - ROPE_LANE0) % HALF_ROPE).astype(F32) / HALF_ROPE)
    invf = jnp.where(in_rope, freq, 0.0).astype(F32)[None, :]
    return pl.pallas_call(
        _rope_table_kernel,
        out_shape=(jax.ShapeDtypeStruct((n_tok, LANES), F32),) * 2,
        grid=(n_tok // tm,),
        in_specs=[pl.BlockSpec((tm, 1), lambda i: (i, 0)), _full((1, LANES))],
        out_specs=(pl.BlockSpec((tm, LANES), lambda i: (i, 0)),) * 2,
        compiler_params=_cparams(("parallel",)),
        name="rope_tables",
    )(pos_col, invf)


def _rope(x, cosf, s_lo, s_hi):
    n = x.shape[-1]
    return (x * cosf + pltpu.roll(x, n - HALF_ROPE, 1) * s_lo + pltpu.roll(x, HALF_ROPE, 1) * s_hi)


def _row_chunks(n_rows, sub):
    sub = min(sub, n_rows)
    return [slice(c * sub, (c + 1) * sub) for c in range(n_rows // sub)]


def _inproj_kernel(h_ref, cos_ref, sin_ref, gmix_ref, win_ref, gq_ref, wqt_ref, gkv_ref, wk_ref, wvt_ref,
                   qt_ref, k_ref, vt_ref, u_ref):
    lane = lax.broadcasted_iota(I32, (1, LANES), 1)
    feat = lax.broadcasted_iota(I32, (N_HEADS * HEAD_PAD, 1), 0)
    ones_row = jnp.where(feat % HEAD_PAD == V_HEAD, 1.0, 0.0)
    tile = qt_ref.shape[-1]
    chunks = _row_chunks(h_ref.shape[0], INPROJ_ROWS)
    zs = []
    for r in chunks:
        xn = _rms(h_ref[r, :], gmix_ref[...]).astype(BF16)
        zs.append(jnp.dot(xn, win_ref[...], preferred_element_type=F32))
    qts, ks, vts = [], [], []
    for z in zs:
        qn_t = jnp.transpose(_rms(z[:, :Q_LORA], gq_ref[...])).astype(BF16)
        qts.append(jnp.dot(wqt_ref[...], qn_t, preferred_element_type=F32))
        kvn = _rms(z[:, Q_LORA:Q_LORA + KV_LORA], gkv_ref[...])
        ks.append(jnp.dot(kvn.astype(BF16), wk_ref[...], preferred_element_type=F32))
        vts.append(jnp.dot(wvt_ref[...], jnp.transpose(kvn).astype(BF16),
                           preferred_element_type=F32))
    for r, z, qt, k, vt in zip(chunks, zs, qts, ks, vts):
        cosf = cos_ref[r, :]
        sinf = sin_ref[r, :]
        s_lo = jnp.where((lane >= ROPE_LANE0) & (lane < ROPE_LANE0 + HALF_ROPE), -sinf, 0.0)
        s_hi = jnp.where((lane >= ROPE_LANE0 + HALF_ROPE) & (lane < ROPE_LANE0 + QK_ROPE), sinf, 0.0)
        kpe = _rope(z[:, Q_LORA + KV_LORA:Q_LORA + KV_LORA + HEAD_PAD], cosf, s_lo, s_hi)
        cos_t = jnp.transpose(cosf)[ROPE_LANE0:ROPE_LANE0 + HALF_ROPE, :]
        sin_t = jnp.transpose(sinf)[ROPE_LANE0:ROPE_LANE0 + HALF_ROPE, :]
        cols = slice(r.start % tile, r.start % tile + (r.stop - r.start))
        for hd in range(N_HEADS):
            sl = slice(hd * HEAD_PAD, (hd + 1) * HEAD_PAD)
            k_ref[r, sl] = (k[:, sl] + kpe).astype(BF16)
            x1 = qt[hd * HEAD_PAD + ROPE_LANE0:hd * HEAD_PAD + ROPE_LANE0 + HALF_ROPE, :]
            x2 = qt[hd * HEAD_PAD + ROPE_LANE0 + HALF_ROPE:hd * HEAD_PAD + ROPE_LANE0 + QK_ROPE, :]
            qt_ref[r.start // tile, hd * HEAD_PAD:hd * HEAD_PAD + ROPE_LANE0, cols] = (
                qt[hd * HEAD_PAD:hd * HEAD_PAD + ROPE_LANE0, :].astype(BF16))
            qt_ref[r.start // tile, hd * HEAD_PAD + ROPE_LANE0:hd * HEAD_PAD + ROPE_LANE0 + QK_ROPE, cols] = (
                jnp.concatenate([x1 * cos_t - x2 * sin_t, x1 * sin_t + x2 * cos_t], axis=0).astype(BF16))
            qt_ref[r.start // tile, hd * HEAD_PAD + ROPE_LANE0 + QK_ROPE:(hd + 1) * HEAD_PAD, cols] = (
                qt[hd * HEAD_PAD + ROPE_LANE0 + QK_ROPE:(hd + 1) * HEAD_PAD, :].astype(BF16))
        vt_ref[r.start // tile, :, cols] = (vt + ones_row).astype(BF16)
        u_ref[r, :] = z[:, Q_LORA + KV_LORA + HEAD_PAD:]


def _inproj(h, cosf, sinf, wts, bsz, seq, tm, tq):
    n_tok = bsz * seq
    nl = seq // tm
    row = lambda b, l: (b * nl + l, 0)
    tile3 = lambda b, l: (b * nl + l, 0, 0)
    qk_w = N_HEADS * HEAD_PAD
    return pl.pallas_call(
        _inproj_kernel,
        out_shape=(jax.ShapeDtypeStruct((n_tok // tq, qk_w, tq), BF16),
                   jax.ShapeDtypeStruct((n_tok, qk_w), BF16),
                   jax.ShapeDtypeStruct((n_tok // tq, qk_w, tq), BF16),
                   jax.ShapeDtypeStruct((seq, bsz * SSM_WIDTH), F32)),
        grid=(bsz, nl),
        in_specs=[pl.BlockSpec((tm, D_MODEL), row),
                  pl.BlockSpec((tm, LANES), row), pl.BlockSpec((tm, LANES), row),
                  _full((1, D_MODEL)), _full((D_MODEL, D_MODEL)),
                  _full((1, Q_LORA)), _full((qk_w, Q_LORA)),
                  _full((1, KV_LORA)), _full((KV_LORA, qk_w)), _full((qk_w, KV_LORA))],
        out_specs=(pl.BlockSpec((tm // tq, qk_w, tq), tile3), pl.BlockSpec((tm, qk_w), row),
                   pl.BlockSpec((tm // tq, qk_w, tq), tile3),
                   pl.BlockSpec((tm, SSM_WIDTH), lambda b, l: (l, b))),
        compiler_params=_cparams(("parallel", "parallel")),
        name="inproj",
    )(h, cosf, sinf, wts["g_mix"], wts["w_in"], wts["g_q"], wts["w_q_t"], wts["g_kv"], wts["w_k"],
      wts["w_v_t"])


ATTN_RB = 16
ATTN_GROUP = 8

def _attn_kernel(qt_ref, k_ref, vt_ref, g_ref, o_ref, s_ref, p_ref, m_ref, acc_ref, out_ref, *, tq):
    qi = pl.program_id(1)
    key = lax.broadcasted_iota(I32, (ATTN_RB, tq), 0)
    qry = lax.broadcasted_iota(I32, (ATTN_RB, tq), 1)
    n_chunks = tq // ATTN_RB
    for pr in range(N_HEADS // ATTN_GROUP):
        heads = [ATTN_GROUP * pr + a for a in range(ATTN_GROUP)]
        m_ref[...] = jnp.full(m_ref.shape, NEG, F32)
        acc_ref[...] = jnp.zeros(acc_ref.shape, F32)

        def scores(j, slot, heads=heads):
            rows = pl.ds(pl.multiple_of(j * tq, tq), tq)
            for a, hd in enumerate(heads):
                hs = slice(hd * HEAD_PAD, (hd + 1) * HEAD_PAD)
                s_ref[slot, a] = jnp.dot(k_ref[rows, hs], qt_ref[hs, :], preferred_element_type=F32)

        def softmax_pv(j, slot, masked, heads=heads):
            alphas = []
            for a in range(ATTN_GROUP):
                def chunk(c, slot=slot, a=a):
                    s = s_ref[slot, a, c * ATTN_RB:(c + 1) * ATTN_RB, :]
                    if masked:
                        s = jnp.where(key + c * ATTN_RB <= qry, s, NEG)
                    return s

                top = chunk(0)
                for c in range(1, n_chunks):
                    top = jnp.maximum(top, chunk(c))
                m_old = m_ref[a]
                m_new = jnp.maximum(m_old, jnp.max(top, axis=0, keepdims=True))
                for c in range(n_chunks):
                    p_ref[a, c * ATTN_RB:(c + 1) * ATTN_RB, :] = jnp.exp2(chunk(c) - m_new).astype(BF16)
                alphas.append(jnp.exp2(m_old - m_new))
                m_ref[a] = m_new
            for a, hd in enumerate(heads):
                pv = jnp.dot(vt_ref[j, hd * HEAD_PAD:(hd + 1) * HEAD_PAD, :], p_ref[a],
                             preferred_element_type=F32)
                acc_ref[a] = alphas[a] * acc_ref[a] + pv

        def two_tiles(jj, carry):
            t = 2 * jj
            scores(t, 0)
            scores(t + 1, 1)
            softmax_pv(t, 0, masked=False)
            softmax_pv(t + 1, 1, masked=False)
            return carry

        lax.fori_loop(0, qi // 2, two_tiles, 0)

        @pl.when(qi % 2 == 0)
        def _():
            scores(qi, 0)
            softmax_pv(qi, 0, masked=True)

        @pl.when(qi % 2 == 1)
        def _():
            scores(qi - 1, 0)
            scores(qi, 1)
            softmax_pv(qi - 1, 0, masked=False)
            softmax_pv(qi, 1, masked=True)

        for a, hd in enumerate(heads):
            acc = acc_ref[a]
            out_ref[hd * V_HEAD:(hd + 1) * V_HEAD, :] = acc[:V_HEAD, :] / acc[V_HEAD:V_HEAD + 1, :]
    o_ref[...] = _rms(jnp.transpose(out_ref[...]), g_ref[...]).astype(BF16)


def _attention(qt, k, vt, g_attn, bsz, seq, tq):
    n_tok = bsz * seq
    nq = seq // tq
    qk_w = N_HEADS * HEAD_PAD
    return pl.pallas_call(
        functools.partial(_attn_kernel, tq=tq),
        out_shape=jax.ShapeDtypeStruct((n_tok, MLA_WIDTH), BF16),
        grid=(bsz, nq),
        in_specs=[pl.BlockSpec((None, qk_w, tq), lambda b, i: (b * nq + i, 0, 0)),
                  pl.BlockSpec((seq, qk_w), lambda b, i: (b, 0)),
                  pl.BlockSpec((nq, qk_w, tq), lambda b, i: (b, 0, 0)),
                  _full((1, MLA_WIDTH))],
        out_specs=pl.BlockSpec((tq, MLA_WIDTH), lambda b, i: (b * nq + i, 0)),
        scratch_shapes=[pltpu.VMEM((2, ATTN_GROUP, tq, tq), F32), pltpu.VMEM((ATTN_GROUP, tq, tq), BF16),
                        pltpu.VMEM((ATTN_GROUP, 1, tq), F32),
                        pltpu.VMEM((ATTN_GROUP, HEAD_PAD, tq), F32), pltpu.VMEM((MLA_WIDTH, tq), F32)],
        compiler_params=_cparams(("parallel", "parallel")),
        name="attn",
    )(qt, k, vt, g_attn)


S5_SLABS = 4
S5_SLAB_U = SSM_WIDTH // S5_SLABS
S5_SLAB_X = N_STATE // S5_SLABS


def _s5_kernel(u_ref, wb_ref, are_ref, aim_ref, wc_ref, d_ref, wglu_ref, bglu_ref, g_ref, after_ref,
               o_ref, xr_ref, xi_ref, sr_ref, si_ref, *, lc, nb):
    del after_ref

    @pl.when(pl.program_id(0) == 0)
    def _():
        xr_ref[...] = jnp.zeros_like(xr_ref)
        xi_ref[...] = jnp.zeros_like(xi_ref)

    rows = lc * nb
    u = u_ref[...].reshape(rows, SSM_WIDTH)
    ub = u.astype(BF16)
    def drive(j):
        bu = jnp.dot(ub[:, j * S5_SLAB_U:(j + 1) * S5_SLAB_U], wb_ref[j], preferred_element_type=F32)
        sr_ref[:, j * S5_SLAB_X:(j + 1) * S5_SLAB_X] = bu[:, :S5_SLAB_X]
        si_ref[:, j * S5_SLAB_X:(j + 1) * S5_SLAB_X] = bu[:, S5_SLAB_X:]

    ys = []
    drive(0)
    for c in range(S5_SLABS):
        if c + 1 < S5_SLABS:
            drive(c + 1)
        sl = slice(c * S5_SLAB_X, (c + 1) * S5_SLAB_X)
        ar = jnp.broadcast_to(are_ref[:, sl], (nb, S5_SLAB_X))
        ai = jnp.broadcast_to(aim_ref[:, sl], (nb, S5_SLAB_X))
        xr = xr_ref[:, sl]
        xi = xi_ref[:, sl]
        for t in range(lc):
            r = slice(t * nb, (t + 1) * nb)
            xr, xi = (ar * xr - ai * xi + sr_ref[r, sl], ar * xi + ai * xr + si_ref[r, sl])
            sr_ref[r, sl] = xr
            si_ref[r, sl] = xi
        xr_ref[:, sl] = xr
        xi_ref[:, sl] = xi
        ys.append(jnp.dot(sr_ref[:, sl].astype(BF16), wc_ref[0, c], preferred_element_type=F32)
                  + jnp.dot(si_ref[:, sl].astype(BF16), wc_ref[1, c], preferred_element_type=F32))
    y_all = jnp.concatenate(ys, axis=-1) + d_ref[...] * u
    chunks = _row_chunks(rows, S5_ROWS)
    ygs, gates = [], []
    for r in chunks:
        y = y_all[r, :]
        ygs.append(0.5 * y * (1.0 + jnp.tanh(math.sqrt(2.0 / math.pi) * (y + 0.044715 * (y * y * y)))))
    for yg in ygs:
        gates.append(jnp.dot(yg.astype(BF16), wglu_ref[...], preferred_element_type=F32) + bglu_ref[...])
    for r, yg, gate in zip(chunks, ygs, gates):
        out = _rms(yg * _sigmoid(gate), g_ref[...]).astype(BF16)
        o_ref[r.start // nb:r.stop // nb] = out.reshape((r.stop - r.start) // nb, nb, SSM_WIDTH)


def _s5(u_tm, wts, after, bsz, seq, lc):
    rows = lc * bsz
    return pl.pallas_call(
        functools.partial(_s5_kernel, lc=lc, nb=bsz),
        out_shape=jax.ShapeDtypeStruct((seq, bsz, SSM_WIDTH), BF16),
        grid=(seq // lc,),
        in_specs=[pl.BlockSpec((lc, bsz, SSM_WIDTH), lambda i: (i, 0, 0)),
                  _full((S5_SLABS, S5_SLAB_U, 2 * S5_SLAB_X)),
                  _full((1, N_STATE)), _full((1, N_STATE)),
                  _full((2, S5_SLABS, S5_SLAB_X, S5_SLAB_U)),
                  _full((1, SSM_WIDTH)), _full((SSM_WIDTH, SSM_WIDTH)), _full((1, SSM_WIDTH)),
                  _full((1, SSM_WIDTH)), pl.BlockSpec(memory_space=pl.ANY)],
        out_specs=pl.BlockSpec((lc, bsz, SSM_WIDTH), lambda i: (i, 0, 0)),
        scratch_shapes=[pltpu.VMEM((bsz, N_STATE), F32), pltpu.VMEM((bsz, N_STATE), F32),
                        pltpu.VMEM((rows, N_STATE), F32), pltpu.VMEM((rows, N_STATE), F32)],
        compiler_params=_cparams(("arbitrary",)),
        name="s5",
    )(u_tm, wts["s5_wb"], wts["s5_are"], wts["s5_aim"], wts["s5_wc"], wts["s5_d"],
      wts["w_glu"], wts["b_glu"], wts["g_ssm"], after)


ROUTE_E0 = N_EXPERT_GROUPS


def _outproj_kernel(h_ref, a_ref, s_ref, wout_ref, gmoe_ref, wrh_ref, wrl_ref, br_ref, tril_ref,
                    h1_ref, xn_ref, route_ref, rw_ref, cnt_ref):
    @pl.when((pl.program_id(0) == 0) & (pl.program_id(1) == 0))
    def _():
        cnt_ref[...] = jnp.zeros_like(cnt_ref)

    sub = tril_ref.shape[0]
    lane = lax.broadcasted_iota(I32, (sub, LANES), 1).astype(F32)
    big = float(LANES)
    chunks = _row_chunks(h_ref.shape[0], sub)
    xns = []
    for r in chunks:
        h1 = (h_ref[r, :]
              + jnp.dot(a_ref[r, :], wout_ref[:MLA_WIDTH, :], preferred_element_type=F32)
              + jnp.dot(s_ref[r, :], wout_ref[MLA_WIDTH:, :], preferred_element_type=F32))
        h1_ref[r, :] = h1
        xns.append(_rms(h1, gmoe_ref[...]))
    lgs = []
    for r, xn in zip(chunks, xns):
        xn_ref[r, :] = _pack_bf16_pairs(xn)
        x_hi = xn.astype(BF16)
        x_lo = (xn - x_hi.astype(F32)).astype(BF16)
        lgs.append((jnp.dot(x_hi, wrh_ref[...], preferred_element_type=F32)
                    + (jnp.dot(x_lo, wrh_ref[...], preferred_element_type=F32)
                       + jnp.dot(x_hi, wrl_ref[...], preferred_element_type=F32))) + br_ref[...])
    for r, lg in zip(chunks, lgs):
        gl = jnp.where(lane < N_EXPERT_GROUPS, lg, NEG)
        gmax = jnp.max(gl, axis=-1, keepdims=True)
        gsum = jnp.sum(jnp.exp(gl - gmax), axis=-1, keepdims=True)
        g_top_p = 1.0 / gsum
        gidx = jnp.min(jnp.where(gl == gmax, lane, big), axis=-1, keepdims=True)
        lo = ROUTE_E0 + EXPERTS_PER_GROUP * gidx
        sel = jnp.where((lane >= lo) & (lane < lo + EXPERTS_PER_GROUP), lg, NEG)
        m1 = jnp.max(sel, axis=-1, keepdims=True)
        i1 = jnp.min(jnp.where(sel == m1, lane, big), axis=-1, keepdims=True)
        sel2 = jnp.where(lane == i1, NEG, sel)
        m2 = jnp.max(sel2, axis=-1, keepdims=True)
        i2 = jnp.min(jnp.where(sel2 == m2, lane, big), axis=-1, keepdims=True)
        e21 = jnp.exp(m2 - m1)
        w1 = g_top_p / (1.0 + e21)
        w2 = g_top_p * e21 / (1.0 + e21)
        e1 = i1 - ROUTE_E0
        e2 = i2 - ROUTE_E0
        oh = jnp.where((lane == e1) | (lane == e2), 1.0, 0.0)
        before = cnt_ref[...] + jnp.dot(tril_ref[...], oh.astype(BF16), preferred_element_type=F32)
        r1 = jnp.sum(jnp.where(lane == e1, before, 0.0), axis=-1, keepdims=True)
        r2 = jnp.sum(jnp.where(lane == e2, before, 0.0), axis=-1, keepdims=True)
        cnt_ref[...] = cnt_ref[...] + jnp.sum(oh, axis=0, keepdims=True)
        route = jnp.where(lane == 0, e1, jnp.where(lane == 1, e2, jnp.where(lane == 2, r1, r2)))
        route_ref[:, r] = jnp.transpose(route)[:SUBLANES, :].astype(I32)
        rw_ref[r, :] = jnp.where(lane == 0, w1, w2)


def _outproj(h, attn_n, ssm_tm, wts, tril, bsz, seq, tm):
    n_tok = bsz * seq
    nl = seq // tm
    row = lambda b, l: (b * nl + l, 0)
    return pl.pallas_call(
        _outproj_kernel,
        out_shape=(jax.ShapeDtypeStruct((n_tok, D_MODEL), F32),
                   jax.ShapeDtypeStruct((n_tok, PACKED), U32),
                   jax.ShapeDtypeStruct((SUBLANES, n_tok), I32),
                   jax.ShapeDtypeStruct((n_tok, LANES), F32),
                   jax.ShapeDtypeStruct((1, LANES), F32)),
        grid=(bsz, nl),
        in_specs=[pl.BlockSpec((tm, D_MODEL), row),
                  pl.BlockSpec((tm, MLA_WIDTH), row),
                  pl.BlockSpec((tm, SSM_WIDTH), lambda b, l: (l, b)),
                  _full((D_MODEL, D_MODEL)), _full((1, D_MODEL)),
                  _full((D_MODEL, LANES)), _full((D_MODEL, LANES)), _full((1, LANES)),
                  _full(tril.shape)],
        out_specs=(pl.BlockSpec((tm, D_MODEL), row), pl.BlockSpec((tm, PACKED), row),
                   pl.BlockSpec((SUBLANES, tm), lambda b, l: (0, b * nl + l)),
                   pl.BlockSpec((tm, LANES), row),
                   _full((1, LANES))),
        compiler_params=_cparams(("arbitrary", "arbitrary")),
        name="outproj",
    )(h, attn_n, ssm_tm, wts["w_out"], wts["g_moe"], wts["w_r_hi"], wts["w_r_lo"], wts["b_r"], tril)


SC_CORES = 2
SC_SUBCORES = 16
SC_WORKERS = SC_CORES * SC_SUBCORES
SC_CHUNK = 64


def _sc_mesh():
    return plsc.VectorSubcoreMesh(core_axis_name="c", subcore_axis_name="s",
                                  num_cores=SC_CORES, num_subcores=SC_SUBCORES)


def _sc_worker_id():
    return lax.axis_index("s") * SC_CORES + lax.axis_index("c")


def _sc_two_slot_pipeline(n_chunks, load, stores):
    assert n_chunks % 2 == 0
    load(0, 0).start()

    @pl.loop(0, n_chunks // 2)
    def _(jj):
        for slot in (0, 1):
            j = 2 * jj + slot

            @pl.when(j >= 1)
            def _():
                for cp in stores(j - 1, 1 - slot):
                    cp.wait()

            @pl.when(j + 1 < n_chunks)
            def _():
                load(j + 1, 1 - slot).start()

            load(j, slot).wait()
            for cp in stores(j, slot):
                cp.start()

    for cp in stores(n_chunks - 1, 1):
        cp.wait()


def _sc_rows(j):
    return pl.ds(pl.multiple_of(j * SC_CHUNK, SC_CHUNK), SC_CHUNK)


def _dispatch(dest_k, pad_rows, xn, n_slots):
    n_tok, width = xn.shape
    n_pad = pad_rows.shape[0]
    per_w = n_tok // SC_WORKERS
    per_p = n_pad // SC_WORKERS
    assert per_p <= per_w

    @functools.partial(
        pl.kernel, mesh=_sc_mesh(),
        out_type=jax.ShapeDtypeStruct((n_slots, width), xn.dtype),
        scratch_types=[pltpu.VMEM((per_w,), I32), pltpu.VMEM((per_w,), I32), pltpu.VMEM((per_p,), I32),
                       pltpu.VMEM((2, SC_CHUNK, width), xn.dtype),
                       pltpu.SemaphoreType.DMA((2,)), pltpu.SemaphoreType.DMA((2, TOP_K))],
        name="sc_dispatch_rows",
    )
    def scatter_kernel(xn_hbm, d0_hbm, d1_hbm, dp_hbm, out_hbm, i0_v, i1_v, ip_v, rows_v, lsem, ssem):
        wid = _sc_worker_id()
        base = wid * per_w
        pltpu.sync_copy(d0_hbm.at[pl.ds(base, per_w)], i0_v)
        pltpu.sync_copy(d1_hbm.at[pl.ds(base, per_w)], i1_v)
        pltpu.sync_copy(dp_hbm.at[pl.ds(wid * per_p, per_p)], ip_v)

        def load(j, slot):
            return pltpu.make_async_copy(xn_hbm.at[pl.ds(base + j * SC_CHUNK, SC_CHUNK)],
                                         rows_v.at[slot], lsem.at[slot])

        def scatter(idx_v, k):
            return lambda j, slot: pltpu.make_async_copy(
                rows_v.at[slot], out_hbm.at[idx_v.at[_sc_rows(j)]], ssem.at[slot, k])

        to_k0, to_k1, to_pad = scatter(i0_v, 0), scatter(i1_v, 1), scatter(ip_v, 0)
        _sc_two_slot_pipeline(per_w // SC_CHUNK, load,
                              lambda j, slot: [to_k0(j, slot), to_k1(j, slot)])
        _sc_two_slot_pipeline(per_p // SC_CHUNK, load, lambda j, slot: [to_pad(j, slot)])

    return scatter_kernel(xn, dest_k[0], dest_k[1], pad_rows)


EXPERT_BLOCKS = 4


def _experts_kernel(be_ref, nused_ref, x_ref, *refs):
    del be_ref
    w_refs, y_ref = refs[:-1], refs[-1]
    te = x_ref.shape[0] // EXPERT_BLOCKS
    for b in range(EXPERT_BLOCKS):
        blk = pl.program_id(0) * EXPERT_BLOCKS + b
        wg_ref, wu_ref, wd_ref = w_refs[3 * b:3 * b + 3]
        rows = slice(b * te, (b + 1) * te)

        @pl.when(blk >= nused_ref[0])
        def _(rows=rows):
            y_ref[rows, :] = jnp.zeros((te, PACKED), U32)

        @pl.when(blk < nused_ref[0])
        def _(rows=rows, wg_ref=wg_ref, wu_ref=wu_ref, wd_ref=wd_ref):
            wg = wg_ref[...].astype(BF16)
            wu = wu_ref[...].astype(BF16)
            wd = wd_ref[...].astype(BF16)
            chunks = [slice(rows.start + r.start, rows.start + r.stop) for r in _row_chunks(te, EXPERT_ROWS)]
            gus = []
            for r in chunks:
                xb = _unpack_bf16_pairs(x_ref[r, :]).astype(BF16)
                gus.append((jnp.dot(xb, wg, preferred_element_type=F32),
                            jnp.dot(xb, wu, preferred_element_type=F32)))
            ys = []
            for g, u in gus:
                hdn = (g * _sigmoid(g) * u).astype(BF16)
                ys.append(jnp.dot(hdn, wd, preferred_element_type=F32))
            for r, y in zip(chunks, ys):
                y_ref[r, :] = _pack_bf16_pairs(y)


def _experts(block_e, n_used, xbuf, w_gate, w_up, w_down, layer, te):
    n_slots = xbuf.shape[0]
    step_rows = EXPERT_BLOCKS * te

    def weights(b):
        expert = lambda i, be, nu: (layer, be[EXPERT_BLOCKS * i + b], 0, 0)
        return [pl.BlockSpec((None, None, D_MODEL, D_EXPERT), expert),
                pl.BlockSpec((None, None, D_MODEL, D_EXPERT), expert),
                pl.BlockSpec((None, None, D_EXPERT, D_MODEL), expert)]

    grid_spec = pltpu.PrefetchScalarGridSpec(
        num_scalar_prefetch=2,
        grid=(n_slots // step_rows,),
        in_specs=[pl.BlockSpec((step_rows, PACKED), lambda i, be, nu: (i, 0))]
        + [spec for b in range(EXPERT_BLOCKS) for spec in weights(b)],
        out_specs=pl.BlockSpec((step_rows, PACKED), lambda i, be, nu: (i, 0)),
    )
    return pl.pallas_call(
        _experts_kernel,
        out_shape=jax.ShapeDtypeStruct((n_slots, PACKED), U32),
        grid_spec=grid_spec,
        compiler_params=_cparams(("arbitrary",)),
        name="experts",
    )(block_e, n_used, xbuf, *([w_gate, w_up, w_down] * EXPERT_BLOCKS))


def _sc_gather_rows(table, idx):
    n_rows = idx.shape[0]
    width = table.shape[1]
    per_w = n_rows // SC_WORKERS

    @functools.partial(
        pl.kernel, mesh=_sc_mesh(),
        out_type=jax.ShapeDtypeStruct((n_rows, width), table.dtype),
        scratch_types=[pltpu.VMEM((per_w,), I32), pltpu.VMEM((2, SC_CHUNK, width), table.dtype),
                       pltpu.SemaphoreType.DMA((2,)), pltpu.SemaphoreType.DMA((2,))],
        name="sc_gather_rows",
    )
    def gather_kernel(table_hbm, idx_hbm, out_hbm, idx_v, rows_v, lsem, ssem):
        base = _sc_worker_id() * per_w
        pltpu.sync_copy(idx_hbm.at[pl.ds(base, per_w)], idx_v)

        def load(j, slot):
            return pltpu.make_async_copy(table_hbm.at[idx_v.at[_sc_rows(j)]], rows_v.at[slot],
                                         lsem.at[slot])

        def store(j, slot):
            return [pltpu.make_async_copy(rows_v.at[slot],
                                          out_hbm.at[pl.ds(base + j * SC_CHUNK, SC_CHUNK)],
                                          ssem.at[slot])]

        _sc_two_slot_pipeline(per_w // SC_CHUNK, load, store)

    return gather_kernel(table, idx)


def _combine_kernel(y0_ref, y1_ref, h1_ref, rw_ref, p_ref, gple_ref, wgate_ref, wproj_ref,
                    gfin_ref, o_ref, *, final):
    chunks = _row_chunks(h1_ref.shape[0], COMBINE_ROWS)
    h2s, gates, ples = [], [], []
    for r in chunks:
        rw = rw_ref[r, :]
        h2s.append(h1_ref[r, :] + rw[:, 0:1] * _unpack_bf16_pairs(y0_ref[r, :])
                   + rw[:, 1:2] * _unpack_bf16_pairs(y1_ref[r, :]))
    for r, h2 in zip(chunks, h2s):
        hn = _rms(h2, gple_ref[...]).astype(BF16)
        gates.append(jnp.dot(hn, wgate_ref[...], preferred_element_type=F32))
        ples.append(jnp.dot(p_ref[r, :].astype(BF16), wproj_ref[...], preferred_element_type=F32))
    for r, h2, gate, ple in zip(chunks, h2s, gates, ples):
        h3 = h2 + ple * _sigmoid(gate)
        if final:
            h3 = _rms(h3, gfin_ref[...])
        o_ref[r, :] = h3


def _combine(dest_k, ybuf, h1, rw, p_all, layer, wts, g_final, tf, final):
    n_tok = h1.shape[0]
    n_steps = n_tok // tf
    row = lambda i: (i, 0)
    yg = _sc_gather_rows(ybuf, dest_k.reshape(-1))
    return pl.pallas_call(
        functools.partial(_combine_kernel, final=final),
        out_shape=jax.ShapeDtypeStruct((n_tok, D_MODEL), F32),
        grid=(n_steps,),
        in_specs=[pl.BlockSpec((tf, PACKED), row),
                  pl.BlockSpec((tf, PACKED), lambda i: (n_steps + i, 0)),
                  pl.BlockSpec((tf, D_MODEL), row), pl.BlockSpec((tf, LANES), row),
                  pl.BlockSpec((None, tf, PLE_DIM), lambda i: (layer, i, 0)),
                  _full((1, D_MODEL)), _full((D_MODEL, D_MODEL)), _full((PLE_DIM, D_MODEL)),
                  _full((1, D_MODEL))],
        out_specs=pl.BlockSpec((tf, D_MODEL), row),
        compiler_params=_cparams(("parallel",)),
        name="combine",
    )(yg, yg, h1, rw, p_all, wts["g_ple"], wts["w_ple_gate"], wts["w_ple_proj"], g_final)


def _layer_weights(i, g_mix_norm, w_in, g_q_lat, w_q_up, g_kv_lat, w_kv_up,
                   ssm_a_re, ssm_a_im, ssm_b_re, ssm_b_im, ssm_c_re, ssm_c_im, ssm_d, ssm_log_step,
                   w_glu, b_glu, g_attn_out, g_ssm_out, w_out, g_moe_norm,
                   w_group_router, b_group_router, w_expert_router, b_expert_router,
                   g_ple_norm, w_ple_gate, w_ple_proj):
    off_kv, off_kr, off_ssm = Q_LORA, Q_LORA + KV_LORA, Q_LORA + KV_LORA + QK_ROPE
    wi = w_in[i]
    kr_block = jnp.zeros((D_MODEL, HEAD_PAD), F32).at[:, ROPE_LANE0:ROPE_LANE0 + QK_ROPE].set(wi[:, off_kr:off_ssm])
    w_in_p = jnp.concatenate([wi[:, :off_kr], kr_block, wi[:, off_ssm:]], axis=1)
    wq = w_q_up[i].reshape(Q_LORA, N_HEADS, QK_NOPE + QK_ROPE) * (ATTN_SCALE * math.log2(math.e))
    wq = jnp.pad(wq, ((0, 0), (0, 0), (0, HEAD_PAD - QK_NOPE - QK_ROPE))).reshape(Q_LORA, N_HEADS * HEAD_PAD)
    wkv = w_kv_up[i].reshape(KV_LORA, N_HEADS, QK_NOPE + V_HEAD)
    wk = jnp.pad(wkv[..., :QK_NOPE], ((0, 0), (0, 0), (0, HEAD_PAD - QK_NOPE))).reshape(KV_LORA, N_HEADS * HEAD_PAD)
    wv = jnp.pad(wkv[..., QK_NOPE:], ((0, 0), (0, 0), (0, HEAD_PAD - V_HEAD))).reshape(KV_LORA, N_HEADS * HEAD_PAD)
    lam = lax.complex(ssm_a_re[i], ssm_a_im[i])
    lam_bar = jnp.exp(lam * jnp.exp(ssm_log_step[i])[:, None])
    b_fac = (lam_bar - 1.0) / lam
    bc = lax.complex(ssm_b_re[i], ssm_b_im[i]) * b_fac[:, :, None]
    n_per = SSM_GROUPS // S5_SLABS

    def block_diag(blocks):
        s, n, a, b = blocks.shape
        eye = jnp.eye(n, dtype=blocks.dtype)
        return jnp.einsum("snab,nm->snamb", blocks, eye).reshape(s, n * a, n * b)

    b_hp_re = jnp.real(bc).transpose(0, 2, 1).reshape(S5_SLABS, n_per, SSM_CH, SSM_STATE)
    b_hp_im = jnp.imag(bc).transpose(0, 2, 1).reshape(S5_SLABS, n_per, SSM_CH, SSM_STATE)
    s5_wb = jnp.concatenate([block_diag(b_hp_re), block_diag(b_hp_im)], axis=-1)
    c_re = ssm_c_re[i].transpose(0, 2, 1).reshape(S5_SLABS, n_per, SSM_STATE, SSM_CH)
    c_im = ssm_c_im[i].transpose(0, 2, 1).reshape(S5_SLABS, n_per, SSM_STATE, SSM_CH)
    s5_wc = jnp.stack([block_diag(c_re), -block_diag(c_im)])
    w_r = jnp.zeros((D_MODEL, LANES), F32)
    w_r = w_r.at[:, :N_EXPERT_GROUPS].set(w_group_router[i])
    w_r = w_r.at[:, ROUTE_E0:ROUTE_E0 + N_EXPERTS].set(w_expert_router[i])
    w_r_hi = w_r.astype(BF16)
    w_r_lo = (w_r - w_r_hi.astype(F32)).astype(BF16)
    b_r = jnp.zeros((1, LANES), F32)
    b_r = b_r.at[0, :N_EXPERT_GROUPS].set(b_group_router[i])
    b_r = b_r.at[0, ROUTE_E0:ROUTE_E0 + N_EXPERTS].set(b_expert_router[i])
    return dict(
        g_mix=g_mix_norm[i][None], w_in=w_in_p.astype(BF16),
        g_q=g_q_lat[i][None], w_q_t=wq.T.astype(BF16),
        g_kv=g_kv_lat[i][None], w_k=wk.astype(BF16), w_v_t=wv.T.astype(BF16),
        s5_wb=s5_wb.astype(BF16), s5_wc=s5_wc.astype(BF16),
        s5_are=jnp.real(lam_bar).reshape(1, N_STATE), s5_aim=jnp.imag(lam_bar).reshape(1, N_STATE),
        s5_d=ssm_d[i].reshape(1, SSM_WIDTH), w_glu=w_glu[i].astype(BF16), b_glu=b_glu[i][None],
        g_attn=g_attn_out[i][None], g_ssm=g_ssm_out[i][None],
        w_out=w_out[i].astype(BF16), g_moe=g_moe_norm[i][None],
        w_r_hi=w_r_hi, w_r_lo=w_r_lo, b_r=b_r,
        g_ple=g_ple_norm[i][None], w_ple_gate=w_ple_gate[i].astype(BF16),
        w_ple_proj=w_ple_proj[i].astype(BF16),
    )


def _slot_layout(route, cnt, n_blocks, te):
    counts = cnt[0, :N_EXPERTS].astype(I32)
    pcounts = (counts + te - 1) // te * te
    pends = jnp.cumsum(pcounts)
    pstarts = pends - pcounts
    eid = route[:TOP_K]
    rank = route[TOP_K:2 * TOP_K]
    start = jnp.sum(jnp.where(eid[:, :, None] == jnp.arange(N_EXPERTS)[None, None, :],
                              pstarts[None, None, :], 0), axis=-1)
    dest_k = (start + rank).astype(I32)
    blk0 = jnp.arange(n_blocks, dtype=I32) * te
    block_e = jnp.minimum(jnp.sum((pends[None, :] <= blk0[:, None]).astype(I32), axis=1), N_EXPERTS - 1)
    n_slots = n_blocks * te
    gap_start = jnp.concatenate([pstarts + counts, pends[-1:]])
    gap_len = jnp.concatenate([pcounts - counts, n_slots - pends[-1:]])
    gap_end = jnp.cumsum(gap_len)
    j = jnp.arange(n_slots - eid.size, dtype=I32)
    gap = jnp.sum((gap_end[None, :] <= j[:, None]).astype(I32), axis=1)
    onehot = gap[:, None] == jnp.arange(N_EXPERTS + 1)[None, :]
    pad_rows = j + jnp.sum(jnp.where(onehot, (gap_start - (gap_end - gap_len))[None, :], 0), axis=1)
    n_used = (pends[-1:] // te).astype(I32)
    return dest_k, block_e, n_used, pad_rows.astype(I32)


def kernel(x, p, positions, g_mix_norm, w_in, g_q_lat, w_q_up, g_kv_lat, w_kv_up, ssm_a_re, ssm_a_im, ssm_b_re, ssm_b_im, ssm_c_re, ssm_c_im, ssm_d, ssm_log_step, w_glu, b_glu, g_attn_out, g_ssm_out, w_out, g_moe_norm, w_group_router, b_group_router, w_expert_router, b_expert_router, w_exp_gate, w_exp_up, w_exp_down, g_ple_norm, w_ple_gate, w_ple_proj, g_final):
    bsz, seq, _ = x.shape
    depth = p.shape[0]
    n_tok = bsz * seq
    t = _tiles(n_tok, seq)
    n_slots = TOP_K * n_tok + N_EXPERTS * t["te"]
    n_blocks = n_slots // t["te"]

    cosf, sinf = _rope_tables(positions.reshape(n_tok, 1), t["tm"])
    sub = min(SUB_ROWS, t["tm"])
    tril = jnp.tril(jnp.ones((sub, sub), F32), k=-1).astype(BF16)
    g_fin = g_final[None]

    h = x.reshape(n_tok, D_MODEL)
    for i in range(depth):
        wts = _layer_weights(i, g_mix_norm, w_in, g_q_lat, w_q_up, g_kv_lat, w_kv_up,
                             ssm_a_re, ssm_a_im, ssm_b_re, ssm_b_im, ssm_c_re, ssm_c_im, ssm_d,
                             ssm_log_step, w_glu, b_glu, g_attn_out, g_ssm_out, w_out, g_moe_norm,
                             w_group_router, b_group_router, w_expert_router, b_expert_router,
                             g_ple_norm, w_ple_gate, w_ple_proj)
        qt, k, vt, u_tm = _inproj(h, cosf, sinf, wts, bsz, seq, t["tm"], t["tq"])
        attn_n = _attention(qt, k, vt, wts["g_attn"], bsz, seq, t["tq"])
        ssm_n = _s5(u_tm.reshape(seq, bsz, SSM_WIDTH), wts, attn_n, bsz, seq, t["lc"])
        h1, xn, route, rw, cnt = _outproj(h, attn_n, ssm_n.reshape(seq, bsz * SSM_WIDTH), wts, tril,
                                          bsz, seq, t["tm"])
        dest_k, block_e, n_used, pad_rows = _slot_layout(route, cnt, n_blocks, t["te"])
        xbuf = _dispatch(dest_k, pad_rows, xn, n_slots)
        ybuf = _experts(block_e, n_used, xbuf, w_exp_gate, w_exp_up, w_exp_down, i, t["te"])
        h = _combine(dest_k, ybuf, h1, rw, p.reshape(depth, n_tok, PLE_DIM), i, wts, g_fin, t["tf"],
                     final=(i == depth - 1))
    return h.reshape(bsz, seq, D_MODEL)
```
